```python
import math
import jax, jax.numpy as jnp
from jax import lax
import numpy as np

D_MODEL = 2048
BATCH = 2
SEQ = 4096
DEPTH = 1
DEC_BATCH = 32
DEC_SEQ = 4
PAST_LEN = 8192
PAGE_SIZE = 128

N_HEADS = 16
N_KV = 4
HPG = N_HEADS // N_KV
HEAD_DIM = 64
NSA_WIDTH = N_HEADS * HEAD_DIM
CMP_LEN = 32
CMP_STRIDE = 16
CMP_HID = 128
SEL_BLOCK = 64
N_SEL = 16
WINDOW = 512
QBLK = 128
KV_SLOTS = 4
HG_HEADS = 8
HG_DK = 128
HG_DV = 128
HG_WIDTH = HG_HEADS * HG_DV
HG_CHUNK = 32
N_GROUPS = 4
EXP_PER_GROUP = 8
N_EXPERTS = N_GROUPS * EXP_PER_GROUP
D_EXPERT = 512
TOP_K_IN_GROUP = 2
DN_ALPHA = (2.0 * DEPTH) ** 0.25
DN_BETA = (8.0 * DEPTH) ** -0.25
LN_EPS = 1e-5
SCALE = HEAD_DIM ** -0.5
NEG = -1e30
BIG = 1e30
SPLIT_SIZES = (NSA_WIDTH, 6 * N_KV * HEAD_DIM, 3 * N_HEADS, HG_HEADS * HG_DK, HG_HEADS * HG_DK, HG_WIDTH, HG_WIDTH, 2 * D_MODEL)
PROJ_COLS = NSA_WIDTH + 6 * N_KV * HEAD_DIM + 3 * N_HEADS + 2 * HG_HEADS * HG_DK + 2 * HG_WIDTH + 2 * D_MODEL

kernel_name = 'nsa_hgrn2_hier_moe_deepnorm_step'


def layer_norm(x, g, b):
    xf = x.astype(jnp.float32)
    mu = jnp.mean(xf, -1, keepdims=True)
    var = jnp.mean(jnp.square(xf - mu), -1, keepdims=True)
    return ((xf - mu) * lax.rsqrt(var + LN_EPS) * g + b).astype(x.dtype)


def alibi_slopes():
    return jnp.asarray(2.0 ** (-8.0 * np.arange(1, N_HEADS + 1) / N_HEADS), dtype=jnp.float32).reshape(N_KV, HPG)


def masked_softmax(s, valid):
    s = jnp.where(valid, s, NEG)
    m = jnp.max(s, -1, keepdims=True)
    e = jnp.where(valid, jnp.exp(s - m), 0.0)
    return e / jnp.maximum(jnp.sum(e, -1, keepdims=True), 1e-30)


def project(x, w, b):
    B, T, _ = x.shape
    offs = [int(v) for v in np.cumsum(SPLIT_SIZES)[:-1]]
    proj = jnp.einsum('btd,dc->btc', x, w) + b
    q, kv, ng, hq, hf, hi, hg, mg = jnp.split(proj, offs, axis=-1)
    return (q.reshape(B, T, N_HEADS, HEAD_DIM), kv.reshape(B, T, 6, N_KV, HEAD_DIM),
            ng.reshape(B, T, 3, N_KV, HPG), hq, hf, hi, hg, mg.reshape(B, T, 2, D_MODEL))


def compress_kv(kvc, w1, w2, pe):
    B, T = kvc.shape[:2]
    R = CMP_LEN // CMP_STRIDE
    n_cmp = (T - CMP_LEN) // CMP_STRIDE + 1
    n_ch = -(-T // CMP_STRIDE)
    kvc = jnp.pad(kvc, ((0, 0), (0, n_ch * CMP_STRIDE - T), (0, 0), (0, 0), (0, 0)))
    ch = kvc.reshape(B, n_ch, CMP_STRIDE, 2, N_KV, HEAD_DIM)
    w1r = w1.reshape(2, R, CMP_STRIDE, HEAD_DIM, CMP_HID)
    pre = jnp.einsum('cjd,cjdh->ch', pe, w1)[:, None, None, None, :]
    for r in range(R):
        pre = pre + jnp.einsum('bnjcgd,cjdh->cbngh', ch[:, r:r + n_cmp], w1r[:, r])
    out = jnp.einsum('cbngh,chd->cbngd', jax.nn.gelu(pre), w2)
    return out[0], out[1], n_cmp


def nsa_sparse(q, kv_full, qpos, w1, w2, pe, slopes):
    B, T = kv_full.shape[:2]
    Tq = q.shape[1]
    kc, vc, n_cmp = compress_kv(kv_full[:, :, 0:2], w1, w2, pe)
    cpos = jnp.arange(n_cmp, dtype=jnp.int32) * CMP_STRIDE + (CMP_LEN - 1)
    nsb = -(-T // SEL_BLOCK)
    k_sel = min(N_SEL, nsb)
    st = np.arange(n_cmp) * CMP_STRIDE
    bs = np.arange(nsb) * SEL_BLOCK
    overlap = jnp.asarray(((st[:, None] <= bs[None, :] + SEL_BLOCK - 1) & (st[:, None] + CMP_LEN - 1 >= bs[None, :])).astype(np.float32))
    ksv = jnp.pad(kv_full[:, :, 2:4], ((0, 0), (0, nsb * SEL_BLOCK - T), (0, 0), (0, 0), (0, 0)))
    ksv = ksv.reshape(B, nsb, SEL_BLOCK, 2, N_KV, HEAD_DIM).transpose(0, 4, 1, 2, 3, 5)
    blk_ids = jnp.arange(nsb, dtype=jnp.int32)
    offs = jnp.arange(SEL_BLOCK, dtype=jnp.int32)
    qb_len = min(QBLK, Tq)
    nb = Tq // qb_len

    def one_block(args):
        qb, pb = args
        qg = qb.reshape(B, qb_len, N_KV, HPG, HEAD_DIM)
        dist_c = pb[:, None] - cpos[None, :]
        s = jnp.einsum('bqghd,bngd->bghqn', qg, kc).astype(jnp.float32) * SCALE
        s = s - slopes[None, :, :, None, None] * dist_c.astype(jnp.float32)
        p = masked_softmax(s, dist_c >= 0)
        o_cmp = jnp.einsum('bghqn,bngd->bqghd', p.astype(vc.dtype), vc)
        imp = jnp.einsum('bghqn,nj->bgqj', p, overlap)
        cur = pb // SEL_BLOCK
        imp = jnp.where(blk_ids[None, :] > cur[:, None], NEG, imp)
        imp = jnp.where((blk_ids[None, :] == cur[:, None]) | (blk_ids[None, :] == 0), BIG, imp)
        _, idx = lax.top_k(imp, k_sel)
        gath = jax.vmap(jax.vmap(lambda kb, ib: kb[ib]))(ksv, idx)
        s2 = jnp.einsum('bqghd,bgqnpd->bghqnp', qg, gath[..., 0, :]).astype(jnp.float32) * SCALE
        spos = idx[..., None] * SEL_BLOCK + offs
        dist_s = pb[None, None, :, None, None] - spos
        s2 = s2 - slopes[None, :, :, None, None, None] * dist_s[:, :, None].astype(jnp.float32)
        s2 = s2.reshape(B, N_KV, HPG, qb_len, k_sel * SEL_BLOCK)
        valid2 = (dist_s >= 0).reshape(B, N_KV, qb_len, k_sel * SEL_BLOCK)[:, :, None]
        p2 = masked_softmax(s2, valid2)
        vg = gath[..., 1, :].reshape(B, N_KV, qb_len, k_sel * SEL_BLOCK, HEAD_DIM)
        o_slc = jnp.einsum('bghqm,bgqmd->bqghd', p2.astype(vg.dtype), vg)
        return o_cmp, o_slc

    qblocks = q.reshape(B, nb, qb_len, N_HEADS, HEAD_DIM).swapaxes(0, 1)
    pblocks = qpos.reshape(nb, qb_len)
    o_cmp, o_slc = lax.map(one_block, (qblocks, pblocks))
    o_cmp = o_cmp.swapaxes(0, 1).reshape(B, Tq, N_KV, HPG, HEAD_DIM)
    o_slc = o_slc.swapaxes(0, 1).reshape(B, Tq, N_KV, HPG, HEAD_DIM)
    return o_cmp, o_slc


def window_attend(q, k, v, qpos, kpos, slopes):
    B, Tq = q.shape[:2]
    qg = q.reshape(B, Tq, N_KV, HPG, HEAD_DIM)
    dist = qpos[:, None] - kpos[None, :]
    s = jnp.einsum('bqghd,bkgd->bghqk', qg, k).astype(jnp.float32) * SCALE
    s = s - slopes[None, :, :, None, None] * dist.astype(jnp.float32)
    valid = (dist >= 0) & (dist < WINDOW) & (kpos[None, :] >= 0)
    p = masked_softmax(s, valid)
    return jnp.einsum('bghqk,bkgd->bqghd', p.astype(v.dtype), v)


def prompt_window(q, kw, vw, slopes):
    B, S = q.shape[:2]
    nb = S // QBLK
    wb = WINDOW // QBLK
    kp = jnp.pad(kw, ((0, 0), (wb * QBLK, 0), (0, 0), (0, 0))).reshape(B, nb + wb, QBLK, N_KV, HEAD_DIM)
    vp = jnp.pad(vw, ((0, 0), (wb * QBLK, 0), (0, 0), (0, 0))).reshape(B, nb + wb, QBLK, N_KV, HEAD_DIM)
    kband = jnp.concatenate([kp[:, i:i + nb] for i in range(wb + 1)], axis=2)
    vband = jnp.concatenate([vp[:, i:i + nb] for i in range(wb + 1)], axis=2)
    kpos = (jnp.arange(nb, dtype=jnp.int32)[:, None] - wb) * QBLK + jnp.arange((wb + 1) * QBLK, dtype=jnp.int32)[None, :]
    qpos = jnp.arange(S, dtype=jnp.int32).reshape(nb, QBLK)
    o = jax.vmap(window_attend, in_axes=(1, 1, 1, 0, 0, None), out_axes=1)(
        q.reshape(B, nb, QBLK, N_HEADS, HEAD_DIM), kband, vband, qpos, kpos, slopes)
    return o.reshape(B, S, N_KV, HPG, HEAD_DIM)


def nsa_combine(gates, o_cmp, o_slc, o_win):
    g = jax.nn.sigmoid(gates)[..., None]
    o = g[:, :, 0] * o_cmp + g[:, :, 1] * o_slc + g[:, :, 2] * o_win
    B, T = o.shape[:2]
    return o.reshape(B, T, NSA_WIDTH)


def chunked_gla(q, k, v, logf, s0):
    B, T, H, _ = q.shape
    C = HG_CHUNK
    nc = -(-T // C)
    pad = nc * C - T

    def prep(a):
        a = jnp.pad(a, ((0, 0), (0, pad), (0, 0), (0, 0)))
        return a.reshape(B, nc, C, H, a.shape[-1]).transpose(1, 0, 3, 2, 4)

    mask = jnp.tril(jnp.ones((C, C), dtype=bool))

    def step(S, inp):
        qc, kc, vc, lc = inp
        bc = jnp.cumsum(lc, axis=-2)
        qe = qc * jnp.exp(bc)
        ke = kc * jnp.exp(-bc)
        att = jnp.where(mask, jnp.einsum('bhtd,bhsd->bhts', qe, ke), 0.0)
        o = jnp.einsum('bhtd,bhde->bhte', qe, S) + jnp.einsum('bhts,bhse->bhte', att, vc)
        bl = bc[:, :, -1:, :]
        S = jnp.exp(bl[:, :, 0, :, None]) * S + jnp.einsum('bhsd,bhse->bhde', kc * jnp.exp(bl - bc), vc)
        return S, o

    s_fin, o = lax.scan(step, s0, (prep(q), prep(k), prep(v), prep(logf)))
    o = o.transpose(1, 0, 3, 2, 4).reshape(B, nc * C, H, v.shape[-1])[:, :T]
    return o, s_fin


def hgrn2(hq, hf, hi, hg, lb, s0, norm_g):
    B, T, _ = hq.shape
    q = jax.nn.silu(hq.astype(jnp.float32).reshape(B, T, HG_HEADS, HG_DK))
    f = lb + (1.0 - lb) * jax.nn.sigmoid(hf.astype(jnp.float32).reshape(B, T, HG_HEADS, HG_DK))
    v = hi.astype(jnp.float32).reshape(B, T, HG_HEADS, HG_DV)
    o, s_fin = chunked_gla(q, 1.0 - f, v, jnp.log(f), s0.astype(jnp.float32))
    o = o * lax.rsqrt(jnp.mean(jnp.square(o), -1, keepdims=True) + LN_EPS) * norm_g
    o = o * jax.nn.silu(hg.astype(jnp.float32).reshape(B, T, HG_HEADS, HG_DV))
    return o.reshape(B, T, HG_WIDTH).astype(hq.dtype), s_fin


def hier_moe(h, w_rg, b_rg, w_re, b_re, w_gate, w_up, w_down):
    shp = h.shape
    hf = h.reshape(-1, D_MODEL)
    n_tok = hf.shape[0]
    lg = (hf @ w_rg + b_rg).astype(jnp.float32)
    grp = jnp.argmax(lg, -1)
    g_w = jnp.take_along_axis(jax.nn.softmax(lg, -1), grp[:, None], -1)
    le = (hf @ w_re + b_re).astype(jnp.float32).reshape(n_tok, N_GROUPS, EXP_PER_GROUP)
    le_sel = jnp.take_along_axis(le, grp[:, None, None], axis=1)[:, 0]
    top_v, top_i = lax.top_k(le_sel, TOP_K_IN_GROUP)
    w_sel = jax.nn.softmax(top_v, -1) * g_w
    eid = grp[:, None] * EXP_PER_GROUP + top_i
    gate = jnp.sum(jax.nn.one_hot(eid, N_EXPERTS, dtype=jnp.float32) * w_sel[..., None], axis=1)
    hid = jax.nn.silu(jnp.einsum('nd,edf->enf', hf, w_gate)) * jnp.einsum('nd,edf->enf', hf, w_up)
    hid = hid * gate.T[:, :, None].astype(hid.dtype)
    return jnp.einsum('enf,efd->nd', hid, w_down).reshape(shp)


def trunk_tail(x, o_nsa, o_hg, mg, lp):
    a = o_nsa @ lp['w_pa']
    b = o_hg @ lp['w_pb']
    y = (jax.nn.sigmoid(mg[:, :, 0]) * a + jax.nn.sigmoid(mg[:, :, 1]) * b) @ lp['w_out']
    h = layer_norm(DN_ALPHA * x + y, lp['ln1_g'], lp['ln1_b'])
    z = hier_moe(h, lp['w_rg'], lp['b_rg'], lp['w_re'], lp['b_re'], lp['w_gate'], lp['w_up'], lp['w_down'])
    return layer_norm(DN_ALPHA * h + z, lp['ln2_g'], lp['ln2_b'])


def setup_inputs(seed: int = 0) -> dict:
    key = jax.random.key(seed)
    ks = jax.random.split(key, 32)
    n_pages = PAST_LEN // PAGE_SIZE
    n_phys = (5 * DEC_BATCH * n_pages + 3) // 4
    w_buf = min(WINDOW, PAST_LEN)

    def nrm(k, shape, scale):
        return jax.random.normal(k, shape, jnp.float32) * scale

    return {
        'x_prompt': nrm(ks[0], (BATCH, SEQ, D_MODEL), 1.0),
        'x_sample': nrm(ks[1], (DEC_BATCH, DEC_SEQ, D_MODEL), 1.0),
        'cache_kv': nrm(ks[2], (DEPTH, n_phys, PAGE_SIZE, KV_SLOTS, N_KV, HEAD_DIM), 1.0),
        'cache_win': nrm(ks[3], (DEPTH, DEC_BATCH, w_buf, 2, N_KV, HEAD_DIM), 1.0),
        'state_hgrn': nrm(ks[4], (DEPTH, DEC_BATCH, HG_HEADS, HG_DK, HG_DV), 0.5),
        'page_table': jax.random.permutation(ks[5], n_phys)[:DEC_BATCH * n_pages].reshape(DEC_BATCH, n_pages).astype(jnp.int32),
        'w_in': nrm(ks[6], (DEPTH, D_MODEL, PROJ_COLS), D_MODEL ** -0.5),
        'b_in': nrm(ks[7], (DEPTH, PROJ_COLS), 0.02),
        'w_cmp1': nrm(ks[8], (DEPTH, 2, CMP_LEN, HEAD_DIM, CMP_HID), (CMP_LEN * HEAD_DIM) ** -0.5),
        'w_cmp2': nrm(ks[9], (DEPTH, 2, CMP_HID, HEAD_DIM), CMP_HID ** -0.5),
        'cmp_pe': nrm(ks[10], (DEPTH, 2, CMP_LEN, HEAD_DIM), 0.1),
        'hgrn_gamma': nrm(ks[11], (DEPTH + 1, HG_HEADS * HG_DK), 0.5),
        'hgrn_norm': 1.0 + nrm(ks[12], (DEPTH, HG_HEADS, HG_DV), 0.02),
        'w_pa': nrm(ks[13], (DEPTH, NSA_WIDTH, D_MODEL), NSA_WIDTH ** -0.5),
        'w_pb': nrm(ks[14], (DEPTH, HG_WIDTH, D_MODEL), HG_WIDTH ** -0.5),
        'w_out': nrm(ks[15], (DEPTH, D_MODEL, D_MODEL), D_MODEL ** -0.5 * DN_BETA),
        'ln1_g': 1.0 + nrm(ks[16], (DEPTH, D_MODEL), 0.02),
        'ln1_b': nrm(ks[17], (DEPTH, D_MODEL), 0.02),
        'w_rg': nrm(ks[18], (DEPTH, D_MODEL, N_GROUPS), D_MODEL ** -0.5),
        'b_rg': nrm(ks[19], (DEPTH, N_GROUPS), 0.01),
        'w_re': nrm(ks[20], (DEPTH, D_MODEL, N_EXPERTS), D_MODEL ** -0.5),
        'b_re': nrm(ks[21], (DEPTH, N_EXPERTS), 0.01),
        'w_gate': nrm(ks[22], (DEPTH, N_EXPERTS, D_MODEL, D_EXPERT), D_MODEL ** -0.5),
        'w_up': nrm(ks[23], (DEPTH, N_EXPERTS, D_MODEL, D_EXPERT), D_MODEL ** -0.5),
        'w_down': nrm(ks[24], (DEPTH, N_EXPERTS, D_EXPERT, D_MODEL), D_EXPERT ** -0.5 * DN_BETA),
        'ln2_g': 1.0 + nrm(ks[25], (DEPTH, D_MODEL), 0.02),
        'ln2_b': nrm(ks[26], (DEPTH, D_MODEL), 0.02),
    }


def reference(x_prompt, x_sample, cache_kv, cache_win, state_hgrn, page_table, w_in, b_in, w_cmp1, w_cmp2, cmp_pe,
              hgrn_gamma, hgrn_norm, w_pa, w_pb, w_out, ln1_g, ln1_b, w_rg, b_rg, w_re, b_re, w_gate, w_up, w_down,
              ln2_g, ln2_b):
    slopes = alibi_slopes()
    lower = jnp.cumsum(jax.nn.softmax(hgrn_gamma.astype(jnp.float32), axis=0), axis=0)
    n_pages = page_table.shape[1]
    past_len = n_pages * PAGE_SIZE
    w_buf = cache_win.shape[2]
    xp, xs = x_prompt, x_sample
    kv_p_l, kv_s_l, win_p_l, win_s_l, st_p_l, st_s_l = [], [], [], [], [], []
    for l in range(DEPTH):
        lp = {'w_pa': w_pa[l], 'w_pb': w_pb[l], 'w_out': w_out[l], 'ln1_g': ln1_g[l], 'ln1_b': ln1_b[l],
              'w_rg': w_rg[l], 'b_rg': b_rg[l], 'w_re': w_re[l], 'b_re': b_re[l], 'w_gate': w_gate[l],
              'w_up': w_up[l], 'w_down': w_down[l], 'ln2_g': ln2_g[l], 'ln2_b': ln2_b[l]}
        lb = lower[l].reshape(HG_HEADS, HG_DK)
        qp, kvp, gp, hqp, hfp, hip, hgp, mgp = project(xp, w_in[l], b_in[l])
        kv_full_p = kvp[:, :, :KV_SLOTS]
        oc_p, os_p = nsa_sparse(qp, kv_full_p, jnp.arange(xp.shape[1], dtype=jnp.int32), w_cmp1[l], w_cmp2[l], cmp_pe[l], slopes)
        ow_p = prompt_window(qp, kvp[:, :, 4], kvp[:, :, 5], slopes)
        o_nsa_p = nsa_combine(gp, oc_p, os_p, ow_p)
        s0 = jnp.zeros((xp.shape[0], HG_HEADS, HG_DK, HG_DV), jnp.float32)
        ohg_p, sfin_p = hgrn2(hqp, hfp, hip, hgp, lb, s0, hgrn_norm[l])
        w_keep_p = min(WINDOW, xp.shape[1])
        kv_p_l.append(kv_full_p)
        win_p_l.append(kvp[:, xp.shape[1] - w_keep_p:, 4:6])
        st_p_l.append(sfin_p.astype(xp.dtype))
        xp = trunk_tail(xp, o_nsa_p, ohg_p, mgp, lp)
        qs, kvs, gs, hqs, hfs, his, hgs, mgs = project(xs, w_in[l], b_in[l])
        past = cache_kv[l][page_table].reshape(xs.shape[0], past_len, KV_SLOTS, N_KV, HEAD_DIM)
        new_rows = kvs[:, :, :KV_SLOTS]
        kv_full_s = jnp.concatenate([past, new_rows.astype(past.dtype)], axis=1)
        qpos_s = past_len + jnp.arange(xs.shape[1], dtype=jnp.int32)
        oc_s, os_s = nsa_sparse(qs, kv_full_s, qpos_s, w_cmp1[l], w_cmp2[l], cmp_pe[l], slopes)
        win = jnp.concatenate([cache_win[l], kvs[:, :, 4:6].astype(cache_win.dtype)], axis=1)
        kpos_w = past_len - w_buf + jnp.arange(w_buf + xs.shape[1], dtype=jnp.int32)
        ow_s = window_attend(qs, win[:, :, 0], win[:, :, 1], qpos_s, kpos_w, slopes)
        o_nsa_s = nsa_combine(gs, oc_s, os_s, ow_s)
        ohg_s, sfin_s = hgrn2(hqs, hfs, his, hgs, lb, state_hgrn[l], hgrn_norm[l])
        w_keep_s = min(WINDOW, w_buf + xs.shape[1])
        kv_s_l.append(new_rows)
        win_s_l.append(win[:, win.shape[1] - w_keep_s:])
        st_s_l.append(sfin_s.astype(state_hgrn.dtype))
        xs = trunk_tail(xs, o_nsa_s, ohg_s, mgs, lp)
    new_kv_prompt = jnp.stack(kv_p_l, 0)
    new_kv_sample = jnp.stack(kv_s_l, 0)
    new_win_prompt = jnp.stack(win_p_l, 0)
    new_win_sample = jnp.stack(win_s_l, 0)
    new_state_prompt = jnp.stack(st_p_l, 0)
    new_state_sample = jnp.stack(st_s_l, 0)
    return (xp, xs, new_kv_prompt, new_kv_sample, new_win_prompt, new_win_sample, new_state_prompt, new_state_sample)
```

```python
import functools

import numpy as np
import jax
import jax.numpy as jnp
from jax import lax
from jax.experimental import pallas as pl
from jax.experimental.pallas import tpu as pltpu

D_MODEL = 2048
N_HEADS = 16
N_KV = 4
HPG = N_HEADS // N_KV
HEAD_DIM = 64
NSA_WIDTH = N_HEADS * HEAD_DIM
CMP_LEN = 32
CMP_STRIDE = 16
CMP_HID = 128
SEL_BLOCK = 64
N_SEL = 16
WINDOW = 512
QBLK = 128
KV_SLOTS = 4
PAGE_SIZE = 128
HG_HEADS = 8
HG_DK = 128
HG_DV = 128
HG_WIDTH = HG_HEADS * HG_DV
HG_CHUNK = 32
N_GROUPS = 4
EXP_PER_GROUP = 8
N_EXPERTS = N_GROUPS * EXP_PER_GROUP
D_EXPERT = 512
DEPTH = 1
DN_ALPHA = (2.0 * DEPTH) ** 0.25
LN_EPS = 1e-5
SCALE = HEAD_DIM ** -0.5
NEG = -1e30
BIG = 1e30

LANES = 128
KV_COLS = 6 * N_KV * HEAD_DIM
NG_COLS = 3 * N_HEADS
OFF_Q = 0
OFF_KV = NSA_WIDTH
OFF_NG = OFF_KV + KV_COLS
OFF_H4 = OFF_NG + NG_COLS
OFF_MG = OFF_H4 + 2 * HG_HEADS * HG_DK + 2 * HG_WIDTH
PROJ_COLS = OFF_MG + 2 * D_MODEL

VMEM_LIMIT = 56 * 1024 * 1024


def _sigmoid(x):
    return 1.0 / (1.0 + jnp.exp(-x))


def _proj_body(x_ref, w_ref, b_ref, *out_refs):
    acc = jnp.dot(x_ref[...], w_ref[...], preferred_element_type=jnp.float32) + b_ref[...]
    for o_ref in out_refs:
        o_ref[...] = acc.astype(o_ref.dtype)


def _proj(x_bf, w_bf, b, out_dtypes, tm, tn, name):
    m, k = x_bf.shape
    n = w_bf.shape[1]
    assert m % tm == 0 and n % tn == 0
    return pl.pallas_call(
        _proj_body,
        grid=(m // tm, n // tn),
        in_specs=[pl.BlockSpec((tm, k), lambda i, j: (i, 0)),
                  pl.BlockSpec((k, tn), lambda i, j: (0, j)),
                  pl.BlockSpec((1, tn), lambda i, j: (0, j))],
        out_specs=[pl.BlockSpec((tm, tn), lambda i, j: (i, j)) for _ in out_dtypes],
        out_shape=[jax.ShapeDtypeStruct((m, n), dt) for dt in out_dtypes],
        compiler_params=pltpu.CompilerParams(dimension_semantics=("parallel", "parallel"),
                                             vmem_limit_bytes=VMEM_LIMIT),
        name=name,
    )(x_bf, w_bf, b.reshape(1, n))


def _hgrn_body(hq_ref, hf_ref, hi_ref, hg_ref, lb_ref, nrm_ref, s0_ref, o_ref, sfin_ref, st_scr,
               *, chunk, n_valid, n_chunks):
    @pl.when(pl.program_id(2) == 0)
    def _():
        st_scr[...] = s0_ref[0, 0]

    rows = chunk * n_chunks
    hq = hq_ref[...]
    lb = lb_ref[0]
    q = hq * _sigmoid(hq)
    f = lb + (1.0 - lb) * _sigmoid(hf_ref[...])
    k = 1.0 - f
    lc = jnp.log(f)
    v = hi_ref[...]
    row_in_chunk = lax.broadcasted_iota(jnp.int32, (rows, HG_DK), 0) % chunk
    if n_valid < chunk:
        live = row_in_chunk < n_valid
        q = jnp.where(live, q, 0.0)
        k = jnp.where(live, k, 0.0)
        v = jnp.where(live, v, 0.0)
        lc = jnp.where(live, lc, 0.0)
    bc = lc
    step = 1
    while step < chunk:
        bc = bc + jnp.where(row_in_chunk >= step, pltpu.roll(bc, step, axis=0), 0.0)
        step *= 2
    bc3 = bc.reshape(n_chunks, chunk, HG_DK)
    bl3 = bc3[:, chunk - 1:chunk, :]
    q3 = q.reshape(n_chunks, chunk, HG_DK)
    k3 = k.reshape(n_chunks, chunk, HG_DK)
    v3 = v.reshape(n_chunks, chunk, HG_DV).astype(jnp.bfloat16)
    qe3 = (q3 * jnp.exp(bc3)).astype(jnp.bfloat16)
    ke3 = (k3 * jnp.exp(-bc3)).astype(jnp.bfloat16)
    kd3 = (k3 * jnp.exp(bl3 - bc3)).astype(jnp.bfloat16)
    dec3 = jnp.exp(bl3)
    att = jnp.einsum('ctd,csd->cts', qe3, ke3, preferred_element_type=jnp.float32)
    tri = (lax.broadcasted_iota(jnp.int32, (chunk, chunk), 0)
           >= lax.broadcasted_iota(jnp.int32, (chunk, chunk), 1))
    att = jnp.where(tri[None], att, 0.0).astype(jnp.bfloat16)
    o_intra = jnp.einsum('cts,cse->cte', att, v3, preferred_element_type=jnp.float32)

    st = st_scr[...]
    outs = []
    for c in range(n_chunks):
        o_c = lax.dot_general(qe3[c], st.astype(jnp.bfloat16), (((1,), (1,)), ((), ())),
                              preferred_element_type=jnp.float32)
        outs.append(o_c + o_intra[c])
        upd = lax.dot_general(v3[c], kd3[c], (((0,), (0,)), ((), ())),
                              preferred_element_type=jnp.float32)
        st = st * dec3[c] + upd
    st_scr[...] = st
    sfin_ref[0, 0] = st
    o = jnp.concatenate(outs, axis=0) if n_chunks > 1 else outs[0]
    o = o * lax.rsqrt(jnp.mean(o * o, axis=-1, keepdims=True) + LN_EPS) * nrm_ref[0]
    hg = hg_ref[...]
    o_ref[...] = (o * (hg * _sigmoid(hg))).astype(o_ref.dtype)


def _hgrn(h4, row0, n_seq, t_seq, lb, nrm, s0_t, chunk, n_valid, block_rows, name):
    assert t_seq % block_rows == 0 and block_rows % chunk == 0 and row0 % block_rows == 0
    nb = t_seq // block_rows
    rb0 = row0 // block_rows

    def col_spec(seg):
        return pl.BlockSpec((block_rows, HG_DK),
                            lambda b, h, i, seg=seg: (rb0 + b * nb + i, seg * HG_HEADS + h))

    body = functools.partial(_hgrn_body, chunk=chunk, n_valid=n_valid, n_chunks=block_rows // chunk)
    return pl.pallas_call(
        body,
        grid=(n_seq, HG_HEADS, nb),
        in_specs=[col_spec(0), col_spec(1), col_spec(2), col_spec(3),
                  pl.BlockSpec((1, 1, HG_DK), lambda b, h, i: (h, 0, 0)),
                  pl.BlockSpec((1, 1, HG_DV), lambda b, h, i: (h, 0, 0)),
                  pl.BlockSpec((1, 1, HG_DV, HG_DK), lambda b, h, i: (b, h, 0, 0))],
        out_specs=[pl.BlockSpec((block_rows, HG_DV), lambda b, h, i: (b * nb + i, h)),
                   pl.BlockSpec((1, 1, HG_DV, HG_DK), lambda b, h, i: (b, h, 0, 0))],
        out_shape=[jax.ShapeDtypeStruct((n_seq * t_seq, HG_WIDTH), jnp.bfloat16),
                   jax.ShapeDtypeStruct((n_seq, HG_HEADS, HG_DV, HG_DK), jnp.float32)],
        scratch_shapes=[pltpu.VMEM((HG_DV, HG_DK), jnp.float32)],
        compiler_params=pltpu.CompilerParams(
            dimension_semantics=("parallel", "parallel", "arbitrary"), vmem_limit_bytes=VMEM_LIMIT),
        name=name,
    )(h4, h4, h4, h4, lb.reshape(HG_HEADS, 1, HG_DK), nrm.reshape(HG_HEADS, 1, HG_DV), s0_t)


UV_COLS = 2 * N_KV * 2 * CMP_HID


def _cmp_uv_body(x0_ref, x1_ref, x2_ref, x3_ref, w_ref, uv_ref):
    n = uv_ref.shape[0]
    for cgp, x_ref in enumerate((x0_ref, x1_ref, x2_ref, x3_ref)):
        c = cgp // 2
        acc = jnp.zeros((n, 4 * CMP_HID), jnp.float32)
        for j in range(CMP_STRIDE):
            xj = x_ref[pl.ds(j, n, stride=CMP_STRIDE), :]
            acc = acc + jnp.dot(xj.astype(jnp.bfloat16), w_ref[c, j], preferred_element_type=jnp.float32)
        uv_ref[:, cgp * 512:(cgp + 1) * 512] = acc


def _cmp_w1_pairs(w_cmp1_l):
    w = jnp.concatenate([w_cmp1_l[:, :CMP_STRIDE], w_cmp1_l[:, CMP_STRIDE:]], axis=-1)
    z = jnp.zeros_like(w)
    top = jnp.concatenate([w, z], axis=-1)
    bot = jnp.concatenate([z, w], axis=-1)
    return jnp.concatenate([top, bot], axis=-2).astype(jnp.bfloat16)


def _cmp_uv_rows(kv, n_rows, rows_per_step, w1p):
    n = rows_per_step // CMP_STRIDE
    return pl.pallas_call(
        _cmp_uv_body,
        grid=(n_rows // rows_per_step,),
        in_specs=[pl.BlockSpec((rows_per_step, LANES), lambda i, cb=cb: (i, cb)) for cb in range(4)]
        + [pl.BlockSpec(w1p.shape, lambda i: (0, 0, 0, 0))],
        out_specs=pl.BlockSpec((n, UV_COLS), lambda i: (i, 0)),
        out_shape=jax.ShapeDtypeStruct((n_rows // CMP_STRIDE, UV_COLS), jnp.float32),
        compiler_params=pltpu.CompilerParams(dimension_semantics=("parallel",),
                                             vmem_limit_bytes=VMEM_LIMIT),
        name="cmp_uv_prompt",
    )(kv, kv, kv, kv, w1p)


def _gelu_tanh(x):
    return 0.5 * x * (1.0 + jnp.tanh(0.7978845608028654 * (x + 0.044715 * x * x * x)))


def _cmp_finish_body(uv_ref, pe_ref, w1_ref, w2_ref, out_ref):
    n_ch = uv_ref.shape[1]
    for c in range(2):
        pe_term = jnp.dot(pe_ref[c], w1_ref[c], preferred_element_type=jnp.float32)[0:1, :]
        for g in range(N_KV):
            base = (c * N_KV + g) * 2 * CMP_HID
            u = uv_ref[0, :, base:base + CMP_HID]
            v = uv_ref[0, :, base + CMP_HID:base + 2 * CMP_HID]
            pre = u + pltpu.roll(v, n_ch - 1, axis=0) + pe_term
            hid = _gelu_tanh(pre).astype(jnp.bfloat16)
            out_ref[0, c * N_KV + g] = jnp.dot(hid, w2_ref[c], preferred_element_type=jnp.float32
                                               ).astype(out_ref.dtype)


def _cmp_finish(uv, n_seq, n_ch, w_cmp1_l, w_cmp2_l, cmp_pe_l, name):
    pe = jnp.zeros((2, 8, CMP_LEN * HEAD_DIM), jnp.float32).at[:, 0].set(cmp_pe_l.reshape(2, -1))
    return pl.pallas_call(
        _cmp_finish_body,
        grid=(n_seq,),
        in_specs=[pl.BlockSpec((1, n_ch, UV_COLS), lambda b: (b, 0, 0)),
                  pl.BlockSpec((2, 8, CMP_LEN * HEAD_DIM), lambda b: (0, 0, 0)),
                  pl.BlockSpec((2, CMP_LEN * HEAD_DIM, CMP_HID), lambda b: (0, 0, 0)),
                  pl.BlockSpec((2, CMP_HID, HEAD_DIM), lambda b: (0, 0, 0))],
        out_specs=pl.BlockSpec((1, 2 * N_KV, n_ch, HEAD_DIM), lambda b: (b, 0, 0, 0)),
        out_shape=jax.ShapeDtypeStruct((n_seq, 2 * N_KV, n_ch, HEAD_DIM), jnp.bfloat16),
        compiler_params=pltpu.CompilerParams(dimension_semantics=("parallel",),
                                             vmem_limit_bytes=VMEM_LIMIT),
        name=name,
    )(uv.reshape(n_seq, n_ch, UV_COLS), pe.astype(jnp.bfloat16),
      w_cmp1_l.reshape(2, CMP_LEN * HEAD_DIM, CMP_HID).astype(jnp.bfloat16),
      w_cmp2_l.astype(jnp.bfloat16))


SLC_TK = 512
WIN_KEYS = WINDOW + QBLK


def _masked_softmax(s, valid):
    s = jnp.where(valid, s, NEG)
    m = jnp.max(s, axis=-1, keepdims=True)
    e = jnp.where(valid, jnp.exp(s - m), 0.0)
    return e / jnp.maximum(jnp.sum(e, axis=-1, keepdims=True), 1e-30)


def _split_bf16(x):
    hi = x.astype(jnp.bfloat16)
    return hi, (x - hi.astype(jnp.float32)).astype(jnp.bfloat16)


def _top_blocks(imp, blk, n_lanes):
    sel = jnp.zeros(imp.shape, jnp.float32)
    work = imp
    for _ in range(N_SEL):
        m = jnp.max(work, axis=-1, keepdims=True)
        idx = jnp.min(jnp.where(work == m, blk, n_lanes), axis=-1, keepdims=True)
        hit = blk == idx
        sel = jnp.where(hit, 1.0, sel)
        work = jnp.where(hit, -jnp.inf, work)
    return sel


def _nsa_prompt_body(q_ref, kc_ref, vc_ref, ks_ref, vs_ref, kw_ref, vw_ref, ng_ref, ov_ref, ex_ref, sl_ref,
                     o_ref, *, n_cmp):
    i = pl.program_id(2)
    nq = QBLK
    rows = HPG * nq
    q2 = (q_ref[0, 0] * SCALE).reshape(rows, HEAD_DIM)
    slopes = sl_ref[0]
    qpos = i * nq + lax.broadcasted_iota(jnp.int32, (nq, 1), 0)
    nt = (((1,), (1,)), ((), ()))

    n_ch = kc_ref.shape[2]
    s = lax.dot_general(q2, kc_ref[0, 0], nt, preferred_element_type=jnp.float32).reshape(HPG, nq, n_ch)
    cidx = lax.broadcasted_iota(jnp.int32, (1, n_ch), 1)
    dist_c = qpos - (cidx * CMP_STRIDE + (CMP_LEN - 1))
    valid_c = ((dist_c >= 0) & (cidx < n_cmp))[None]
    p = _masked_softmax(s - slopes * dist_c.astype(jnp.float32)[None], valid_c)
    o_cmp = jnp.dot(p.reshape(rows, n_ch).astype(jnp.bfloat16), vc_ref[0, 0],
                    preferred_element_type=jnp.float32).reshape(HPG, nq, HEAD_DIM)

    p_hi, p_lo = _split_bf16(p[0] + p[1] + p[2] + p[3])
    imp = (jnp.dot(p_hi, ov_ref[...], preferred_element_type=jnp.float32)
           + jnp.dot(p_lo, ov_ref[...], preferred_element_type=jnp.float32))
    nsb = imp.shape[-1]
    blk = lax.broadcasted_iota(jnp.int32, (1, nsb), 1)
    cur = qpos // SEL_BLOCK
    imp = jnp.where(blk > cur, NEG, imp)
    imp = jnp.where((blk == cur) | (blk == 0), BIG, imp)
    sel = _top_blocks(imp, blk, nsb).astype(jnp.bfloat16)

    def slc_step(kt, carry):
        m, l, acc = carry
        k0 = pl.multiple_of(kt * SLC_TK, SLC_TK)
        s = lax.dot_general(q2, ks_ref[0, 0, 0, pl.ds(k0, SLC_TK), :], nt,
                            preferred_element_type=jnp.float32).reshape(HPG, nq, SLC_TK)
        dist = qpos - (k0 + lax.broadcasted_iota(jnp.int32, (1, SLC_TK), 1))
        picked = jnp.dot(sel, ex_ref[:, pl.ds(k0, SLC_TK)], preferred_element_type=jnp.float32)
        valid = ((dist >= 0) & (picked > 0.5))[None]
        s = jnp.where(valid, s - slopes * dist.astype(jnp.float32)[None], NEG)
        m_new = jnp.maximum(m, jnp.max(s, axis=-1, keepdims=True))
        a = jnp.exp(m - m_new)
        e = jnp.where(valid, jnp.exp(s - m_new), 0.0)
        l = a * l + jnp.sum(e, axis=-1, keepdims=True)
        pv = jnp.dot(e.reshape(rows, SLC_TK).astype(jnp.bfloat16), vs_ref[0, 0, 0, pl.ds(k0, SLC_TK), :],
                     preferred_element_type=jnp.float32).reshape(HPG, nq, HEAD_DIM)
        return m_new, l, a * acc + pv

    init = (jnp.full((HPG, nq, 1), NEG, jnp.float32), jnp.zeros((HPG, nq, 1), jnp.float32),
            jnp.zeros((HPG, nq, HEAD_DIM), jnp.float32))
    n_kt = (i * nq) // SLC_TK + 1
    _, l, acc = lax.fori_loop(0, n_kt, slc_step, init)
    o_slc = acc / jnp.maximum(l, 1e-30)

    w0 = pl.multiple_of(jnp.maximum(i - WINDOW // QBLK, 0) * nq, nq)
    s = lax.dot_general(q2, kw_ref[0, 0, 0, pl.ds(w0, WIN_KEYS), :], nt,
                        preferred_element_type=jnp.float32).reshape(HPG, nq, WIN_KEYS)
    dist = qpos - (w0 + lax.broadcasted_iota(jnp.int32, (1, WIN_KEYS), 1))
    valid = ((dist >= 0) & (dist < WINDOW))[None]
    pw = _masked_softmax(s - slopes * dist.astype(jnp.float32)[None], valid)
    o_win = jnp.dot(pw.reshape(rows, WIN_KEYS).astype(jnp.bfloat16), vw_ref[0, 0, 0, pl.ds(w0, WIN_KEYS), :],
                    preferred_element_type=jnp.float32).reshape(HPG, nq, HEAD_DIM)

    gates = _sigmoid(ng_ref[0, 0])
    for h in range(HPG):
        o_ref[0, 0, h] = (gates[:, h:h + 1] * o_cmp[h] + gates[:, HPG + h:HPG + h + 1] * o_slc[h]
                          + gates[:, 2 * HPG + h:2 * HPG + h + 1] * o_win[h]).astype(o_ref.dtype)


def _alibi_slopes():
    return jnp.asarray(2.0 ** (-8.0 * np.arange(1, N_HEADS + 1) / N_HEADS), jnp.float32).reshape(N_KV, HPG, 1, 1)


def _nsa_prompt(q_t, kcvc, kv_t, ng_t, n_cmp):
    bsz, _, _, t, _ = q_t.shape
    n_ch = kcvc.shape[2]
    nsb = t // SEL_BLOCK
    st = np.arange(n_ch) * CMP_STRIDE
    bs = np.arange(nsb) * SEL_BLOCK
    overlap = ((st[:, None] <= bs[None, :] + SEL_BLOCK - 1) & (st[:, None] + CMP_LEN - 1 >= bs[None, :])
               & (np.arange(n_ch)[:, None] < n_cmp)).astype(np.float32)
    expand = (np.arange(t)[None, :] // SEL_BLOCK == np.arange(nsb)[:, None]).astype(np.float32)

    def kv_spec(slot):
        return pl.BlockSpec((1, 1, 1, t, HEAD_DIM), lambda b, g, i, slot=slot: (slot, b, g, 0, 0))

    return pl.pallas_call(
        functools.partial(_nsa_prompt_body, n_cmp=n_cmp),
        grid=(bsz, N_KV, t // QBLK),
        in_specs=[pl.BlockSpec((1, 1, HPG, QBLK, HEAD_DIM), lambda b, g, i: (b, g, 0, i, 0)),
                  pl.BlockSpec((1, 1, n_ch, HEAD_DIM), lambda b, g, i: (b, g, 0, 0)),
                  pl.BlockSpec((1, 1, n_ch, HEAD_DIM), lambda b, g, i: (b, N_KV + g, 0, 0)),
                  kv_spec(2), kv_spec(3), kv_spec(4), kv_spec(5),
                  pl.BlockSpec((1, 1, QBLK, 3 * HPG), lambda b, g, i: (b, g, i, 0)),
                  pl.BlockSpec((n_ch, nsb), lambda b, g, i: (0, 0)),
                  pl.BlockSpec((nsb, t), lambda b, g, i: (0, 0)),
                  pl.BlockSpec((1, HPG, 1, 1), lambda b, g, i: (g, 0, 0, 0))],
        out_specs=pl.BlockSpec((1, 1, HPG, QBLK, HEAD_DIM), lambda b, g, i: (b, g, 0, i, 0)),
        out_shape=jax.ShapeDtypeStruct(q_t.shape, jnp.bfloat16),
        compiler_params=pltpu.CompilerParams(
            dimension_semantics=("parallel", "parallel", "arbitrary"), vmem_limit_bytes=VMEM_LIMIT),
        name="nsa_prompt",
    )(q_t, kcvc, kcvc, kv_t, kv_t, kv_t, kv_t, ng_t,
      jnp.asarray(overlap, jnp.bfloat16), jnp.asarray(expand, jnp.bfloat16), _alibi_slopes())


HALF_COLS = 2 * N_KV * HEAD_DIM


def _page_copy(pt_ref, cache_ref, buf, sem, seq, slot, page, cb, col0):
    width = buf.shape[-1]
    return pltpu.make_async_copy(cache_ref.at[pt_ref[seq, page], :, pl.ds(col0 + cb * width, width)],
                                 buf.at[slot, page, cb], sem.at[slot])


def _fetch_pages(pt_ref, cache_ref, buf, sem, col0):
    b = pl.program_id(0)
    n_pages, n_cb = buf.shape[1], buf.shape[2]
    slot = b % 2

    def start_all(seq, sl):
        def one(page, carry):
            for cb in range(n_cb):
                _page_copy(pt_ref, cache_ref, buf, sem, seq, sl, page, cb, col0).start()
            return carry
        lax.fori_loop(0, n_pages, one, 0)

    @pl.when(b == 0)
    def _():
        start_all(0, 0)

    @pl.when(b + 1 < pl.num_programs(0))
    def _():
        start_all(b + 1, 1 - slot)

    def wait_one(page, carry):
        for cb in range(n_cb):
            _page_copy(pt_ref, cache_ref, buf, sem, b, slot, page, cb, col0).wait()
        return carry
    lax.fori_loop(0, n_pages, wait_one, 0)
    return slot


def _cmp_uv_sample_body(pt_ref, cache_ref, w_ref, uv_ref, buf, sem):
    slot = _fetch_pages(pt_ref, cache_ref, buf, sem, 0)
    n_pages = buf.shape[1]
    per_page = PAGE_SIZE // CMP_STRIDE
    n = n_pages * per_page
    for cgp in range(4):
        c = cgp // 2
        acc = jnp.zeros((n, 4 * CMP_HID), jnp.float32)
        for j in range(CMP_STRIDE):
            xj = buf[slot, :, cgp, pl.ds(j, per_page, stride=CMP_STRIDE), :]
            acc = acc + jnp.dot(xj.reshape(n, LANES).astype(jnp.bfloat16), w_ref[c, j],
                                preferred_element_type=jnp.float32)
        uv_ref[0, :, cgp * 512:(cgp + 1) * 512] = acc


def _cmp_uv_sample(page_table, cache, w1p):
    n_seq, n_pages = page_table.shape
    n_ch = n_pages * (PAGE_SIZE // CMP_STRIDE)
    return pl.pallas_call(
        _cmp_uv_sample_body,
        grid_spec=pltpu.PrefetchScalarGridSpec(
            num_scalar_prefetch=1, grid=(n_seq,),
            in_specs=[pl.BlockSpec(memory_space=pl.ANY),
                      pl.BlockSpec(w1p.shape, lambda b, pt: (0, 0, 0, 0))],
            out_specs=pl.BlockSpec((1, n_ch, UV_COLS), lambda b, pt: (b, 0, 0)),
            scratch_shapes=[pltpu.VMEM((2, n_pages, HALF_COLS // LANES, PAGE_SIZE, LANES), jnp.float32),
                            pltpu.SemaphoreType.DMA((2,))]),
        out_shape=jax.ShapeDtypeStruct((n_seq, n_ch, UV_COLS), jnp.float32),
        compiler_params=pltpu.CompilerParams(dimension_semantics=("arbitrary",),
                                             vmem_limit_bytes=VMEM_LIMIT),
        name="cmp_uv_sample",
    )(page_table, cache, w1p)


def _merge_softmax(parts):
    m = parts[0][0]
    for mp, _, _ in parts[1:]:
        m = jnp.maximum(m, mp)
    l = sum(jnp.exp(mp - m) * lp for mp, lp, _ in parts)
    acc = sum(jnp.exp(mp - m) * ap for mp, _, ap in parts)
    return acc / jnp.maximum(l, 1e-30)


def _partial_softmax(s, valid, v):
    s = jnp.where(valid, s, NEG)
    m = jnp.max(s, axis=-1, keepdims=True)
    e = jnp.where(valid, jnp.exp(s - m), 0.0)
    acc = jnp.dot(e.astype(jnp.bfloat16), v, preferred_element_type=jnp.float32)
    return m, jnp.sum(e, axis=-1, keepdims=True), acc


def _nsa_sample_body(pt_ref, cache_ref, q_ref, kcvc_ref, new_ref, win_ref, ng_ref, ov_ref, same_q_ref, sl_ref,
                     o_ref, buf, sem, *, past_len, n_cmp, t_new):
    slot = _fetch_pages(pt_ref, cache_ref, buf, sem, HALF_COLS)
    n_pages = buf.shape[1]
    rows = q_ref.shape[2]
    nt = (((1,), (1,)), ((), ()))
    row = lax.broadcasted_iota(jnp.int32, (rows, 1), 0)
    qtok = row % t_new
    qpos = past_len + qtok
    n_ch = kcvc_ref.shape[2]
    w_buf = win_ref.shape[1]
    new_rows = new_ref.shape[3]
    new_idx = lax.broadcasted_iota(jnp.int32, (1, new_rows), 1)
    new_valid = (new_idx <= qtok) & (new_idx < t_new)
    new_dist = (qtok - new_idx).astype(jnp.float32)

    for g in range(N_KV):
        q2 = q_ref[0, g] * SCALE
        slope = sl_ref[g]
        s = lax.dot_general(q2, kcvc_ref[0, g], nt, preferred_element_type=jnp.float32)
        cidx = lax.broadcasted_iota(jnp.int32, (1, n_ch), 1)
        dist_c = qpos - (cidx * CMP_STRIDE + (CMP_LEN - 1))
        valid_c = (dist_c >= 0) & (cidx < n_cmp)
        p = _masked_softmax(s - slope * dist_c.astype(jnp.float32), valid_c)
        o_cmp = jnp.dot(p.astype(jnp.bfloat16), kcvc_ref[0, N_KV + g], preferred_element_type=jnp.float32)
        p_hi, p_lo = _split_bf16(p)
        psum = (jnp.dot(same_q_ref[...], p_hi, preferred_element_type=jnp.float32)
                + jnp.dot(same_q_ref[...], p_lo, preferred_element_type=jnp.float32))
        ps_hi, ps_lo = _split_bf16(psum)
        imp = (jnp.dot(ps_hi, ov_ref[...], preferred_element_type=jnp.float32)
               + jnp.dot(ps_lo, ov_ref[...], preferred_element_type=jnp.float32))
        n_lanes = imp.shape[-1]
        blk = lax.broadcasted_iota(jnp.int32, (1, n_lanes), 1)
        cur = qpos // SEL_BLOCK
        imp = jnp.where(blk > cur, -jnp.inf, imp)
        imp = jnp.where((blk == cur) | (blk == 0), BIG, imp)
        kpos = lax.broadcasted_iota(jnp.int32, (1, past_len), 1)
        kblk = kpos // SEL_BLOCK
        picked = jnp.zeros((rows, past_len), jnp.bool_)
        work = imp
        for _ in range(N_SEL):
            m = jnp.max(work, axis=-1, keepdims=True)
            idx = jnp.min(jnp.where(work == m, blk, n_lanes), axis=-1, keepdims=True)
            picked = picked | (kblk == idx)
            work = jnp.where(blk == idx, -jnp.inf, work)
        k_past = buf[slot, :, 0, :, g * HEAD_DIM:(g + 1) * HEAD_DIM].reshape(past_len, HEAD_DIM)
        v_past = buf[slot, :, 0, :, (N_KV + g) * HEAD_DIM:(N_KV + g + 1) * HEAD_DIM].reshape(past_len, HEAD_DIM)
        s = lax.dot_general(q2, k_past.astype(jnp.bfloat16), nt, preferred_element_type=jnp.float32)
        dist = (qpos - kpos).astype(jnp.float32)
        part_past = _partial_softmax(s - slope * dist, picked, v_past.astype(jnp.bfloat16))
        s = lax.dot_general(q2, new_ref[0, 2, g], nt, preferred_element_type=jnp.float32)
        part_new = _partial_softmax(s - slope * new_dist, new_valid, new_ref[0, 3, g])
        o_slc = _merge_softmax([part_past, part_new])
        k_win = win_ref[0, :, g * HEAD_DIM:(g + 1) * HEAD_DIM].astype(jnp.bfloat16)
        v_win = win_ref[0, :, (N_KV + g) * HEAD_DIM:(N_KV + g + 1) * HEAD_DIM].astype(jnp.bfloat16)
        widx = lax.broadcasted_iota(jnp.int32, (1, w_buf), 1)
        wdist = qpos - (past_len - w_buf + widx)
        s = lax.dot_general(q2, k_win, nt, preferred_element_type=jnp.float32)
        part_w = _partial_softmax(s - slope * wdist.astype(jnp.float32), (wdist >= 0) & (wdist < WINDOW), v_win)
        s = lax.dot_general(q2, new_ref[0, 4, g], nt, preferred_element_type=jnp.float32)
        part_wn = _partial_softmax(s - slope * new_dist, new_valid, new_ref[0, 5, g])
        o_win = _merge_softmax([part_w, part_wn])
        gates = _sigmoid(ng_ref[0, g])
        o_ref[0, g] = (gates[:, 0:1] * o_cmp + gates[:, 1:2] * o_slc + gates[:, 2:3] * o_win
                       ).astype(o_ref.dtype)


def _nsa_sample(page_table, cache, q_g, kcvc, new_kv, cache_win, ng_g, n_cmp, t_new):
    n_seq, n_pages = page_table.shape
    past_len = n_pages * PAGE_SIZE
    n_ch = kcvc.shape[2]
    rows = q_g.shape[2]
    nsb = -(-(past_len + t_new) // SEL_BLOCK)
    n_lanes = -(-nsb // LANES) * LANES
    st = np.arange(n_ch) * CMP_STRIDE
    bs = np.arange(n_lanes) * SEL_BLOCK
    overlap = ((st[:, None] <= bs[None, :] + SEL_BLOCK - 1) & (st[:, None] + CMP_LEN - 1 >= bs[None, :])
               & (np.arange(n_ch)[:, None] < n_cmp) & (np.arange(n_lanes)[None, :] < nsb)).astype(np.float32)
    r = np.arange(rows)
    same_q = (r[:, None] % t_new == r[None, :] % t_new).astype(np.float32)
    slopes = jnp.repeat(_alibi_slopes().reshape(N_KV, HPG), t_new, axis=1).reshape(N_KV, rows, 1)
    w_buf = cache_win.shape[1]
    return pl.pallas_call(
        functools.partial(_nsa_sample_body, past_len=past_len, n_cmp=n_cmp, t_new=t_new),
        grid_spec=pltpu.PrefetchScalarGridSpec(
            num_scalar_prefetch=1, grid=(n_seq,),
            in_specs=[pl.BlockSpec(memory_space=pl.ANY),
                      pl.BlockSpec((1, N_KV, rows, HEAD_DIM), lambda b, pt: (b, 0, 0, 0)),
                      pl.BlockSpec((1, 2 * N_KV, n_ch, HEAD_DIM), lambda b, pt: (b, 0, 0, 0)),
                      pl.BlockSpec((1, 6, N_KV, new_kv.shape[3], HEAD_DIM), lambda b, pt: (b, 0, 0, 0, 0)),
                      pl.BlockSpec((1, w_buf, HALF_COLS), lambda b, pt: (b, 0, 0)),
                      pl.BlockSpec((1, N_KV, rows, 3), lambda b, pt: (b, 0, 0, 0)),
                      pl.BlockSpec((n_ch, n_lanes), lambda b, pt: (0, 0)),
                      pl.BlockSpec((rows, rows), lambda b, pt: (0, 0)),
                      pl.BlockSpec((N_KV, rows, 1), lambda b, pt: (0, 0, 0))],
            out_specs=pl.BlockSpec((1, N_KV, rows, HEAD_DIM), lambda b, pt: (b, 0, 0, 0)),
            scratch_shapes=[pltpu.VMEM((2, n_pages, 1, PAGE_SIZE, HALF_COLS), jnp.float32),
                            pltpu.SemaphoreType.DMA((2,))]),
        out_shape=jax.ShapeDtypeStruct((n_seq, N_KV, rows, HEAD_DIM), jnp.bfloat16),
        compiler_params=pltpu.CompilerParams(dimension_semantics=("arbitrary",),
                                             vmem_limit_bytes=VMEM_LIMIT),
        name="nsa_sample",
    )(page_table, cache, q_g, kcvc, new_kv, cache_win, ng_g,
      jnp.asarray(overlap, jnp.bfloat16), jnp.asarray(same_q, jnp.bfloat16), slopes)


def _mix_body(a_ref, b_ref, mga_ref, mgb_ref, wpa_ref, wpb_ref, u_ref):
    a = jnp.dot(a_ref[...], wpa_ref[...], preferred_element_type=jnp.float32)
    b = jnp.dot(b_ref[...], wpb_ref[...], preferred_element_type=jnp.float32)
    u_ref[...] = (_sigmoid(mga_ref[...]) * a + _sigmoid(mgb_ref[...]) * b).astype(u_ref.dtype)


def _mix(o_nsa, o_hg, mg, w_pa_bf, w_pb_bf, tm):
    n = o_nsa.shape[0]
    const = lambda i: (0, 0)
    return pl.pallas_call(
        _mix_body,
        grid=(n // tm,),
        in_specs=[pl.BlockSpec((tm, NSA_WIDTH), lambda i: (i, 0)),
                  pl.BlockSpec((tm, HG_WIDTH), lambda i: (i, 0)),
                  pl.BlockSpec((tm, D_MODEL), lambda i: (i, 0)),
                  pl.BlockSpec((tm, D_MODEL), lambda i: (i, 1)),
                  pl.BlockSpec((NSA_WIDTH, D_MODEL), const, pipeline_mode=pl.Buffered(1)),
                  pl.BlockSpec((HG_WIDTH, D_MODEL), const, pipeline_mode=pl.Buffered(1))],
        out_specs=pl.BlockSpec((tm, D_MODEL), lambda i: (i, 0)),
        out_shape=jax.ShapeDtypeStruct((n, D_MODEL), jnp.bfloat16),
        compiler_params=pltpu.CompilerParams(dimension_semantics=("parallel",),
                                             vmem_limit_bytes=VMEM_LIMIT),
        name="tail_mix",
    )(o_nsa, o_hg, mg, mg, w_pa_bf, w_pb_bf)


def _layer_norm(z, g, b):
    mu = jnp.mean(z, axis=-1, keepdims=True)
    zc = z - mu
    var = jnp.mean(zc * zc, axis=-1, keepdims=True)
    return zc * lax.rsqrt(var + LN_EPS) * g + b


def _ln1_body(u_ref, x_ref, wout_ref, g_ref, b_ref, wr_hi_ref, wr_lo_ref, br_ref, h_ref, lg_ref):
    y = jnp.dot(u_ref[...], wout_ref[...], preferred_element_type=jnp.float32)
    h = _layer_norm(DN_ALPHA * x_ref[...] + y, g_ref[...], b_ref[...])
    h_ref[...] = h
    h_hi = h.astype(jnp.bfloat16)
    h_lo = (h - h_hi.astype(jnp.float32)).astype(jnp.bfloat16)
    lg = jnp.dot(h_hi, wr_hi_ref[...], preferred_element_type=jnp.float32)
    lg = lg + jnp.dot(h_lo, wr_hi_ref[...], preferred_element_type=jnp.float32)
    lg = lg + jnp.dot(h_hi, wr_lo_ref[...], preferred_element_type=jnp.float32)
    lg_ref[...] = lg + br_ref[...]


def _ln1(u, x_all, w_out_bf, g, b, wr_hi, wr_lo, br, tm):
    n = u.shape[0]
    const = lambda i: (0, 0)
    return pl.pallas_call(
        _ln1_body,
        grid=(n // tm,),
        in_specs=[pl.BlockSpec((tm, D_MODEL), lambda i: (i, 0)),
                  pl.BlockSpec((tm, D_MODEL), lambda i: (i, 0)),
                  pl.BlockSpec((D_MODEL, D_MODEL), const, pipeline_mode=pl.Buffered(1)),
                  pl.BlockSpec((1, D_MODEL), const),
                  pl.BlockSpec((1, D_MODEL), const),
                  pl.BlockSpec((D_MODEL, LANES), const),
                  pl.BlockSpec((D_MODEL, LANES), const),
                  pl.BlockSpec((1, LANES), const)],
        out_specs=[pl.BlockSpec((tm, D_MODEL), lambda i: (i, 0)),
                   pl.BlockSpec((tm, LANES), lambda i: (i, 0))],
        out_shape=[jax.ShapeDtypeStruct((n, D_MODEL), jnp.float32),
                   jax.ShapeDtypeStruct((n, LANES), jnp.float32)],
        compiler_params=pltpu.CompilerParams(dimension_semantics=("parallel",),
                                             vmem_limit_bytes=VMEM_LIMIT),
        name="tail_ln1",
    )(u, x_all, w_out_bf, g.reshape(1, -1), b.reshape(1, -1), wr_hi, wr_lo, br)


LG0 = N_GROUPS


def _route_body(lg_ref, tri_ref, out_ref, cnt_ref, carry_scr):
    @pl.when(pl.program_id(0) == 0)
    def _():
        carry_scr[...] = jnp.zeros_like(carry_scr)

    lg = lg_ref[...]
    tm = lg.shape[0]
    lane = lax.broadcasted_iota(jnp.int32, lg.shape, 1)
    is_g = lane < N_GROUPS
    gl = jnp.where(is_g, lg, NEG)
    gmax = jnp.max(gl, axis=-1, keepdims=True)
    grp = jnp.min(jnp.where(gl == gmax, lane, LANES), axis=-1, keepdims=True)
    g_w = 1.0 / jnp.sum(jnp.where(is_g, jnp.exp(lg - gmax), 0.0), axis=-1, keepdims=True)
    lo = LG0 + grp * EXP_PER_GROUP
    el = jnp.where((lane >= lo) & (lane < lo + EXP_PER_GROUP), lg, NEG)
    v1 = jnp.max(el, axis=-1, keepdims=True)
    i1 = jnp.min(jnp.where(el == v1, lane, LANES), axis=-1, keepdims=True)
    el2 = jnp.where(lane == i1, NEG, el)
    v2 = jnp.max(el2, axis=-1, keepdims=True)
    i2 = jnp.min(jnp.where(el2 == v2, lane, LANES), axis=-1, keepdims=True)
    e21 = jnp.exp(v2 - v1)
    w1 = g_w / (1.0 + e21)
    w2 = g_w * e21 / (1.0 + e21)
    hit1 = lane == i1
    hit2 = lane == i2
    onehot = jnp.where(hit1 | hit2, 1.0, 0.0)
    incl = jnp.dot(tri_ref[...], onehot.astype(jnp.bfloat16), preferred_element_type=jnp.float32)
    carry = carry_scr[...]
    before = incl - onehot + carry
    r1 = jnp.sum(jnp.where(hit1, before, 0.0), axis=-1, keepdims=True)
    r2 = jnp.sum(jnp.where(hit2, before, 0.0), axis=-1, keepdims=True)
    carry = carry + incl[tm - 1:tm, :]
    carry_scr[...] = carry
    cnt_ref[...] = carry
    out = jnp.where(lane == 0, (i1 - LG0).astype(jnp.float32), 0.0)
    out = jnp.where(lane == 1, (i2 - LG0).astype(jnp.float32), out)
    out = jnp.where(lane == 2, w1, out)
    out = jnp.where(lane == 3, w2, out)
    out = jnp.where(lane == 4, r1, out)
    out = jnp.where(lane == 5, r2, out)
    out_ref[...] = out


def _route(lg, tm):
    n = lg.shape[0]
    tri = (np.arange(tm)[:, None] >= np.arange(tm)[None, :]).astype(np.float32)
    return pl.pallas_call(
        _route_body,
        grid=(n // tm,),
        in_specs=[pl.BlockSpec((tm, LANES), lambda i: (i, 0)),
                  pl.BlockSpec((tm, tm), lambda i: (0, 0))],
        out_specs=[pl.BlockSpec((tm, LANES), lambda i: (i, 0)),
                   pl.BlockSpec((1, LANES), lambda i: (0, 0))],
        out_shape=[jax.ShapeDtypeStruct((n, LANES), jnp.float32),
                   jax.ShapeDtypeStruct((1, LANES), jnp.float32)],
        scratch_shapes=[pltpu.VMEM((1, LANES), jnp.float32)],
        compiler_params=pltpu.CompilerParams(dimension_semantics=("arbitrary",)),
        name="moe_route",
    )(lg, jnp.asarray(tri, jnp.bfloat16))


DMA_WINDOW = 64


def _dispatch_body(pos_ref, h_ref, xs_in_ref, xs_ref, sem):
    del xs_in_ref
    n = h_ref.shape[0]

    def row_copy(tok, slot):
        dst = pos_ref[2 * tok + slot]
        return pltpu.make_async_copy(h_ref.at[pl.ds(tok, 1)], xs_ref.at[pl.ds(dst, 1)], sem)

    def step(tok, carry):
        @pl.when(tok >= DMA_WINDOW)
        def _():
            row_copy(tok - DMA_WINDOW, 0).wait()
            row_copy(tok - DMA_WINDOW, 1).wait()
        row_copy(tok, 0).start()
        row_copy(tok, 1).start()
        return carry

    lax.fori_loop(0, n, step, 0)

    def drain(tok, carry):
        row_copy(tok, 0).wait()
        row_copy(tok, 1).wait()
        return carry

    lax.fori_loop(n - DMA_WINDOW, n, drain, 0)


def _dispatch(pos_flat, h, n_slots):
    n, d = h.shape
    zeros = jnp.zeros((n_slots, d), h.dtype)
    return pl.pallas_call(
        _dispatch_body,
        grid_spec=pltpu.PrefetchScalarGridSpec(
            num_scalar_prefetch=1, grid=(1,),
            in_specs=[pl.BlockSpec(memory_space=pl.ANY), pl.BlockSpec(memory_space=pl.ANY)],
            out_specs=pl.BlockSpec(memory_space=pl.ANY),
            scratch_shapes=[pltpu.SemaphoreType.DMA(())]),
        out_shape=jax.ShapeDtypeStruct((n_slots, d), h.dtype),
        input_output_aliases={2: 0},
        compiler_params=pltpu.CompilerParams(dimension_semantics=("arbitrary",)),
        name="moe_dispatch",
    )(pos_flat, h, zeros)


def _ffn_body(te_ref, nu_ref, x_ref, wg_ref, wu_ref, wd_ref, y_ref, wg_bf, wu_bf, wd_bf):
    i = pl.program_id(0)
    prev = te_ref[jnp.maximum(i - 1, 0)]

    @pl.when((i == 0) | (te_ref[i] != prev))
    def _():
        wg_bf[...] = wg_ref[0].astype(jnp.bfloat16)
        wu_bf[...] = wu_ref[0].astype(jnp.bfloat16)
        wd_bf[...] = wd_ref[0].astype(jnp.bfloat16)

    @pl.when(i < nu_ref[0])
    def _():
        x = x_ref[...].astype(jnp.bfloat16)
        g = jnp.dot(x, wg_bf[...], preferred_element_type=jnp.float32)
        u = jnp.dot(x, wu_bf[...], preferred_element_type=jnp.float32)
        hid = (g * _sigmoid(g) * u).astype(jnp.bfloat16)
        y_ref[...] = jnp.dot(hid, wd_bf[...], preferred_element_type=jnp.float32)

    @pl.when(i >= nu_ref[0])
    def _():
        y_ref[...] = jnp.zeros_like(y_ref)


def _ffn(tile_expert, n_used, xs, w_gate, w_up, w_down, tm):
    n_slots, d = xs.shape
    nt = n_slots // tm
    return pl.pallas_call(
        _ffn_body,
        grid_spec=pltpu.PrefetchScalarGridSpec(
            num_scalar_prefetch=2, grid=(nt,),
            in_specs=[pl.BlockSpec((tm, d), lambda i, te, nu: (i, 0)),
                      pl.BlockSpec((1, d, D_EXPERT), lambda i, te, nu: (te[i], 0, 0)),
                      pl.BlockSpec((1, d, D_EXPERT), lambda i, te, nu: (te[i], 0, 0)),
                      pl.BlockSpec((1, D_EXPERT, d), lambda i, te, nu: (te[i], 0, 0))],
            out_specs=pl.BlockSpec((tm, d), lambda i, te, nu: (i, 0)),
            scratch_shapes=[pltpu.VMEM((d, D_EXPERT), jnp.bfloat16),
                            pltpu.VMEM((d, D_EXPERT), jnp.bfloat16),
                            pltpu.VMEM((D_EXPERT, d), jnp.bfloat16)]),
        out_shape=jax.ShapeDtypeStruct((n_slots, d), jnp.float32),
        compiler_params=pltpu.CompilerParams(dimension_semantics=("arbitrary",),
                                             vmem_limit_bytes=VMEM_LIMIT),
        name="moe_ffn",
    )(tile_expert, n_used, xs, w_gate, w_up, w_down)


def _combine_body(pos_ref, h_ref, rw_ref, g_ref, b_ref, y_ref, out_ref, buf, sem):
    tm = h_ref.shape[0]
    base = pl.program_id(0) * tm

    def issue(t, carry):
        for slot in range(2):
            src = pos_ref[2 * (base + t) + slot]
            pltpu.make_async_copy(y_ref.at[pl.ds(src, 1)], buf.at[slot, pl.ds(t, 1)], sem.at[slot]).start()
        return carry

    lax.fori_loop(0, tm, issue, 0)
    for slot in range(2):
        pltpu.make_async_copy(y_ref.at[pl.ds(0, tm)], buf.at[slot], sem.at[slot]).wait()
    rw = rw_ref[...]
    z = rw[:, 2:3] * buf[0] + rw[:, 3:4] * buf[1]
    out_ref[...] = _layer_norm(DN_ALPHA * h_ref[...] + z, g_ref[...], b_ref[...])


def _combine(pos_flat, h, route_out, g, b, y, tm):
    n, d = h.shape
    return pl.pallas_call(
        _combine_body,
        grid_spec=pltpu.PrefetchScalarGridSpec(
            num_scalar_prefetch=1, grid=(n // tm,),
            in_specs=[pl.BlockSpec((tm, d), lambda i, pos: (i, 0)),
                      pl.BlockSpec((tm, LANES), lambda i, pos: (i, 0)),
                      pl.BlockSpec((1, d), lambda i, pos: (0, 0)),
                      pl.BlockSpec((1, d), lambda i, pos: (0, 0)),
                      pl.BlockSpec(memory_space=pl.ANY)],
            out_specs=pl.BlockSpec((tm, d), lambda i, pos: (i, 0)),
            scratch_shapes=[pltpu.VMEM((2, tm, d), jnp.float32),
                            pltpu.SemaphoreType.DMA((2,))]),
        out_shape=jax.ShapeDtypeStruct((n, d), jnp.float32),
        compiler_params=pltpu.CompilerParams(dimension_semantics=("arbitrary",),
                                             vmem_limit_bytes=VMEM_LIMIT),
        name="moe_combine",
    )(pos_flat, h, route_out, g.reshape(1, -1), b.reshape(1, -1), y)


FFN_TM = 256
PROJ_TM = 640
TAIL_TM = 320


def _moe_and_norm(h, lg, w_gate, w_up, w_down, ln2_g, ln2_b):
    n = h.shape[0]
    route_out, cnt = _route(lg, 640)
    eid = route_out[:, 0:2].astype(jnp.int32)
    rank = route_out[:, 4:6].astype(jnp.int32)
    counts = cnt[0, LG0:LG0 + N_EXPERTS].astype(jnp.int32)
    tiles_per = (counts + FFN_TM - 1) // FFN_TM
    tile_end = jnp.cumsum(tiles_per)
    row_start = (tile_end - tiles_per) * FFN_TM
    pos_flat = (row_start[eid] + rank).reshape(-1)
    nt = (2 * n) // FFN_TM + N_EXPERTS
    n_used = tile_end[-1]
    tile_ids = jnp.minimum(jnp.arange(nt, dtype=jnp.int32), n_used - 1)
    tile_expert = jnp.searchsorted(tile_end, tile_ids, side='right').astype(jnp.int32)
    xs = _dispatch(pos_flat, h, nt * FFN_TM)
    y = _ffn(tile_expert, n_used.reshape(1).astype(jnp.int32), xs, w_gate, w_up, w_down, FFN_TM)
    return _combine(pos_flat, h, route_out, ln2_g, ln2_b, y, 128)


def kernel(x_prompt, x_sample, cache_kv, cache_win, state_hgrn, page_table, w_in, b_in, w_cmp1, w_cmp2, cmp_pe,
           hgrn_gamma, hgrn_norm, w_pa, w_pb, w_out, ln1_g, ln1_b, w_rg, b_rg, w_re, b_re, w_gate, w_up, w_down,
           ln2_g, ln2_b):
    n_p = x_prompt.shape[0] * x_prompt.shape[1]
    n_s = x_sample.shape[0] * x_sample.shape[1]
    x_all = jnp.concatenate([x_prompt.reshape(n_p, D_MODEL), x_sample.reshape(n_s, D_MODEL)], axis=0)
    x_bf = x_all.astype(jnp.bfloat16)
    bsz, seq = x_prompt.shape[:2]
    n_seq, t_new = x_sample.shape[:2]
    n_pages = page_table.shape[1]
    past_len = n_pages * PAGE_SIZE
    w = w_in[0]
    b = b_in[0]

    def seg(lo, hi, dtypes, tn, name, pad_to=None):
        ws, bs = w[:, lo:hi], b[lo:hi]
        if pad_to is not None:
            ws = jnp.pad(ws, ((0, 0), (0, pad_to - (hi - lo))))
            bs = jnp.pad(bs, (0, pad_to - (hi - lo)))
        return _proj(x_bf, ws.astype(jnp.bfloat16), bs, dtypes, PROJ_TM, tn, name)

    q_all, = seg(OFF_Q, OFF_KV, [jnp.bfloat16], 512, "proj_q")
    kv32, kvbf = seg(OFF_KV, OFF_NG, [jnp.float32, jnp.bfloat16], 512, "proj_kv")
    ng, = seg(OFF_NG, OFF_H4, [jnp.float32], LANES, "proj_ng", pad_to=LANES)
    h4, = seg(OFF_H4, OFF_MG, [jnp.float32], 512, "proj_h4")
    mg, = seg(OFF_MG, PROJ_COLS, [jnp.float32], 512, "proj_mg")

    kv_p = kv32[:n_p].reshape(1, bsz, seq, 6, N_KV, HEAD_DIM)
    kv_s = kv32[n_p:].reshape(1, n_seq, t_new, 6, N_KV, HEAD_DIM)
    new_kv_prompt = kv_p[:, :, :, :KV_SLOTS]
    new_kv_sample = kv_s[:, :, :, :KV_SLOTS]
    new_win_prompt = kv_p[:, :, seq - min(WINDOW, seq):, 4:6]
    win_all = jnp.concatenate([cache_win, kv_s[:, :, :, 4:6].astype(cache_win.dtype)], axis=2)
    new_win_sample = win_all[:, :, win_all.shape[2] - min(WINDOW, win_all.shape[2]):]

    w1p = _cmp_w1_pairs(w_cmp1[0])
    n_cmp_p = (seq - CMP_LEN) // CMP_STRIDE + 1
    uv_p = _cmp_uv_rows(kv32, n_p, 2048, w1p)
    kcvc_p = _cmp_finish(uv_p, bsz, seq // CMP_STRIDE, w_cmp1[0], w_cmp2[0], cmp_pe[0], "cmp_finish_prompt")
    q_t = q_all[:n_p].reshape(bsz, seq, N_KV, HPG, HEAD_DIM).transpose(0, 2, 3, 1, 4)
    kv_t = kvbf[:n_p].reshape(bsz, seq, 6, N_KV, HEAD_DIM).transpose(2, 0, 3, 1, 4)
    ng_t = ng[:n_p, :NG_COLS].reshape(bsz, seq, 3, N_KV, HPG).transpose(0, 3, 1, 2, 4).reshape(bsz, N_KV, seq, 3 * HPG)
    o_nsa_p = _nsa_prompt(q_t, kcvc_p, kv_t, ng_t, n_cmp_p).transpose(0, 3, 1, 2, 4).reshape(n_p, NSA_WIDTH)

    cache = cache_kv[0].reshape(cache_kv.shape[1], PAGE_SIZE, 2 * HALF_COLS)
    n_cmp_s = (past_len + t_new - CMP_LEN) // CMP_STRIDE + 1
    uv_s = _cmp_uv_sample(page_table, cache, w1p)
    kcvc_s = _cmp_finish(uv_s, n_seq, uv_s.shape[1], w_cmp1[0], w_cmp2[0], cmp_pe[0], "cmp_finish_sample")
    rows_s = HPG * t_new
    q_g = q_all[n_p:].reshape(n_seq, t_new, N_KV, HPG, HEAD_DIM).transpose(0, 2, 3, 1, 4).reshape(
        n_seq, N_KV, rows_s, HEAD_DIM)
    new_kv = kvbf[n_p:].reshape(n_seq, t_new, 6, N_KV, HEAD_DIM).transpose(0, 2, 3, 1, 4)
    new_kv = jnp.pad(new_kv, ((0, 0), (0, 0), (0, 0), (0, 8 - t_new), (0, 0)))
    ng_g = ng[n_p:, :NG_COLS].reshape(n_seq, t_new, 3, N_KV, HPG).transpose(0, 3, 4, 1, 2).reshape(
        n_seq, N_KV, rows_s, 3)
    cw = cache_win[0].reshape(n_seq, cache_win.shape[2], HALF_COLS)
    o_nsa_s = _nsa_sample(page_table, cache, q_g, kcvc_s, new_kv, cw, ng_g, n_cmp_s, t_new)
    o_nsa_s = o_nsa_s.reshape(n_seq, N_KV, HPG, t_new, HEAD_DIM).transpose(0, 3, 1, 2, 4).reshape(n_s, NSA_WIDTH)

    lower = jnp.cumsum(jax.nn.softmax(hgrn_gamma.astype(jnp.float32), axis=0), axis=0)
    lb = lower[0].reshape(HG_HEADS, HG_DK)
    zero_state = jnp.zeros((bsz, HG_HEADS, HG_DV, HG_DK), jnp.float32)
    o_hg_p, st_p = _hgrn(h4, 0, bsz, seq, lb, hgrn_norm[0], zero_state, HG_CHUNK, HG_CHUNK, 512, "hgrn_prompt")
    h4_s = jnp.pad(h4[n_p:].reshape(n_seq, t_new, -1), ((0, 0), (0, 8 - t_new), (0, 0))).reshape(n_seq * 8, -1)
    o_hg_s, st_s = _hgrn(h4_s, 0, n_seq, 8, lb, hgrn_norm[0], state_hgrn[0].transpose(0, 1, 3, 2),
                         8, t_new, 8, "hgrn_sample")
    o_hg_s = o_hg_s.reshape(n_seq, 8, HG_WIDTH)[:, :t_new].reshape(n_s, HG_WIDTH)
    new_state_prompt = st_p.transpose(0, 1, 3, 2)[None].astype(x_prompt.dtype)
    new_state_sample = st_s.transpose(0, 1, 3, 2)[None].astype(state_hgrn.dtype)

    o_nsa = jnp.concatenate([o_nsa_p, o_nsa_s], axis=0)
    o_hg = jnp.concatenate([o_hg_p, o_hg_s], axis=0)
    u = _mix(o_nsa, o_hg, mg, w_pa[0].astype(jnp.bfloat16), w_pb[0].astype(jnp.bfloat16), TAIL_TM)
    wr = jnp.zeros((D_MODEL, LANES), jnp.float32).at[:, :N_GROUPS].set(w_rg[0]).at[:, LG0:LG0 + N_EXPERTS].set(w_re[0])
    br = jnp.zeros((1, LANES), jnp.float32).at[0, :N_GROUPS].set(b_rg[0]).at[0, LG0:LG0 + N_EXPERTS].set(b_re[0])
    wr_hi, wr_lo = _split_bf16(wr)
    h, lg = _ln1(u, x_all, w_out[0].astype(jnp.bfloat16), ln1_g[0], ln1_b[0], wr_hi, wr_lo, br, TAIL_TM)
    out = _moe_and_norm(h, lg, w_gate[0], w_up[0], w_down[0], ln2_g[0], ln2_b[0])
    y_prompt = out[:n_p].reshape(bsz, seq, D_MODEL)
    y_sample = out[n_p:].reshape(n_seq, t_new, D_MODEL)
    return (y_prompt, y_sample, new_kv_prompt, new_kv_sample, new_win_prompt, new_win_sample,
            new_state_prompt, new_state_sample)
```

```python
import functools

import numpy as np
import jax
import jax.numpy as jnp
from jax import lax
from jax.experimental import pallas as pl
from jax.experimental.pallas import tpu as pltpu

D_MODEL = 2048
N_HEADS = 16
N_KV = 4
HPG = N_HEADS // N_KV
HEAD_DIM = 64
NSA_WIDTH = N_HEADS * HEAD_DIM
CMP_LEN = 32
CMP_STRIDE = 16
CMP_HID = 128
SEL_BLOCK = 64
N_SEL = 16
WINDOW = 512
QBLK = 128
KV_SLOTS = 4
PAGE_SIZE = 128
HG_HEADS = 8
HG_DK = 128
HG_DV = 128
HG_WIDTH = HG_HEADS * HG_DV
HG_CHUNK = 32
N_GROUPS = 4
EXP_PER_GROUP = 8
N_EXPERTS = N_GROUPS * EXP_PER_GROUP
D_EXPERT = 512
DEPTH = 1
DN_ALPHA = (2.0 * DEPTH) ** 0.25
LN_EPS = 1e-5
SCALE = HEAD_DIM ** -0.5
NEG = -1e30
BIG = 1e30

LANES = 128
KV_COLS = 6 * N_KV * HEAD_DIM
NG_COLS = 3 * N_HEADS
OFF_Q = 0
OFF_KV = NSA_WIDTH
OFF_NG = OFF_KV + KV_COLS
OFF_H4 = OFF_NG + NG_COLS
OFF_MG = OFF_H4 + 2 * HG_HEADS * HG_DK + 2 * HG_WIDTH
PROJ_COLS = OFF_MG + 2 * D_MODEL

VMEM_LIMIT = 56 * 1024 * 1024


def _sigmoid(x):
    return 1.0 / (1.0 + jnp.exp(-x))


def _proj_body(x_ref, w_ref, b_ref, *out_refs):
    acc = jnp.dot(x_ref[...], w_ref[...], preferred_element_type=jnp.float32) + b_ref[...]
    for o_ref in out_refs:
        o_ref[...] = acc.astype(o_ref.dtype)


def _proj(x_bf, w_bf, b, out_dtypes, tm, tn, name):
    m, k = x_bf.shape
    n = w_bf.shape[1]
    assert m % tm == 0 and n % tn == 0
    return pl.pallas_call(
        _proj_body,
        grid=(m // tm, n // tn),
        in_specs=[pl.BlockSpec((tm, k), lambda i, j: (i, 0)),
                  pl.BlockSpec((k, tn), lambda i, j: (0, j)),
                  pl.BlockSpec((1, tn), lambda i, j: (0, j))],
        out_specs=[pl.BlockSpec((tm, tn), lambda i, j: (i, j)) for _ in out_dtypes],
        out_shape=[jax.ShapeDtypeStruct((m, n), dt) for dt in out_dtypes],
        compiler_params=pltpu.CompilerParams(dimension_semantics=("parallel", "parallel"),
                                             vmem_limit_bytes=VMEM_LIMIT),
        name=name,
    )(x_bf, w_bf, b.reshape(1, n))


def _hgrn_body(hq_ref, hf_ref, hi_ref, hg_ref, lb_ref, nrm_ref, s0_ref, o_ref, sfin_ref, st_scr,
               *, chunk, n_valid, n_chunks):
    @pl.when(pl.program_id(2) == 0)
    def _():
        st_scr[...] = s0_ref[0, 0]

    rows = chunk * n_chunks
    hq = hq_ref[...]
    lb = lb_ref[0]
    q = hq * _sigmoid(hq)
    f = lb + (1.0 - lb) * _sigmoid(hf_ref[...])
    k = 1.0 - f
    lc = jnp.log(f)
    v = hi_ref[...]
    row_in_chunk = lax.broadcasted_iota(jnp.int32, (rows, HG_DK), 0) % chunk
    if n_valid < chunk:
        live = row_in_chunk < n_valid
        q = jnp.where(live, q, 0.0)
        k = jnp.where(live, k, 0.0)
        v = jnp.where(live, v, 0.0)
        lc = jnp.where(live, lc, 0.0)
    bc = lc
    step = 1
    while step < chunk:
        bc = bc + jnp.where(row_in_chunk >= step, pltpu.roll(bc, step, axis=0), 0.0)
        step *= 2
    bc3 = bc.reshape(n_chunks, chunk, HG_DK)
    bl3 = bc3[:, chunk - 1:chunk, :]
    q3 = q.reshape(n_chunks, chunk, HG_DK)
    k3 = k.reshape(n_chunks, chunk, HG_DK)
    v3 = v.reshape(n_chunks, chunk, HG_DV).astype(jnp.bfloat16)
    qe3 = (q3 * jnp.exp(bc3)).astype(jnp.bfloat16)
    ke3 = (k3 * jnp.exp(-bc3)).astype(jnp.bfloat16)
    kd3 = (k3 * jnp.exp(bl3 - bc3)).astype(jnp.bfloat16)
    dec3 = jnp.exp(bl3)
    att = jnp.einsum('ctd,csd->cts', qe3, ke3, preferred_element_type=jnp.float32)
    tri = (lax.broadcasted_iota(jnp.int32, (chunk, chunk), 0)
           >= lax.broadcasted_iota(jnp.int32, (chunk, chunk), 1))
    att = jnp.where(tri[None], att, 0.0).astype(jnp.bfloat16)
    o_intra = jnp.einsum('cts,cse->cte', att, v3, preferred_element_type=jnp.float32)

    st = st_scr[...]
    outs = []
    for c in range(n_chunks):
        o_c = lax.dot_general(qe3[c], st.astype(jnp.bfloat16), (((1,), (1,)), ((), ())),
                              preferred_element_type=jnp.float32)
        outs.append(o_c + o_intra[c])
        upd = lax.dot_general(v3[c], kd3[c], (((0,), (0,)), ((), ())),
                              preferred_element_type=jnp.float32)
        st = st * dec3[c] + upd
    st_scr[...] = st
    sfin_ref[0, 0] = st
    o = jnp.concatenate(outs, axis=0) if n_chunks > 1 else outs[0]
    o = o * lax.rsqrt(jnp.mean(o * o, axis=-1, keepdims=True) + LN_EPS) * nrm_ref[0]
    hg = hg_ref[...]
    o_ref[...] = (o * (hg * _sigmoid(hg))).astype(o_ref.dtype)


def _hgrn(h4, row0, n_seq, t_seq, lb, nrm, s0_t, chunk, n_valid, block_rows, name):
    assert t_seq % block_rows == 0 and block_rows % chunk == 0 and row0 % block_rows == 0
    nb = t_seq // block_rows
    rb0 = row0 // block_rows

    def col_spec(seg):
        return pl.BlockSpec((block_rows, HG_DK),
                            lambda b, h, i, seg=seg: (rb0 + b * nb + i, seg * HG_HEADS + h))

    body = functools.partial(_hgrn_body, chunk=chunk, n_valid=n_valid, n_chunks=block_rows // chunk)
    return pl.pallas_call(
        body,
        grid=(n_seq, HG_HEADS, nb),
        in_specs=[col_spec(0), col_spec(1), col_spec(2), col_spec(3),
                  pl.BlockSpec((1, 1, HG_DK), lambda b, h, i: (h, 0, 0)),
                  pl.BlockSpec((1, 1, HG_DV), lambda b, h, i: (h, 0, 0)),
                  pl.BlockSpec((1, 1, HG_DV, HG_DK), lambda b, h, i: (b, h, 0, 0))],
        out_specs=[pl.BlockSpec((block_rows, HG_DV), lambda b, h, i: (b * nb + i, h)),
                   pl.BlockSpec((1, 1, HG_DV, HG_DK), lambda b, h, i: (b, h, 0, 0))],
        out_shape=[jax.ShapeDtypeStruct((n_seq * t_seq, HG_WIDTH), jnp.bfloat16),
                   jax.ShapeDtypeStruct((n_seq, HG_HEADS, HG_DV, HG_DK), jnp.float32)],
        scratch_shapes=[pltpu.VMEM((HG_DV, HG_DK), jnp.float32)],
        compiler_params=pltpu.CompilerParams(
            dimension_semantics=("parallel", "parallel", "arbitrary"), vmem_limit_bytes=VMEM_LIMIT),
        name=name,
    )(h4, h4, h4, h4, lb.reshape(HG_HEADS, 1, HG_DK), nrm.reshape(HG_HEADS, 1, HG_DV), s0_t)


UV_COLS = 2 * N_KV * 2 * CMP_HID


def _cmp_uv_body(x0_ref, x1_ref, x2_ref, x3_ref, w_ref, uv_ref):
    n = uv_ref.shape[0]
    for cgp, x_ref in enumerate((x0_ref, x1_ref, x2_ref, x3_ref)):
        c = cgp // 2
        acc = jnp.zeros((n, 4 * CMP_HID), jnp.float32)
        for j in range(CMP_STRIDE):
            xj = x_ref[pl.ds(j, n, stride=CMP_STRIDE), :]
            acc = acc + jnp.dot(xj.astype(jnp.bfloat16), w_ref[c, j], preferred_element_type=jnp.float32)
        uv_ref[:, cgp * 512:(cgp + 1) * 512] = acc


def _cmp_w1_pairs(w_cmp1_l):
    w = jnp.concatenate([w_cmp1_l[:, :CMP_STRIDE], w_cmp1_l[:, CMP_STRIDE:]], axis=-1)
    z = jnp.zeros_like(w)
    top = jnp.concatenate([w, z], axis=-1)
    bot = jnp.concatenate([z, w], axis=-1)
    return jnp.concatenate([top, bot], axis=-2).astype(jnp.bfloat16)


def _cmp_uv_rows(kv, n_rows, rows_per_step, w1p):
    n = rows_per_step // CMP_STRIDE
    return pl.pallas_call(
        _cmp_uv_body,
        grid=(n_rows // rows_per_step,),
        in_specs=[pl.BlockSpec((rows_per_step, LANES), lambda i, cb=cb: (i, cb)) for cb in range(4)]
        + [pl.BlockSpec(w1p.shape, lambda i: (0, 0, 0, 0))],
        out_specs=pl.BlockSpec((n, UV_COLS), lambda i: (i, 0)),
        out_shape=jax.ShapeDtypeStruct((n_rows // CMP_STRIDE, UV_COLS), jnp.float32),
        compiler_params=pltpu.CompilerParams(dimension_semantics=("parallel",),
                                             vmem_limit_bytes=VMEM_LIMIT),
        name="cmp_uv_prompt",
    )(kv, kv, kv, kv, w1p)


def _gelu_tanh(x):
    return 0.5 * x * (1.0 + jnp.tanh(0.7978845608028654 * (x + 0.044715 * x * x * x)))


def _cmp_finish_body(uv_ref, pe_ref, w1_ref, w2_ref, out_ref):
    n_ch = uv_ref.shape[1]
    for c in range(2):
        pe_term = jnp.dot(pe_ref[c], w1_ref[c], preferred_element_type=jnp.float32)[0:1, :]
        for g in range(N_KV):
            base = (c * N_KV + g) * 2 * CMP_HID
            u = uv_ref[0, :, base:base + CMP_HID]
            v = uv_ref[0, :, base + CMP_HID:base + 2 * CMP_HID]
            pre = u + pltpu.roll(v, n_ch - 1, axis=0) + pe_term
            hid = _gelu_tanh(pre).astype(jnp.bfloat16)
            res = jnp.dot(hid, w2_ref[c], preferred_element_type=jnp.float32).astype(out_ref.dtype)
            if out_ref.shape[1] == 2:
                out_ref[0, c, :, g * HEAD_DIM:(g + 1) * HEAD_DIM] = res
            else:
                out_ref[0, c * N_KV + g] = res


def _cmp_finish(uv, n_seq, n_ch, w_cmp1_l, w_cmp2_l, cmp_pe_l, merged, name):
    pe = jnp.zeros((2, 8, CMP_LEN * HEAD_DIM), jnp.float32).at[:, 0].set(cmp_pe_l.reshape(2, -1))
    out_dims = (2, n_ch, N_KV * HEAD_DIM) if merged else (2 * N_KV, n_ch, HEAD_DIM)
    return pl.pallas_call(
        _cmp_finish_body,
        grid=(n_seq,),
        in_specs=[pl.BlockSpec((1, n_ch, UV_COLS), lambda b: (b, 0, 0)),
                  pl.BlockSpec((2, 8, CMP_LEN * HEAD_DIM), lambda b: (0, 0, 0)),
                  pl.BlockSpec((2, CMP_LEN * HEAD_DIM, CMP_HID), lambda b: (0, 0, 0)),
                  pl.BlockSpec((2, CMP_HID, HEAD_DIM), lambda b: (0, 0, 0))],
        out_specs=pl.BlockSpec((1,) + out_dims, lambda b: (b, 0, 0, 0)),
        out_shape=jax.ShapeDtypeStruct((n_seq,) + out_dims, jnp.bfloat16),
        compiler_params=pltpu.CompilerParams(dimension_semantics=("parallel",),
                                             vmem_limit_bytes=VMEM_LIMIT),
        name=name,
    )(uv.reshape(n_seq, n_ch, UV_COLS), pe.astype(jnp.bfloat16),
      w_cmp1_l.reshape(2, CMP_LEN * HEAD_DIM, CMP_HID).astype(jnp.bfloat16),
      w_cmp2_l.astype(jnp.bfloat16))


SLC_TK = 512
WIN_KEYS = WINDOW + QBLK


def _masked_softmax(s, valid):
    s = jnp.where(valid, s, NEG)
    m = jnp.max(s, axis=-1, keepdims=True)
    e = jnp.where(valid, jnp.exp(s - m), 0.0)
    return e / jnp.maximum(jnp.sum(e, axis=-1, keepdims=True), 1e-30)


def _split_bf16(x):
    hi = x.astype(jnp.bfloat16)
    return hi, (x - hi.astype(jnp.float32)).astype(jnp.bfloat16)


SUBLANES = 8


def _top_blocks_t(imp_t):
    nb, nq = imp_t.shape
    groups = [imp_t[SUBLANES * v:SUBLANES * (v + 1), :] for v in range(nb // SUBLANES)]
    sub = lax.broadcasted_iota(jnp.int32, (SUBLANES, nq), 0)
    beaten_by = [jnp.zeros((SUBLANES, nq), jnp.float32) for _ in groups]
    for i in range(nb):
        row = jnp.broadcast_to(imp_t[i:i + 1, :], (SUBLANES, nq))
        for v, gv in enumerate(groups):
            if v > i // SUBLANES:
                beats = row >= gv
            elif v < i // SUBLANES:
                beats = row > gv
            else:
                beats = (row > gv) | ((row == gv) & (sub > i % SUBLANES))
            beaten_by[v] = beaten_by[v] + jnp.where(beats, 1.0, 0.0)
    return jnp.concatenate([jnp.where(c < N_SEL, 1.0, 0.0) for c in beaten_by], axis=0)


def _nsa_prompt_body(q_ref, kc_ref, vc_ref, ks_ref, vs_ref, kw_ref, vw_ref, ng_ref, ov_ref, ex_ref, sl_ref,
                     o_ref, *, n_cmp):
    i = pl.program_id(2)
    nq = QBLK
    rows = HPG * nq
    q2 = (q_ref[0, 0] * SCALE).reshape(rows, HEAD_DIM)
    slopes = sl_ref[0]
    qpos = i * nq + lax.broadcasted_iota(jnp.int32, (nq, 1), 0)
    nt = (((1,), (1,)), ((), ()))

    n_ch = kc_ref.shape[2]
    s = lax.dot_general(q2, kc_ref[0, 0], nt, preferred_element_type=jnp.float32).reshape(HPG, nq, n_ch)
    cidx = lax.broadcasted_iota(jnp.int32, (1, n_ch), 1)
    dist_c = qpos - (cidx * CMP_STRIDE + (CMP_LEN - 1))
    valid_c = ((dist_c >= 0) & (cidx < n_cmp))[None]
    p = _masked_softmax(s - slopes * dist_c.astype(jnp.float32)[None], valid_c)
    o_cmp = jnp.dot(p.reshape(rows, n_ch).astype(jnp.bfloat16), vc_ref[0, 0],
                    preferred_element_type=jnp.float32).reshape(HPG, nq, HEAD_DIM)

    p_hi, p_lo = _split_bf16(p[0] + p[1] + p[2] + p[3])
    imp_t = (lax.dot_general(ov_ref[...], p_hi, nt, preferred_element_type=jnp.float32)
             + lax.dot_general(ov_ref[...], p_lo, nt, preferred_element_type=jnp.float32))
    nsb = imp_t.shape[0]
    blk = lax.broadcasted_iota(jnp.int32, (nsb, 1), 0)
    cur = (i * nq + lax.broadcasted_iota(jnp.int32, (1, nq), 1)) // SEL_BLOCK
    imp_t = jnp.where(blk > cur, NEG, imp_t)
    imp_t = jnp.where((blk == cur) | (blk == 0), BIG, imp_t)
    sel_t = _top_blocks_t(imp_t).astype(jnp.bfloat16)

    def slc_step(kt, carry):
        m, l, acc = carry
        k0 = pl.multiple_of(kt * SLC_TK, SLC_TK)
        s = lax.dot_general(q2, ks_ref[0, 0, 0, pl.ds(k0, SLC_TK), :], nt,
                            preferred_element_type=jnp.float32).reshape(HPG, nq, SLC_TK)
        dist = qpos - (k0 + lax.broadcasted_iota(jnp.int32, (1, SLC_TK), 1))
        picked = lax.dot_general(sel_t, ex_ref[:, pl.ds(k0, SLC_TK)], (((0,), (0,)), ((), ())),
                                 preferred_element_type=jnp.float32)
        valid = ((dist >= 0) & (picked > 0.5))[None]
        s = jnp.where(valid, s - slopes * dist.astype(jnp.float32)[None], NEG)
        m_new = jnp.maximum(m, jnp.max(s, axis=-1, keepdims=True))
        a = jnp.exp(m - m_new)
        e = jnp.exp(s - m_new)
        l = a * l + jnp.sum(e, axis=-1, keepdims=True)
        pv = jnp.dot(e.reshape(rows, SLC_TK).astype(jnp.bfloat16), vs_ref[0, 0, 0, pl.ds(k0, SLC_TK), :],
                     preferred_element_type=jnp.float32).reshape(HPG, nq, HEAD_DIM)
        return m_new, l, a * acc + pv

    init = (jnp.full((HPG, nq, 1), NEG, jnp.float32), jnp.zeros((HPG, nq, 1), jnp.float32),
            jnp.zeros((HPG, nq, HEAD_DIM), jnp.float32))
    n_kt = (i * nq) // SLC_TK + 1
    _, l, acc = lax.fori_loop(0, n_kt, slc_step, init)
    o_slc = acc / jnp.maximum(l, 1e-30)

    w0 = pl.multiple_of(jnp.maximum(i - WINDOW // QBLK, 0) * nq, nq)
    s = lax.dot_general(q2, kw_ref[0, 0, 0, pl.ds(w0, WIN_KEYS), :], nt,
                        preferred_element_type=jnp.float32).reshape(HPG, nq, WIN_KEYS)
    dist = qpos - (w0 + lax.broadcasted_iota(jnp.int32, (1, WIN_KEYS), 1))
    valid = ((dist >= 0) & (dist < WINDOW))[None]
    pw = _masked_softmax(s - slopes * dist.astype(jnp.float32)[None], valid)
    o_win = jnp.dot(pw.reshape(rows, WIN_KEYS).astype(jnp.bfloat16), vw_ref[0, 0, 0, pl.ds(w0, WIN_KEYS), :],
                    preferred_element_type=jnp.float32).reshape(HPG, nq, HEAD_DIM)

    gates = _sigmoid(ng_ref[0, 0])
    for h in range(HPG):
        o_ref[0, 0, h] = (gates[:, h:h + 1] * o_cmp[h] + gates[:, HPG + h:HPG + h + 1] * o_slc[h]
                          + gates[:, 2 * HPG + h:2 * HPG + h + 1] * o_win[h]).astype(o_ref.dtype)


def _alibi_slopes():
    return jnp.asarray(2.0 ** (-8.0 * np.arange(1, N_HEADS + 1) / N_HEADS), jnp.float32).reshape(N_KV, HPG, 1, 1)


def _nsa_prompt(q_t, kcvc, kv_t, ng_t, n_cmp):
    bsz, _, _, t, _ = q_t.shape
    n_ch = kcvc.shape[2]
    nsb = t // SEL_BLOCK
    st = np.arange(n_ch) * CMP_STRIDE
    bs = np.arange(nsb) * SEL_BLOCK
    overlap = ((st[:, None] <= bs[None, :] + SEL_BLOCK - 1) & (st[:, None] + CMP_LEN - 1 >= bs[None, :])
               & (np.arange(n_ch)[:, None] < n_cmp)).astype(np.float32)
    expand = (np.arange(t)[None, :] // SEL_BLOCK == np.arange(nsb)[:, None]).astype(np.float32)

    def kv_spec(slot):
        return pl.BlockSpec((1, 1, 1, t, HEAD_DIM), lambda b, g, i, slot=slot: (slot, b, g, 0, 0))

    return pl.pallas_call(
        functools.partial(_nsa_prompt_body, n_cmp=n_cmp),
        grid=(bsz, N_KV, t // QBLK),
        in_specs=[pl.BlockSpec((1, 1, HPG, QBLK, HEAD_DIM), lambda b, g, i: (b, g, 0, i, 0)),
                  pl.BlockSpec((1, 1, n_ch, HEAD_DIM), lambda b, g, i: (b, g, 0, 0)),
                  pl.BlockSpec((1, 1, n_ch, HEAD_DIM), lambda b, g, i: (b, N_KV + g, 0, 0)),
                  kv_spec(2), kv_spec(3), kv_spec(4), kv_spec(5),
                  pl.BlockSpec((1, 1, QBLK, 3 * HPG), lambda b, g, i: (b, g, i, 0)),
                  pl.BlockSpec((nsb, n_ch), lambda b, g, i: (0, 0)),
                  pl.BlockSpec((nsb, t), lambda b, g, i: (0, 0)),
                  pl.BlockSpec((1, HPG, 1, 1), lambda b, g, i: (g, 0, 0, 0))],
        out_specs=pl.BlockSpec((1, 1, HPG, QBLK, HEAD_DIM), lambda b, g, i: (b, g, 0, i, 0)),
        out_shape=jax.ShapeDtypeStruct(q_t.shape, jnp.bfloat16),
        compiler_params=pltpu.CompilerParams(
            dimension_semantics=("parallel", "parallel", "arbitrary"), vmem_limit_bytes=VMEM_LIMIT),
        name="nsa_prompt",
    )(q_t, kcvc, kcvc, kv_t, kv_t, kv_t, kv_t, ng_t,
      jnp.asarray(overlap.T, jnp.bfloat16), jnp.asarray(expand, jnp.bfloat16), _alibi_slopes())


HALF_ROWS = 2 * N_KV * HEAD_DIM
GD = N_KV * HEAD_DIM


def _page_copy(pt_ref, cache_ref, buf, sem, flat_page, slot, k, row0):
    n_pages = pt_ref.shape[1]
    phys = pt_ref[flat_page // n_pages, flat_page % n_pages]
    return pltpu.make_async_copy(cache_ref.at[phys, pl.ds(row0, HALF_ROWS), :], buf.at[slot, k], sem.at[slot])


def _fetch_pages(pt_ref, cache_ref, buf, sem, row0):
    b = pl.program_id(0)
    per_step = buf.shape[1]
    slot = b % 2

    def start_all(step, sl):
        def one(k, carry):
            _page_copy(pt_ref, cache_ref, buf, sem, step * per_step + k, sl, k, row0).start()
            return carry
        lax.fori_loop(0, per_step, one, 0)

    @pl.when(b == 0)
    def _():
        start_all(0, 0)

    @pl.when(b + 1 < pl.num_programs(0))
    def _():
        start_all(b + 1, 1 - slot)

    def wait_one(k, carry):
        _page_copy(pt_ref, cache_ref, buf, sem, b * per_step + k, slot, k, row0).wait()
        return carry
    lax.fori_loop(0, per_step, wait_one, 0)
    return slot


def _cmp_uv_sample_body(pt_ref, cache_ref, w_ref, uv_ref, buf, sem, xt):
    slot = _fetch_pages(pt_ref, cache_ref, buf, sem, 0)
    per_step = buf.shape[1]
    n_cb = HALF_ROWS // LANES

    def to_rows(k, carry):
        for cb in range(n_cb):
            xt[k, cb] = buf[slot, k, cb * LANES:(cb + 1) * LANES, :].T
        return carry
    lax.fori_loop(0, per_step, to_rows, 0)

    per_page = PAGE_SIZE // CMP_STRIDE
    n = per_step * per_page
    for cgp in range(n_cb):
        c = cgp // 2
        acc = jnp.zeros((n, 4 * CMP_HID), jnp.float32)
        for j in range(CMP_STRIDE):
            xj = xt[:, cgp, pl.ds(j, per_page, stride=CMP_STRIDE), :]
            acc = acc + jnp.dot(xj.reshape(n, LANES).astype(jnp.bfloat16), w_ref[c, j],
                                preferred_element_type=jnp.float32)
        uv_ref[0, :, cgp * 512:(cgp + 1) * 512] = acc


UV_PAGES = 32


def _cmp_uv_sample(page_table, cache_t, w1p):
    n_seq, n_pages = page_table.shape
    assert n_pages % UV_PAGES == 0
    steps = n_pages // UV_PAGES
    per_page = PAGE_SIZE // CMP_STRIDE
    return pl.pallas_call(
        _cmp_uv_sample_body,
        grid_spec=pltpu.PrefetchScalarGridSpec(
            num_scalar_prefetch=1, grid=(n_seq * steps,),
            in_specs=[pl.BlockSpec(memory_space=pl.ANY),
                      pl.BlockSpec(w1p.shape, lambda b, pt: (0, 0, 0, 0))],
            out_specs=pl.BlockSpec((1, UV_PAGES * per_page, UV_COLS), lambda b, pt: (b // steps, b % steps, 0)),
            scratch_shapes=[pltpu.VMEM((2, UV_PAGES, HALF_ROWS, PAGE_SIZE), jnp.float32),
                            pltpu.SemaphoreType.DMA((2,)),
                            pltpu.VMEM((UV_PAGES, HALF_ROWS // LANES, PAGE_SIZE, LANES), jnp.float32)]),
        out_shape=jax.ShapeDtypeStruct((n_seq, n_pages * per_page, UV_COLS), jnp.float32),
        compiler_params=pltpu.CompilerParams(dimension_semantics=("arbitrary",),
                                             vmem_limit_bytes=VMEM_LIMIT),
        name="cmp_uv_sample",
    )(page_table, cache_t, w1p)


def _group_diag(x, rows_per_group):
    grp = lax.broadcasted_iota(jnp.int32, (x.shape[0], 1), 0) // rows_per_group
    out = jnp.zeros((x.shape[0], HEAD_DIM), x.dtype)
    for g in range(N_KV):
        out = out + jnp.where(grp == g, x[:, g * HEAD_DIM:(g + 1) * HEAD_DIM], 0.0)
    return out


def _nsa_sample_body(pt_ref, cache_ref, q_ref, kcvc_ref, new_ref, win_ref, ng_ref, ov_ref, same_q_ref, sl_ref,
                     o_ref, buf, sem, s_scr, e_scr, *, n_cmp, t_new):
    slot = _fetch_pages(pt_ref, cache_ref, buf, sem, HALF_ROWS)
    n_pages = buf.shape[1]
    past_len = n_pages * PAGE_SIZE
    rows = q_ref.shape[1]
    rpg = rows // N_KV
    nt = (((1,), (1,)), ((), ()))
    qbd = q_ref[0] * SCALE
    slope = sl_ref[...]
    row = lax.broadcasted_iota(jnp.int32, (rows, 1), 0)
    qtok = row % t_new
    qpos = past_len + qtok
    new_rows = new_ref.shape[2]
    new_idx = lax.broadcasted_iota(jnp.int32, (1, new_rows), 1)
    new_valid = (new_idx <= qtok) & (new_idx < t_new)
    new_bias = slope * (qtok - new_idx).astype(jnp.float32)

    n_ch = kcvc_ref.shape[2]
    s = lax.dot_general(qbd, kcvc_ref[0, 0], nt, preferred_element_type=jnp.float32)
    cidx = lax.broadcasted_iota(jnp.int32, (1, n_ch), 1)
    dist_c = qpos - (cidx * CMP_STRIDE + (CMP_LEN - 1))
    p = _masked_softmax(s - slope * dist_c.astype(jnp.float32), (dist_c >= 0) & (cidx < n_cmp))
    o_cmp = _group_diag(jnp.dot(p.astype(jnp.bfloat16), kcvc_ref[0, 1], preferred_element_type=jnp.float32), rpg)

    p_hi, p_lo = _split_bf16(p)
    psum = (jnp.dot(same_q_ref[...], p_hi, preferred_element_type=jnp.float32)
            + jnp.dot(same_q_ref[...], p_lo, preferred_element_type=jnp.float32))
    ps_hi, ps_lo = _split_bf16(psum)
    imp = (jnp.dot(ps_hi, ov_ref[...], preferred_element_type=jnp.float32)
           + jnp.dot(ps_lo, ov_ref[...], preferred_element_type=jnp.float32))
    n_lanes = imp.shape[-1]
    blk = lax.broadcasted_iota(jnp.int32, (1, n_lanes), 1)
    cur = qpos // SEL_BLOCK
    imp = jnp.where(blk > cur, -jnp.inf, imp)
    imp = jnp.where((blk == cur) | (blk == 0), BIG, imp)
    chosen = []
    for _ in range(N_SEL):
        m = jnp.max(imp, axis=-1, keepdims=True)
        idx = jnp.min(jnp.where(imp == m, blk, n_lanes), axis=-1, keepdims=True)
        chosen.append(idx)
        imp = jnp.where(blk == idx, -jnp.inf, imp)

    def score(pg, carry):
        s_scr[pg] = jnp.dot(qbd, buf[slot, pg, 0:GD, :].astype(jnp.bfloat16), preferred_element_type=jnp.float32)
        return carry
    lax.fori_loop(0, n_pages, score, 0)
    kpos = (lax.broadcasted_iota(jnp.int32, (n_pages, 1, PAGE_SIZE), 0) * PAGE_SIZE
            + lax.broadcasted_iota(jnp.int32, (n_pages, 1, PAGE_SIZE), 2))
    kblk = kpos // SEL_BLOCK
    picked = kblk == chosen[0][None]
    for idx in chosen[1:]:
        picked = picked | (kblk == idx[None])
    s_all = jnp.where(picked, s_scr[...] - slope[None] * (qpos[None] - kpos).astype(jnp.float32), NEG)
    s_new = lax.dot_general(qbd, new_ref[0, 2], nt, preferred_element_type=jnp.float32)
    s_new = jnp.where(new_valid, s_new - new_bias, NEG)
    m = jnp.maximum(jnp.max(jnp.max(s_all, axis=0), axis=-1, keepdims=True),
                    jnp.max(s_new, axis=-1, keepdims=True))
    e_all = jnp.exp(s_all - m[None])
    e_new = jnp.where(new_valid, jnp.exp(s_new - m), 0.0)
    l = (jnp.sum(jnp.sum(e_all, axis=0), axis=-1, keepdims=True) + jnp.sum(e_new, axis=-1, keepdims=True))
    e_scr[...] = e_all.astype(jnp.bfloat16)

    def weigh(pg, acc):
        v_t = buf[slot, pg, GD:2 * GD, :].astype(jnp.bfloat16)
        return acc + lax.dot_general(e_scr[pg], v_t, nt, preferred_element_type=jnp.float32)
    acc = lax.fori_loop(0, n_pages, weigh, jnp.zeros((rows, GD), jnp.float32))
    acc = acc + jnp.dot(e_new.astype(jnp.bfloat16), new_ref[0, 3], preferred_element_type=jnp.float32)
    o_slc = _group_diag(acc, rpg) / jnp.maximum(l, 1e-30)

    w_buf = win_ref.shape[2]
    wdist = qpos - (past_len - w_buf + lax.broadcasted_iota(jnp.int32, (1, w_buf), 1))
    valid_w = (wdist >= 0) & (wdist < WINDOW)
    s_w = jnp.dot(qbd, win_ref[0, 0:GD, :].astype(jnp.bfloat16), preferred_element_type=jnp.float32)
    s_w = jnp.where(valid_w, s_w - slope * wdist.astype(jnp.float32), NEG)
    s_wn = lax.dot_general(qbd, new_ref[0, 4], nt, preferred_element_type=jnp.float32)
    s_wn = jnp.where(new_valid, s_wn - new_bias, NEG)
    m = jnp.maximum(jnp.max(s_w, axis=-1, keepdims=True), jnp.max(s_wn, axis=-1, keepdims=True))
    e_w = jnp.where(valid_w, jnp.exp(s_w - m), 0.0)
    e_wn = jnp.where(new_valid, jnp.exp(s_wn - m), 0.0)
    acc = (lax.dot_general(e_w.astype(jnp.bfloat16), win_ref[0, GD:2 * GD, :].astype(jnp.bfloat16), nt,
                           preferred_element_type=jnp.float32)
           + jnp.dot(e_wn.astype(jnp.bfloat16), new_ref[0, 5], preferred_element_type=jnp.float32))
    l = jnp.sum(e_w, axis=-1, keepdims=True) + jnp.sum(e_wn, axis=-1, keepdims=True)
    o_win = _group_diag(acc, rpg) / jnp.maximum(l, 1e-30)

    gates = _sigmoid(ng_ref[0])
    o_ref[0] = (gates[:, 0:1] * o_cmp + gates[:, 1:2] * o_slc + gates[:, 2:3] * o_win).astype(o_ref.dtype)


def _nsa_sample(page_table, cache_t, q_bd, kcvc, new_kv, win_t, ng_r, n_cmp, t_new):
    n_seq, n_pages = page_table.shape
    past_len = n_pages * PAGE_SIZE
    n_ch = kcvc.shape[2]
    rows = q_bd.shape[1]
    rpg = rows // N_KV
    nsb = -(-(past_len + t_new) // SEL_BLOCK)
    n_lanes = -(-nsb // LANES) * LANES
    st = np.arange(n_ch) * CMP_STRIDE
    bs = np.arange(n_lanes) * SEL_BLOCK
    overlap = ((st[:, None] <= bs[None, :] + SEL_BLOCK - 1) & (st[:, None] + CMP_LEN - 1 >= bs[None, :])
               & (np.arange(n_ch)[:, None] < n_cmp) & (np.arange(n_lanes)[None, :] < nsb)).astype(np.float32)
    r = np.arange(rows)
    same_q = ((r[:, None] // rpg == r[None, :] // rpg) & (r[:, None] % t_new == r[None, :] % t_new)
              ).astype(np.float32)
    slopes = jnp.repeat(_alibi_slopes().reshape(N_HEADS), t_new).reshape(rows, 1)
    w_buf = win_t.shape[2]
    return pl.pallas_call(
        functools.partial(_nsa_sample_body, n_cmp=n_cmp, t_new=t_new),
        grid_spec=pltpu.PrefetchScalarGridSpec(
            num_scalar_prefetch=1, grid=(n_seq,),
            in_specs=[pl.BlockSpec(memory_space=pl.ANY),
                      pl.BlockSpec((1, rows, GD), lambda b, pt: (b, 0, 0)),
                      pl.BlockSpec((1, 2, n_ch, GD), lambda b, pt: (b, 0, 0, 0)),
                      pl.BlockSpec((1, 6, new_kv.shape[2], GD), lambda b, pt: (b, 0, 0, 0)),
                      pl.BlockSpec((1, HALF_ROWS, w_buf), lambda b, pt: (b, 0, 0)),
                      pl.BlockSpec((1, rows, 3), lambda b, pt: (b, 0, 0)),
                      pl.BlockSpec((n_ch, n_lanes), lambda b, pt: (0, 0)),
                      pl.BlockSpec((rows, rows), lambda b, pt: (0, 0)),
                      pl.BlockSpec((rows, 1), lambda b, pt: (0, 0))],
            out_specs=pl.BlockSpec((1, rows, HEAD_DIM), lambda b, pt: (b, 0, 0)),
            scratch_shapes=[pltpu.VMEM((2, n_pages, HALF_ROWS, PAGE_SIZE), jnp.float32),
                            pltpu.SemaphoreType.DMA((2,)),
                            pltpu.VMEM((n_pages, rows, PAGE_SIZE), jnp.float32),
                            pltpu.VMEM((n_pages, rows, PAGE_SIZE), jnp.bfloat16)]),
        out_shape=jax.ShapeDtypeStruct((n_seq, rows, HEAD_DIM), jnp.bfloat16),
        compiler_params=pltpu.CompilerParams(dimension_semantics=("arbitrary",),
                                             vmem_limit_bytes=VMEM_LIMIT),
        name="nsa_sample",
    )(page_table, cache_t, q_bd, kcvc, new_kv, win_t, ng_r,
      jnp.asarray(overlap, jnp.bfloat16), jnp.asarray(same_q, jnp.bfloat16), slopes)


def _mix_body(a_ref, b_ref, mga_ref, mgb_ref, wpa_ref, wpb_ref, u_ref):
    a = jnp.dot(a_ref[...], wpa_ref[...], preferred_element_type=jnp.float32)
    b = jnp.dot(b_ref[...], wpb_ref[...], preferred_element_type=jnp.float32)
    u_ref[...] = (_sigmoid(mga_ref[...]) * a + _sigmoid(mgb_ref[...]) * b).astype(u_ref.dtype)


def _mix(o_nsa, o_hg, mg, w_pa_bf, w_pb_bf, tm):
    n = o_nsa.shape[0]
    const = lambda i: (0, 0)
    return pl.pallas_call(
        _mix_body,
        grid=(n // tm,),
        in_specs=[pl.BlockSpec((tm, NSA_WIDTH), lambda i: (i, 0)),
                  pl.BlockSpec((tm, HG_WIDTH), lambda i: (i, 0)),
                  pl.BlockSpec((tm, D_MODEL), lambda i: (i, 0)),
                  pl.BlockSpec((tm, D_MODEL), lambda i: (i, 1)),
                  pl.BlockSpec((NSA_WIDTH, D_MODEL), const, pipeline_mode=pl.Buffered(1)),
                  pl.BlockSpec((HG_WIDTH, D_MODEL), const, pipeline_mode=pl.Buffered(1))],
        out_specs=pl.BlockSpec((tm, D_MODEL), lambda i: (i, 0)),
        out_shape=jax.ShapeDtypeStruct((n, D_MODEL), jnp.bfloat16),
        compiler_params=pltpu.CompilerParams(dimension_semantics=("parallel",),
                                             vmem_limit_bytes=VMEM_LIMIT),
        name="tail_mix",
    )(o_nsa, o_hg, mg, mg, w_pa_bf, w_pb_bf)


def _layer_norm(z, g, b):
    mu = jnp.mean(z, axis=-1, keepdims=True)
    zc = z - mu
    var = jnp.mean(zc * zc, axis=-1, keepdims=True)
    return zc * lax.rsqrt(var + LN_EPS) * g + b


CH = D_MODEL // LANES


def _store_chunked(ref, val):
    tm = val.shape[0]
    for k in range(CH):
        ref[pl.ds(k, tm, stride=CH), :] = val[:, k * LANES:(k + 1) * LANES]


def _load_chunked(ref, tm, lead=()):
    return jnp.concatenate([ref[lead + (pl.ds(k, tm, stride=CH), slice(None))] for k in range(CH)], axis=1)


def _ln1_body(u_ref, x_ref, wout_ref, g_ref, b_ref, wr_hi_ref, wr_lo_ref, br_ref, h_ref, lg_ref):
    y = jnp.dot(u_ref[...], wout_ref[...], preferred_element_type=jnp.float32)
    h = _layer_norm(DN_ALPHA * x_ref[...] + y, g_ref[...], b_ref[...])
    _store_chunked(h_ref, h)
    h_hi = h.astype(jnp.bfloat16)
    h_lo = (h - h_hi.astype(jnp.float32)).astype(jnp.bfloat16)
    lg = jnp.dot(h_hi, wr_hi_ref[...], preferred_element_type=jnp.float32)
    lg = lg + jnp.dot(h_lo, wr_hi_ref[...], preferred_element_type=jnp.float32)
    lg = lg + jnp.dot(h_hi, wr_lo_ref[...], preferred_element_type=jnp.float32)
    lg_ref[...] = lg + br_ref[...]


def _ln1(u, x_all, w_out_bf, g, b, wr_hi, wr_lo, br, tm):
    n = u.shape[0]
    const = lambda i: (0, 0)
    return pl.pallas_call(
        _ln1_body,
        grid=(n // tm,),
        in_specs=[pl.BlockSpec((tm, D_MODEL), lambda i: (i, 0)),
                  pl.BlockSpec((tm, D_MODEL), lambda i: (i, 0)),
                  pl.BlockSpec((D_MODEL, D_MODEL), const, pipeline_mode=pl.Buffered(1)),
                  pl.BlockSpec((1, D_MODEL), const),
                  pl.BlockSpec((1, D_MODEL), const),
                  pl.BlockSpec((D_MODEL, LANES), const),
                  pl.BlockSpec((D_MODEL, LANES), const),
                  pl.BlockSpec((1, LANES), const)],
        out_specs=[pl.BlockSpec((tm * CH, LANES), lambda i: (i, 0)),
                   pl.BlockSpec((tm, LANES), lambda i: (i, 0))],
        out_shape=[jax.ShapeDtypeStruct((n * CH, LANES), jnp.float32),
                   jax.ShapeDtypeStruct((n, LANES), jnp.float32)],
        compiler_params=pltpu.CompilerParams(dimension_semantics=("parallel",),
                                             vmem_limit_bytes=VMEM_LIMIT),
        name="tail_ln1",
    )(u, x_all, w_out_bf, g.reshape(1, -1), b.reshape(1, -1), wr_hi, wr_lo, br)


LG0 = N_GROUPS


def _route_body(lg_ref, tri_ref, out_ref, cnt_ref, carry_scr):
    @pl.when(pl.program_id(0) == 0)
    def _():
        carry_scr[...] = jnp.zeros_like(carry_scr)

    lg = lg_ref[...]
    tm = lg.shape[0]
    lane = lax.broadcasted_iota(jnp.int32, lg.shape, 1)
    is_g = lane < N_GROUPS
    gl = jnp.where(is_g, lg, NEG)
    gmax = jnp.max(gl, axis=-1, keepdims=True)
    grp = jnp.min(jnp.where(gl == gmax, lane, LANES), axis=-1, keepdims=True)
    g_w = 1.0 / jnp.sum(jnp.where(is_g, jnp.exp(lg - gmax), 0.0), axis=-1, keepdims=True)
    lo = LG0 + grp * EXP_PER_GROUP
    el = jnp.where((lane >= lo) & (lane < lo + EXP_PER_GROUP), lg, NEG)
    v1 = jnp.max(el, axis=-1, keepdims=True)
    i1 = jnp.min(jnp.where(el == v1, lane, LANES), axis=-1, keepdims=True)
    el2 = jnp.where(lane == i1, NEG, el)
    v2 = jnp.max(el2, axis=-1, keepdims=True)
    i2 = jnp.min(jnp.where(el2 == v2, lane, LANES), axis=-1, keepdims=True)
    e21 = jnp.exp(v2 - v1)
    w1 = g_w / (1.0 + e21)
    w2 = g_w * e21 / (1.0 + e21)
    hit1 = lane == i1
    hit2 = lane == i2
    onehot = jnp.where(hit1 | hit2, 1.0, 0.0)
    incl = jnp.dot(tri_ref[...], onehot.astype(jnp.bfloat16), preferred_element_type=jnp.float32)
    carry = carry_scr[...]
    before = incl - onehot + carry
    r1 = jnp.sum(jnp.where(hit1, before, 0.0), axis=-1, keepdims=True)
    r2 = jnp.sum(jnp.where(hit2, before, 0.0), axis=-1, keepdims=True)
    carry = carry + incl[tm - 1:tm, :]
    carry_scr[...] = carry
    cnt_ref[...] = carry
    out = jnp.where(lane == 0, (i1 - LG0).astype(jnp.float32), 0.0)
    out = jnp.where(lane == 1, (i2 - LG0).astype(jnp.float32), out)
    out = jnp.where(lane == 2, w1, out)
    out = jnp.where(lane == 3, w2, out)
    out = jnp.where(lane == 4, r1, out)
    out = jnp.where(lane == 5, r2, out)
    out_ref[...] = out


def _route(lg, tm):
    n = lg.shape[0]
    tri = (np.arange(tm)[:, None] >= np.arange(tm)[None, :]).astype(np.float32)
    return pl.pallas_call(
        _route_body,
        grid=(n // tm,),
        in_specs=[pl.BlockSpec((tm, LANES), lambda i: (i, 0)),
                  pl.BlockSpec((tm, tm), lambda i: (0, 0))],
        out_specs=[pl.BlockSpec((tm, LANES), lambda i: (i, 0)),
                   pl.BlockSpec((1, LANES), lambda i: (0, 0))],
        out_shape=[jax.ShapeDtypeStruct((n, LANES), jnp.float32),
                   jax.ShapeDtypeStruct((1, LANES), jnp.float32)],
        scratch_shapes=[pltpu.VMEM((1, LANES), jnp.float32)],
        compiler_params=pltpu.CompilerParams(dimension_semantics=("arbitrary",)),
        name="moe_route",
    )(lg, jnp.asarray(tri, jnp.bfloat16))


DMA_WINDOW = 64


def _dispatch_body(pos_ref, h_ref, xs_in_ref, xs_ref, sem):
    del xs_in_ref
    n = h_ref.shape[0] // CH

    def row_copy(tok, slot):
        src = pl.multiple_of(tok * CH, CH)
        dst = pl.multiple_of(pos_ref[2 * tok + slot] * CH, CH)
        return pltpu.make_async_copy(h_ref.at[pl.ds(src, CH)], xs_ref.at[pl.ds(dst, CH)], sem)

    def step(tok, carry):
        @pl.when(tok >= DMA_WINDOW)
        def _():
            row_copy(tok - DMA_WINDOW, 0).wait()
            row_copy(tok - DMA_WINDOW, 1).wait()
        row_copy(tok, 0).start()
        row_copy(tok, 1).start()
        return carry

    lax.fori_loop(0, n, step, 0)

    def drain(tok, carry):
        row_copy(tok, 0).wait()
        row_copy(tok, 1).wait()
        return carry

    lax.fori_loop(n - DMA_WINDOW, n, drain, 0)


def _dispatch(pos_flat, h_c, n_slots):
    zeros = jnp.zeros((n_slots * CH, LANES), h_c.dtype)
    return pl.pallas_call(
        _dispatch_body,
        grid_spec=pltpu.PrefetchScalarGridSpec(
            num_scalar_prefetch=1, grid=(1,),
            in_specs=[pl.BlockSpec(memory_space=pl.ANY), pl.BlockSpec(memory_space=pl.ANY)],
            out_specs=pl.BlockSpec(memory_space=pl.ANY),
            scratch_shapes=[pltpu.SemaphoreType.DMA(())]),
        out_shape=jax.ShapeDtypeStruct(zeros.shape, h_c.dtype),
        input_output_aliases={2: 0},
        compiler_params=pltpu.CompilerParams(dimension_semantics=("arbitrary",)),
        name="moe_dispatch",
    )(pos_flat, h_c, zeros)


def _ffn_body(te_ref, nu_ref, x_ref, wg_ref, wu_ref, wd_ref, y_ref, wg_bf, wu_bf, wd_bf):
    i = pl.program_id(0)
    prev = te_ref[jnp.maximum(i - 1, 0)]

    @pl.when((i == 0) | (te_ref[i] != prev))
    def _():
        wg_bf[...] = wg_ref[0].astype(jnp.bfloat16)
        wu_bf[...] = wu_ref[0].astype(jnp.bfloat16)
        wd_bf[...] = wd_ref[0].astype(jnp.bfloat16)

    @pl.when(i < nu_ref[0])
    def _():
        x = _load_chunked(x_ref, x_ref.shape[0] // CH).astype(jnp.bfloat16)
        g = jnp.dot(x, wg_bf[...], preferred_element_type=jnp.float32)
        u = jnp.dot(x, wu_bf[...], preferred_element_type=jnp.float32)
        hid = (g * _sigmoid(g) * u).astype(jnp.bfloat16)
        _store_chunked(y_ref, jnp.dot(hid, wd_bf[...], preferred_element_type=jnp.float32))

    @pl.when(i >= nu_ref[0])
    def _():
        y_ref[...] = jnp.zeros_like(y_ref)


def _ffn(tile_expert, n_used, xs_c, w_gate, w_up, w_down, tm):
    d = D_MODEL
    nt = xs_c.shape[0] // (tm * CH)
    return pl.pallas_call(
        _ffn_body,
        grid_spec=pltpu.PrefetchScalarGridSpec(
            num_scalar_prefetch=2, grid=(nt,),
            in_specs=[pl.BlockSpec((tm * CH, LANES), lambda i, te, nu: (i, 0)),
                      pl.BlockSpec((1, d, D_EXPERT), lambda i, te, nu: (te[i], 0, 0)),
                      pl.BlockSpec((1, d, D_EXPERT), lambda i, te, nu: (te[i], 0, 0)),
                      pl.BlockSpec((1, D_EXPERT, d), lambda i, te, nu: (te[i], 0, 0))],
            out_specs=pl.BlockSpec((tm * CH, LANES), lambda i, te, nu: (i, 0)),
            scratch_shapes=[pltpu.VMEM((d, D_EXPERT), jnp.bfloat16),
                            pltpu.VMEM((d, D_EXPERT), jnp.bfloat16),
                            pltpu.VMEM((D_EXPERT, d), jnp.bfloat16)]),
        out_shape=jax.ShapeDtypeStruct(xs_c.shape, jnp.float32),
        compiler_params=pltpu.CompilerParams(dimension_semantics=("arbitrary",),
                                             vmem_limit_bytes=VMEM_LIMIT),
        name="moe_ffn",
    )(tile_expert, n_used, xs_c, w_gate, w_up, w_down)


def _combine_body(pos_ref, h_ref, rw_ref, g_ref, b_ref, y_ref, out_ref, buf, sem):
    tm = out_ref.shape[0]
    base = pl.program_id(0) * tm

    def issue(t, carry):
        for slot in range(2):
            src = pl.multiple_of(pos_ref[2 * (base + t) + slot] * CH, CH)
            dst = pl.multiple_of(t * CH, CH)
            pltpu.make_async_copy(y_ref.at[pl.ds(src, CH)], buf.at[slot, pl.ds(dst, CH)], sem.at[slot]).start()
        return carry

    lax.fori_loop(0, tm, issue, 0)
    for slot in range(2):
        pltpu.make_async_copy(y_ref.at[pl.ds(0, tm * CH)], buf.at[slot], sem.at[slot]).wait()
    rw = rw_ref[...]
    z = rw[:, 2:3] * _load_chunked(buf, tm, (0,)) + rw[:, 3:4] * _load_chunked(buf, tm, (1,))
    out_ref[...] = _layer_norm(DN_ALPHA * _load_chunked(h_ref, tm) + z, g_ref[...], b_ref[...])


def _combine(pos_flat, h_c, route_out, g, b, y_c, tm):
    n, d = h_c.shape[0] // CH, D_MODEL
    return pl.pallas_call(
        _combine_body,
        grid_spec=pltpu.PrefetchScalarGridSpec(
            num_scalar_prefetch=1, grid=(n // tm,),
            in_specs=[pl.BlockSpec((tm * CH, LANES), lambda i, pos: (i, 0)),
                      pl.BlockSpec((tm, LANES), lambda i, pos: (i, 0)),
                      pl.BlockSpec((1, d), lambda i, pos: (0, 0)),
                      pl.BlockSpec((1, d), lambda i, pos: (0, 0)),
                      pl.BlockSpec(memory_space=pl.ANY)],
            out_specs=pl.BlockSpec((tm, d), lambda i, pos: (i, 0)),
            scratch_shapes=[pltpu.VMEM((2, tm * CH, LANES), jnp.float32),
                            pltpu.SemaphoreType.DMA((2,))]),
        out_shape=jax.ShapeDtypeStruct((n, d), jnp.float32),
        compiler_params=pltpu.CompilerParams(dimension_semantics=("arbitrary",),
                                             vmem_limit_bytes=VMEM_LIMIT),
        name="moe_combine",
    )(pos_flat, h_c, route_out, g.reshape(1, -1), b.reshape(1, -1), y_c)


FFN_TM = 256
PROJ_TM = 640
TAIL_TM = 320


def _moe_and_norm(h_c, lg, w_gate, w_up, w_down, ln2_g, ln2_b):
    n = lg.shape[0]
    route_out, cnt = _route(lg, 640)
    eid = route_out[:, 0:2].astype(jnp.int32)
    rank = route_out[:, 4:6].astype(jnp.int32)
    counts = cnt[0, LG0:LG0 + N_EXPERTS].astype(jnp.int32)
    tiles_per = (counts + FFN_TM - 1) // FFN_TM
    tile_end = jnp.cumsum(tiles_per)
    row_start = (tile_end - tiles_per) * FFN_TM
    pos_flat = (row_start[eid] + rank).reshape(-1)
    nt = (2 * n) // FFN_TM + N_EXPERTS
    n_used = tile_end[-1]
    tile_ids = jnp.minimum(jnp.arange(nt, dtype=jnp.int32), n_used - 1)
    tile_expert = jnp.sum((tile_end[None, :] <= tile_ids[:, None]).astype(jnp.int32), axis=1)
    xs_c = _dispatch(pos_flat, h_c, nt * FFN_TM)
    y_c = _ffn(tile_expert, n_used.reshape(1).astype(jnp.int32), xs_c, w_gate, w_up, w_down, FFN_TM)
    return _combine(pos_flat, h_c, route_out, ln2_g, ln2_b, y_c, 128)


def kernel(x_prompt, x_sample, cache_kv, cache_win, state_hgrn, page_table, w_in, b_in, w_cmp1, w_cmp2, cmp_pe,
           hgrn_gamma, hgrn_norm, w_pa, w_pb, w_out, ln1_g, ln1_b, w_rg, b_rg, w_re, b_re, w_gate, w_up, w_down,
           ln2_g, ln2_b):
    n_p = x_prompt.shape[0] * x_prompt.shape[1]
    n_s = x_sample.shape[0] * x_sample.shape[1]
    x_all = jnp.concatenate([x_prompt.reshape(n_p, D_MODEL), x_sample.reshape(n_s, D_MODEL)], axis=0)
    x_bf = x_all.astype(jnp.bfloat16)
    bsz, seq = x_prompt.shape[:2]
    n_seq, t_new = x_sample.shape[:2]
    n_pages = page_table.shape[1]
    past_len = n_pages * PAGE_SIZE
    w = w_in[0]
    b = b_in[0]

    def seg(lo, hi, dtypes, tn, name, pad_to=None):
        ws, bs = w[:, lo:hi], b[lo:hi]
        if pad_to is not None:
            ws = jnp.pad(ws, ((0, 0), (0, pad_to - (hi - lo))))
            bs = jnp.pad(bs, (0, pad_to - (hi - lo)))
        return _proj(x_bf, ws.astype(jnp.bfloat16), bs, dtypes, PROJ_TM, tn, name)

    q_all, = seg(OFF_Q, OFF_KV, [jnp.bfloat16], 512, "proj_q")
    kv32, kvbf = seg(OFF_KV, OFF_NG, [jnp.float32, jnp.bfloat16], 512, "proj_kv")
    ng, = seg(OFF_NG, OFF_H4, [jnp.float32], LANES, "proj_ng", pad_to=LANES)
    h4, = seg(OFF_H4, OFF_MG, [jnp.float32], 512, "proj_h4")
    mg, = seg(OFF_MG, PROJ_COLS, [jnp.float32], 512, "proj_mg")

    kv_p = kv32[:n_p].reshape(1, bsz, seq, 6, N_KV, HEAD_DIM)
    kv_s = kv32[n_p:].reshape(1, n_seq, t_new, 6, N_KV, HEAD_DIM)
    new_kv_prompt = kv_p[:, :, :, :KV_SLOTS]
    new_kv_sample = kv_s[:, :, :, :KV_SLOTS]
    new_win_prompt = kv_p[:, :, seq - min(WINDOW, seq):, 4:6]
    win_all = jnp.concatenate([cache_win, kv_s[:, :, :, 4:6].astype(cache_win.dtype)], axis=2)
    new_win_sample = win_all[:, :, win_all.shape[2] - min(WINDOW, win_all.shape[2]):]

    w1p = _cmp_w1_pairs(w_cmp1[0])
    n_cmp_p = (seq - CMP_LEN) // CMP_STRIDE + 1
    uv_p = _cmp_uv_rows(kv32, n_p, 2048, w1p)
    kcvc_p = _cmp_finish(uv_p, bsz, seq // CMP_STRIDE, w_cmp1[0], w_cmp2[0], cmp_pe[0], False, "cmp_finish_prompt")
    q_t = q_all[:n_p].reshape(bsz, seq, N_KV, HPG, HEAD_DIM).transpose(0, 2, 3, 1, 4)
    kv_t = kvbf[:n_p].reshape(bsz, seq, 6, N_KV, HEAD_DIM).transpose(2, 0, 3, 1, 4)
    ng_t = ng[:n_p, :NG_COLS].reshape(bsz, seq, 3, N_KV, HPG).transpose(0, 3, 1, 2, 4).reshape(bsz, N_KV, seq, 3 * HPG)
    o_nsa_p = _nsa_prompt(q_t, kcvc_p, kv_t, ng_t, n_cmp_p).transpose(0, 3, 1, 2, 4).reshape(n_p, NSA_WIDTH)

    cache_t = cache_kv[0].transpose(0, 2, 3, 4, 1).reshape(cache_kv.shape[1], 2 * HALF_ROWS, PAGE_SIZE)
    win_t = cache_win[0].transpose(0, 2, 3, 4, 1).reshape(n_seq, HALF_ROWS, cache_win.shape[2])
    n_cmp_s = (past_len + t_new - CMP_LEN) // CMP_STRIDE + 1
    uv_s = _cmp_uv_sample(page_table, cache_t, w1p)
    kcvc_s = _cmp_finish(uv_s, n_seq, uv_s.shape[1], w_cmp1[0], w_cmp2[0], cmp_pe[0], True, "cmp_finish_sample")
    rows_s = N_HEADS * t_new
    q_s = q_all[n_p:].reshape(n_seq, t_new, N_KV, HPG, HEAD_DIM).transpose(0, 2, 3, 1, 4).reshape(
        n_seq, N_KV, HPG * t_new, HEAD_DIM)
    q_bd = jnp.einsum('sgrd,gk->sgrkd', q_s, jnp.eye(N_KV, dtype=q_s.dtype)).reshape(n_seq, rows_s, GD)
    new_kv = kvbf[n_p:].reshape(n_seq, t_new, 6, GD).transpose(0, 2, 1, 3)
    new_kv = jnp.pad(new_kv, ((0, 0), (0, 0), (0, 8 - t_new), (0, 0)))
    ng_r = ng[n_p:, :NG_COLS].reshape(n_seq, t_new, 3, N_KV, HPG).transpose(0, 3, 4, 1, 2).reshape(
        n_seq, rows_s, 3)
    o_nsa_s = _nsa_sample(page_table, cache_t, q_bd, kcvc_s, new_kv, win_t, ng_r, n_cmp_s, t_new)
    o_nsa_s = o_nsa_s.reshape(n_seq, N_KV, HPG, t_new, HEAD_DIM).transpose(0, 3, 1, 2, 4).reshape(n_s, NSA_WIDTH)

    lower = jnp.cumsum(jax.nn.softmax(hgrn_gamma.astype(jnp.float32), axis=0), axis=0)
    lb = lower[0].reshape(HG_HEADS, HG_DK)
    zero_state = jnp.zeros((bsz, HG_HEADS, HG_DV, HG_DK), jnp.float32)
    o_hg_p, st_p = _hgrn(h4, 0, bsz, seq, lb, hgrn_norm[0], zero_state, HG_CHUNK, HG_CHUNK, 512, "hgrn_prompt")
    h4_s = jnp.pad(h4[n_p:].reshape(n_seq, t_new, -1), ((0, 0), (0, 8 - t_new), (0, 0))).reshape(n_seq * 8, -1)
    o_hg_s, st_s = _hgrn(h4_s, 0, n_seq, 8, lb, hgrn_norm[0], state_hgrn[0].transpose(0, 1, 3, 2),
                         8, t_new, 8, "hgrn_sample")
    o_hg_s = o_hg_s.reshape(n_seq, 8, HG_WIDTH)[:, :t_new].reshape(n_s, HG_WIDTH)
    new_state_prompt = st_p.transpose(0, 1, 3, 2)[None].astype(x_prompt.dtype)
    new_state_sample = st_s.transpose(0, 1, 3, 2)[None].astype(state_hgrn.dtype)

    o_nsa = jnp.concatenate([o_nsa_p, o_nsa_s], axis=0)
    o_hg = jnp.concatenate([o_hg_p, o_hg_s], axis=0)
    u = _mix(o_nsa, o_hg, mg, w_pa[0].astype(jnp.bfloat16), w_pb[0].astype(jnp.bfloat16), TAIL_TM)
    wr = jnp.zeros((D_MODEL, LANES), jnp.float32).at[:, :N_GROUPS].set(w_rg[0]).at[:, LG0:LG0 + N_EXPERTS].set(w_re[0])
    br = jnp.zeros((1, LANES), jnp.float32).at[0, :N_GROUPS].set(b_rg[0]).at[0, LG0:LG0 + N_EXPERTS].set(b_re[0])
    wr_hi, wr_lo = _split_bf16(wr)
    h, lg = _ln1(u, x_all, w_out[0].astype(jnp.bfloat16), ln1_g[0], ln1_b[0], wr_hi, wr_lo, br, TAIL_TM)
    out = _moe_and_norm(h, lg, w_gate[0], w_up[0], w_down[0], ln2_g[0], ln2_b[0])
    y_prompt = out[:n_p].reshape(bsz, seq, D_MODEL)
    y_sample = out[n_p:].reshape(n_seq, t_new, D_MODEL)
    return (y_prompt, y_sample, new_kv_prompt, new_kv_sample, new_win_prompt, new_win_sample,
            new_state_prompt, new_state_sample)
```

```python
import functools

import numpy as np
import jax
import jax.numpy as jnp
from jax import lax
from jax.experimental import pallas as pl
from jax.experimental.pallas import tpu as pltpu

D_MODEL = 2048
N_HEADS = 16
N_KV = 4
HPG = N_HEADS // N_KV
HEAD_DIM = 64
NSA_WIDTH = N_HEADS * HEAD_DIM
CMP_LEN = 32
CMP_STRIDE = 16
CMP_HID = 128
SEL_BLOCK = 64
N_SEL = 16
WINDOW = 512
QBLK = 128
KV_SLOTS = 4
PAGE_SIZE = 128
HG_HEADS = 8
HG_DK = 128
HG_DV = 128
HG_WIDTH = HG_HEADS * HG_DV
HG_CHUNK = 32
N_GROUPS = 4
EXP_PER_GROUP = 8
N_EXPERTS = N_GROUPS * EXP_PER_GROUP
D_EXPERT = 512
DEPTH = 1
DN_ALPHA = (2.0 * DEPTH) ** 0.25
LN_EPS = 1e-5
SCALE = HEAD_DIM ** -0.5
NEG = -1e30
BIG = 1e30

LANES = 128
KV_COLS = 6 * N_KV * HEAD_DIM
NG_COLS = 3 * N_HEADS
OFF_Q = 0
OFF_KV = NSA_WIDTH
OFF_NG = OFF_KV + KV_COLS
OFF_H4 = OFF_NG + NG_COLS
OFF_MG = OFF_H4 + 2 * HG_HEADS * HG_DK + 2 * HG_WIDTH
PROJ_COLS = OFF_MG + 2 * D_MODEL

VMEM_LIMIT = 56 * 1024 * 1024


def _sigmoid(x):
    return 1.0 / (1.0 + jnp.exp(-x))


def _proj_body(x_ref, w_ref, b_ref, *out_refs):
    acc = jnp.dot(x_ref[...], w_ref[...], preferred_element_type=jnp.float32) + b_ref[...]
    for o_ref in out_refs:
        if len(o_ref.shape) == 3:
            for k in range(o_ref.shape[0]):
                o_ref[k] = acc[:, k * HEAD_DIM:(k + 1) * HEAD_DIM].astype(o_ref.dtype)
        else:
            o_ref[...] = acc.astype(o_ref.dtype)


def _proj(x_bf, w_bf, b, outs, tm, tn, name):
    m, k = x_bf.shape
    n = w_bf.shape[1]
    assert m % tm == 0 and n % tn == 0
    out_specs, out_shape = [], []
    for dt, per_head in outs:
        if per_head:
            out_specs.append(pl.BlockSpec((tn // HEAD_DIM, tm, HEAD_DIM), lambda i, j: (j, i, 0)))
            out_shape.append(jax.ShapeDtypeStruct((n // HEAD_DIM, m, HEAD_DIM), dt))
        else:
            out_specs.append(pl.BlockSpec((tm, tn), lambda i, j: (i, j)))
            out_shape.append(jax.ShapeDtypeStruct((m, n), dt))
    return pl.pallas_call(
        _proj_body,
        grid=(m // tm, n // tn),
        in_specs=[pl.BlockSpec((tm, k), lambda i, j: (i, 0)),
                  pl.BlockSpec((k, tn), lambda i, j: (0, j)),
                  pl.BlockSpec((1, tn), lambda i, j: (0, j))],
        out_specs=out_specs,
        out_shape=out_shape,
        compiler_params=pltpu.CompilerParams(dimension_semantics=("parallel", "parallel"),
                                             vmem_limit_bytes=VMEM_LIMIT),
        name=name,
    )(x_bf, w_bf, b.reshape(1, n))


def _hgrn_body(hq_ref, hf_ref, hi_ref, hg_ref, lb_ref, nrm_ref, s0_ref, o_ref, sfin_ref, st_scr,
               *, chunk, n_valid, n_chunks):
    @pl.when(pl.program_id(2) == 0)
    def _():
        st_scr[...] = s0_ref[0, 0]

    rows = chunk * n_chunks
    hq = hq_ref[...]
    lb = lb_ref[0]
    q = hq * _sigmoid(hq)
    f = lb + (1.0 - lb) * _sigmoid(hf_ref[...])
    k = 1.0 - f
    lc = jnp.log(f)
    v = hi_ref[...]
    row_in_chunk = lax.broadcasted_iota(jnp.int32, (rows, HG_DK), 0) % chunk
    if n_valid < chunk:
        live = row_in_chunk < n_valid
        q = jnp.where(live, q, 0.0)
        k = jnp.where(live, k, 0.0)
        v = jnp.where(live, v, 0.0)
        lc = jnp.where(live, lc, 0.0)
    bc = lc
    step = 1
    while step < chunk:
        bc = bc + jnp.where(row_in_chunk >= step, pltpu.roll(bc, step, axis=0), 0.0)
        step *= 2
    bc3 = bc.reshape(n_chunks, chunk, HG_DK)
    bl3 = bc3[:, chunk - 1:chunk, :]
    q3 = q.reshape(n_chunks, chunk, HG_DK)
    k3 = k.reshape(n_chunks, chunk, HG_DK)
    v3 = v.reshape(n_chunks, chunk, HG_DV).astype(jnp.bfloat16)
    qe3 = (q3 * jnp.exp(bc3)).astype(jnp.bfloat16)
    ke3 = (k3 * jnp.exp(-bc3)).astype(jnp.bfloat16)
    kd3 = (k3 * jnp.exp(bl3 - bc3)).astype(jnp.bfloat16)
    dec3 = jnp.exp(bl3)
    att = jnp.einsum('ctd,csd->cts', qe3, ke3, preferred_element_type=jnp.float32)
    tri = (lax.broadcasted_iota(jnp.int32, (chunk, chunk), 0)
           >= lax.broadcasted_iota(jnp.int32, (chunk, chunk), 1))
    att = jnp.where(tri[None], att, 0.0).astype(jnp.bfloat16)
    o_intra = jnp.einsum('cts,cse->cte', att, v3, preferred_element_type=jnp.float32)

    st = st_scr[...]
    outs = []
    for c in range(n_chunks):
        o_c = lax.dot_general(qe3[c], st.astype(jnp.bfloat16), (((1,), (1,)), ((), ())),
                              preferred_element_type=jnp.float32)
        outs.append(o_c + o_intra[c])
        upd = lax.dot_general(v3[c], kd3[c], (((0,), (0,)), ((), ())),
                              preferred_element_type=jnp.float32)
        st = st * dec3[c] + upd
    st_scr[...] = st
    sfin_ref[0, 0] = st
    o = jnp.concatenate(outs, axis=0) if n_chunks > 1 else outs[0]
    o = o * lax.rsqrt(jnp.mean(o * o, axis=-1, keepdims=True) + LN_EPS) * nrm_ref[0]
    hg = hg_ref[...]
    o_ref[...] = (o * (hg * _sigmoid(hg))).astype(o_ref.dtype)


def _hgrn(h4, row0, n_seq, t_seq, lb, nrm, s0_t, chunk, n_valid, block_rows, name):
    assert t_seq % block_rows == 0 and block_rows % chunk == 0 and row0 % block_rows == 0
    nb = t_seq // block_rows
    rb0 = row0 // block_rows

    def col_spec(seg):
        return pl.BlockSpec((block_rows, HG_DK),
                            lambda b, h, i, seg=seg: (rb0 + b * nb + i, seg * HG_HEADS + h))

    body = functools.partial(_hgrn_body, chunk=chunk, n_valid=n_valid, n_chunks=block_rows // chunk)
    return pl.pallas_call(
        body,
        grid=(n_seq, HG_HEADS, nb),
        in_specs=[col_spec(0), col_spec(1), col_spec(2), col_spec(3),
                  pl.BlockSpec((1, 1, HG_DK), lambda b, h, i: (h, 0, 0)),
                  pl.BlockSpec((1, 1, HG_DV), lambda b, h, i: (h, 0, 0)),
                  pl.BlockSpec((1, 1, HG_DV, HG_DK), lambda b, h, i: (b, h, 0, 0))],
        out_specs=[pl.BlockSpec((block_rows, HG_DV), lambda b, h, i: (b * nb + i, h)),
                   pl.BlockSpec((1, 1, HG_DV, HG_DK), lambda b, h, i: (b, h, 0, 0))],
        out_shape=[jax.ShapeDtypeStruct((n_seq * t_seq, HG_WIDTH), jnp.bfloat16),
                   jax.ShapeDtypeStruct((n_seq, HG_HEADS, HG_DV, HG_DK), jnp.float32)],
        scratch_shapes=[pltpu.VMEM((HG_DV, HG_DK), jnp.float32)],
        compiler_params=pltpu.CompilerParams(
            dimension_semantics=("parallel", "parallel", "arbitrary"), vmem_limit_bytes=VMEM_LIMIT),
        name=name,
    )(h4, h4, h4, h4, lb.reshape(HG_HEADS, 1, HG_DK), nrm.reshape(HG_HEADS, 1, HG_DV), s0_t)


UV_COLS = 2 * N_KV * 2 * CMP_HID


def _uv_taps(tap, w_ref, c, n):
    acc = jnp.zeros((n, 4 * CMP_HID), jnp.float32)
    for p in range(CMP_STRIDE // 2):
        x2 = jnp.concatenate([tap(2 * p), tap(2 * p + 1)], axis=-1)
        acc = acc + jnp.dot(x2, w_ref[c, p], preferred_element_type=jnp.float32)
    return acc


def _cmp_uv_body(x0_ref, x1_ref, x2_ref, x3_ref, w_ref, uv_ref):
    n = uv_ref.shape[0]
    for cgp, x_ref in enumerate((x0_ref, x1_ref, x2_ref, x3_ref)):
        tap = lambda j, x_ref=x_ref: x_ref[pl.ds(j, n, stride=CMP_STRIDE), :].astype(jnp.bfloat16)
        uv_ref[:, cgp * 512:(cgp + 1) * 512] = _uv_taps(tap, w_ref, cgp // 2, n)


def _cmp_w1_pairs(w_cmp1_l):
    w = jnp.concatenate([w_cmp1_l[:, :CMP_STRIDE], w_cmp1_l[:, CMP_STRIDE:]], axis=-1)
    z = jnp.zeros_like(w)
    top = jnp.concatenate([w, z], axis=-1)
    bot = jnp.concatenate([z, w], axis=-1)
    per_tap = jnp.concatenate([top, bot], axis=-2)
    return per_tap.reshape(2, CMP_STRIDE // 2, 2 * LANES, 4 * CMP_HID).astype(jnp.bfloat16)


def _cmp_uv_rows(kv, n_rows, rows_per_step, w1p):
    n = rows_per_step // CMP_STRIDE
    return pl.pallas_call(
        _cmp_uv_body,
        grid=(n_rows // rows_per_step,),
        in_specs=[pl.BlockSpec((rows_per_step, LANES), lambda i, cb=cb: (i, cb)) for cb in range(4)]
        + [pl.BlockSpec(w1p.shape, lambda i: (0, 0, 0, 0))],
        out_specs=pl.BlockSpec((n, UV_COLS), lambda i: (i, 0)),
        out_shape=jax.ShapeDtypeStruct((n_rows // CMP_STRIDE, UV_COLS), jnp.float32),
        compiler_params=pltpu.CompilerParams(dimension_semantics=("parallel",),
                                             vmem_limit_bytes=VMEM_LIMIT),
        name="cmp_uv_prompt",
    )(kv, kv, kv, kv, w1p)


def _gelu_tanh(x):
    return 0.5 * x * (1.0 + jnp.tanh(0.7978845608028654 * (x + 0.044715 * x * x * x)))


def _cmp_finish_body(uv_ref, pe_ref, w1_ref, w2_ref, out_ref):
    n_ch = uv_ref.shape[1]
    for c in range(2):
        pe_term = jnp.dot(pe_ref[c], w1_ref[c], preferred_element_type=jnp.float32)[0:1, :]
        for g in range(N_KV):
            base = (c * N_KV + g) * 2 * CMP_HID
            u = uv_ref[0, :, base:base + CMP_HID]
            v = uv_ref[0, :, base + CMP_HID:base + 2 * CMP_HID]
            pre = u + pltpu.roll(v, n_ch - 1, axis=0) + pe_term
            hid = _gelu_tanh(pre).astype(jnp.bfloat16)
            res = jnp.dot(hid, w2_ref[c], preferred_element_type=jnp.float32).astype(out_ref.dtype)
            if out_ref.shape[1] == 2:
                out_ref[0, c, :, g * HEAD_DIM:(g + 1) * HEAD_DIM] = res
            else:
                out_ref[0, c * N_KV + g] = res


def _cmp_finish(uv, n_seq, n_ch, w_cmp1_l, w_cmp2_l, cmp_pe_l, merged, name):
    pe = jnp.zeros((2, 8, CMP_LEN * HEAD_DIM), jnp.float32).at[:, 0].set(cmp_pe_l.reshape(2, -1))
    out_dims = (2, n_ch, N_KV * HEAD_DIM) if merged else (2 * N_KV, n_ch, HEAD_DIM)
    return pl.pallas_call(
        _cmp_finish_body,
        grid=(n_seq,),
        in_specs=[pl.BlockSpec((1, n_ch, UV_COLS), lambda b: (b, 0, 0)),
                  pl.BlockSpec((2, 8, CMP_LEN * HEAD_DIM), lambda b: (0, 0, 0)),
                  pl.BlockSpec((2, CMP_LEN * HEAD_DIM, CMP_HID), lambda b: (0, 0, 0)),
                  pl.BlockSpec((2, CMP_HID, HEAD_DIM), lambda b: (0, 0, 0))],
        out_specs=pl.BlockSpec((1,) + out_dims, lambda b: (b, 0, 0, 0)),
        out_shape=jax.ShapeDtypeStruct((n_seq,) + out_dims, jnp.bfloat16),
        compiler_params=pltpu.CompilerParams(dimension_semantics=("parallel",),
                                             vmem_limit_bytes=VMEM_LIMIT),
        name=name,
    )(uv.reshape(n_seq, n_ch, UV_COLS), pe.astype(jnp.bfloat16),
      w_cmp1_l.reshape(2, CMP_LEN * HEAD_DIM, CMP_HID).astype(jnp.bfloat16),
      w_cmp2_l.astype(jnp.bfloat16))


SLC_TK = 512
WIN_KEYS = WINDOW + QBLK


def _masked_softmax(s, valid):
    s = jnp.where(valid, s, NEG)
    m = jnp.max(s, axis=-1, keepdims=True)
    e = jnp.where(valid, jnp.exp(s - m), 0.0)
    return e / jnp.maximum(jnp.sum(e, axis=-1, keepdims=True), 1e-30)


def _split_bf16(x):
    hi = x.astype(jnp.bfloat16)
    return hi, (x - hi.astype(jnp.float32)).astype(jnp.bfloat16)


SUBLANES = 8


def _top_blocks_t(imp_t):
    nb, nq = imp_t.shape
    groups = [imp_t[SUBLANES * v:SUBLANES * (v + 1), :] for v in range(nb // SUBLANES)]
    sub = lax.broadcasted_iota(jnp.int32, (SUBLANES, nq), 0)
    beaten_by = [jnp.zeros((SUBLANES, nq), jnp.float32) for _ in groups]
    for i in range(nb):
        row = jnp.broadcast_to(imp_t[i:i + 1, :], (SUBLANES, nq))
        for v, gv in enumerate(groups):
            if v > i // SUBLANES:
                beats = row >= gv
            elif v < i // SUBLANES:
                beats = row > gv
            else:
                beats = (row > gv) | ((row == gv) & (sub > i % SUBLANES))
            beaten_by[v] = beaten_by[v] + jnp.where(beats, 1.0, 0.0)
    return jnp.concatenate([jnp.where(c < N_SEL, 1.0, 0.0) for c in beaten_by], axis=0)


def _nsa_prompt_body(q_ref, kc_ref, vc_ref, ks_ref, vs_ref, kw_ref, vw_ref, ng_ref, ov_ref, ex_ref, sl_ref,
                     o_ref, *, n_cmp):
    i = pl.program_id(2)
    nq = QBLK
    rows = HPG * nq
    q2 = (q_ref[...] * SCALE).reshape(rows, HEAD_DIM)
    slopes = sl_ref[0]
    qpos = i * nq + lax.broadcasted_iota(jnp.int32, (nq, 1), 0)
    nt = (((1,), (1,)), ((), ()))

    n_ch = kc_ref.shape[2]
    s = lax.dot_general(q2, kc_ref[0, 0], nt, preferred_element_type=jnp.float32).reshape(HPG, nq, n_ch)
    cidx = lax.broadcasted_iota(jnp.int32, (1, n_ch), 1)
    dist_c = qpos - (cidx * CMP_STRIDE + (CMP_LEN - 1))
    valid_c = ((dist_c >= 0) & (cidx < n_cmp))[None]
    p = _masked_softmax(s - slopes * dist_c.astype(jnp.float32)[None], valid_c)
    o_cmp = jnp.dot(p.reshape(rows, n_ch).astype(jnp.bfloat16), vc_ref[0, 0],
                    preferred_element_type=jnp.float32).reshape(HPG, nq, HEAD_DIM)

    p_hi, p_lo = _split_bf16(p[0] + p[1] + p[2] + p[3])
    imp_t = (lax.dot_general(ov_ref[...], p_hi, nt, preferred_element_type=jnp.float32)
             + lax.dot_general(ov_ref[...], p_lo, nt, preferred_element_type=jnp.float32))
    nsb = imp_t.shape[0]
    blk = lax.broadcasted_iota(jnp.int32, (nsb, 1), 0)
    cur = (i * nq + lax.broadcasted_iota(jnp.int32, (1, nq), 1)) // SEL_BLOCK
    imp_t = jnp.where(blk > cur, NEG, imp_t)
    imp_t = jnp.where((blk == cur) | (blk == 0), BIG, imp_t)
    sel_t = _top_blocks_t(imp_t).astype(jnp.bfloat16)

    def slc_step(kt, carry):
        m, l, acc = carry
        k0 = pl.multiple_of(kt * SLC_TK, SLC_TK)
        s = lax.dot_general(q2, ks_ref[0, pl.ds(k0, SLC_TK), :], nt,
                            preferred_element_type=jnp.float32).reshape(HPG, nq, SLC_TK)
        dist = qpos - (k0 + lax.broadcasted_iota(jnp.int32, (1, SLC_TK), 1))
        picked = lax.dot_general(sel_t, ex_ref[:, pl.ds(k0, SLC_TK)], (((0,), (0,)), ((), ())),
                                 preferred_element_type=jnp.float32)
        valid = ((dist >= 0) & (picked > 0.5))[None]
        s = jnp.where(valid, s - slopes * dist.astype(jnp.float32)[None], NEG)
        m_new = jnp.maximum(m, jnp.max(s, axis=-1, keepdims=True))
        a = jnp.exp(m - m_new)
        e = jnp.exp(s - m_new)
        l = a * l + jnp.sum(e, axis=-1, keepdims=True)
        pv = jnp.dot(e.reshape(rows, SLC_TK).astype(jnp.bfloat16), vs_ref[0, pl.ds(k0, SLC_TK), :],
                     preferred_element_type=jnp.float32).reshape(HPG, nq, HEAD_DIM)
        return m_new, l, a * acc + pv

    init = (jnp.full((HPG, nq, 1), NEG, jnp.float32), jnp.zeros((HPG, nq, 1), jnp.float32),
            jnp.zeros((HPG, nq, HEAD_DIM), jnp.float32))
    n_kt = (i * nq) // SLC_TK + 1
    _, l, acc = lax.fori_loop(0, n_kt, slc_step, init)
    o_slc = acc / jnp.maximum(l, 1e-30)

    w0 = pl.multiple_of(jnp.maximum(i - WINDOW // QBLK, 0) * nq, nq)
    s = lax.dot_general(q2, kw_ref[0, pl.ds(w0, WIN_KEYS), :], nt,
                        preferred_element_type=jnp.float32).reshape(HPG, nq, WIN_KEYS)
    dist = qpos - (w0 + lax.broadcasted_iota(jnp.int32, (1, WIN_KEYS), 1))
    valid = ((dist >= 0) & (dist < WINDOW))[None]
    pw = _masked_softmax(s - slopes * dist.astype(jnp.float32)[None], valid)
    o_win = jnp.dot(pw.reshape(rows, WIN_KEYS).astype(jnp.bfloat16), vw_ref[0, pl.ds(w0, WIN_KEYS), :],
                    preferred_element_type=jnp.float32).reshape(HPG, nq, HEAD_DIM)

    gates = _sigmoid(ng_ref[0, 0])
    o_ref[...] = jnp.concatenate(
        [gates[:, h:h + 1] * o_cmp[h] + gates[:, HPG + h:HPG + h + 1] * o_slc[h]
         + gates[:, 2 * HPG + h:2 * HPG + h + 1] * o_win[h] for h in range(HPG)], axis=-1).astype(o_ref.dtype)


def _alibi_slopes():
    return jnp.asarray(2.0 ** (-8.0 * np.arange(1, N_HEADS + 1) / N_HEADS), jnp.float32).reshape(N_KV, HPG, 1, 1)


def _nsa_prompt(q_h, kcvc, kv_h, ng_t, n_cmp):
    bsz, _, t, _ = ng_t.shape
    nqb = t // QBLK
    n_ch = kcvc.shape[2]
    nsb = t // SEL_BLOCK
    st = np.arange(n_ch) * CMP_STRIDE
    bs = np.arange(nsb) * SEL_BLOCK
    overlap = ((st[:, None] <= bs[None, :] + SEL_BLOCK - 1) & (st[:, None] + CMP_LEN - 1 >= bs[None, :])
               & (np.arange(n_ch)[:, None] < n_cmp)).astype(np.float32)
    expand = (np.arange(t)[None, :] // SEL_BLOCK == np.arange(nsb)[:, None]).astype(np.float32)

    def kv_spec(slot):
        return pl.BlockSpec((1, t, HEAD_DIM), lambda b, g, i, slot=slot: (slot * N_KV + g, b, 0))

    return pl.pallas_call(
        functools.partial(_nsa_prompt_body, n_cmp=n_cmp),
        grid=(bsz, N_KV, nqb),
        in_specs=[pl.BlockSpec((HPG, QBLK, HEAD_DIM), lambda b, g, i: (g, b * nqb + i, 0)),
                  pl.BlockSpec((1, 1, n_ch, HEAD_DIM), lambda b, g, i: (b, g, 0, 0)),
                  pl.BlockSpec((1, 1, n_ch, HEAD_DIM), lambda b, g, i: (b, N_KV + g, 0, 0)),
                  kv_spec(2), kv_spec(3), kv_spec(4), kv_spec(5),
                  pl.BlockSpec((1, 1, QBLK, 3 * HPG), lambda b, g, i: (b, g, i, 0)),
                  pl.BlockSpec((nsb, n_ch), lambda b, g, i: (0, 0)),
                  pl.BlockSpec((nsb, t), lambda b, g, i: (0, 0)),
                  pl.BlockSpec((1, HPG, 1, 1), lambda b, g, i: (g, 0, 0, 0))],
        out_specs=pl.BlockSpec((QBLK, HPG * HEAD_DIM), lambda b, g, i: (b * nqb + i, g)),
        out_shape=jax.ShapeDtypeStruct((bsz * t, NSA_WIDTH), jnp.bfloat16),
        compiler_params=pltpu.CompilerParams(
            dimension_semantics=("parallel", "parallel", "arbitrary"), vmem_limit_bytes=VMEM_LIMIT),
        name="nsa_prompt",
    )(q_h, kcvc, kcvc, kv_h, kv_h, kv_h, kv_h, ng_t,
      jnp.asarray(overlap.T, jnp.bfloat16), jnp.asarray(expand, jnp.bfloat16), _alibi_slopes())


HALF_ROWS = 2 * N_KV * HEAD_DIM
GD = N_KV * HEAD_DIM


def _fetch_pages(pt_ref, cache_ref, sem, row0, per_step, dst_of):
    b = pl.program_id(0)
    slot = b % 2
    n_pages = pt_ref.shape[1]

    def page_copy(flat_page, sl, k):
        phys = pt_ref[flat_page // n_pages, flat_page % n_pages]
        return pltpu.make_async_copy(cache_ref.at[phys, pl.ds(row0, HALF_ROWS), :], dst_of(sl, k), sem.at[sl])

    def start_all(step, sl):
        def one(k, carry):
            page_copy(step * per_step + k, sl, k).start()
            return carry
        lax.fori_loop(0, per_step, one, 0)

    @pl.when(b == 0)
    def _():
        start_all(0, 0)

    @pl.when(b + 1 < pl.num_programs(0))
    def _():
        start_all(b + 1, 1 - slot)

    def wait_one(k, carry):
        page_copy(b * per_step + k, slot, k).wait()
        return carry
    lax.fori_loop(0, per_step, wait_one, 0)
    return slot


def _cmp_uv_sample_body(pt_ref, cache_ref, w_ref, uv_ref, buf, sem, xt):
    per_step = buf.shape[1]
    slot = _fetch_pages(pt_ref, cache_ref, sem, 0, per_step, lambda sl, k: buf.at[sl, k])
    n_cb = HALF_ROWS // LANES

    def to_rows(k, carry):
        for cb in range(n_cb):
            xt[k, cb] = buf[slot, k, cb * LANES:(cb + 1) * LANES, :].T
        return carry
    lax.fori_loop(0, per_step, to_rows, 0)

    per_page = PAGE_SIZE // CMP_STRIDE
    n = per_step * per_page

    def tap(cgp, j):
        return xt[:, cgp, pl.ds(j, per_page, stride=CMP_STRIDE), :].reshape(n, LANES).astype(jnp.bfloat16)

    for cgp in range(n_cb):
        uv_ref[0, :, cgp * 512:(cgp + 1) * 512] = _uv_taps(functools.partial(tap, cgp), w_ref, cgp // 2, n)


UV_PAGES = 32


def _cmp_uv_sample(page_table, cache_t, w1p):
    n_seq, n_pages = page_table.shape
    assert n_pages % UV_PAGES == 0
    steps = n_pages // UV_PAGES
    per_page = PAGE_SIZE // CMP_STRIDE
    return pl.pallas_call(
        _cmp_uv_sample_body,
        grid_spec=pltpu.PrefetchScalarGridSpec(
            num_scalar_prefetch=1, grid=(n_seq * steps,),
            in_specs=[pl.BlockSpec(memory_space=pl.ANY),
                      pl.BlockSpec(w1p.shape, lambda b, pt: (0, 0, 0, 0))],
            out_specs=pl.BlockSpec((1, UV_PAGES * per_page, UV_COLS), lambda b, pt: (b // steps, b % steps, 0)),
            scratch_shapes=[pltpu.VMEM((2, UV_PAGES, HALF_ROWS, PAGE_SIZE), jnp.float32),
                            pltpu.SemaphoreType.DMA((2,)),
                            pltpu.VMEM((UV_PAGES, HALF_ROWS // LANES, PAGE_SIZE, LANES), jnp.float32)]),
        out_shape=jax.ShapeDtypeStruct((n_seq, n_pages * per_page, UV_COLS), jnp.float32),
        compiler_params=pltpu.CompilerParams(dimension_semantics=("arbitrary",),
                                             vmem_limit_bytes=VMEM_LIMIT),
        name="cmp_uv_sample",
    )(page_table, cache_t, w1p)


def _group_diag(x, rows_per_group):
    grp = lax.broadcasted_iota(jnp.int32, (x.shape[0], 1), 0) // rows_per_group
    out = jnp.zeros((x.shape[0], HEAD_DIM), x.dtype)
    for g in range(N_KV):
        out = out + jnp.where(grp == g, x[:, g * HEAD_DIM:(g + 1) * HEAD_DIM], 0.0)
    return out


def _nsa_sample_body(pt_ref, cache_ref, q_ref, kcvc_ref, new_ref, win_ref, ng_ref, ov_ref, same_q_ref, sl_ref,
                     o_ref, buf, sem, *, n_cmp, t_new):
    past_len = buf.shape[2]
    slot = _fetch_pages(pt_ref, cache_ref, sem, HALF_ROWS, past_len // PAGE_SIZE,
                        lambda sl, k: buf.at[sl, :, pl.ds(pl.multiple_of(k * PAGE_SIZE, PAGE_SIZE), PAGE_SIZE)])
    rows = q_ref.shape[1]
    rpg = rows // N_KV
    nt = (((1,), (1,)), ((), ()))
    qbd = q_ref[0] * SCALE
    slope = sl_ref[...]
    row = lax.broadcasted_iota(jnp.int32, (rows, 1), 0)
    qtok = row % t_new
    qpos = past_len + qtok
    new_rows = new_ref.shape[2]
    new_idx = lax.broadcasted_iota(jnp.int32, (1, new_rows), 1)
    new_valid = (new_idx <= qtok) & (new_idx < t_new)
    new_bias = slope * (qtok - new_idx).astype(jnp.float32)

    n_ch = kcvc_ref.shape[2]
    s = lax.dot_general(qbd, kcvc_ref[0, 0], nt, preferred_element_type=jnp.float32)
    cidx = lax.broadcasted_iota(jnp.int32, (1, n_ch), 1)
    dist_c = qpos - (cidx * CMP_STRIDE + (CMP_LEN - 1))
    p = _masked_softmax(s - slope * dist_c.astype(jnp.float32), (dist_c >= 0) & (cidx < n_cmp))
    o_cmp = _group_diag(jnp.dot(p.astype(jnp.bfloat16), kcvc_ref[0, 1], preferred_element_type=jnp.float32), rpg)

    p_hi, p_lo = _split_bf16(p)
    psum = (jnp.dot(same_q_ref[...], p_hi, preferred_element_type=jnp.float32)
            + jnp.dot(same_q_ref[...], p_lo, preferred_element_type=jnp.float32))
    ps_hi, ps_lo = _split_bf16(psum)
    imp = (jnp.dot(ps_hi, ov_ref[...], preferred_element_type=jnp.float32)
           + jnp.dot(ps_lo, ov_ref[...], preferred_element_type=jnp.float32))
    n_lanes = imp.shape[-1]
    blk = lax.broadcasted_iota(jnp.int32, (1, n_lanes), 1)
    cur = qpos // SEL_BLOCK
    imp = jnp.where(blk > cur, -jnp.inf, imp)
    imp = jnp.where((blk == cur) | (blk == 0), BIG, imp)
    chosen = []
    for _ in range(N_SEL):
        m = jnp.max(imp, axis=-1, keepdims=True)
        idx = jnp.min(jnp.where(imp == m, blk, n_lanes), axis=-1, keepdims=True)
        chosen.append(idx)
        imp = jnp.where(blk == idx, -jnp.inf, imp)

    s_all = jnp.dot(qbd, buf[slot, 0:GD, :].astype(jnp.bfloat16), preferred_element_type=jnp.float32)
    kpos = lax.broadcasted_iota(jnp.int32, (1, past_len), 1)
    kblk = kpos // SEL_BLOCK
    picked = kblk == chosen[0]
    for idx in chosen[1:]:
        picked = picked | (kblk == idx)
    s_all = jnp.where(picked, s_all - slope * (qpos - kpos).astype(jnp.float32), NEG)
    s_new = lax.dot_general(qbd, new_ref[0, 2], nt, preferred_element_type=jnp.float32)
    s_new = jnp.where(new_valid, s_new - new_bias, NEG)
    m = jnp.maximum(jnp.max(s_all, axis=-1, keepdims=True), jnp.max(s_new, axis=-1, keepdims=True))
    e_all = jnp.exp(s_all - m)
    e_new = jnp.where(new_valid, jnp.exp(s_new - m), 0.0)
    l = jnp.sum(e_all, axis=-1, keepdims=True) + jnp.sum(e_new, axis=-1, keepdims=True)
    acc = (lax.dot_general(e_all.astype(jnp.bfloat16), buf[slot, GD:2 * GD, :].astype(jnp.bfloat16), nt,
                           preferred_element_type=jnp.float32)
           + jnp.dot(e_new.astype(jnp.bfloat16), new_ref[0, 3], preferred_element_type=jnp.float32))
    o_slc = _group_diag(acc, rpg) / jnp.maximum(l, 1e-30)

    w_buf = win_ref.shape[2]
    wdist = qpos - (past_len - w_buf + lax.broadcasted_iota(jnp.int32, (1, w_buf), 1))
    valid_w = (wdist >= 0) & (wdist < WINDOW)
    s_w = jnp.dot(qbd, win_ref[0, 0:GD, :].astype(jnp.bfloat16), preferred_element_type=jnp.float32)
    s_w = jnp.where(valid_w, s_w - slope * wdist.astype(jnp.float32), NEG)
    s_wn = lax.dot_general(qbd, new_ref[0, 4], nt, preferred_element_type=jnp.float32)
    s_wn = jnp.where(new_valid, s_wn - new_bias, NEG)
    m = jnp.maximum(jnp.max(s_w, axis=-1, keepdims=True), jnp.max(s_wn, axis=-1, keepdims=True))
    e_w = jnp.where(valid_w, jnp.exp(s_w - m), 0.0)
    e_wn = jnp.where(new_valid, jnp.exp(s_wn - m), 0.0)
    acc = (lax.dot_general(e_w.astype(jnp.bfloat16), win_ref[0, GD:2 * GD, :].astype(jnp.bfloat16), nt,
                           preferred_element_type=jnp.float32)
           + jnp.dot(e_wn.astype(jnp.bfloat16), new_ref[0, 5], preferred_element_type=jnp.float32))
    l = jnp.sum(e_w, axis=-1, keepdims=True) + jnp.sum(e_wn, axis=-1, keepdims=True)
    o_win = _group_diag(acc, rpg) / jnp.maximum(l, 1e-30)

    gates = _sigmoid(ng_ref[0])
    o_ref[0] = (gates[:, 0:1] * o_cmp + gates[:, 1:2] * o_slc + gates[:, 2:3] * o_win).astype(o_ref.dtype)


def _nsa_sample(page_table, cache_t, q_bd, kcvc, new_kv, win_t, ng_r, n_cmp, t_new):
    n_seq, n_pages = page_table.shape
    past_len = n_pages * PAGE_SIZE
    n_ch = kcvc.shape[2]
    rows = q_bd.shape[1]
    rpg = rows // N_KV
    nsb = -(-(past_len + t_new) // SEL_BLOCK)
    n_lanes = -(-nsb // LANES) * LANES
    st = np.arange(n_ch) * CMP_STRIDE
    bs = np.arange(n_lanes) * SEL_BLOCK
    overlap = ((st[:, None] <= bs[None, :] + SEL_BLOCK - 1) & (st[:, None] + CMP_LEN - 1 >= bs[None, :])
               & (np.arange(n_ch)[:, None] < n_cmp) & (np.arange(n_lanes)[None, :] < nsb)).astype(np.float32)
    r = np.arange(rows)
    same_q = ((r[:, None] // rpg == r[None, :] // rpg) & (r[:, None] % t_new == r[None, :] % t_new)
              ).astype(np.float32)
    slopes = jnp.repeat(_alibi_slopes().reshape(N_HEADS), t_new).reshape(rows, 1)
    w_buf = win_t.shape[2]
    return pl.pallas_call(
        functools.partial(_nsa_sample_body, n_cmp=n_cmp, t_new=t_new),
        grid_spec=pltpu.PrefetchScalarGridSpec(
            num_scalar_prefetch=1, grid=(n_seq,),
            in_specs=[pl.BlockSpec(memory_space=pl.ANY),
                      pl.BlockSpec((1, rows, GD), lambda b, pt: (b, 0, 0)),
                      pl.BlockSpec((1, 2, n_ch, GD), lambda b, pt: (b, 0, 0, 0)),
                      pl.BlockSpec((1, 6, new_kv.shape[2], GD), lambda b, pt: (b, 0, 0, 0)),
                      pl.BlockSpec((1, HALF_ROWS, w_buf), lambda b, pt: (b, 0, 0)),
                      pl.BlockSpec((1, rows, 3), lambda b, pt: (b, 0, 0)),
                      pl.BlockSpec((n_ch, n_lanes), lambda b, pt: (0, 0)),
                      pl.BlockSpec((rows, rows), lambda b, pt: (0, 0)),
                      pl.BlockSpec((rows, 1), lambda b, pt: (0, 0))],
            out_specs=pl.BlockSpec((1, rows, HEAD_DIM), lambda b, pt: (b, 0, 0)),
            scratch_shapes=[pltpu.VMEM((2, HALF_ROWS, past_len), jnp.float32),
                            pltpu.SemaphoreType.DMA((2,))]),
        out_shape=jax.ShapeDtypeStruct((n_seq, rows, HEAD_DIM), jnp.bfloat16),
        compiler_params=pltpu.CompilerParams(dimension_semantics=("arbitrary",),
                                             vmem_limit_bytes=VMEM_LIMIT),
        name="nsa_sample",
    )(page_table, cache_t, q_bd, kcvc, new_kv, win_t, ng_r,
      jnp.asarray(overlap, jnp.bfloat16), jnp.asarray(same_q, jnp.bfloat16), slopes)


def _mix_body(a_ref, b_ref, mga_ref, mgb_ref, wpa_ref, wpb_ref, u_ref):
    a = jnp.dot(a_ref[...], wpa_ref[...], preferred_element_type=jnp.float32)
    b = jnp.dot(b_ref[...], wpb_ref[...], preferred_element_type=jnp.float32)
    u_ref[...] = (_sigmoid(mga_ref[...]) * a + _sigmoid(mgb_ref[...]) * b).astype(u_ref.dtype)


def _mix(o_nsa, o_hg, mg, w_pa_bf, w_pb_bf, tm):
    n = o_nsa.shape[0]
    const = lambda i: (0, 0)
    return pl.pallas_call(
        _mix_body,
        grid=(n // tm,),
        in_specs=[pl.BlockSpec((tm, NSA_WIDTH), lambda i: (i, 0)),
                  pl.BlockSpec((tm, HG_WIDTH), lambda i: (i, 0)),
                  pl.BlockSpec((tm, D_MODEL), lambda i: (i, 0)),
                  pl.BlockSpec((tm, D_MODEL), lambda i: (i, 1)),
                  pl.BlockSpec((NSA_WIDTH, D_MODEL), const, pipeline_mode=pl.Buffered(1)),
                  pl.BlockSpec((HG_WIDTH, D_MODEL), const, pipeline_mode=pl.Buffered(1))],
        out_specs=pl.BlockSpec((tm, D_MODEL), lambda i: (i, 0)),
        out_shape=jax.ShapeDtypeStruct((n, D_MODEL), jnp.bfloat16),
        compiler_params=pltpu.CompilerParams(dimension_semantics=("parallel",),
                                             vmem_limit_bytes=VMEM_LIMIT),
        name="tail_mix",
    )(o_nsa, o_hg, mg, mg, w_pa_bf, w_pb_bf)


def _layer_norm(z, g, b):
    mu = jnp.mean(z, axis=-1, keepdims=True)
    zc = z - mu
    var = jnp.mean(zc * zc, axis=-1, keepdims=True)
    return zc * lax.rsqrt(var + LN_EPS) * g + b


CH = D_MODEL // LANES


def _store_chunked(ref, val):
    tm = val.shape[0]
    for k in range(CH):
        ref[pl.ds(k, tm, stride=CH), :] = val[:, k * LANES:(k + 1) * LANES]


def _load_chunked(ref, tm, lead=()):
    return jnp.concatenate([ref[lead + (pl.ds(k, tm, stride=CH), slice(None))] for k in range(CH)], axis=1)


def _ln1_body(u_ref, x_ref, wout_ref, g_ref, b_ref, wr_hi_ref, wr_lo_ref, br_ref, h_ref, lg_ref):
    y = jnp.dot(u_ref[...], wout_ref[...], preferred_element_type=jnp.float32)
    h = _layer_norm(DN_ALPHA * x_ref[...] + y, g_ref[...], b_ref[...])
    _store_chunked(h_ref, h)
    h_hi = h.astype(jnp.bfloat16)
    h_lo = (h - h_hi.astype(jnp.float32)).astype(jnp.bfloat16)
    lg = jnp.dot(h_hi, wr_hi_ref[...], preferred_element_type=jnp.float32)
    lg = lg + jnp.dot(h_lo, wr_hi_ref[...], preferred_element_type=jnp.float32)
    lg = lg + jnp.dot(h_hi, wr_lo_ref[...], preferred_element_type=jnp.float32)
    lg_ref[...] = lg + br_ref[...]


def _ln1(u, x_all, w_out_bf, g, b, wr_hi, wr_lo, br, tm):
    n = u.shape[0]
    const = lambda i: (0, 0)
    return pl.pallas_call(
        _ln1_body,
        grid=(n // tm,),
        in_specs=[pl.BlockSpec((tm, D_MODEL), lambda i: (i, 0)),
                  pl.BlockSpec((tm, D_MODEL), lambda i: (i, 0)),
                  pl.BlockSpec((D_MODEL, D_MODEL), const, pipeline_mode=pl.Buffered(1)),
                  pl.BlockSpec((1, D_MODEL), const),
                  pl.BlockSpec((1, D_MODEL), const),
                  pl.BlockSpec((D_MODEL, LANES), const),
                  pl.BlockSpec((D_MODEL, LANES), const),
                  pl.BlockSpec((1, LANES), const)],
        out_specs=[pl.BlockSpec((tm * CH, LANES), lambda i: (i, 0)),
                   pl.BlockSpec((tm, LANES), lambda i: (i, 0))],
        out_shape=[jax.ShapeDtypeStruct((n * CH, LANES), jnp.float32),
                   jax.ShapeDtypeStruct((n, LANES), jnp.float32)],
        compiler_params=pltpu.CompilerParams(dimension_semantics=("parallel",),
                                             vmem_limit_bytes=VMEM_LIMIT),
        name="tail_ln1",
    )(u, x_all, w_out_bf, g.reshape(1, -1), b.reshape(1, -1), wr_hi, wr_lo, br)


LG0 = N_GROUPS


def _route_body(lg_ref, tri_ref, out_ref, cnt_ref, carry_scr):
    @pl.when(pl.program_id(0) == 0)
    def _():
        carry_scr[...] = jnp.zeros_like(carry_scr)

    lg = lg_ref[...]
    tm = lg.shape[0]
    lane = lax.broadcasted_iota(jnp.int32, lg.shape, 1)
    is_g = lane < N_GROUPS
    gl = jnp.where(is_g, lg, NEG)
    gmax = jnp.max(gl, axis=-1, keepdims=True)
    grp = jnp.min(jnp.where(gl == gmax, lane, LANES), axis=-1, keepdims=True)
    g_w = 1.0 / jnp.sum(jnp.where(is_g, jnp.exp(lg - gmax), 0.0), axis=-1, keepdims=True)
    lo = LG0 + grp * EXP_PER_GROUP
    el = jnp.where((lane >= lo) & (lane < lo + EXP_PER_GROUP), lg, NEG)
    v1 = jnp.max(el, axis=-1, keepdims=True)
    i1 = jnp.min(jnp.where(el == v1, lane, LANES), axis=-1, keepdims=True)
    el2 = jnp.where(lane == i1, NEG, el)
    v2 = jnp.max(el2, axis=-1, keepdims=True)
    i2 = jnp.min(jnp.where(el2 == v2, lane, LANES), axis=-1, keepdims=True)
    e21 = jnp.exp(v2 - v1)
    w1 = g_w / (1.0 + e21)
    w2 = g_w * e21 / (1.0 + e21)
    hit1 = lane == i1
    hit2 = lane == i2
    onehot = jnp.where(hit1 | hit2, 1.0, 0.0)
    incl = jnp.dot(tri_ref[...], onehot.astype(jnp.bfloat16), preferred_element_type=jnp.float32)
    carry = carry_scr[...]
    before = incl - onehot + carry
    r1 = jnp.sum(jnp.where(hit1, before, 0.0), axis=-1, keepdims=True)
    r2 = jnp.sum(jnp.where(hit2, before, 0.0), axis=-1, keepdims=True)
    carry = carry + incl[tm - 1:tm, :]
    carry_scr[...] = carry
    cnt_ref[...] = carry
    out = jnp.where(lane == 0, (i1 - LG0).astype(jnp.float32), 0.0)
    out = jnp.where(lane == 1, (i2 - LG0).astype(jnp.float32), out)
    out = jnp.where(lane == 2, w1, out)
    out = jnp.where(lane == 3, w2, out)
    out = jnp.where(lane == 4, r1, out)
    out = jnp.where(lane == 5, r2, out)
    out_ref[...] = out


def _route(lg, tm):
    n = lg.shape[0]
    tri = (np.arange(tm)[:, None] >= np.arange(tm)[None, :]).astype(np.float32)
    return pl.pallas_call(
        _route_body,
        grid=(n // tm,),
        in_specs=[pl.BlockSpec((tm, LANES), lambda i: (i, 0)),
                  pl.BlockSpec((tm, tm), lambda i: (0, 0))],
        out_specs=[pl.BlockSpec((tm, LANES), lambda i: (i, 0)),
                   pl.BlockSpec((1, LANES), lambda i: (0, 0))],
        out_shape=[jax.ShapeDtypeStruct((n, LANES), jnp.float32),
                   jax.ShapeDtypeStruct((1, LANES), jnp.float32)],
        scratch_shapes=[pltpu.VMEM((1, LANES), jnp.float32)],
        compiler_params=pltpu.CompilerParams(dimension_semantics=("arbitrary",)),
        name="moe_route",
    )(lg, jnp.asarray(tri, jnp.bfloat16))


def _dispatch_body(pos_ref, h_ref, xs_in_ref, xs_ref, sem):
    del xs_in_ref
    tm = h_ref.shape[0] // CH
    base = pl.program_id(0) * tm

    def issue(t, carry):
        src = pl.multiple_of(t * CH, CH)
        for slot in range(2):
            dst = pl.multiple_of(pos_ref[2 * (base + t) + slot] * CH, CH)
            pltpu.make_async_copy(h_ref.at[pl.ds(src, CH)], xs_ref.at[pl.ds(dst, CH)], sem).start()
        return carry

    lax.fori_loop(0, tm, issue, 0)
    for _ in range(2):
        pltpu.make_async_copy(h_ref, xs_ref.at[pl.ds(0, tm * CH)], sem).wait()


DISPATCH_TM = 128


def _dispatch(pos_flat, h_c, n_slots):
    zeros = jnp.zeros((n_slots * CH, LANES), h_c.dtype)
    tm = DISPATCH_TM
    return pl.pallas_call(
        _dispatch_body,
        grid_spec=pltpu.PrefetchScalarGridSpec(
            num_scalar_prefetch=1, grid=(h_c.shape[0] // (tm * CH),),
            in_specs=[pl.BlockSpec((tm * CH, LANES), lambda i, pos: (i, 0)),
                      pl.BlockSpec(memory_space=pl.ANY)],
            out_specs=pl.BlockSpec(memory_space=pl.ANY),
            scratch_shapes=[pltpu.SemaphoreType.DMA(())]),
        out_shape=jax.ShapeDtypeStruct(zeros.shape, h_c.dtype),
        input_output_aliases={2: 0},
        compiler_params=pltpu.CompilerParams(dimension_semantics=("arbitrary",)),
        name="moe_dispatch",
    )(pos_flat, h_c, zeros)


def _ffn_body(te_ref, nu_ref, x_ref, wg_ref, wu_ref, wd_ref, y_ref, wg_bf, wu_bf, wd_bf):
    i = pl.program_id(0)
    prev = te_ref[jnp.maximum(i - 1, 0)]

    @pl.when((i == 0) | (te_ref[i] != prev))
    def _():
        wg_bf[...] = wg_ref[0].astype(jnp.bfloat16)
        wu_bf[...] = wu_ref[0].astype(jnp.bfloat16)
        wd_bf[...] = wd_ref[0].astype(jnp.bfloat16)

    @pl.when(i < nu_ref[0])
    def _():
        x = _load_chunked(x_ref, x_ref.shape[0] // CH).astype(jnp.bfloat16)
        g = jnp.dot(x, wg_bf[...], preferred_element_type=jnp.float32)
        u = jnp.dot(x, wu_bf[...], preferred_element_type=jnp.float32)
        hid = (g * _sigmoid(g) * u).astype(jnp.bfloat16)
        _store_chunked(y_ref, jnp.dot(hid, wd_bf[...], preferred_element_type=jnp.float32))

    @pl.when(i >= nu_ref[0])
    def _():
        y_ref[...] = jnp.zeros_like(y_ref)


def _ffn(tile_expert, n_used, xs_c, w_gate, w_up, w_down, tm):
    d = D_MODEL
    nt = xs_c.shape[0] // (tm * CH)
    return pl.pallas_call(
        _ffn_body,
        grid_spec=pltpu.PrefetchScalarGridSpec(
            num_scalar_prefetch=2, grid=(nt,),
            in_specs=[pl.BlockSpec((tm * CH, LANES), lambda i, te, nu: (i, 0)),
                      pl.BlockSpec((1, d, D_EXPERT), lambda i, te, nu: (te[i], 0, 0)),
                      pl.BlockSpec((1, d, D_EXPERT), lambda i, te, nu: (te[i], 0, 0)),
                      pl.BlockSpec((1, D_EXPERT, d), lambda i, te, nu: (te[i], 0, 0))],
            out_specs=pl.BlockSpec((tm * CH, LANES), lambda i, te, nu: (i, 0)),
            scratch_shapes=[pltpu.VMEM((d, D_EXPERT), jnp.bfloat16),
                            pltpu.VMEM((d, D_EXPERT), jnp.bfloat16),
                            pltpu.VMEM((D_EXPERT, d), jnp.bfloat16)]),
        out_shape=jax.ShapeDtypeStruct(xs_c.shape, jnp.float32),
        compiler_params=pltpu.CompilerParams(dimension_semantics=("arbitrary",),
                                             vmem_limit_bytes=VMEM_LIMIT),
        name="moe_ffn",
    )(tile_expert, n_used, xs_c, w_gate, w_up, w_down)


def _combine_body(pos_ref, h_ref, rw_ref, g_ref, b_ref, y_ref, out_ref, buf, sem):
    tm = out_ref.shape[0]
    base = pl.program_id(0) * tm

    def issue(t, carry):
        for slot in range(2):
            src = pl.multiple_of(pos_ref[2 * (base + t) + slot] * CH, CH)
            dst = pl.multiple_of(t * CH, CH)
            pltpu.make_async_copy(y_ref.at[pl.ds(src, CH)], buf.at[slot, pl.ds(dst, CH)], sem.at[slot]).start()
        return carry

    lax.fori_loop(0, tm, issue, 0)
    for slot in range(2):
        pltpu.make_async_copy(y_ref.at[pl.ds(0, tm * CH)], buf.at[slot], sem.at[slot]).wait()
    rw = rw_ref[...]
    z = rw[:, 2:3] * _load_chunked(buf, tm, (0,)) + rw[:, 3:4] * _load_chunked(buf, tm, (1,))
    out_ref[...] = _layer_norm(DN_ALPHA * _load_chunked(h_ref, tm) + z, g_ref[...], b_ref[...])


def _combine(pos_flat, h_c, route_out, g, b, y_c, tm):
    n, d = h_c.shape[0] // CH, D_MODEL
    return pl.pallas_call(
        _combine_body,
        grid_spec=pltpu.PrefetchScalarGridSpec(
            num_scalar_prefetch=1, grid=(n // tm,),
            in_specs=[pl.BlockSpec((tm * CH, LANES), lambda i, pos: (i, 0)),
                      pl.BlockSpec((tm, LANES), lambda i, pos: (i, 0)),
                      pl.BlockSpec((1, d), lambda i, pos: (0, 0)),
                      pl.BlockSpec((1, d), lambda i, pos: (0, 0)),
                      pl.BlockSpec(memory_space=pl.ANY)],
            out_specs=pl.BlockSpec((tm, d), lambda i, pos: (i, 0)),
            scratch_shapes=[pltpu.VMEM((2, tm * CH, LANES), jnp.float32),
                            pltpu.SemaphoreType.DMA((2,))]),
        out_shape=jax.ShapeDtypeStruct((n, d), jnp.float32),
        compiler_params=pltpu.CompilerParams(dimension_semantics=("arbitrary",),
                                             vmem_limit_bytes=VMEM_LIMIT),
        name="moe_combine",
    )(pos_flat, h_c, route_out, g.reshape(1, -1), b.reshape(1, -1), y_c)


FFN_TM = 256
PROJ_TM = 640
TAIL_TM = 320


def _moe_and_norm(h_c, lg, w_gate, w_up, w_down, ln2_g, ln2_b):
    n = lg.shape[0]
    route_out, cnt = _route(lg, 640)
    eid = route_out[:, 0:2].astype(jnp.int32)
    rank = route_out[:, 4:6].astype(jnp.int32)
    counts = cnt[0, LG0:LG0 + N_EXPERTS].astype(jnp.int32)
    tiles_per = (counts + FFN_TM - 1) // FFN_TM
    tile_end = jnp.cumsum(tiles_per)
    row_start = (tile_end - tiles_per) * FFN_TM
    pos_flat = (row_start[eid] + rank).reshape(-1)
    nt = (2 * n) // FFN_TM + N_EXPERTS
    n_used = tile_end[-1]
    tile_ids = jnp.minimum(jnp.arange(nt, dtype=jnp.int32), n_used - 1)
    tile_expert = jnp.sum((tile_end[None, :] <= tile_ids[:, None]).astype(jnp.int32), axis=1)
    xs_c = _dispatch(pos_flat, h_c, nt * FFN_TM)
    y_c = _ffn(tile_expert, n_used.reshape(1).astype(jnp.int32), xs_c, w_gate, w_up, w_down, FFN_TM)
    return _combine(pos_flat, h_c, route_out, ln2_g, ln2_b, y_c, DISPATCH_TM)


def kernel(x_prompt, x_sample, cache_kv, cache_win, state_hgrn, page_table, w_in, b_in, w_cmp1, w_cmp2, cmp_pe,
           hgrn_gamma, hgrn_norm, w_pa, w_pb, w_out, ln1_g, ln1_b, w_rg, b_rg, w_re, b_re, w_gate, w_up, w_down,
           ln2_g, ln2_b):
    n_p = x_prompt.shape[0] * x_prompt.shape[1]
    n_s = x_sample.shape[0] * x_sample.shape[1]
    x_all = jnp.concatenate([x_prompt.reshape(n_p, D_MODEL), x_sample.reshape(n_s, D_MODEL)], axis=0)
    x_bf = x_all.astype(jnp.bfloat16)
    bsz, seq = x_prompt.shape[:2]
    n_seq, t_new = x_sample.shape[:2]
    n_pages = page_table.shape[1]
    past_len = n_pages * PAGE_SIZE
    w = w_in[0]
    b = b_in[0]

    def seg(lo, hi, outs, tn, name, pad_to=None):
        ws, bs = w[:, lo:hi], b[lo:hi]
        if pad_to is not None:
            ws = jnp.pad(ws, ((0, 0), (0, pad_to - (hi - lo))))
            bs = jnp.pad(bs, (0, pad_to - (hi - lo)))
        return _proj(x_bf, ws.astype(jnp.bfloat16), bs, outs, PROJ_TM, tn, name)

    q_h, = seg(OFF_Q, OFF_KV, [(jnp.bfloat16, True)], 512, "proj_q")
    kv32, kv_h = seg(OFF_KV, OFF_NG, [(jnp.float32, False), (jnp.bfloat16, True)], 512, "proj_kv")
    ng, = seg(OFF_NG, OFF_H4, [(jnp.float32, False)], LANES, "proj_ng", pad_to=LANES)
    h4, = seg(OFF_H4, OFF_MG, [(jnp.float32, False)], 512, "proj_h4")
    mg, = seg(OFF_MG, PROJ_COLS, [(jnp.float32, False)], 512, "proj_mg")

    kv_p = kv32[:n_p].reshape(1, bsz, seq, 6, N_KV, HEAD_DIM)
    kv_s = kv32[n_p:].reshape(1, n_seq, t_new, 6, N_KV, HEAD_DIM)
    new_kv_prompt = kv_p[:, :, :, :KV_SLOTS]
    new_kv_sample = kv_s[:, :, :, :KV_SLOTS]
    new_win_prompt = kv_p[:, :, seq - min(WINDOW, seq):, 4:6]
    win_all = jnp.concatenate([cache_win, kv_s[:, :, :, 4:6].astype(cache_win.dtype)], axis=2)
    new_win_sample = win_all[:, :, win_all.shape[2] - min(WINDOW, win_all.shape[2]):]

    w1p = _cmp_w1_pairs(w_cmp1[0])
    n_cmp_p = (seq - CMP_LEN) // CMP_STRIDE + 1
    uv_p = _cmp_uv_rows(kv32, n_p, 2048, w1p)
    kcvc_p = _cmp_finish(uv_p, bsz, seq // CMP_STRIDE, w_cmp1[0], w_cmp2[0], cmp_pe[0], False, "cmp_finish_prompt")
    ng_t = ng[:n_p, :NG_COLS].reshape(bsz, seq, 3, N_KV, HPG).transpose(0, 3, 1, 2, 4).reshape(bsz, N_KV, seq, 3 * HPG)
    o_nsa_p = _nsa_prompt(q_h, kcvc_p, kv_h, ng_t, n_cmp_p)

    cache_t = cache_kv[0].transpose(0, 2, 3, 4, 1).reshape(cache_kv.shape[1], 2 * HALF_ROWS, PAGE_SIZE)
    win_t = cache_win[0].transpose(0, 2, 3, 4, 1).reshape(n_seq, HALF_ROWS, cache_win.shape[2])
    n_cmp_s = (past_len + t_new - CMP_LEN) // CMP_STRIDE + 1
    uv_s = _cmp_uv_sample(page_table, cache_t, w1p)
    kcvc_s = _cmp_finish(uv_s, n_seq, uv_s.shape[1], w_cmp1[0], w_cmp2[0], cmp_pe[0], True, "cmp_finish_sample")
    rows_s = N_HEADS * t_new
    q_s = q_h[:, n_p:].reshape(N_KV, HPG, n_seq, t_new, HEAD_DIM).transpose(2, 0, 1, 3, 4).reshape(
        n_seq, N_KV, HPG * t_new, HEAD_DIM)
    q_bd = jnp.einsum('sgrd,gk->sgrkd', q_s, jnp.eye(N_KV, dtype=q_s.dtype)).reshape(n_seq, rows_s, GD)
    new_kv = kv_h[:, n_p:].reshape(6, N_KV, n_seq, t_new, HEAD_DIM).transpose(2, 0, 3, 1, 4).reshape(
        n_seq, 6, t_new, GD)
    new_kv = jnp.pad(new_kv, ((0, 0), (0, 0), (0, 8 - t_new), (0, 0)))
    ng_r = ng[n_p:, :NG_COLS].reshape(n_seq, t_new, 3, N_KV, HPG).transpose(0, 3, 4, 1, 2).reshape(
        n_seq, rows_s, 3)
    o_nsa_s = _nsa_sample(page_table, cache_t, q_bd, kcvc_s, new_kv, win_t, ng_r, n_cmp_s, t_new)
    o_nsa_s = o_nsa_s.reshape(n_seq, N_KV, HPG, t_new, HEAD_DIM).transpose(0, 3, 1, 2, 4).reshape(n_s, NSA_WIDTH)

    lower = jnp.cumsum(jax.nn.softmax(hgrn_gamma.astype(jnp.float32), axis=0), axis=0)
    lb = lower[0].reshape(HG_HEADS, HG_DK)
    zero_state = jnp.zeros((bsz, HG_HEADS, HG_DV, HG_DK), jnp.float32)
    o_hg_p, st_p = _hgrn(h4, 0, bsz, seq, lb, hgrn_norm[0], zero_state, HG_CHUNK, HG_CHUNK, 512, "hgrn_prompt")
    h4_s = jnp.pad(h4[n_p:].reshape(n_seq, t_new, -1), ((0, 0), (0, 8 - t_new), (0, 0))).reshape(n_seq * 8, -1)
    o_hg_s, st_s = _hgrn(h4_s, 0, n_seq, 8, lb, hgrn_norm[0], state_hgrn[0].transpose(0, 1, 3, 2),
                         8, t_new, 8, "hgrn_sample")
    o_hg_s = o_hg_s.reshape(n_seq, 8, HG_WIDTH)[:, :t_new].reshape(n_s, HG_WIDTH)
    new_state_prompt = st_p.transpose(0, 1, 3, 2)[None].astype(x_prompt.dtype)
    new_state_sample = st_s.transpose(0, 1, 3, 2)[None].astype(state_hgrn.dtype)

    o_nsa = jnp.concatenate([o_nsa_p, o_nsa_s], axis=0)
    o_hg = jnp.concatenate([o_hg_p, o_hg_s], axis=0)
    u = _mix(o_nsa, o_hg, mg, w_pa[0].astype(jnp.bfloat16), w_pb[0].astype(jnp.bfloat16), TAIL_TM)
    wr = jnp.zeros((D_MODEL, LANES), jnp.float32).at[:, :N_GROUPS].set(w_rg[0]).at[:, LG0:LG0 + N_EXPERTS].set(w_re[0])
    br = jnp.zeros((1, LANES), jnp.float32).at[0, :N_GROUPS].set(b_rg[0]).at[0, LG0:LG0 + N_EXPERTS].set(b_re[0])
    wr_hi, wr_lo = _split_bf16(wr)
    h, lg = _ln1(u, x_all, w_out[0].astype(jnp.bfloat16), ln1_g[0], ln1_b[0], wr_hi, wr_lo, br, TAIL_TM)
    out = _moe_and_norm(h, lg, w_gate[0], w_up[0], w_down[0], ln2_g[0], ln2_b[0])
    y_prompt = out[:n_p].reshape(bsz, seq, D_MODEL)
    y_sample = out[n_p:].reshape(n_seq, t_new, D_MODEL)
    return (y_prompt, y_sample, new_kv_prompt, new_kv_sample, new_win_prompt, new_win_sample,
            new_state_prompt, new_state_sample)
```

```python
import functools

import numpy as np
import jax
import jax.numpy as jnp
from jax import lax
from jax.experimental import pallas as pl
from jax.experimental.pallas import tpu as pltpu

D_MODEL = 2048
N_HEADS = 16
N_KV = 4
HPG = N_HEADS // N_KV
HEAD_DIM = 64
NSA_WIDTH = N_HEADS * HEAD_DIM
CMP_LEN = 32
CMP_STRIDE = 16
CMP_HID = 128
SEL_BLOCK = 64
N_SEL = 16
WINDOW = 512
QBLK = 128
KV_SLOTS = 4
PAGE_SIZE = 128
HG_HEADS = 8
HG_DK = 128
HG_DV = 128
HG_WIDTH = HG_HEADS * HG_DV
HG_CHUNK = 32
N_GROUPS = 4
EXP_PER_GROUP = 8
N_EXPERTS = N_GROUPS * EXP_PER_GROUP
D_EXPERT = 512
DEPTH = 1
DN_ALPHA = (2.0 * DEPTH) ** 0.25
LN_EPS = 1e-5
SCALE = HEAD_DIM ** -0.5
NEG = -1e30
BIG = 1e30

LANES = 128
KV_COLS = 6 * N_KV * HEAD_DIM
NG_COLS = 3 * N_HEADS
OFF_Q = 0
OFF_KV = NSA_WIDTH
OFF_NG = OFF_KV + KV_COLS
OFF_H4 = OFF_NG + NG_COLS
OFF_MG = OFF_H4 + 2 * HG_HEADS * HG_DK + 2 * HG_WIDTH
PROJ_COLS = OFF_MG + 2 * D_MODEL

VMEM_LIMIT = 56 * 1024 * 1024


def _sigmoid(x):
    return 1.0 / (1.0 + jnp.exp(-x))


def _proj_body(x_ref, w_ref, b_ref, *out_refs):
    acc = jnp.dot(x_ref[...], w_ref[...], preferred_element_type=jnp.float32) + b_ref[...]
    for o_ref in out_refs:
        if len(o_ref.shape) == 3:
            for k in range(o_ref.shape[0]):
                o_ref[k] = acc[:, k * HEAD_DIM:(k + 1) * HEAD_DIM].astype(o_ref.dtype)
        else:
            o_ref[...] = acc.astype(o_ref.dtype)


def _proj(x_bf, w_bf, b, outs, tm, tn, name):
    m, k = x_bf.shape
    n = w_bf.shape[1]
    assert m % tm == 0 and n % tn == 0
    out_specs, out_shape = [], []
    for dt, per_head in outs:
        if per_head:
            out_specs.append(pl.BlockSpec((tn // HEAD_DIM, tm, HEAD_DIM), lambda i, j: (j, i, 0)))
            out_shape.append(jax.ShapeDtypeStruct((n // HEAD_DIM, m, HEAD_DIM), dt))
        else:
            out_specs.append(pl.BlockSpec((tm, tn), lambda i, j: (i, j)))
            out_shape.append(jax.ShapeDtypeStruct((m, n), dt))
    return pl.pallas_call(
        _proj_body,
        grid=(m // tm, n // tn),
        in_specs=[pl.BlockSpec((tm, k), lambda i, j: (i, 0)),
                  pl.BlockSpec((k, tn), lambda i, j: (0, j)),
                  pl.BlockSpec((1, tn), lambda i, j: (0, j))],
        out_specs=out_specs,
        out_shape=out_shape,
        compiler_params=pltpu.CompilerParams(dimension_semantics=("parallel", "parallel"),
                                             vmem_limit_bytes=VMEM_LIMIT),
        name=name,
    )(x_bf, w_bf, b.reshape(1, n))


def _hgrn_body(hq_ref, hf_ref, hi_ref, hg_ref, lb_ref, nrm_ref, s0_ref, o_ref, sfin_ref, st_scr,
               *, chunk, n_valid, n_chunks):
    @pl.when(pl.program_id(2) == 0)
    def _():
        st_scr[...] = s0_ref[0]

    for h in range(st_scr.shape[0]):
        cols = slice(h * HG_DK, (h + 1) * HG_DK)
        o, st = _hgrn_head(hq_ref[:, cols], hf_ref[:, cols], hi_ref[:, cols], hg_ref[:, cols],
                           lb_ref[h], nrm_ref[h], st_scr[h], chunk, n_valid, n_chunks)
        st_scr[h] = st
        sfin_ref[0, h] = st
        o_ref[:, cols] = o.astype(o_ref.dtype)


def _hgrn_head(hq, hf, v, hg, lb, nrm, st, chunk, n_valid, n_chunks):
    rows = chunk * n_chunks
    q = hq * _sigmoid(hq)
    f = lb + (1.0 - lb) * _sigmoid(hf)
    k = 1.0 - f
    lc = jnp.log(f)
    row_in_chunk = lax.broadcasted_iota(jnp.int32, (rows, HG_DK), 0) % chunk
    if n_valid < chunk:
        live = row_in_chunk < n_valid
        q = jnp.where(live, q, 0.0)
        k = jnp.where(live, k, 0.0)
        v = jnp.where(live, v, 0.0)
        lc = jnp.where(live, lc, 0.0)
    bc = lc
    step = 1
    while step < chunk:
        bc = bc + jnp.where(row_in_chunk >= step, pltpu.roll(bc, step, axis=0), 0.0)
        step *= 2
    bc3 = bc.reshape(n_chunks, chunk, HG_DK)
    bl3 = bc3[:, chunk - 1:chunk, :]
    q3 = q.reshape(n_chunks, chunk, HG_DK)
    k3 = k.reshape(n_chunks, chunk, HG_DK)
    v3 = v.reshape(n_chunks, chunk, HG_DV).astype(jnp.bfloat16)
    qe3 = (q3 * jnp.exp(bc3)).astype(jnp.bfloat16)
    ke3 = (k3 * jnp.exp(-bc3)).astype(jnp.bfloat16)
    kd3 = (k3 * jnp.exp(bl3 - bc3)).astype(jnp.bfloat16)
    dec3 = jnp.exp(bl3)
    att = jnp.einsum('ctd,csd->cts', qe3, ke3, preferred_element_type=jnp.float32)
    tri = (lax.broadcasted_iota(jnp.int32, (chunk, chunk), 0)
           >= lax.broadcasted_iota(jnp.int32, (chunk, chunk), 1))
    att = jnp.where(tri[None], att, 0.0).astype(jnp.bfloat16)
    o_intra = jnp.einsum('cts,cse->cte', att, v3, preferred_element_type=jnp.float32)

    outs = []
    for c in range(n_chunks):
        o_c = lax.dot_general(qe3[c], st.astype(jnp.bfloat16), (((1,), (1,)), ((), ())),
                              preferred_element_type=jnp.float32)
        outs.append(o_c + o_intra[c])
        upd = lax.dot_general(v3[c], kd3[c], (((0,), (0,)), ((), ())),
                              preferred_element_type=jnp.float32)
        st = st * dec3[c] + upd
    o = jnp.concatenate(outs, axis=0) if n_chunks > 1 else outs[0]
    o = o * lax.rsqrt(jnp.mean(o * o, axis=-1, keepdims=True) + LN_EPS) * nrm
    return o * (hg * _sigmoid(hg)), st


def _hgrn(h4, row0, n_seq, t_seq, lb, nrm, s0_t, chunk, n_valid, block_rows, heads_per_step, name):
    assert t_seq % block_rows == 0 and block_rows % chunk == 0 and row0 % block_rows == 0
    nb = t_seq // block_rows
    rb0 = row0 // block_rows
    nh = heads_per_step
    hb = HG_HEADS // nh

    def col_spec(seg):
        return pl.BlockSpec((block_rows, nh * HG_DK),
                            lambda b, h, i, seg=seg: (rb0 + b * nb + i, seg * hb + h))

    body = functools.partial(_hgrn_body, chunk=chunk, n_valid=n_valid, n_chunks=block_rows // chunk)
    return pl.pallas_call(
        body,
        grid=(n_seq, hb, nb),
        in_specs=[col_spec(0), col_spec(1), col_spec(2), col_spec(3),
                  pl.BlockSpec((nh, 1, HG_DK), lambda b, h, i: (h, 0, 0)),
                  pl.BlockSpec((nh, 1, HG_DV), lambda b, h, i: (h, 0, 0)),
                  pl.BlockSpec((1, nh, HG_DV, HG_DK), lambda b, h, i: (b, h, 0, 0))],
        out_specs=[pl.BlockSpec((block_rows, nh * HG_DV), lambda b, h, i: (b * nb + i, h)),
                   pl.BlockSpec((1, nh, HG_DV, HG_DK), lambda b, h, i: (b, h, 0, 0))],
        out_shape=[jax.ShapeDtypeStruct((n_seq * t_seq, HG_WIDTH), jnp.bfloat16),
                   jax.ShapeDtypeStruct((n_seq, HG_HEADS, HG_DV, HG_DK), jnp.float32)],
        scratch_shapes=[pltpu.VMEM((nh, HG_DV, HG_DK), jnp.float32)],
        compiler_params=pltpu.CompilerParams(
            dimension_semantics=("parallel", "parallel", "arbitrary"), vmem_limit_bytes=VMEM_LIMIT),
        name=name,
    )(h4, h4, h4, h4, lb.reshape(HG_HEADS, 1, HG_DK), nrm.reshape(HG_HEADS, 1, HG_DV), s0_t)


UV_COLS = 2 * N_KV * 2 * CMP_HID


def _uv_taps(tap, w_ref, c, n):
    acc = jnp.zeros((n, 4 * CMP_HID), jnp.float32)
    for p in range(CMP_STRIDE // 2):
        x2 = jnp.concatenate([tap(2 * p), tap(2 * p + 1)], axis=-1)
        acc = acc + jnp.dot(x2, w_ref[c, p], preferred_element_type=jnp.float32)
    return acc


def _cmp_uv_body(x0_ref, x1_ref, x2_ref, x3_ref, w_ref, uv_ref):
    n = uv_ref.shape[0]
    for cgp, x_ref in enumerate((x0_ref, x1_ref, x2_ref, x3_ref)):
        tap = lambda j, x_ref=x_ref: x_ref[pl.ds(j, n, stride=CMP_STRIDE), :].astype(jnp.bfloat16)
        uv_ref[:, cgp * 512:(cgp + 1) * 512] = _uv_taps(tap, w_ref, cgp // 2, n)


def _cmp_w1_pairs(w_cmp1_l):
    w = jnp.concatenate([w_cmp1_l[:, :CMP_STRIDE], w_cmp1_l[:, CMP_STRIDE:]], axis=-1)
    z = jnp.zeros_like(w)
    top = jnp.concatenate([w, z], axis=-1)
    bot = jnp.concatenate([z, w], axis=-1)
    per_tap = jnp.concatenate([top, bot], axis=-2)
    return per_tap.reshape(2, CMP_STRIDE // 2, 2 * LANES, 4 * CMP_HID).astype(jnp.bfloat16)


def _cmp_uv_rows(kv, n_rows, rows_per_step, w1p):
    n = rows_per_step // CMP_STRIDE
    return pl.pallas_call(
        _cmp_uv_body,
        grid=(n_rows // rows_per_step,),
        in_specs=[pl.BlockSpec((rows_per_step, LANES), lambda i, cb=cb: (i, cb)) for cb in range(4)]
        + [pl.BlockSpec(w1p.shape, lambda i: (0, 0, 0, 0))],
        out_specs=pl.BlockSpec((n, UV_COLS), lambda i: (i, 0)),
        out_shape=jax.ShapeDtypeStruct((n_rows // CMP_STRIDE, UV_COLS), jnp.float32),
        compiler_params=pltpu.CompilerParams(dimension_semantics=("parallel",),
                                             vmem_limit_bytes=VMEM_LIMIT),
        name="cmp_uv_prompt",
    )(kv, kv, kv, kv, w1p)


def _gelu_tanh(x):
    return 0.5 * x * (1.0 + jnp.tanh(0.7978845608028654 * (x + 0.044715 * x * x * x)))


def _cmp_finish_body(uv_ref, pe_ref, w1_ref, w2_ref, out_ref):
    n_ch = uv_ref.shape[1]
    for c in range(2):
        pe_term = jnp.dot(pe_ref[c], w1_ref[c], preferred_element_type=jnp.float32)[0:1, :]
        for g in range(N_KV):
            base = (c * N_KV + g) * 2 * CMP_HID
            u = uv_ref[0, :, base:base + CMP_HID]
            v = uv_ref[0, :, base + CMP_HID:base + 2 * CMP_HID]
            pre = u + pltpu.roll(v, n_ch - 1, axis=0) + pe_term
            hid = _gelu_tanh(pre).astype(jnp.bfloat16)
            res = jnp.dot(hid, w2_ref[c], preferred_element_type=jnp.float32).astype(out_ref.dtype)
            if out_ref.shape[1] == 2:
                out_ref[0, c, :, g * HEAD_DIM:(g + 1) * HEAD_DIM] = res
            else:
                out_ref[0, c * N_KV + g] = res


def _cmp_finish(uv, n_seq, n_ch, w_cmp1_l, w_cmp2_l, cmp_pe_l, merged, name):
    pe = jnp.zeros((2, 8, CMP_LEN * HEAD_DIM), jnp.float32).at[:, 0].set(cmp_pe_l.reshape(2, -1))
    out_dims = (2, n_ch, N_KV * HEAD_DIM) if merged else (2 * N_KV, n_ch, HEAD_DIM)
    return pl.pallas_call(
        _cmp_finish_body,
        grid=(n_seq,),
        in_specs=[pl.BlockSpec((1, n_ch, UV_COLS), lambda b: (b, 0, 0)),
                  pl.BlockSpec((2, 8, CMP_LEN * HEAD_DIM), lambda b: (0, 0, 0)),
                  pl.BlockSpec((2, CMP_LEN * HEAD_DIM, CMP_HID), lambda b: (0, 0, 0)),
                  pl.BlockSpec((2, CMP_HID, HEAD_DIM), lambda b: (0, 0, 0))],
        out_specs=pl.BlockSpec((1,) + out_dims, lambda b: (b, 0, 0, 0)),
        out_shape=jax.ShapeDtypeStruct((n_seq,) + out_dims, jnp.bfloat16),
        compiler_params=pltpu.CompilerParams(dimension_semantics=("parallel",),
                                             vmem_limit_bytes=VMEM_LIMIT),
        name=name,
    )(uv.reshape(n_seq, n_ch, UV_COLS), pe.astype(jnp.bfloat16),
      w_cmp1_l.reshape(2, CMP_LEN * HEAD_DIM, CMP_HID).astype(jnp.bfloat16),
      w_cmp2_l.astype(jnp.bfloat16))


SLC_TK = 512
WIN_KEYS = WINDOW + QBLK


def _masked_softmax(s, valid):
    s = jnp.where(valid, s, NEG)
    m = jnp.max(s, axis=-1, keepdims=True)
    e = jnp.where(valid, jnp.exp(s - m), 0.0)
    return e / jnp.maximum(jnp.sum(e, axis=-1, keepdims=True), 1e-30)


def _split_bf16(x):
    hi = x.astype(jnp.bfloat16)
    return hi, (x - hi.astype(jnp.float32)).astype(jnp.bfloat16)


SUBLANES = 8


def _top_blocks_t(imp_t):
    nb, nq = imp_t.shape
    groups = [imp_t[SUBLANES * v:SUBLANES * (v + 1), :] for v in range(nb // SUBLANES)]
    sub = lax.broadcasted_iota(jnp.int32, (SUBLANES, nq), 0)
    beaten_by = [jnp.zeros((SUBLANES, nq), jnp.float32) for _ in groups]
    for i in range(nb):
        row = jnp.broadcast_to(imp_t[i:i + 1, :], (SUBLANES, nq))
        for v, gv in enumerate(groups):
            if v > i // SUBLANES:
                beats = row >= gv
            elif v < i // SUBLANES:
                beats = row > gv
            else:
                beats = (row > gv) | ((row == gv) & (sub > i % SUBLANES))
            beaten_by[v] = beaten_by[v] + jnp.where(beats, 1.0, 0.0)
    return jnp.concatenate([jnp.where(c < N_SEL, 1.0, 0.0) for c in beaten_by], axis=0)


def _nsa_prompt_body(q_ref, kc_ref, vc_ref, ks_ref, vs_ref, kw_ref, vw_ref, ng_ref, ov_ref, ex_ref, sl_ref,
                     o_ref, *, n_cmp):
    i = pl.program_id(2)
    nq = QBLK
    rows = HPG * nq
    q2 = (q_ref[...] * SCALE).reshape(rows, HEAD_DIM)
    slopes = sl_ref[0]
    qpos = i * nq + lax.broadcasted_iota(jnp.int32, (nq, 1), 0)
    nt = (((1,), (1,)), ((), ()))

    n_ch = kc_ref.shape[2]
    s = lax.dot_general(q2, kc_ref[0, 0], nt, preferred_element_type=jnp.float32).reshape(HPG, nq, n_ch)
    cidx = lax.broadcasted_iota(jnp.int32, (1, n_ch), 1)
    dist_c = qpos - (cidx * CMP_STRIDE + (CMP_LEN - 1))
    valid_c = ((dist_c >= 0) & (cidx < n_cmp))[None]
    p = _masked_softmax(s - slopes * dist_c.astype(jnp.float32)[None], valid_c)
    o_cmp = jnp.dot(p.reshape(rows, n_ch).astype(jnp.bfloat16), vc_ref[0, 0],
                    preferred_element_type=jnp.float32).reshape(HPG, nq, HEAD_DIM)

    p_hi, p_lo = _split_bf16(p[0] + p[1] + p[2] + p[3])
    imp_t = (lax.dot_general(ov_ref[...], p_hi, nt, preferred_element_type=jnp.float32)
             + lax.dot_general(ov_ref[...], p_lo, nt, preferred_element_type=jnp.float32))
    nsb = imp_t.shape[0]
    blk = lax.broadcasted_iota(jnp.int32, (nsb, 1), 0)
    cur = (i * nq + lax.broadcasted_iota(jnp.int32, (1, nq), 1)) // SEL_BLOCK
    imp_t = jnp.where(blk > cur, NEG, imp_t)
    imp_t = jnp.where((blk == cur) | (blk == 0), BIG, imp_t)
    sel_t = _top_blocks_t(imp_t).astype(jnp.bfloat16)

    def slc_step(kt, carry):
        m, l, acc = carry
        k0 = pl.multiple_of(kt * SLC_TK, SLC_TK)
        s = lax.dot_general(q2, ks_ref[0, pl.ds(k0, SLC_TK), :], nt,
                            preferred_element_type=jnp.float32).reshape(HPG, nq, SLC_TK)
        dist = qpos - (k0 + lax.broadcasted_iota(jnp.int32, (1, SLC_TK), 1))
        picked = lax.dot_general(sel_t, ex_ref[:, pl.ds(k0, SLC_TK)], (((0,), (0,)), ((), ())),
                                 preferred_element_type=jnp.float32)
        valid = ((dist >= 0) & (picked > 0.5))[None]
        s = jnp.where(valid, s - slopes * dist.astype(jnp.float32)[None], NEG)
        m_new = jnp.maximum(m, jnp.max(s, axis=-1, keepdims=True))
        a = jnp.exp(m - m_new)
        e = jnp.exp(s - m_new)
        l = a * l + jnp.sum(e, axis=-1, keepdims=True)
        pv = jnp.dot(e.reshape(rows, SLC_TK).astype(jnp.bfloat16), vs_ref[0, pl.ds(k0, SLC_TK), :],
                     preferred_element_type=jnp.float32).reshape(HPG, nq, HEAD_DIM)
        return m_new, l, a * acc + pv

    init = (jnp.full((HPG, nq, 1), NEG, jnp.float32), jnp.zeros((HPG, nq, 1), jnp.float32),
            jnp.zeros((HPG, nq, HEAD_DIM), jnp.float32))
    n_kt = (i * nq) // SLC_TK + 1
    _, l, acc = lax.fori_loop(0, n_kt, slc_step, init)
    o_slc = acc / jnp.maximum(l, 1e-30)

    w0 = pl.multiple_of(jnp.maximum(i - WINDOW // QBLK, 0) * nq, nq)
    s = lax.dot_general(q2, kw_ref[0, pl.ds(w0, WIN_KEYS), :], nt,
                        preferred_element_type=jnp.float32).reshape(HPG, nq, WIN_KEYS)
    dist = qpos - (w0 + lax.broadcasted_iota(jnp.int32, (1, WIN_KEYS), 1))
    valid = ((dist >= 0) & (dist < WINDOW))[None]
    pw = _masked_softmax(s - slopes * dist.astype(jnp.float32)[None], valid)
    o_win = jnp.dot(pw.reshape(rows, WIN_KEYS).astype(jnp.bfloat16), vw_ref[0, pl.ds(w0, WIN_KEYS), :],
                    preferred_element_type=jnp.float32).reshape(HPG, nq, HEAD_DIM)

    gates = _sigmoid(ng_ref[0, 0])
    o_ref[...] = jnp.concatenate(
        [gates[:, h:h + 1] * o_cmp[h] + gates[:, HPG + h:HPG + h + 1] * o_slc[h]
         + gates[:, 2 * HPG + h:2 * HPG + h + 1] * o_win[h] for h in range(HPG)], axis=-1).astype(o_ref.dtype)


def _alibi_slopes():
    return jnp.asarray(2.0 ** (-8.0 * np.arange(1, N_HEADS + 1) / N_HEADS), jnp.float32).reshape(N_KV, HPG, 1, 1)


def _nsa_prompt(q_h, kcvc, kv_h, ng_t, n_cmp):
    bsz, _, t, _ = ng_t.shape
    nqb = t // QBLK
    n_ch = kcvc.shape[2]
    nsb = t // SEL_BLOCK
    st = np.arange(n_ch) * CMP_STRIDE
    bs = np.arange(nsb) * SEL_BLOCK
    overlap = ((st[:, None] <= bs[None, :] + SEL_BLOCK - 1) & (st[:, None] + CMP_LEN - 1 >= bs[None, :])
               & (np.arange(n_ch)[:, None] < n_cmp)).astype(np.float32)
    expand = (np.arange(t)[None, :] // SEL_BLOCK == np.arange(nsb)[:, None]).astype(np.float32)

    def kv_spec(slot):
        return pl.BlockSpec((1, t, HEAD_DIM), lambda b, g, i, slot=slot: (slot * N_KV + g, b, 0))

    return pl.pallas_call(
        functools.partial(_nsa_prompt_body, n_cmp=n_cmp),
        grid=(bsz, N_KV, nqb),
        in_specs=[pl.BlockSpec((HPG, QBLK, HEAD_DIM), lambda b, g, i: (g, b * nqb + i, 0)),
                  pl.BlockSpec((1, 1, n_ch, HEAD_DIM), lambda b, g, i: (b, g, 0, 0)),
                  pl.BlockSpec((1, 1, n_ch, HEAD_DIM), lambda b, g, i: (b, N_KV + g, 0, 0)),
                  kv_spec(2), kv_spec(3), kv_spec(4), kv_spec(5),
                  pl.BlockSpec((1, 1, QBLK, 3 * HPG), lambda b, g, i: (b, g, i, 0)),
                  pl.BlockSpec((nsb, n_ch), lambda b, g, i: (0, 0)),
                  pl.BlockSpec((nsb, t), lambda b, g, i: (0, 0)),
                  pl.BlockSpec((1, HPG, 1, 1), lambda b, g, i: (g, 0, 0, 0))],
        out_specs=pl.BlockSpec((QBLK, HPG * HEAD_DIM), lambda b, g, i: (b * nqb + i, g)),
        out_shape=jax.ShapeDtypeStruct((bsz * t, NSA_WIDTH), jnp.bfloat16),
        compiler_params=pltpu.CompilerParams(
            dimension_semantics=("parallel", "parallel", "arbitrary"), vmem_limit_bytes=VMEM_LIMIT),
        name="nsa_prompt",
    )(q_h, kcvc, kcvc, kv_h, kv_h, kv_h, kv_h, ng_t,
      jnp.asarray(overlap.T, jnp.bfloat16), jnp.asarray(expand, jnp.bfloat16), _alibi_slopes())


HALF_ROWS = 2 * N_KV * HEAD_DIM
GD = N_KV * HEAD_DIM


def _fetch_pages(pt_ref, cache_ref, sem, row0, per_step, dst_of):
    b = pl.program_id(0)
    slot = b % 2
    n_pages = pt_ref.shape[1]

    def page_copy(flat_page, sl, k):
        phys = pt_ref[flat_page // n_pages, flat_page % n_pages]
        return pltpu.make_async_copy(cache_ref.at[phys, pl.ds(row0, HALF_ROWS), :], dst_of(sl, k), sem.at[sl])

    def start_all(step, sl):
        def one(k, carry):
            page_copy(step * per_step + k, sl, k).start()
            return carry
        lax.fori_loop(0, per_step, one, 0)

    @pl.when(b == 0)
    def _():
        start_all(0, 0)

    @pl.when(b + 1 < pl.num_programs(0))
    def _():
        start_all(b + 1, 1 - slot)

    def wait_one(k, carry):
        page_copy(b * per_step + k, slot, k).wait()
        return carry
    lax.fori_loop(0, per_step, wait_one, 0)
    return slot


def _cmp_uv_sample_body(pt_ref, cache_ref, w_ref, uv_ref, buf, sem, xt):
    per_step = buf.shape[1]
    slot = _fetch_pages(pt_ref, cache_ref, sem, 0, per_step, lambda sl, k: buf.at[sl, k])
    n_cb = HALF_ROWS // LANES

    eye = (lax.broadcasted_iota(jnp.int32, (LANES, LANES), 0)
           == lax.broadcasted_iota(jnp.int32, (LANES, LANES), 1)).astype(jnp.bfloat16)

    def to_rows(k, carry):
        for cb in range(n_cb):
            blk = buf[slot, k, cb * LANES:(cb + 1) * LANES, :].astype(jnp.bfloat16)
            xt[k, cb] = lax.dot_general(eye, blk, (((1,), (1,)), ((), ())), preferred_element_type=jnp.float32)
        return carry
    lax.fori_loop(0, per_step, to_rows, 0, unroll=8)

    per_page = PAGE_SIZE // CMP_STRIDE
    n = per_step * per_page

    def tap(cgp, j):
        return xt[:, cgp, pl.ds(j, per_page, stride=CMP_STRIDE), :].reshape(n, LANES).astype(jnp.bfloat16)

    for cgp in range(n_cb):
        uv_ref[0, :, cgp * 512:(cgp + 1) * 512] = _uv_taps(functools.partial(tap, cgp), w_ref, cgp // 2, n)


UV_PAGES = 32


def _cmp_uv_sample(page_table, cache_t, w1p):
    n_seq, n_pages = page_table.shape
    assert n_pages % UV_PAGES == 0
    steps = n_pages // UV_PAGES
    per_page = PAGE_SIZE // CMP_STRIDE
    return pl.pallas_call(
        _cmp_uv_sample_body,
        grid_spec=pltpu.PrefetchScalarGridSpec(
            num_scalar_prefetch=1, grid=(n_seq * steps,),
            in_specs=[pl.BlockSpec(memory_space=pl.ANY),
                      pl.BlockSpec(w1p.shape, lambda b, pt: (0, 0, 0, 0))],
            out_specs=pl.BlockSpec((1, UV_PAGES * per_page, UV_COLS), lambda b, pt: (b // steps, b % steps, 0)),
            scratch_shapes=[pltpu.VMEM((2, UV_PAGES, HALF_ROWS, PAGE_SIZE), jnp.float32),
                            pltpu.SemaphoreType.DMA((2,)),
                            pltpu.VMEM((UV_PAGES, HALF_ROWS // LANES, PAGE_SIZE, LANES), jnp.float32)]),
        out_shape=jax.ShapeDtypeStruct((n_seq, n_pages * per_page, UV_COLS), jnp.float32),
        compiler_params=pltpu.CompilerParams(dimension_semantics=("arbitrary",),
                                             vmem_limit_bytes=VMEM_LIMIT),
        name="cmp_uv_sample",
    )(page_table, cache_t, w1p)


def _group_diag(x, rows_per_group):
    grp = lax.broadcasted_iota(jnp.int32, (x.shape[0], 1), 0) // rows_per_group
    out = jnp.zeros((x.shape[0], HEAD_DIM), x.dtype)
    for g in range(N_KV):
        out = out + jnp.where(grp == g, x[:, g * HEAD_DIM:(g + 1) * HEAD_DIM], 0.0)
    return out


def _nsa_sample_body(pt_ref, cache_ref, q_ref, kcvc_ref, new_ref, win_ref, ng_ref, ov_ref, same_q_ref, sl_ref,
                     o_ref, buf, sem, *, n_cmp, t_new):
    past_len = buf.shape[2]
    slot = _fetch_pages(pt_ref, cache_ref, sem, HALF_ROWS, past_len // PAGE_SIZE,
                        lambda sl, k: buf.at[sl, :, pl.ds(pl.multiple_of(k * PAGE_SIZE, PAGE_SIZE), PAGE_SIZE)])
    rows = q_ref.shape[1]
    rpg = rows // N_KV
    nt = (((1,), (1,)), ((), ()))
    qbd = q_ref[0] * SCALE
    slope = sl_ref[...]
    row = lax.broadcasted_iota(jnp.int32, (rows, 1), 0)
    qtok = row % t_new
    qpos = past_len + qtok
    new_rows = new_ref.shape[2]
    new_idx = lax.broadcasted_iota(jnp.int32, (1, new_rows), 1)
    new_valid = (new_idx <= qtok) & (new_idx < t_new)
    new_bias = slope * (qtok - new_idx).astype(jnp.float32)

    n_ch = kcvc_ref.shape[2]
    s = lax.dot_general(qbd, kcvc_ref[0, 0], nt, preferred_element_type=jnp.float32)
    cidx = lax.broadcasted_iota(jnp.int32, (1, n_ch), 1)
    dist_c = qpos - (cidx * CMP_STRIDE + (CMP_LEN - 1))
    p = _masked_softmax(s - slope * dist_c.astype(jnp.float32), (dist_c >= 0) & (cidx < n_cmp))
    o_cmp = _group_diag(jnp.dot(p.astype(jnp.bfloat16), kcvc_ref[0, 1], preferred_element_type=jnp.float32), rpg)

    p_hi, p_lo = _split_bf16(p)
    psum = (jnp.dot(same_q_ref[...], p_hi, preferred_element_type=jnp.float32)
            + jnp.dot(same_q_ref[...], p_lo, preferred_element_type=jnp.float32))
    ps_hi, ps_lo = _split_bf16(psum)
    imp = (jnp.dot(ps_hi, ov_ref[...], preferred_element_type=jnp.float32)
           + jnp.dot(ps_lo, ov_ref[...], preferred_element_type=jnp.float32))
    n_lanes = imp.shape[-1]
    blk = lax.broadcasted_iota(jnp.int32, (1, n_lanes), 1)
    cur = qpos // SEL_BLOCK
    imp = jnp.where(blk > cur, -jnp.inf, imp)
    imp = jnp.where((blk == cur) | (blk == 0), BIG, imp)
    chosen = []
    for _ in range(N_SEL):
        m = jnp.max(imp, axis=-1, keepdims=True)
        idx = jnp.min(jnp.where(imp == m, blk, n_lanes), axis=-1, keepdims=True)
        chosen.append(idx)
        imp = jnp.where(blk == idx, -jnp.inf, imp)

    s_all = jnp.dot(qbd, buf[slot, 0:GD, :].astype(jnp.bfloat16), preferred_element_type=jnp.float32)
    kpos = lax.broadcasted_iota(jnp.int32, (1, past_len), 1)
    kblk = kpos // SEL_BLOCK
    picked = kblk == chosen[0]
    for idx in chosen[1:]:
        picked = picked | (kblk == idx)
    s_all = jnp.where(picked, s_all - slope * (qpos - kpos).astype(jnp.float32), NEG)
    s_new = lax.dot_general(qbd, new_ref[0, 2], nt, preferred_element_type=jnp.float32)
    s_new = jnp.where(new_valid, s_new - new_bias, NEG)
    m = jnp.maximum(jnp.max(s_all, axis=-1, keepdims=True), jnp.max(s_new, axis=-1, keepdims=True))
    e_all = jnp.exp(s_all - m)
    e_new = jnp.where(new_valid, jnp.exp(s_new - m), 0.0)
    l = jnp.sum(e_all, axis=-1, keepdims=True) + jnp.sum(e_new, axis=-1, keepdims=True)
    acc = (lax.dot_general(e_all.astype(jnp.bfloat16), buf[slot, GD:2 * GD, :].astype(jnp.bfloat16), nt,
                           preferred_element_type=jnp.float32)
           + jnp.dot(e_new.astype(jnp.bfloat16), new_ref[0, 3], preferred_element_type=jnp.float32))
    o_slc = _group_diag(acc, rpg) / jnp.maximum(l, 1e-30)

    w_buf = win_ref.shape[2]
    wdist = qpos - (past_len - w_buf + lax.broadcasted_iota(jnp.int32, (1, w_buf), 1))
    valid_w = (wdist >= 0) & (wdist < WINDOW)
    s_w = jnp.dot(qbd, win_ref[0, 0:GD, :].astype(jnp.bfloat16), preferred_element_type=jnp.float32)
    s_w = jnp.where(valid_w, s_w - slope * wdist.astype(jnp.float32), NEG)
    s_wn = lax.dot_general(qbd, new_ref[0, 4], nt, preferred_element_type=jnp.float32)
    s_wn = jnp.where(new_valid, s_wn - new_bias, NEG)
    m = jnp.maximum(jnp.max(s_w, axis=-1, keepdims=True), jnp.max(s_wn, axis=-1, keepdims=True))
    e_w = jnp.where(valid_w, jnp.exp(s_w - m), 0.0)
    e_wn = jnp.where(new_valid, jnp.exp(s_wn - m), 0.0)
    acc = (lax.dot_general(e_w.astype(jnp.bfloat16), win_ref[0, GD:2 * GD, :].astype(jnp.bfloat16), nt,
                           preferred_element_type=jnp.float32)
           + jnp.dot(e_wn.astype(jnp.bfloat16), new_ref[0, 5], preferred_element_type=jnp.float32))
    l = jnp.sum(e_w, axis=-1, keepdims=True) + jnp.sum(e_wn, axis=-1, keepdims=True)
    o_win = _group_diag(acc, rpg) / jnp.maximum(l, 1e-30)

    gates = _sigmoid(ng_ref[0])
    o_ref[0] = (gates[:, 0:1] * o_cmp + gates[:, 1:2] * o_slc + gates[:, 2:3] * o_win).astype(o_ref.dtype)


def _nsa_sample(page_table, cache_t, q_bd, kcvc, new_kv, win_t, ng_r, n_cmp, t_new):
    n_seq, n_pages = page_table.shape
    past_len = n_pages * PAGE_SIZE
    n_ch = kcvc.shape[2]
    rows = q_bd.shape[1]
    rpg = rows // N_KV
    nsb = -(-(past_len + t_new) // SEL_BLOCK)
    n_lanes = -(-nsb // LANES) * LANES
    st = np.arange(n_ch) * CMP_STRIDE
    bs = np.arange(n_lanes) * SEL_BLOCK
    overlap = ((st[:, None] <= bs[None, :] + SEL_BLOCK - 1) & (st[:, None] + CMP_LEN - 1 >= bs[None, :])
               & (np.arange(n_ch)[:, None] < n_cmp) & (np.arange(n_lanes)[None, :] < nsb)).astype(np.float32)
    r = np.arange(rows)
    same_q = ((r[:, None] // rpg == r[None, :] // rpg) & (r[:, None] % t_new == r[None, :] % t_new)
              ).astype(np.float32)
    slopes = jnp.repeat(_alibi_slopes().reshape(N_HEADS), t_new).reshape(rows, 1)
    w_buf = win_t.shape[2]
    return pl.pallas_call(
        functools.partial(_nsa_sample_body, n_cmp=n_cmp, t_new=t_new),
        grid_spec=pltpu.PrefetchScalarGridSpec(
            num_scalar_prefetch=1, grid=(n_seq,),
            in_specs=[pl.BlockSpec(memory_space=pl.ANY),
                      pl.BlockSpec((1, rows, GD), lambda b, pt: (b, 0, 0)),
                      pl.BlockSpec((1, 2, n_ch, GD), lambda b, pt: (b, 0, 0, 0)),
                      pl.BlockSpec((1, 6, new_kv.shape[2], GD), lambda b, pt: (b, 0, 0, 0)),
                      pl.BlockSpec((1, HALF_ROWS, w_buf), lambda b, pt: (b, 0, 0)),
                      pl.BlockSpec((1, rows, 3), lambda b, pt: (b, 0, 0)),
                      pl.BlockSpec((n_ch, n_lanes), lambda b, pt: (0, 0)),
                      pl.BlockSpec((rows, rows), lambda b, pt: (0, 0)),
                      pl.BlockSpec((rows, 1), lambda b, pt: (0, 0))],
            out_specs=pl.BlockSpec((1, rows, HEAD_DIM), lambda b, pt: (b, 0, 0)),
            scratch_shapes=[pltpu.VMEM((2, HALF_ROWS, past_len), jnp.float32),
                            pltpu.SemaphoreType.DMA((2,))]),
        out_shape=jax.ShapeDtypeStruct((n_seq, rows, HEAD_DIM), jnp.bfloat16),
        compiler_params=pltpu.CompilerParams(dimension_semantics=("arbitrary",),
                                             vmem_limit_bytes=VMEM_LIMIT),
        name="nsa_sample",
    )(page_table, cache_t, q_bd, kcvc, new_kv, win_t, ng_r,
      jnp.asarray(overlap, jnp.bfloat16), jnp.asarray(same_q, jnp.bfloat16), slopes)


def _mix_body(a_ref, b_ref, mga_ref, mgb_ref, wpa_ref, wpb_ref, u_ref):
    a = jnp.dot(a_ref[...], wpa_ref[...], preferred_element_type=jnp.float32)
    b = jnp.dot(b_ref[...], wpb_ref[...], preferred_element_type=jnp.float32)
    u_ref[...] = (_sigmoid(mga_ref[...]) * a + _sigmoid(mgb_ref[...]) * b).astype(u_ref.dtype)


def _mix(o_nsa, o_hg, mg, w_pa_bf, w_pb_bf, tm):
    n = o_nsa.shape[0]
    const = lambda i: (0, 0)
    return pl.pallas_call(
        _mix_body,
        grid=(n // tm,),
        in_specs=[pl.BlockSpec((tm, NSA_WIDTH), lambda i: (i, 0)),
                  pl.BlockSpec((tm, HG_WIDTH), lambda i: (i, 0)),
                  pl.BlockSpec((tm, D_MODEL), lambda i: (i, 0)),
                  pl.BlockSpec((tm, D_MODEL), lambda i: (i, 1)),
                  pl.BlockSpec((NSA_WIDTH, D_MODEL), const, pipeline_mode=pl.Buffered(1)),
                  pl.BlockSpec((HG_WIDTH, D_MODEL), const, pipeline_mode=pl.Buffered(1))],
        out_specs=pl.BlockSpec((tm, D_MODEL), lambda i: (i, 0)),
        out_shape=jax.ShapeDtypeStruct((n, D_MODEL), jnp.bfloat16),
        compiler_params=pltpu.CompilerParams(dimension_semantics=("parallel",),
                                             vmem_limit_bytes=VMEM_LIMIT),
        name="tail_mix",
    )(o_nsa, o_hg, mg, mg, w_pa_bf, w_pb_bf)


def _layer_norm(z, g, b):
    mu = jnp.mean(z, axis=-1, keepdims=True)
    zc = z - mu
    var = jnp.mean(zc * zc, axis=-1, keepdims=True)
    return zc * lax.rsqrt(var + LN_EPS) * g + b


CH = D_MODEL // LANES


def _store_chunked(ref, val):
    tm = val.shape[0]
    for k in range(CH):
        ref[pl.ds(k, tm, stride=CH), :] = val[:, k * LANES:(k + 1) * LANES]


def _load_chunked(ref, tm, lead=()):
    return jnp.concatenate([ref[lead + (pl.ds(k, tm, stride=CH), slice(None))] for k in range(CH)], axis=1)


def _ln1_body(u_ref, x_ref, wout_ref, g_ref, b_ref, wr_hi_ref, wr_lo_ref, br_ref, h_ref, lg_ref):
    y = jnp.dot(u_ref[...], wout_ref[...], preferred_element_type=jnp.float32)
    h = _layer_norm(DN_ALPHA * x_ref[...] + y, g_ref[...], b_ref[...])
    _store_chunked(h_ref, h)
    h_hi = h.astype(jnp.bfloat16)
    h_lo = (h - h_hi.astype(jnp.float32)).astype(jnp.bfloat16)
    lg = jnp.dot(h_hi, wr_hi_ref[...], preferred_element_type=jnp.float32)
    lg = lg + jnp.dot(h_lo, wr_hi_ref[...], preferred_element_type=jnp.float32)
    lg = lg + jnp.dot(h_hi, wr_lo_ref[...], preferred_element_type=jnp.float32)
    lg_ref[...] = lg + br_ref[...]


def _ln1(u, x_all, w_out_bf, g, b, wr_hi, wr_lo, br, tm):
    n = u.shape[0]
    const = lambda i: (0, 0)
    return pl.pallas_call(
        _ln1_body,
        grid=(n // tm,),
        in_specs=[pl.BlockSpec((tm, D_MODEL), lambda i: (i, 0)),
                  pl.BlockSpec((tm, D_MODEL), lambda i: (i, 0)),
                  pl.BlockSpec((D_MODEL, D_MODEL), const, pipeline_mode=pl.Buffered(1)),
                  pl.BlockSpec((1, D_MODEL), const),
                  pl.BlockSpec((1, D_MODEL), const),
                  pl.BlockSpec((D_MODEL, LANES), const),
                  pl.BlockSpec((D_MODEL, LANES), const),
                  pl.BlockSpec((1, LANES), const)],
        out_specs=[pl.BlockSpec((tm * CH, LANES), lambda i: (i, 0)),
                   pl.BlockSpec((tm, LANES), lambda i: (i, 0))],
        out_shape=[jax.ShapeDtypeStruct((n * CH, LANES), jnp.float32),
                   jax.ShapeDtypeStruct((n, LANES), jnp.float32)],
        compiler_params=pltpu.CompilerParams(dimension_semantics=("parallel",),
                                             vmem_limit_bytes=VMEM_LIMIT),
        name="tail_ln1",
    )(u, x_all, w_out_bf, g.reshape(1, -1), b.reshape(1, -1), wr_hi, wr_lo, br)


LG0 = N_GROUPS


def _route_body(lg_ref, tri_ref, out_ref, cnt_ref, carry_scr):
    @pl.when(pl.program_id(0) == 0)
    def _():
        carry_scr[...] = jnp.zeros_like(carry_scr)

    lg = lg_ref[...]
    tm = lg.shape[0]
    lane = lax.broadcasted_iota(jnp.int32, lg.shape, 1)
    is_g = lane < N_GROUPS
    gl = jnp.where(is_g, lg, NEG)
    gmax = jnp.max(gl, axis=-1, keepdims=True)
    grp = jnp.min(jnp.where(gl == gmax, lane, LANES), axis=-1, keepdims=True)
    g_w = 1.0 / jnp.sum(jnp.where(is_g, jnp.exp(lg - gmax), 0.0), axis=-1, keepdims=True)
    lo = LG0 + grp * EXP_PER_GROUP
    el = jnp.where((lane >= lo) & (lane < lo + EXP_PER_GROUP), lg, NEG)
    v1 = jnp.max(el, axis=-1, keepdims=True)
    i1 = jnp.min(jnp.where(el == v1, lane, LANES), axis=-1, keepdims=True)
    el2 = jnp.where(lane == i1, NEG, el)
    v2 = jnp.max(el2, axis=-1, keepdims=True)
    i2 = jnp.min(jnp.where(el2 == v2, lane, LANES), axis=-1, keepdims=True)
    e21 = jnp.exp(v2 - v1)
    w1 = g_w / (1.0 + e21)
    w2 = g_w * e21 / (1.0 + e21)
    hit1 = lane == i1
    hit2 = lane == i2
    onehot = jnp.where(hit1 | hit2, 1.0, 0.0)
    incl = jnp.dot(tri_ref[...], onehot.astype(jnp.bfloat16), preferred_element_type=jnp.float32)
    carry = carry_scr[...]
    before = incl - onehot + carry
    r1 = jnp.sum(jnp.where(hit1, before, 0.0), axis=-1, keepdims=True)
    r2 = jnp.sum(jnp.where(hit2, before, 0.0), axis=-1, keepdims=True)
    carry = carry + incl[tm - 1:tm, :]
    carry_scr[...] = carry
    cnt_ref[...] = carry
    out = jnp.where(lane == 0, (i1 - LG0).astype(jnp.float32), 0.0)
    out = jnp.where(lane == 1, (i2 - LG0).astype(jnp.float32), out)
    out = jnp.where(lane == 2, w1, out)
    out = jnp.where(lane == 3, w2, out)
    out = jnp.where(lane == 4, r1, out)
    out = jnp.where(lane == 5, r2, out)
    out_ref[...] = out


def _route(lg, tm):
    n = lg.shape[0]
    tri = (np.arange(tm)[:, None] >= np.arange(tm)[None, :]).astype(np.float32)
    return pl.pallas_call(
        _route_body,
        grid=(n // tm,),
        in_specs=[pl.BlockSpec((tm, LANES), lambda i: (i, 0)),
                  pl.BlockSpec((tm, tm), lambda i: (0, 0))],
        out_specs=[pl.BlockSpec((tm, LANES), lambda i: (i, 0)),
                   pl.BlockSpec((1, LANES), lambda i: (0, 0))],
        out_shape=[jax.ShapeDtypeStruct((n, LANES), jnp.float32),
                   jax.ShapeDtypeStruct((1, LANES), jnp.float32)],
        scratch_shapes=[pltpu.VMEM((1, LANES), jnp.float32)],
        compiler_params=pltpu.CompilerParams(dimension_semantics=("arbitrary",)),
        name="moe_route",
    )(lg, jnp.asarray(tri, jnp.bfloat16))


def _dispatch_body(pos_ref, h_ref, xs_in_ref, xs_ref, sem):
    del xs_in_ref
    tm = h_ref.shape[0] // CH
    base = pl.program_id(0) * tm

    def issue(t, carry):
        src = pl.multiple_of(t * CH, CH)
        for slot in range(2):
            dst = pl.multiple_of(pos_ref[2 * (base + t) + slot] * CH, CH)
            pltpu.make_async_copy(h_ref.at[pl.ds(src, CH)], xs_ref.at[pl.ds(dst, CH)], sem).start()
        return carry

    lax.fori_loop(0, tm, issue, 0)
    for _ in range(2):
        pltpu.make_async_copy(h_ref, xs_ref.at[pl.ds(0, tm * CH)], sem).wait()


DISPATCH_TM = 128


def _dispatch(pos_flat, h_c, n_slots):
    zeros = jnp.zeros((n_slots * CH, LANES), h_c.dtype)
    tm = DISPATCH_TM
    return pl.pallas_call(
        _dispatch_body,
        grid_spec=pltpu.PrefetchScalarGridSpec(
            num_scalar_prefetch=1, grid=(h_c.shape[0] // (tm * CH),),
            in_specs=[pl.BlockSpec((tm * CH, LANES), lambda i, pos: (i, 0)),
                      pl.BlockSpec(memory_space=pl.ANY)],
            out_specs=pl.BlockSpec(memory_space=pl.ANY),
            scratch_shapes=[pltpu.SemaphoreType.DMA(())]),
        out_shape=jax.ShapeDtypeStruct(zeros.shape, h_c.dtype),
        input_output_aliases={2: 0},
        compiler_params=pltpu.CompilerParams(dimension_semantics=("arbitrary",)),
        name="moe_dispatch",
    )(pos_flat, h_c, zeros)


def _ffn_body(te_ref, nu_ref, x_ref, wg_ref, wu_ref, wd_ref, y_ref, wg_bf, wu_bf, wd_bf):
    i = pl.program_id(0)
    prev = te_ref[jnp.maximum(i - 1, 0)]

    @pl.when((i == 0) | (te_ref[i] != prev))
    def _():
        wg_bf[...] = wg_ref[0].astype(jnp.bfloat16)
        wu_bf[...] = wu_ref[0].astype(jnp.bfloat16)
        wd_bf[...] = wd_ref[0].astype(jnp.bfloat16)

    @pl.when(i < nu_ref[0])
    def _():
        x = _load_chunked(x_ref, x_ref.shape[0] // CH).astype(jnp.bfloat16)
        g = jnp.dot(x, wg_bf[...], preferred_element_type=jnp.float32)
        u = jnp.dot(x, wu_bf[...], preferred_element_type=jnp.float32)
        hid = (g * _sigmoid(g) * u).astype(jnp.bfloat16)
        _store_chunked(y_ref, jnp.dot(hid, wd_bf[...], preferred_element_type=jnp.float32))

    @pl.when(i >= nu_ref[0])
    def _():
        y_ref[...] = jnp.zeros_like(y_ref)


def _ffn(tile_expert, n_used, xs_c, w_gate, w_up, w_down, tm):
    d = D_MODEL
    nt = xs_c.shape[0] // (tm * CH)
    return pl.pallas_call(
        _ffn_body,
        grid_spec=pltpu.PrefetchScalarGridSpec(
            num_scalar_prefetch=2, grid=(nt,),
            in_specs=[pl.BlockSpec((tm * CH, LANES), lambda i, te, nu: (i, 0)),
                      pl.BlockSpec((1, d, D_EXPERT), lambda i, te, nu: (te[i], 0, 0)),
                      pl.BlockSpec((1, d, D_EXPERT), lambda i, te, nu: (te[i], 0, 0)),
                      pl.BlockSpec((1, D_EXPERT, d), lambda i, te, nu: (te[i], 0, 0))],
            out_specs=pl.BlockSpec((tm * CH, LANES), lambda i, te, nu: (i, 0)),
            scratch_shapes=[pltpu.VMEM((d, D_EXPERT), jnp.bfloat16),
                            pltpu.VMEM((d, D_EXPERT), jnp.bfloat16),
                            pltpu.VMEM((D_EXPERT, d), jnp.bfloat16)]),
        out_shape=jax.ShapeDtypeStruct(xs_c.shape, jnp.float32),
        compiler_params=pltpu.CompilerParams(dimension_semantics=("arbitrary",),
                                             vmem_limit_bytes=VMEM_LIMIT),
        name="moe_ffn",
    )(tile_expert, n_used, xs_c, w_gate, w_up, w_down)


def _combine_body(pos_ref, h_ref, rw_ref, g_ref, b_ref, y_ref, out_ref, buf, sem):
    tm = out_ref.shape[0]
    base = pl.program_id(0) * tm

    def issue(t, carry):
        for slot in range(2):
            src = pl.multiple_of(pos_ref[2 * (base + t) + slot] * CH, CH)
            dst = pl.multiple_of(t * CH, CH)
            pltpu.make_async_copy(y_ref.at[pl.ds(src, CH)], buf.at[slot, pl.ds(dst, CH)], sem.at[slot]).start()
        return carry

    lax.fori_loop(0, tm, issue, 0)
    for slot in range(2):
        pltpu.make_async_copy(y_ref.at[pl.ds(0, tm * CH)], buf.at[slot], sem.at[slot]).wait()
    rw = rw_ref[...]
    z = rw[:, 2:3] * _load_chunked(buf, tm, (0,)) + rw[:, 3:4] * _load_chunked(buf, tm, (1,))
    out_ref[...] = _layer_norm(DN_ALPHA * _load_chunked(h_ref, tm) + z, g_ref[...], b_ref[...])


def _combine(pos_flat, h_c, route_out, g, b, y_c, tm):
    n, d = h_c.shape[0] // CH, D_MODEL
    return pl.pallas_call(
        _combine_body,
        grid_spec=pltpu.PrefetchScalarGridSpec(
            num_scalar_prefetch=1, grid=(n // tm,),
            in_specs=[pl.BlockSpec((tm * CH, LANES), lambda i, pos: (i, 0)),
                      pl.BlockSpec((tm, LANES), lambda i, pos: (i, 0)),
                      pl.BlockSpec((1, d), lambda i, pos: (0, 0)),
                      pl.BlockSpec((1, d), lambda i, pos: (0, 0)),
                      pl.BlockSpec(memory_space=pl.ANY)],
            out_specs=pl.BlockSpec((tm, d), lambda i, pos: (i, 0)),
            scratch_shapes=[pltpu.VMEM((2, tm * CH, LANES), jnp.float32),
                            pltpu.SemaphoreType.DMA((2,))]),
        out_shape=jax.ShapeDtypeStruct((n, d), jnp.float32),
        compiler_params=pltpu.CompilerParams(dimension_semantics=("arbitrary",),
                                             vmem_limit_bytes=VMEM_LIMIT),
        name="moe_combine",
    )(pos_flat, h_c, route_out, g.reshape(1, -1), b.reshape(1, -1), y_c)


FFN_TM = 256
PROJ_TM = 1664
TAIL_TM = 320


def _moe_and_norm(h_c, lg, w_gate, w_up, w_down, ln2_g, ln2_b):
    n = lg.shape[0]
    route_out, cnt = _route(lg, 640)
    eid = route_out[:, 0:2].astype(jnp.int32)
    rank = route_out[:, 4:6].astype(jnp.int32)
    counts = cnt[0, LG0:LG0 + N_EXPERTS].astype(jnp.int32)
    tiles_per = (counts + FFN_TM - 1) // FFN_TM
    tile_end = jnp.cumsum(tiles_per)
    row_start = (tile_end - tiles_per) * FFN_TM
    pos_flat = (row_start[eid] + rank).reshape(-1)
    nt = (2 * n) // FFN_TM + N_EXPERTS
    n_used = tile_end[-1]
    tile_ids = jnp.minimum(jnp.arange(nt, dtype=jnp.int32), n_used - 1)
    tile_expert = jnp.sum((tile_end[None, :] <= tile_ids[:, None]).astype(jnp.int32), axis=1)
    xs_c = _dispatch(pos_flat, h_c, nt * FFN_TM)
    y_c = _ffn(tile_expert, n_used.reshape(1).astype(jnp.int32), xs_c, w_gate, w_up, w_down, FFN_TM)
    return _combine(pos_flat, h_c, route_out, ln2_g, ln2_b, y_c, DISPATCH_TM)


def kernel(x_prompt, x_sample, cache_kv, cache_win, state_hgrn, page_table, w_in, b_in, w_cmp1, w_cmp2, cmp_pe,
           hgrn_gamma, hgrn_norm, w_pa, w_pb, w_out, ln1_g, ln1_b, w_rg, b_rg, w_re, b_re, w_gate, w_up, w_down,
           ln2_g, ln2_b):
    n_p = x_prompt.shape[0] * x_prompt.shape[1]
    n_s = x_sample.shape[0] * x_sample.shape[1]
    x_all = jnp.concatenate([x_prompt.reshape(n_p, D_MODEL), x_sample.reshape(n_s, D_MODEL)], axis=0)
    x_bf = x_all.astype(jnp.bfloat16)
    bsz, seq = x_prompt.shape[:2]
    n_seq, t_new = x_sample.shape[:2]
    n_pages = page_table.shape[1]
    past_len = n_pages * PAGE_SIZE
    w = w_in[0]
    b = b_in[0]

    def seg(lo, hi, outs, tn, name, pad_to=None):
        ws, bs = w[:, lo:hi], b[lo:hi]
        if pad_to is not None:
            ws = jnp.pad(ws, ((0, 0), (0, pad_to - (hi - lo))))
            bs = jnp.pad(bs, (0, pad_to - (hi - lo)))
        return _proj(x_bf, ws.astype(jnp.bfloat16), bs, outs, PROJ_TM, tn, name)

    q_h, = seg(OFF_Q, OFF_KV, [(jnp.bfloat16, True)], 512, "proj_q")
    kv32, kv_h = seg(OFF_KV, OFF_NG, [(jnp.float32, False), (jnp.bfloat16, True)], 512, "proj_kv")
    ng, = seg(OFF_NG, OFF_H4, [(jnp.float32, False)], LANES, "proj_ng", pad_to=LANES)
    h4, = seg(OFF_H4, OFF_MG, [(jnp.float32, False)], 512, "proj_h4")
    mg, = seg(OFF_MG, PROJ_COLS, [(jnp.float32, False)], 512, "proj_mg")

    kv_p = kv32[:n_p].reshape(1, bsz, seq, 6, N_KV, HEAD_DIM)
    kv_s = kv32[n_p:].reshape(1, n_seq, t_new, 6, N_KV, HEAD_DIM)
    new_kv_prompt = kv_p[:, :, :, :KV_SLOTS]
    new_kv_sample = kv_s[:, :, :, :KV_SLOTS]
    new_win_prompt = kv_p[:, :, seq - min(WINDOW, seq):, 4:6]
    win_all = jnp.concatenate([cache_win, kv_s[:, :, :, 4:6].astype(cache_win.dtype)], axis=2)
    new_win_sample = win_all[:, :, win_all.shape[2] - min(WINDOW, win_all.shape[2]):]

    w1p = _cmp_w1_pairs(w_cmp1[0])
    n_cmp_p = (seq - CMP_LEN) // CMP_STRIDE + 1
    uv_p = _cmp_uv_rows(kv32, n_p, 2048, w1p)
    kcvc_p = _cmp_finish(uv_p, bsz, seq // CMP_STRIDE, w_cmp1[0], w_cmp2[0], cmp_pe[0], False, "cmp_finish_prompt")
    ng_t = ng[:n_p, :NG_COLS].reshape(bsz, seq, 3, N_KV, HPG).transpose(0, 3, 1, 2, 4).reshape(bsz, N_KV, seq, 3 * HPG)
    o_nsa_p = _nsa_prompt(q_h, kcvc_p, kv_h, ng_t, n_cmp_p)

    cache_t = cache_kv[0].transpose(0, 2, 3, 4, 1).reshape(cache_kv.shape[1], 2 * HALF_ROWS, PAGE_SIZE)
    win_t = cache_win[0].transpose(0, 2, 3, 4, 1).reshape(n_seq, HALF_ROWS, cache_win.shape[2])
    n_cmp_s = (past_len + t_new - CMP_LEN) // CMP_STRIDE + 1
    uv_s = _cmp_uv_sample(page_table, cache_t, w1p)
    kcvc_s = _cmp_finish(uv_s, n_seq, uv_s.shape[1], w_cmp1[0], w_cmp2[0], cmp_pe[0], True, "cmp_finish_sample")
    rows_s = N_HEADS * t_new
    q_s = q_h[:, n_p:].reshape(N_KV, HPG, n_seq, t_new, HEAD_DIM).transpose(2, 0, 1, 3, 4).reshape(
        n_seq, N_KV, HPG * t_new, HEAD_DIM)
    q_bd = jnp.einsum('sgrd,gk->sgrkd', q_s, jnp.eye(N_KV, dtype=q_s.dtype)).reshape(n_seq, rows_s, GD)
    new_kv = kv_h[:, n_p:].reshape(6, N_KV, n_seq, t_new, HEAD_DIM).transpose(2, 0, 3, 1, 4).reshape(
        n_seq, 6, t_new, GD)
    new_kv = jnp.pad(new_kv, ((0, 0), (0, 0), (0, 8 - t_new), (0, 0)))
    ng_r = ng[n_p:, :NG_COLS].reshape(n_seq, t_new, 3, N_KV, HPG).transpose(0, 3, 4, 1, 2).reshape(
        n_seq, rows_s, 3)
    o_nsa_s = _nsa_sample(page_table, cache_t, q_bd, kcvc_s, new_kv, win_t, ng_r, n_cmp_s, t_new)
    o_nsa_s = o_nsa_s.reshape(n_seq, N_KV, HPG, t_new, HEAD_DIM).transpose(0, 3, 1, 2, 4).reshape(n_s, NSA_WIDTH)

    lower = jnp.cumsum(jax.nn.softmax(hgrn_gamma.astype(jnp.float32), axis=0), axis=0)
    lb = lower[0].reshape(HG_HEADS, HG_DK)
    zero_state = jnp.zeros((bsz, HG_HEADS, HG_DV, HG_DK), jnp.float32)
    o_hg_p, st_p = _hgrn(h4, 0, bsz, seq, lb, hgrn_norm[0], zero_state, HG_CHUNK, HG_CHUNK, 512, 4, "hgrn_prompt")
    h4_s = jnp.pad(h4[n_p:].reshape(n_seq, t_new, -1), ((0, 0), (0, 8 - t_new), (0, 0))).reshape(n_seq * 8, -1)
    o_hg_s, st_s = _hgrn(h4_s, 0, n_seq, 8, lb, hgrn_norm[0], state_hgrn[0].transpose(0, 1, 3, 2),
                         8, t_new, 8, HG_HEADS, "hgrn_sample")
    o_hg_s = o_hg_s.reshape(n_seq, 8, HG_WIDTH)[:, :t_new].reshape(n_s, HG_WIDTH)
    new_state_prompt = st_p.transpose(0, 1, 3, 2)[None].astype(x_prompt.dtype)
    new_state_sample = st_s.transpose(0, 1, 3, 2)[None].astype(state_hgrn.dtype)

    o_nsa = jnp.concatenate([o_nsa_p, o_nsa_s], axis=0)
    o_hg = jnp.concatenate([o_hg_p, o_hg_s], axis=0)
    u = _mix(o_nsa, o_hg, mg, w_pa[0].astype(jnp.bfloat16), w_pb[0].astype(jnp.bfloat16), TAIL_TM)
    wr = jnp.zeros((D_MODEL, LANES), jnp.float32).at[:, :N_GROUPS].set(w_rg[0]).at[:, LG0:LG0 + N_EXPERTS].set(w_re[0])
    br = jnp.zeros((1, LANES), jnp.float32).at[0, :N_GROUPS].set(b_rg[0]).at[0, LG0:LG0 + N_EXPERTS].set(b_re[0])
    wr_hi, wr_lo = _split_bf16(wr)
    h, lg = _ln1(u, x_all, w_out[0].astype(jnp.bfloat16), ln1_g[0], ln1_b[0], wr_hi, wr_lo, br, TAIL_TM)
    out = _moe_and_norm(h, lg, w_gate[0], w_up[0], w_down[0], ln2_g[0], ln2_b[0])
    y_prompt = out[:n_p].reshape(bsz, seq, D_MODEL)
    y_sample = out[n_p:].reshape(n_seq, t_new, D_MODEL)
    return (y_prompt, y_sample, new_kv_prompt, new_kv_sample, new_win_prompt, new_win_sample,
            new_state_prompt, new_state_sample)
```

```python
import functools

import numpy as np
import jax
import jax.numpy as jnp
from jax import lax
from jax.experimental import pallas as pl
from jax.experimental.pallas import tpu as pltpu

D_MODEL = 2048
N_HEADS = 16
N_KV = 4
HPG = N_HEADS // N_KV
HEAD_DIM = 64
NSA_WIDTH = N_HEADS * HEAD_DIM
CMP_LEN = 32
CMP_STRIDE = 16
CMP_HID = 128
SEL_BLOCK = 64
N_SEL = 16
WINDOW = 512
QBLK = 128
KV_SLOTS = 4
PAGE_SIZE = 128
HG_HEADS = 8
HG_DK = 128
HG_DV = 128
HG_WIDTH = HG_HEADS * HG_DV
HG_CHUNK = 32
N_GROUPS = 4
EXP_PER_GROUP = 8
N_EXPERTS = N_GROUPS * EXP_PER_GROUP
D_EXPERT = 512
DEPTH = 1
DN_ALPHA = (2.0 * DEPTH) ** 0.25
LN_EPS = 1e-5
SCALE = HEAD_DIM ** -0.5
NEG = -1e30
BIG = 1e30

LANES = 128
KV_COLS = 6 * N_KV * HEAD_DIM
NG_COLS = 3 * N_HEADS
OFF_Q = 0
OFF_KV = NSA_WIDTH
OFF_NG = OFF_KV + KV_COLS
OFF_H4 = OFF_NG + NG_COLS
OFF_MG = OFF_H4 + 2 * HG_HEADS * HG_DK + 2 * HG_WIDTH
PROJ_COLS = OFF_MG + 2 * D_MODEL

VMEM_LIMIT = 56 * 1024 * 1024


def _sigmoid(x):
    return 1.0 / (1.0 + jnp.exp(-x))


def _proj_body(x_ref, w_ref, b_ref, *out_refs):
    acc = jnp.dot(x_ref[...], w_ref[...], preferred_element_type=jnp.float32) + b_ref[...]
    for o_ref in out_refs:
        if len(o_ref.shape) == 3:
            for k in range(o_ref.shape[0]):
                o_ref[k] = acc[:, k * HEAD_DIM:(k + 1) * HEAD_DIM].astype(o_ref.dtype)
        else:
            o_ref[...] = acc.astype(o_ref.dtype)


def _proj(x_bf, w_bf, b, outs, tm, tn, name):
    m, k = x_bf.shape
    n = w_bf.shape[1]
    assert m % tm == 0 and n % tn == 0
    out_specs, out_shape = [], []
    for dt, per_head in outs:
        if per_head:
            out_specs.append(pl.BlockSpec((tn // HEAD_DIM, tm, HEAD_DIM), lambda i, j: (j, i, 0)))
            out_shape.append(jax.ShapeDtypeStruct((n // HEAD_DIM, m, HEAD_DIM), dt))
        else:
            out_specs.append(pl.BlockSpec((tm, tn), lambda i, j: (i, j)))
            out_shape.append(jax.ShapeDtypeStruct((m, n), dt))
    return pl.pallas_call(
        _proj_body,
        grid=(m // tm, n // tn),
        in_specs=[pl.BlockSpec((tm, k), lambda i, j: (i, 0)),
                  pl.BlockSpec((k, tn), lambda i, j: (0, j)),
                  pl.BlockSpec((1, tn), lambda i, j: (0, j))],
        out_specs=out_specs,
        out_shape=out_shape,
        compiler_params=pltpu.CompilerParams(dimension_semantics=("parallel", "parallel"),
                                             vmem_limit_bytes=VMEM_LIMIT),
        name=name,
    )(x_bf, w_bf, b.reshape(1, n))


def _hgrn_body(hq_ref, hf_ref, hi_ref, hg_ref, lb_ref, nrm_ref, s0_ref, o_ref, sfin_ref, st_scr,
               *, chunk, n_valid, n_chunks):
    @pl.when(pl.program_id(2) == 0)
    def _():
        st_scr[...] = s0_ref[0]

    for h in range(st_scr.shape[0]):
        cols = slice(h * HG_DK, (h + 1) * HG_DK)
        o, st = _hgrn_head(hq_ref[:, cols], hf_ref[:, cols], hi_ref[:, cols], hg_ref[:, cols],
                           lb_ref[h], nrm_ref[h], st_scr[h], chunk, n_valid, n_chunks)
        st_scr[h] = st
        sfin_ref[0, h] = st
        o_ref[:, cols] = o.astype(o_ref.dtype)


def _hgrn_head(hq, hf, v, hg, lb, nrm, st, chunk, n_valid, n_chunks):
    rows = chunk * n_chunks
    q = hq * _sigmoid(hq)
    f = lb + (1.0 - lb) * _sigmoid(hf)
    k = 1.0 - f
    lc = jnp.log(f)
    row_in_chunk = lax.broadcasted_iota(jnp.int32, (rows, HG_DK), 0) % chunk
    if n_valid < chunk:
        live = row_in_chunk < n_valid
        q = jnp.where(live, q, 0.0)
        k = jnp.where(live, k, 0.0)
        v = jnp.where(live, v, 0.0)
        lc = jnp.where(live, lc, 0.0)
    bc = lc
    step = 1
    while step < chunk:
        bc = bc + jnp.where(row_in_chunk >= step, pltpu.roll(bc, step, axis=0), 0.0)
        step *= 2
    bc3 = bc.reshape(n_chunks, chunk, HG_DK)
    bl3 = bc3[:, chunk - 1:chunk, :]
    q3 = q.reshape(n_chunks, chunk, HG_DK)
    k3 = k.reshape(n_chunks, chunk, HG_DK)
    v3 = v.reshape(n_chunks, chunk, HG_DV).astype(jnp.bfloat16)
    qe3 = (q3 * jnp.exp(bc3)).astype(jnp.bfloat16)
    ke3 = (k3 * jnp.exp(-bc3)).astype(jnp.bfloat16)
    kd3 = (k3 * jnp.exp(bl3 - bc3)).astype(jnp.bfloat16)
    dec3 = jnp.exp(bl3)
    att = jnp.einsum('ctd,csd->cts', qe3, ke3, preferred_element_type=jnp.float32)
    tri = (lax.broadcasted_iota(jnp.int32, (chunk, chunk), 0)
           >= lax.broadcasted_iota(jnp.int32, (chunk, chunk), 1))
    att = jnp.where(tri[None], att, 0.0).astype(jnp.bfloat16)
    o_intra = jnp.einsum('cts,cse->cte', att, v3, preferred_element_type=jnp.float32)

    outs = []
    for c in range(n_chunks):
        o_c = lax.dot_general(qe3[c], st.astype(jnp.bfloat16), (((1,), (1,)), ((), ())),
                              preferred_element_type=jnp.float32)
        outs.append(o_c + o_intra[c])
        upd = lax.dot_general(v3[c], kd3[c], (((0,), (0,)), ((), ())),
                              preferred_element_type=jnp.float32)
        st = st * dec3[c] + upd
    o = jnp.concatenate(outs, axis=0) if n_chunks > 1 else outs[0]
    o = o * lax.rsqrt(jnp.mean(o * o, axis=-1, keepdims=True) + LN_EPS) * nrm
    return o * (hg * _sigmoid(hg)), st


def _hgrn(h4, row0, n_seq, t_seq, lb, nrm, s0_t, chunk, n_valid, block_rows, heads_per_step, name):
    assert t_seq % block_rows == 0 and block_rows % chunk == 0 and row0 % block_rows == 0
    nb = t_seq // block_rows
    rb0 = row0 // block_rows
    nh = heads_per_step
    hb = HG_HEADS // nh

    def col_spec(seg):
        return pl.BlockSpec((block_rows, nh * HG_DK),
                            lambda b, h, i, seg=seg: (rb0 + b * nb + i, seg * hb + h))

    body = functools.partial(_hgrn_body, chunk=chunk, n_valid=n_valid, n_chunks=block_rows // chunk)
    return pl.pallas_call(
        body,
        grid=(n_seq, hb, nb),
        in_specs=[col_spec(0), col_spec(1), col_spec(2), col_spec(3),
                  pl.BlockSpec((nh, 1, HG_DK), lambda b, h, i: (h, 0, 0)),
                  pl.BlockSpec((nh, 1, HG_DV), lambda b, h, i: (h, 0, 0)),
                  pl.BlockSpec((1, nh, HG_DV, HG_DK), lambda b, h, i: (b, h, 0, 0))],
        out_specs=[pl.BlockSpec((block_rows, nh * HG_DV), lambda b, h, i: (b * nb + i, h)),
                   pl.BlockSpec((1, nh, HG_DV, HG_DK), lambda b, h, i: (b, h, 0, 0))],
        out_shape=[jax.ShapeDtypeStruct((n_seq * t_seq, HG_WIDTH), jnp.bfloat16),
                   jax.ShapeDtypeStruct((n_seq, HG_HEADS, HG_DV, HG_DK), jnp.float32)],
        scratch_shapes=[pltpu.VMEM((nh, HG_DV, HG_DK), jnp.float32)],
        compiler_params=pltpu.CompilerParams(
            dimension_semantics=("parallel", "parallel", "arbitrary"), vmem_limit_bytes=VMEM_LIMIT),
        name=name,
    )(h4, h4, h4, h4, lb.reshape(HG_HEADS, 1, HG_DK), nrm.reshape(HG_HEADS, 1, HG_DV), s0_t)


UV_COLS = 2 * N_KV * 2 * CMP_HID


def _uv_taps(tap, w_ref, c, n):
    acc = jnp.zeros((n, 4 * CMP_HID), jnp.float32)
    for p in range(CMP_STRIDE // 2):
        x2 = jnp.concatenate([tap(2 * p), tap(2 * p + 1)], axis=-1)
        acc = acc + jnp.dot(x2, w_ref[c, p], preferred_element_type=jnp.float32)
    return acc


def _cmp_uv_body(x0_ref, x1_ref, x2_ref, x3_ref, w_ref, uv_ref):
    n = uv_ref.shape[0]
    for cgp, x_ref in enumerate((x0_ref, x1_ref, x2_ref, x3_ref)):
        tap = lambda j, x_ref=x_ref: x_ref[pl.ds(j, n, stride=CMP_STRIDE), :].astype(jnp.bfloat16)
        uv_ref[:, cgp * 512:(cgp + 1) * 512] = _uv_taps(tap, w_ref, cgp // 2, n)


def _cmp_w1_pairs(w_cmp1_l):
    w = jnp.concatenate([w_cmp1_l[:, :CMP_STRIDE], w_cmp1_l[:, CMP_STRIDE:]], axis=-1)
    z = jnp.zeros_like(w)
    top = jnp.concatenate([w, z], axis=-1)
    bot = jnp.concatenate([z, w], axis=-1)
    per_tap = jnp.concatenate([top, bot], axis=-2)
    return per_tap.reshape(2, CMP_STRIDE // 2, 2 * LANES, 4 * CMP_HID).astype(jnp.bfloat16)


def _cmp_uv_rows(kv, n_rows, rows_per_step, w1p):
    n = rows_per_step // CMP_STRIDE
    return pl.pallas_call(
        _cmp_uv_body,
        grid=(n_rows // rows_per_step,),
        in_specs=[pl.BlockSpec((rows_per_step, LANES), lambda i, cb=cb: (i, cb)) for cb in range(4)]
        + [pl.BlockSpec(w1p.shape, lambda i: (0, 0, 0, 0))],
        out_specs=pl.BlockSpec((n, UV_COLS), lambda i: (i, 0)),
        out_shape=jax.ShapeDtypeStruct((n_rows // CMP_STRIDE, UV_COLS), jnp.float32),
        compiler_params=pltpu.CompilerParams(dimension_semantics=("parallel",),
                                             vmem_limit_bytes=VMEM_LIMIT),
        name="cmp_uv_prompt",
    )(kv, kv, kv, kv, w1p)


def _gelu_tanh(x):
    return 0.5 * x * (1.0 + jnp.tanh(0.7978845608028654 * (x + 0.044715 * x * x * x)))


def _cmp_finish_body(uv_ref, pe_ref, w1_ref, w2_ref, out_ref):
    n_ch = uv_ref.shape[1]
    for c in range(2):
        pe_term = jnp.dot(pe_ref[c], w1_ref[c], preferred_element_type=jnp.float32)[0:1, :]
        for g in range(N_KV):
            base = (c * N_KV + g) * 2 * CMP_HID
            u = uv_ref[0, :, base:base + CMP_HID]
            v = uv_ref[0, :, base + CMP_HID:base + 2 * CMP_HID]
            pre = u + pltpu.roll(v, n_ch - 1, axis=0) + pe_term
            hid = _gelu_tanh(pre).astype(jnp.bfloat16)
            res = jnp.dot(hid, w2_ref[c], preferred_element_type=jnp.float32).astype(out_ref.dtype)
            if out_ref.shape[1] == 2:
                out_ref[0, c, :, g * HEAD_DIM:(g + 1) * HEAD_DIM] = res
            else:
                out_ref[0, c * N_KV + g] = res


def _cmp_finish(uv, n_seq, n_ch, w_cmp1_l, w_cmp2_l, cmp_pe_l, merged, name):
    pe = jnp.zeros((2, 8, CMP_LEN * HEAD_DIM), jnp.float32).at[:, 0].set(cmp_pe_l.reshape(2, -1))
    out_dims = (2, n_ch, N_KV * HEAD_DIM) if merged else (2 * N_KV, n_ch, HEAD_DIM)
    return pl.pallas_call(
        _cmp_finish_body,
        grid=(n_seq,),
        in_specs=[pl.BlockSpec((1, n_ch, UV_COLS), lambda b: (b, 0, 0)),
                  pl.BlockSpec((2, 8, CMP_LEN * HEAD_DIM), lambda b: (0, 0, 0)),
                  pl.BlockSpec((2, CMP_LEN * HEAD_DIM, CMP_HID), lambda b: (0, 0, 0)),
                  pl.BlockSpec((2, CMP_HID, HEAD_DIM), lambda b: (0, 0, 0))],
        out_specs=pl.BlockSpec((1,) + out_dims, lambda b: (b, 0, 0, 0)),
        out_shape=jax.ShapeDtypeStruct((n_seq,) + out_dims, jnp.bfloat16),
        compiler_params=pltpu.CompilerParams(dimension_semantics=("parallel",),
                                             vmem_limit_bytes=VMEM_LIMIT),
        name=name,
    )(uv.reshape(n_seq, n_ch, UV_COLS), pe.astype(jnp.bfloat16),
      w_cmp1_l.reshape(2, CMP_LEN * HEAD_DIM, CMP_HID).astype(jnp.bfloat16),
      w_cmp2_l.astype(jnp.bfloat16))


SLC_TK = 512
WIN_KEYS = WINDOW + QBLK


def _masked_softmax(s, valid):
    s = jnp.where(valid, s, NEG)
    m = jnp.max(s, axis=-1, keepdims=True)
    e = jnp.where(valid, jnp.exp(s - m), 0.0)
    return e / jnp.maximum(jnp.sum(e, axis=-1, keepdims=True), 1e-30)


def _split_bf16(x):
    hi = x.astype(jnp.bfloat16)
    return hi, (x - hi.astype(jnp.float32)).astype(jnp.bfloat16)


SUBLANES = 8


def _top_blocks_t(imp_t):
    nb, nq = imp_t.shape
    groups = [imp_t[SUBLANES * v:SUBLANES * (v + 1), :] for v in range(nb // SUBLANES)]
    sub = lax.broadcasted_iota(jnp.int32, (SUBLANES, nq), 0)
    beaten_by = [jnp.zeros((SUBLANES, nq), jnp.float32) for _ in groups]
    for i in range(nb):
        row = jnp.broadcast_to(imp_t[i:i + 1, :], (SUBLANES, nq))
        for v, gv in enumerate(groups):
            if v > i // SUBLANES:
                beats = row >= gv
            elif v < i // SUBLANES:
                beats = row > gv
            else:
                beats = (row > gv) | ((row == gv) & (sub > i % SUBLANES))
            beaten_by[v] = beaten_by[v] + jnp.where(beats, 1.0, 0.0)
    return jnp.concatenate([jnp.where(c < N_SEL, 1.0, 0.0) for c in beaten_by], axis=0)


def _split3_bf16(x):
    x1 = x.astype(jnp.bfloat16)
    r = x - x1.astype(jnp.float32)
    x2 = r.astype(jnp.bfloat16)
    return x1, x2, (r - x2.astype(jnp.float32)).astype(jnp.bfloat16)


XK = 2 * LANES


def _slc_key_columns(t):
    kpos = np.arange(t)
    cols = np.zeros((t, XK - HEAD_DIM), np.float32)
    cols[kpos, kpos // SEL_BLOCK] = 1.0
    cols[:, 64:67] = (kpos // 64)[:, None]
    cols[:, 67:70] = (kpos % 64)[:, None]
    cols[:, 70:73] = 1.0
    return cols


def _slc_query_columns():
    s1, s2, s3 = _split3_bf16(_alibi_slopes().reshape(N_KV, HPG, 1))
    cols = jnp.concatenate([64.0 * s1, 64.0 * s2, 64.0 * s3, s1, s2, s3], axis=-1)
    cols = jnp.pad(cols, ((0, 0), (0, 0), (0, LANES - 6)))
    return jnp.broadcast_to(cols[:, :, None, :], (N_KV, HPG, QBLK, LANES)).reshape(N_KV, HPG * QBLK, LANES)


def _nsa_prompt_body(q_ref, kc_ref, vc_ref, kx_ref, vs_ref, kw_ref, vw_ref, ng_ref, ov_ref, sx_ref, cb_ref, sl_ref,
                     o_ref, *, n_cmp):
    i = pl.program_id(2)
    nq = QBLK
    rows = HPG * nq
    q2 = (q_ref[...] * SCALE).reshape(rows, HEAD_DIM)
    slopes = sl_ref[0]
    qpos = i * nq + lax.broadcasted_iota(jnp.int32, (nq, 1), 0)
    nt = (((1,), (1,)), ((), ()))

    n_ch = kc_ref.shape[2]
    s = lax.dot_general(q2, kc_ref[0, 0], nt, preferred_element_type=jnp.float32).reshape(HPG, nq, n_ch)
    cidx = lax.broadcasted_iota(jnp.int32, (1, n_ch), 1)
    dist_c = qpos - (cidx * CMP_STRIDE + (CMP_LEN - 1))
    valid_c = ((dist_c >= 0) & (cidx < n_cmp))[None]
    p = _masked_softmax(s - slopes * dist_c.astype(jnp.float32)[None], valid_c)
    o_cmp = jnp.dot(p.reshape(rows, n_ch).astype(jnp.bfloat16), vc_ref[0, 0],
                    preferred_element_type=jnp.float32).reshape(HPG, nq, HEAD_DIM)

    p_hi, p_lo = _split_bf16(p[0] + p[1] + p[2] + p[3])
    imp_t = (lax.dot_general(ov_ref[...], p_hi, nt, preferred_element_type=jnp.float32)
             + lax.dot_general(ov_ref[...], p_lo, nt, preferred_element_type=jnp.float32))
    nsb = imp_t.shape[0]
    blk = lax.broadcasted_iota(jnp.int32, (nsb, 1), 0)
    cur = (i * nq + lax.broadcasted_iota(jnp.int32, (1, nq), 1)) // SEL_BLOCK
    imp_t = jnp.where(blk > cur, NEG, imp_t)
    imp_t = jnp.where((blk == cur) | (blk == 0), BIG, imp_t)
    sel_t = jnp.where(blk <= cur, _top_blocks_t(imp_t), 0.0)

    eye = (lax.broadcasted_iota(jnp.int32, (nsb, nsb), 0)
           == lax.broadcasted_iota(jnp.int32, (nsb, nsb), 1)).astype(jnp.bfloat16)
    drop = lax.dot_general(((sel_t - 1.0) * BIG).astype(jnp.bfloat16), eye, (((0,), (0,)), ((), ())),
                           preferred_element_type=jnp.float32)
    drop = jnp.concatenate([drop.astype(jnp.bfloat16)] * HPG, axis=0)
    slope_row = jnp.broadcast_to(slopes, (HPG, nq, 1)).reshape(rows, 1)
    qpos_row = jnp.concatenate([qpos] * HPG, axis=0).astype(jnp.float32)
    c1, c2, c3 = _split3_bf16(slope_row * qpos_row)
    lane = lax.broadcasted_iota(jnp.int32, (rows, LANES), 1)
    pos_cols = sx_ref[0].astype(jnp.float32)
    for k, ck in enumerate((c1, c2, c3)):
        pos_cols = jnp.where(lane == 6 + k, -ck.astype(jnp.float32), pos_cols)
    q_ext = jnp.concatenate([q2, drop, pos_cols.astype(jnp.bfloat16)], axis=1)

    def slc_tile(kt, carry, causal_bias):
        m, l, acc = carry
        k0 = pl.multiple_of(kt * SLC_TK, SLC_TK)
        s = lax.dot_general(q_ext, kx_ref[0, pl.ds(k0, SLC_TK), :], nt,
                            preferred_element_type=jnp.float32).reshape(HPG, nq, SLC_TK)
        if causal_bias is not None:
            s = s + causal_bias[None]
        m_new = jnp.maximum(m, jnp.max(s, axis=-1, keepdims=True))
        a = jnp.exp(m - m_new)
        e = jnp.exp(s - m_new)
        l = a * l + jnp.sum(e, axis=-1, keepdims=True)
        pv = jnp.dot(e.reshape(rows, SLC_TK).astype(jnp.bfloat16), vs_ref[0, pl.ds(k0, SLC_TK), :],
                     preferred_element_type=jnp.float32).reshape(HPG, nq, HEAD_DIM)
        return m_new, l, a * acc + pv

    init = (jnp.full((HPG, nq, 1), NEG, jnp.float32), jnp.zeros((HPG, nq, 1), jnp.float32),
            jnp.zeros((HPG, nq, HEAD_DIM), jnp.float32))
    last = (i * nq) // SLC_TK
    carry = lax.fori_loop(0, last // 2, lambda p, c: slc_tile(2 * p + 1, slc_tile(2 * p, c, None), None), init)
    carry = lax.fori_loop(2 * (last // 2), last, lambda kt, c: slc_tile(kt, c, None), carry)
    _, l, acc = slc_tile(last, carry, cb_ref[i % (SLC_TK // QBLK)])
    o_slc = acc / jnp.maximum(l, 1e-30)

    w0 = pl.multiple_of(jnp.maximum(i - WINDOW // QBLK, 0) * nq, nq)
    s = lax.dot_general(q2, kw_ref[0, pl.ds(w0, WIN_KEYS), :], nt,
                        preferred_element_type=jnp.float32).reshape(HPG, nq, WIN_KEYS)
    dist = qpos - (w0 + lax.broadcasted_iota(jnp.int32, (1, WIN_KEYS), 1))
    valid = ((dist >= 0) & (dist < WINDOW))[None]
    pw = _masked_softmax(s - slopes * dist.astype(jnp.float32)[None], valid)
    o_win = jnp.dot(pw.reshape(rows, WIN_KEYS).astype(jnp.bfloat16), vw_ref[0, pl.ds(w0, WIN_KEYS), :],
                    preferred_element_type=jnp.float32).reshape(HPG, nq, HEAD_DIM)

    gates = _sigmoid(ng_ref[0, 0])
    o_ref[...] = jnp.concatenate(
        [gates[:, h:h + 1] * o_cmp[h] + gates[:, HPG + h:HPG + h + 1] * o_slc[h]
         + gates[:, 2 * HPG + h:2 * HPG + h + 1] * o_win[h] for h in range(HPG)], axis=-1).astype(o_ref.dtype)


def _alibi_slopes():
    return jnp.asarray(2.0 ** (-8.0 * np.arange(1, N_HEADS + 1) / N_HEADS), jnp.float32).reshape(N_KV, HPG, 1, 1)


def _nsa_prompt(q_h, kcvc, kv_h, ng_t, n_cmp):
    bsz, _, t, _ = ng_t.shape
    nqb = t // QBLK
    n_ch = kcvc.shape[2]
    nsb = t // SEL_BLOCK
    st = np.arange(n_ch) * CMP_STRIDE
    bs = np.arange(nsb) * SEL_BLOCK
    overlap = ((st[:, None] <= bs[None, :] + SEL_BLOCK - 1) & (st[:, None] + CMP_LEN - 1 >= bs[None, :])
               & (np.arange(n_ch)[:, None] < n_cmp)).astype(np.float32)
    assert nsb == LANES - HEAD_DIM and t % SLC_TK == 0
    key_cols = jnp.asarray(np.tile(_slc_key_columns(t), (bsz, 1)), jnp.bfloat16)
    kx = jnp.concatenate([kv_h[2 * N_KV:3 * N_KV, :bsz * t],
                          jnp.broadcast_to(key_cols[None], (N_KV,) + key_cols.shape)], axis=-1)
    r = np.arange(SLC_TK // QBLK)[:, None, None] * QBLK + np.arange(QBLK)[None, :, None]
    causal = np.where(np.arange(SLC_TK)[None, None, :] <= r, 0.0, NEG).astype(np.float32)

    def kv_spec(slot):
        return pl.BlockSpec((1, t, HEAD_DIM), lambda b, g, i, slot=slot: (slot * N_KV + g, b, 0))

    return pl.pallas_call(
        functools.partial(_nsa_prompt_body, n_cmp=n_cmp),
        grid=(bsz, N_KV, nqb),
        in_specs=[pl.BlockSpec((HPG, QBLK, HEAD_DIM), lambda b, g, i: (g, b * nqb + i, 0)),
                  pl.BlockSpec((1, 1, n_ch, HEAD_DIM), lambda b, g, i: (b, g, 0, 0)),
                  pl.BlockSpec((1, 1, n_ch, HEAD_DIM), lambda b, g, i: (b, N_KV + g, 0, 0)),
                  pl.BlockSpec((1, t, XK), lambda b, g, i: (g, b, 0)),
                  kv_spec(3), kv_spec(4), kv_spec(5),
                  pl.BlockSpec((1, 1, QBLK, 3 * HPG), lambda b, g, i: (b, g, i, 0)),
                  pl.BlockSpec((nsb, n_ch), lambda b, g, i: (0, 0)),
                  pl.BlockSpec((1, HPG * QBLK, LANES), lambda b, g, i: (g, 0, 0)),
                  pl.BlockSpec(causal.shape, lambda b, g, i: (0, 0, 0)),
                  pl.BlockSpec((1, HPG, 1, 1), lambda b, g, i: (g, 0, 0, 0))],
        out_specs=pl.BlockSpec((QBLK, HPG * HEAD_DIM), lambda b, g, i: (b * nqb + i, g)),
        out_shape=jax.ShapeDtypeStruct((bsz * t, NSA_WIDTH), jnp.bfloat16),
        compiler_params=pltpu.CompilerParams(
            dimension_semantics=("parallel", "parallel", "arbitrary"), vmem_limit_bytes=VMEM_LIMIT),
        name="nsa_prompt",
    )(q_h, kcvc, kcvc, kx, kv_h, kv_h, kv_h, ng_t,
      jnp.asarray(overlap.T, jnp.bfloat16), _slc_query_columns(), jnp.asarray(causal), _alibi_slopes())


HALF_ROWS = 2 * N_KV * HEAD_DIM
GD = N_KV * HEAD_DIM


def _fetch_pages(pt_ref, cache_ref, sem, row0, per_step, dst_of):
    b = pl.program_id(0)
    slot = b % 2
    n_pages = pt_ref.shape[1]

    def page_copy(flat_page, sl, k):
        phys = pt_ref[flat_page // n_pages, flat_page % n_pages]
        return pltpu.make_async_copy(cache_ref.at[phys, pl.ds(row0, HALF_ROWS), :], dst_of(sl, k), sem.at[sl])

    def start_all(step, sl):
        def one(k, carry):
            page_copy(step * per_step + k, sl, k).start()
            return carry
        lax.fori_loop(0, per_step, one, 0)

    @pl.when(b == 0)
    def _():
        start_all(0, 0)

    @pl.when(b + 1 < pl.num_programs(0))
    def _():
        start_all(b + 1, 1 - slot)

    def wait_one(k, carry):
        page_copy(b * per_step + k, slot, k).wait()
        return carry
    lax.fori_loop(0, per_step, wait_one, 0)
    return slot


def _cmp_uv_sample_body(pt_ref, cache_ref, w_ref, uv_ref, buf, sem, xt):
    per_step = buf.shape[1]
    slot = _fetch_pages(pt_ref, cache_ref, sem, 0, per_step, lambda sl, k: buf.at[sl, k])
    n_cb = HALF_ROWS // LANES

    eye = (lax.broadcasted_iota(jnp.int32, (LANES, LANES), 0)
           == lax.broadcasted_iota(jnp.int32, (LANES, LANES), 1)).astype(jnp.bfloat16)

    def to_rows(k, carry):
        for cb in range(n_cb):
            blk = buf[slot, k, cb * LANES:(cb + 1) * LANES, :].astype(jnp.bfloat16)
            xt[k, cb] = lax.dot_general(eye, blk, (((1,), (1,)), ((), ())), preferred_element_type=jnp.float32)
        return carry
    lax.fori_loop(0, per_step, to_rows, 0, unroll=8)

    per_page = PAGE_SIZE // CMP_STRIDE
    n = per_step * per_page

    def tap(cgp, j):
        return xt[:, cgp, pl.ds(j, per_page, stride=CMP_STRIDE), :].reshape(n, LANES).astype(jnp.bfloat16)

    for cgp in range(n_cb):
        uv_ref[0, :, cgp * 512:(cgp + 1) * 512] = _uv_taps(functools.partial(tap, cgp), w_ref, cgp // 2, n)


UV_PAGES = 32


def _cmp_uv_sample(page_table, cache_t, w1p):
    n_seq, n_pages = page_table.shape
    assert n_pages % UV_PAGES == 0
    steps = n_pages // UV_PAGES
    per_page = PAGE_SIZE // CMP_STRIDE
    return pl.pallas_call(
        _cmp_uv_sample_body,
        grid_spec=pltpu.PrefetchScalarGridSpec(
            num_scalar_prefetch=1, grid=(n_seq * steps,),
            in_specs=[pl.BlockSpec(memory_space=pl.ANY),
                      pl.BlockSpec(w1p.shape, lambda b, pt: (0, 0, 0, 0))],
            out_specs=pl.BlockSpec((1, UV_PAGES * per_page, UV_COLS), lambda b, pt: (b // steps, b % steps, 0)),
            scratch_shapes=[pltpu.VMEM((2, UV_PAGES, HALF_ROWS, PAGE_SIZE), jnp.float32),
                            pltpu.SemaphoreType.DMA((2,)),
                            pltpu.VMEM((UV_PAGES, HALF_ROWS // LANES, PAGE_SIZE, LANES), jnp.float32)]),
        out_shape=jax.ShapeDtypeStruct((n_seq, n_pages * per_page, UV_COLS), jnp.float32),
        compiler_params=pltpu.CompilerParams(dimension_semantics=("arbitrary",),
                                             vmem_limit_bytes=VMEM_LIMIT),
        name="cmp_uv_sample",
    )(page_table, cache_t, w1p)


def _group_diag(x, rows_per_group):
    grp = lax.broadcasted_iota(jnp.int32, (x.shape[0], 1), 0) // rows_per_group
    out = jnp.zeros((x.shape[0], HEAD_DIM), x.dtype)
    for g in range(N_KV):
        out = out + jnp.where(grp == g, x[:, g * HEAD_DIM:(g + 1) * HEAD_DIM], 0.0)
    return out


BIAS_POS_ROWS = 16


def _sample_bias_rows(n_lanes, past_len):
    kpos = np.arange(past_len)
    rows = np.zeros((n_lanes + BIAS_POS_ROWS, past_len), np.float32)
    rows[kpos // SEL_BLOCK, kpos] = 1.0
    rows[n_lanes:n_lanes + 3] = kpos // 64
    rows[n_lanes + 3:n_lanes + 6] = kpos % 64
    rows[n_lanes + 6:n_lanes + 9] = 1.0
    return rows


def _nsa_sample_body(pt_ref, cache_ref, q_ref, kcvc_ref, new_ref, win_ref, ng_ref, ov_ref, same_q_ref, sl_ref,
                     bq_ref, bk_ref, o_ref, buf, sem, *, n_cmp, t_new):
    past_len = buf.shape[2]
    slot = _fetch_pages(pt_ref, cache_ref, sem, HALF_ROWS, past_len // PAGE_SIZE,
                        lambda sl, k: buf.at[sl, :, pl.ds(pl.multiple_of(k * PAGE_SIZE, PAGE_SIZE), PAGE_SIZE)])
    rows = q_ref.shape[1]
    rpg = rows // N_KV
    nt = (((1,), (1,)), ((), ()))
    qbd = q_ref[0] * SCALE
    slope = sl_ref[...]
    row = lax.broadcasted_iota(jnp.int32, (rows, 1), 0)
    qtok = row % t_new
    qpos = past_len + qtok
    new_rows = new_ref.shape[2]
    new_idx = lax.broadcasted_iota(jnp.int32, (1, new_rows), 1)
    new_valid = (new_idx <= qtok) & (new_idx < t_new)
    new_bias = slope * (qtok - new_idx).astype(jnp.float32)

    n_ch = kcvc_ref.shape[2]
    s = lax.dot_general(qbd, kcvc_ref[0, 0], nt, preferred_element_type=jnp.float32)
    cidx = lax.broadcasted_iota(jnp.int32, (1, n_ch), 1)
    dist_c = qpos - (cidx * CMP_STRIDE + (CMP_LEN - 1))
    p = _masked_softmax(s - slope * dist_c.astype(jnp.float32), (dist_c >= 0) & (cidx < n_cmp))
    o_cmp = _group_diag(jnp.dot(p.astype(jnp.bfloat16), kcvc_ref[0, 1], preferred_element_type=jnp.float32), rpg)

    p_hi, p_lo = _split_bf16(p)
    psum = (jnp.dot(same_q_ref[...], p_hi, preferred_element_type=jnp.float32)
            + jnp.dot(same_q_ref[...], p_lo, preferred_element_type=jnp.float32))
    ps_hi, ps_lo = _split_bf16(psum)
    imp = (jnp.dot(ps_hi, ov_ref[...], preferred_element_type=jnp.float32)
           + jnp.dot(ps_lo, ov_ref[...], preferred_element_type=jnp.float32))
    n_lanes = imp.shape[-1]
    blk = lax.broadcasted_iota(jnp.int32, (1, n_lanes), 1)
    cur = qpos // SEL_BLOCK
    imp = jnp.where(blk > cur, -jnp.inf, imp)
    imp = jnp.where((blk == cur) | (blk == 0), BIG, imp)
    sel = jnp.zeros(imp.shape, jnp.float32)
    for _ in range(N_SEL):
        m = jnp.max(imp, axis=-1, keepdims=True)
        idx = jnp.min(jnp.where(imp == m, blk, n_lanes), axis=-1, keepdims=True)
        sel = jnp.where(blk == idx, 1.0, sel)
        imp = jnp.where(blk == idx, -jnp.inf, imp)

    c1, c2, c3 = _split3_bf16(slope * qpos.astype(jnp.float32))
    lane = lax.broadcasted_iota(jnp.int32, (rows, bq_ref.shape[1]), 1)
    pos_cols = bq_ref[...].astype(jnp.float32)
    for k, ck in enumerate((c1, c2, c3)):
        pos_cols = jnp.where(lane == 6 + k, -ck.astype(jnp.float32), pos_cols)
    bias_q = jnp.concatenate([((sel - 1.0) * BIG).astype(jnp.bfloat16), pos_cols.astype(jnp.bfloat16)], axis=1)
    s_all = (jnp.dot(qbd, buf[slot, 0:GD, :].astype(jnp.bfloat16), preferred_element_type=jnp.float32)
             + jnp.dot(bias_q, bk_ref[...], preferred_element_type=jnp.float32))
    s_new = lax.dot_general(qbd, new_ref[0, 2], nt, preferred_element_type=jnp.float32)
    s_new = jnp.where(new_valid, s_new - new_bias, NEG)
    m = jnp.maximum(jnp.max(s_all, axis=-1, keepdims=True), jnp.max(s_new, axis=-1, keepdims=True))
    e_all = jnp.exp(s_all - m)
    e_new = jnp.where(new_valid, jnp.exp(s_new - m), 0.0)
    l = jnp.sum(e_all, axis=-1, keepdims=True) + jnp.sum(e_new, axis=-1, keepdims=True)
    acc = (lax.dot_general(e_all.astype(jnp.bfloat16), buf[slot, GD:2 * GD, :].astype(jnp.bfloat16), nt,
                           preferred_element_type=jnp.float32)
           + jnp.dot(e_new.astype(jnp.bfloat16), new_ref[0, 3], preferred_element_type=jnp.float32))
    o_slc = _group_diag(acc, rpg) / jnp.maximum(l, 1e-30)

    w_buf = win_ref.shape[2]
    wdist = qpos - (past_len - w_buf + lax.broadcasted_iota(jnp.int32, (1, w_buf), 1))
    valid_w = (wdist >= 0) & (wdist < WINDOW)
    s_w = jnp.dot(qbd, win_ref[0, 0:GD, :].astype(jnp.bfloat16), preferred_element_type=jnp.float32)
    s_w = jnp.where(valid_w, s_w - slope * wdist.astype(jnp.float32), NEG)
    s_wn = lax.dot_general(qbd, new_ref[0, 4], nt, preferred_element_type=jnp.float32)
    s_wn = jnp.where(new_valid, s_wn - new_bias, NEG)
    m = jnp.maximum(jnp.max(s_w, axis=-1, keepdims=True), jnp.max(s_wn, axis=-1, keepdims=True))
    e_w = jnp.where(valid_w, jnp.exp(s_w - m), 0.0)
    e_wn = jnp.where(new_valid, jnp.exp(s_wn - m), 0.0)
    acc = (lax.dot_general(e_w.astype(jnp.bfloat16), win_ref[0, GD:2 * GD, :].astype(jnp.bfloat16), nt,
                           preferred_element_type=jnp.float32)
           + jnp.dot(e_wn.astype(jnp.bfloat16), new_ref[0, 5], preferred_element_type=jnp.float32))
    l = jnp.sum(e_w, axis=-1, keepdims=True) + jnp.sum(e_wn, axis=-1, keepdims=True)
    o_win = _group_diag(acc, rpg) / jnp.maximum(l, 1e-30)

    gates = _sigmoid(ng_ref[0])
    o_ref[0] = (gates[:, 0:1] * o_cmp + gates[:, 1:2] * o_slc + gates[:, 2:3] * o_win).astype(o_ref.dtype)


def _nsa_sample(page_table, cache_t, q_bd, kcvc, new_kv, win_t, ng_r, n_cmp, t_new):
    n_seq, n_pages = page_table.shape
    past_len = n_pages * PAGE_SIZE
    n_ch = kcvc.shape[2]
    rows = q_bd.shape[1]
    rpg = rows // N_KV
    nsb = -(-(past_len + t_new) // SEL_BLOCK)
    n_lanes = -(-nsb // LANES) * LANES
    st = np.arange(n_ch) * CMP_STRIDE
    bs = np.arange(n_lanes) * SEL_BLOCK
    overlap = ((st[:, None] <= bs[None, :] + SEL_BLOCK - 1) & (st[:, None] + CMP_LEN - 1 >= bs[None, :])
               & (np.arange(n_ch)[:, None] < n_cmp) & (np.arange(n_lanes)[None, :] < nsb)).astype(np.float32)
    r = np.arange(rows)
    same_q = ((r[:, None] // rpg == r[None, :] // rpg) & (r[:, None] % t_new == r[None, :] % t_new)
              ).astype(np.float32)
    slopes = jnp.repeat(_alibi_slopes().reshape(N_HEADS), t_new).reshape(rows, 1)
    s1, s2, s3 = _split3_bf16(slopes)
    bias_q = jnp.pad(jnp.concatenate([64.0 * s1, 64.0 * s2, 64.0 * s3, s1, s2, s3], axis=1),
                     ((0, 0), (0, BIAS_POS_ROWS - 6)))
    bias_k = jnp.asarray(_sample_bias_rows(n_lanes, past_len), jnp.bfloat16)
    w_buf = win_t.shape[2]
    return pl.pallas_call(
        functools.partial(_nsa_sample_body, n_cmp=n_cmp, t_new=t_new),
        grid_spec=pltpu.PrefetchScalarGridSpec(
            num_scalar_prefetch=1, grid=(n_seq,),
            in_specs=[pl.BlockSpec(memory_space=pl.ANY),
                      pl.BlockSpec((1, rows, GD), lambda b, pt: (b, 0, 0)),
                      pl.BlockSpec((1, 2, n_ch, GD), lambda b, pt: (b, 0, 0, 0)),
                      pl.BlockSpec((1, 6, new_kv.shape[2], GD), lambda b, pt: (b, 0, 0, 0)),
                      pl.BlockSpec((1, HALF_ROWS, w_buf), lambda b, pt: (b, 0, 0)),
                      pl.BlockSpec((1, rows, 3), lambda b, pt: (b, 0, 0)),
                      pl.BlockSpec((n_ch, n_lanes), lambda b, pt: (0, 0)),
                      pl.BlockSpec((rows, rows), lambda b, pt: (0, 0)),
                      pl.BlockSpec((rows, 1), lambda b, pt: (0, 0)),
                      pl.BlockSpec(bias_q.shape, lambda b, pt: (0, 0)),
                      pl.BlockSpec(bias_k.shape, lambda b, pt: (0, 0), pipeline_mode=pl.Buffered(1))],
            out_specs=pl.BlockSpec((1, rows, HEAD_DIM), lambda b, pt: (b, 0, 0)),
            scratch_shapes=[pltpu.VMEM((2, HALF_ROWS, past_len), jnp.float32),
                            pltpu.SemaphoreType.DMA((2,))]),
        out_shape=jax.ShapeDtypeStruct((n_seq, rows, HEAD_DIM), jnp.bfloat16),
        compiler_params=pltpu.CompilerParams(dimension_semantics=("arbitrary",),
                                             vmem_limit_bytes=VMEM_LIMIT),
        name="nsa_sample",
    )(page_table, cache_t, q_bd, kcvc, new_kv, win_t, ng_r,
      jnp.asarray(overlap, jnp.bfloat16), jnp.asarray(same_q, jnp.bfloat16), slopes, bias_q, bias_k)


def _mix_body(a_ref, b_ref, mga_ref, mgb_ref, wpa_ref, wpb_ref, u_ref):
    a = jnp.dot(a_ref[...], wpa_ref[...], preferred_element_type=jnp.float32)
    b = jnp.dot(b_ref[...], wpb_ref[...], preferred_element_type=jnp.float32)
    u_ref[...] = (_sigmoid(mga_ref[...]) * a + _sigmoid(mgb_ref[...]) * b).astype(u_ref.dtype)


def _mix(o_nsa, o_hg, mg, w_pa_bf, w_pb_bf, tm):
    n = o_nsa.shape[0]
    const = lambda i: (0, 0)
    return pl.pallas_call(
        _mix_body,
        grid=(n // tm,),
        in_specs=[pl.BlockSpec((tm, NSA_WIDTH), lambda i: (i, 0)),
                  pl.BlockSpec((tm, HG_WIDTH), lambda i: (i, 0)),
                  pl.BlockSpec((tm, D_MODEL), lambda i: (i, 0)),
                  pl.BlockSpec((tm, D_MODEL), lambda i: (i, 1)),
                  pl.BlockSpec((NSA_WIDTH, D_MODEL), const, pipeline_mode=pl.Buffered(1)),
                  pl.BlockSpec((HG_WIDTH, D_MODEL), const, pipeline_mode=pl.Buffered(1))],
        out_specs=pl.BlockSpec((tm, D_MODEL), lambda i: (i, 0)),
        out_shape=jax.ShapeDtypeStruct((n, D_MODEL), jnp.bfloat16),
        compiler_params=pltpu.CompilerParams(dimension_semantics=("parallel",),
                                             vmem_limit_bytes=VMEM_LIMIT),
        name="tail_mix",
    )(o_nsa, o_hg, mg, mg, w_pa_bf, w_pb_bf)


def _layer_norm(z, g, b):
    mu = jnp.mean(z, axis=-1, keepdims=True)
    zc = z - mu
    var = jnp.mean(zc * zc, axis=-1, keepdims=True)
    return zc * lax.rsqrt(var + LN_EPS) * g + b


CH = D_MODEL // LANES


def _store_chunked(ref, val):
    tm = val.shape[0]
    for k in range(CH):
        ref[pl.ds(k, tm, stride=CH), :] = val[:, k * LANES:(k + 1) * LANES]


def _load_chunked(ref, tm, lead=()):
    return jnp.concatenate([ref[lead + (pl.ds(k, tm, stride=CH), slice(None))] for k in range(CH)], axis=1)


def _ln1_body(u_ref, x_ref, wout_ref, g_ref, b_ref, wr_hi_ref, wr_lo_ref, br_ref, h_ref, lg_ref):
    y = jnp.dot(u_ref[...], wout_ref[...], preferred_element_type=jnp.float32)
    h = _layer_norm(DN_ALPHA * x_ref[...] + y, g_ref[...], b_ref[...])
    _store_chunked(h_ref, h)
    h_hi = h.astype(jnp.bfloat16)
    h_lo = (h - h_hi.astype(jnp.float32)).astype(jnp.bfloat16)
    lg = jnp.dot(h_hi, wr_hi_ref[...], preferred_element_type=jnp.float32)
    lg = lg + jnp.dot(h_lo, wr_hi_ref[...], preferred_element_type=jnp.float32)
    lg = lg + jnp.dot(h_hi, wr_lo_ref[...], preferred_element_type=jnp.float32)
    lg_ref[...] = lg + br_ref[...]


def _ln1(u, x_all, w_out_bf, g, b, wr_hi, wr_lo, br, tm):
    n = u.shape[0]
    const = lambda i: (0, 0)
    return pl.pallas_call(
        _ln1_body,
        grid=(n // tm,),
        in_specs=[pl.BlockSpec((tm, D_MODEL), lambda i: (i, 0)),
                  pl.BlockSpec((tm, D_MODEL), lambda i: (i, 0)),
                  pl.BlockSpec((D_MODEL, D_MODEL), const, pipeline_mode=pl.Buffered(1)),
                  pl.BlockSpec((1, D_MODEL), const),
                  pl.BlockSpec((1, D_MODEL), const),
                  pl.BlockSpec((D_MODEL, LANES), const),
                  pl.BlockSpec((D_MODEL, LANES), const),
                  pl.BlockSpec((1, LANES), const)],
        out_specs=[pl.BlockSpec((tm * CH, LANES), lambda i: (i, 0)),
                   pl.BlockSpec((tm, LANES), lambda i: (i, 0))],
        out_shape=[jax.ShapeDtypeStruct((n * CH, LANES), jnp.float32),
                   jax.ShapeDtypeStruct((n, LANES), jnp.float32)],
        compiler_params=pltpu.CompilerParams(dimension_semantics=("parallel",),
                                             vmem_limit_bytes=VMEM_LIMIT),
        name="tail_ln1",
    )(u, x_all, w_out_bf, g.reshape(1, -1), b.reshape(1, -1), wr_hi, wr_lo, br)


LG0 = N_GROUPS


def _route_body(lg_ref, tri_ref, out_ref, cnt_ref, carry_scr):
    @pl.when(pl.program_id(0) == 0)
    def _():
        carry_scr[...] = jnp.zeros_like(carry_scr)

    lg = lg_ref[...]
    tm = lg.shape[0]
    lane = lax.broadcasted_iota(jnp.int32, lg.shape, 1)
    is_g = lane < N_GROUPS
    gl = jnp.where(is_g, lg, NEG)
    gmax = jnp.max(gl, axis=-1, keepdims=True)
    grp = jnp.min(jnp.where(gl == gmax, lane, LANES), axis=-1, keepdims=True)
    g_w = 1.0 / jnp.sum(jnp.where(is_g, jnp.exp(lg - gmax), 0.0), axis=-1, keepdims=True)
    lo = LG0 + grp * EXP_PER_GROUP
    el = jnp.where((lane >= lo) & (lane < lo + EXP_PER_GROUP), lg, NEG)
    v1 = jnp.max(el, axis=-1, keepdims=True)
    i1 = jnp.min(jnp.where(el == v1, lane, LANES), axis=-1, keepdims=True)
    el2 = jnp.where(lane == i1, NEG, el)
    v2 = jnp.max(el2, axis=-1, keepdims=True)
    i2 = jnp.min(jnp.where(el2 == v2, lane, LANES), axis=-1, keepdims=True)
    e21 = jnp.exp(v2 - v1)
    w1 = g_w / (1.0 + e21)
    w2 = g_w * e21 / (1.0 + e21)
    hit1 = lane == i1
    hit2 = lane == i2
    onehot = jnp.where(hit1 | hit2, 1.0, 0.0)
    incl = jnp.dot(tri_ref[...], onehot.astype(jnp.bfloat16), preferred_element_type=jnp.float32)
    carry = carry_scr[...]
    before = incl - onehot + carry
    r1 = jnp.sum(jnp.where(hit1, before, 0.0), axis=-1, keepdims=True)
    r2 = jnp.sum(jnp.where(hit2, before, 0.0), axis=-1, keepdims=True)
    carry = carry + incl[tm - 1:tm, :]
    carry_scr[...] = carry
    cnt_ref[...] = carry
    out = jnp.where(lane == 0, (i1 - LG0).astype(jnp.float32), 0.0)
    out = jnp.where(lane == 1, (i2 - LG0).astype(jnp.float32), out)
    out = jnp.where(lane == 2, w1, out)
    out = jnp.where(lane == 3, w2, out)
    out = jnp.where(lane == 4, r1, out)
    out = jnp.where(lane == 5, r2, out)
    out_ref[...] = out


def _route(lg, tm):
    n = lg.shape[0]
    tri = (np.arange(tm)[:, None] >= np.arange(tm)[None, :]).astype(np.float32)
    return pl.pallas_call(
        _route_body,
        grid=(n // tm,),
        in_specs=[pl.BlockSpec((tm, LANES), lambda i: (i, 0)),
                  pl.BlockSpec((tm, tm), lambda i: (0, 0))],
        out_specs=[pl.BlockSpec((tm, LANES), lambda i: (i, 0)),
                   pl.BlockSpec((1, LANES), lambda i: (0, 0))],
        out_shape=[jax.ShapeDtypeStruct((n, LANES), jnp.float32),
                   jax.ShapeDtypeStruct((1, LANES), jnp.float32)],
        scratch_shapes=[pltpu.VMEM((1, LANES), jnp.float32)],
        compiler_params=pltpu.CompilerParams(dimension_semantics=("arbitrary",)),
        name="moe_route",
    )(lg, jnp.asarray(tri, jnp.bfloat16))


def _dispatch_body(pos_ref, h_ref, xs_in_ref, xs_ref, sem):
    del xs_in_ref
    tm = h_ref.shape[0] // CH
    base = pl.program_id(0) * tm

    def issue(t, carry):
        src = pl.multiple_of(t * CH, CH)
        for slot in range(2):
            dst = pl.multiple_of(pos_ref[2 * (base + t) + slot] * CH, CH)
            pltpu.make_async_copy(h_ref.at[pl.ds(src, CH)], xs_ref.at[pl.ds(dst, CH)], sem).start()
        return carry

    lax.fori_loop(0, tm, issue, 0)
    for _ in range(2):
        pltpu.make_async_copy(h_ref, xs_ref.at[pl.ds(0, tm * CH)], sem).wait()


DISPATCH_TM = 128


def _dispatch(pos_flat, h_c, n_slots):
    zeros = jnp.zeros((n_slots * CH, LANES), h_c.dtype)
    tm = DISPATCH_TM
    return pl.pallas_call(
        _dispatch_body,
        grid_spec=pltpu.PrefetchScalarGridSpec(
            num_scalar_prefetch=1, grid=(h_c.shape[0] // (tm * CH),),
            in_specs=[pl.BlockSpec((tm * CH, LANES), lambda i, pos: (i, 0)),
                      pl.BlockSpec(memory_space=pl.ANY)],
            out_specs=pl.BlockSpec(memory_space=pl.ANY),
            scratch_shapes=[pltpu.SemaphoreType.DMA(())]),
        out_shape=jax.ShapeDtypeStruct(zeros.shape, h_c.dtype),
        input_output_aliases={2: 0},
        compiler_params=pltpu.CompilerParams(dimension_semantics=("arbitrary",)),
        name="moe_dispatch",
    )(pos_flat, h_c, zeros)


def _ffn_body(te_ref, nu_ref, x_ref, wg_ref, wu_ref, wd_ref, y_ref, wg_bf, wu_bf, wd_bf):
    i = pl.program_id(0)
    prev = te_ref[jnp.maximum(i - 1, 0)]

    @pl.when((i == 0) | (te_ref[i] != prev))
    def _():
        wg_bf[...] = wg_ref[0].astype(jnp.bfloat16)
        wu_bf[...] = wu_ref[0].astype(jnp.bfloat16)
        wd_bf[...] = wd_ref[0].astype(jnp.bfloat16)

    @pl.when(i < nu_ref[0])
    def _():
        x = _load_chunked(x_ref, x_ref.shape[0] // CH).astype(jnp.bfloat16)
        g = jnp.dot(x, wg_bf[...], preferred_element_type=jnp.float32)
        u = jnp.dot(x, wu_bf[...], preferred_element_type=jnp.float32)
        hid = (g * _sigmoid(g) * u).astype(jnp.bfloat16)
        _store_chunked(y_ref, jnp.dot(hid, wd_bf[...], preferred_element_type=jnp.float32))

    @pl.when(i >= nu_ref[0])
    def _():
        y_ref[...] = jnp.zeros_like(y_ref)


def _ffn(tile_expert, n_used, xs_c, w_gate, w_up, w_down, tm):
    d = D_MODEL
    nt = xs_c.shape[0] // (tm * CH)
    return pl.pallas_call(
        _ffn_body,
        grid_spec=pltpu.PrefetchScalarGridSpec(
            num_scalar_prefetch=2, grid=(nt,),
            in_specs=[pl.BlockSpec((tm * CH, LANES), lambda i, te, nu: (i, 0)),
                      pl.BlockSpec((1, d, D_EXPERT), lambda i, te, nu: (te[i], 0, 0)),
                      pl.BlockSpec((1, d, D_EXPERT), lambda i, te, nu: (te[i], 0, 0)),
                      pl.BlockSpec((1, D_EXPERT, d), lambda i, te, nu: (te[i], 0, 0))],
            out_specs=pl.BlockSpec((tm * CH, LANES), lambda i, te, nu: (i, 0)),
            scratch_shapes=[pltpu.VMEM((d, D_EXPERT), jnp.bfloat16),
                            pltpu.VMEM((d, D_EXPERT), jnp.bfloat16),
                            pltpu.VMEM((D_EXPERT, d), jnp.bfloat16)]),
        out_shape=jax.ShapeDtypeStruct(xs_c.shape, jnp.float32),
        compiler_params=pltpu.CompilerParams(dimension_semantics=("arbitrary",),
                                             vmem_limit_bytes=VMEM_LIMIT),
        name="moe_ffn",
    )(tile_expert, n_used, xs_c, w_gate, w_up, w_down)


def _combine_body(pos_ref, h_ref, rw_ref, g_ref, b_ref, y_ref, out_a_ref, out_b_ref, buf, sem, *, tiles_a):
    tm = out_a_ref.shape[0]
    i = pl.program_id(0)
    cur = i % 2

    def gather(tile, parity):
        def one(t, carry):
            for slot in range(2):
                src = pl.multiple_of(pos_ref[2 * (tile * tm + t) + slot] * CH, CH)
                dst = pl.multiple_of(t * CH, CH)
                pltpu.make_async_copy(y_ref.at[pl.ds(src, CH)], buf.at[parity, slot, pl.ds(dst, CH)],
                                      sem.at[parity, slot]).start()
            return carry
        lax.fori_loop(0, tm, one, 0)

    @pl.when(i == 0)
    def _():
        gather(0, 0)

    @pl.when(i + 1 < pl.num_programs(0))
    def _():
        gather(i + 1, 1 - cur)

    for slot in range(2):
        pltpu.make_async_copy(y_ref.at[pl.ds(0, tm * CH)], buf.at[cur, slot], sem.at[cur, slot]).wait()
    rw = rw_ref[...]
    z = rw[:, 2:3] * _load_chunked(buf, tm, (cur, 0)) + rw[:, 3:4] * _load_chunked(buf, tm, (cur, 1))
    res = _layer_norm(DN_ALPHA * _load_chunked(h_ref, tm) + z, g_ref[...], b_ref[...])

    @pl.when(i < tiles_a)
    def _():
        out_a_ref[...] = res

    @pl.when(i >= tiles_a)
    def _():
        out_b_ref[...] = res


def _combine(pos_flat, h_c, route_out, g, b, y_c, tm, n_a):
    n, d = h_c.shape[0] // CH, D_MODEL
    assert n_a % tm == 0 and (n - n_a) % tm == 0 and 0 < n_a < n
    tiles_a = n_a // tm
    return pl.pallas_call(
        functools.partial(_combine_body, tiles_a=tiles_a),
        grid_spec=pltpu.PrefetchScalarGridSpec(
            num_scalar_prefetch=1, grid=(n // tm,),
            in_specs=[pl.BlockSpec((tm * CH, LANES), lambda i, pos: (i, 0)),
                      pl.BlockSpec((tm, LANES), lambda i, pos: (i, 0)),
                      pl.BlockSpec((1, d), lambda i, pos: (0, 0)),
                      pl.BlockSpec((1, d), lambda i, pos: (0, 0)),
                      pl.BlockSpec(memory_space=pl.ANY)],
            out_specs=[pl.BlockSpec((tm, d), lambda i, pos: (jnp.minimum(i, tiles_a - 1), 0)),
                       pl.BlockSpec((tm, d), lambda i, pos: (jnp.maximum(i - tiles_a, 0), 0))],
            scratch_shapes=[pltpu.VMEM((2, 2, tm * CH, LANES), jnp.float32),
                            pltpu.SemaphoreType.DMA((2, 2))]),
        out_shape=[jax.ShapeDtypeStruct((n_a, d), jnp.float32),
                   jax.ShapeDtypeStruct((n - n_a, d), jnp.float32)],
        compiler_params=pltpu.CompilerParams(dimension_semantics=("arbitrary",),
                                             vmem_limit_bytes=VMEM_LIMIT),
        name="moe_combine",
    )(pos_flat, h_c, route_out, g.reshape(1, -1), b.reshape(1, -1), y_c)


FFN_TM = 256
PROJ_TM = 1664
TAIL_TM = 320


def _moe_and_norm(h_c, lg, w_gate, w_up, w_down, ln2_g, ln2_b, n_first):
    n = lg.shape[0]
    route_out, cnt = _route(lg, 640)
    eid = route_out[:, 0:2].astype(jnp.int32)
    rank = route_out[:, 4:6].astype(jnp.int32)
    counts = cnt[0, LG0:LG0 + N_EXPERTS].astype(jnp.int32)
    tiles_per = (counts + FFN_TM - 1) // FFN_TM
    tile_end = jnp.cumsum(tiles_per)
    row_start = (tile_end - tiles_per) * FFN_TM
    pos_flat = (row_start[eid] + rank).reshape(-1)
    nt = (2 * n) // FFN_TM + N_EXPERTS
    n_used = tile_end[-1]
    tile_ids = jnp.minimum(jnp.arange(nt, dtype=jnp.int32), n_used - 1)
    tile_expert = jnp.sum((tile_end[None, :] <= tile_ids[:, None]).astype(jnp.int32), axis=1)
    xs_c = _dispatch(pos_flat, h_c, nt * FFN_TM)
    y_c = _ffn(tile_expert, n_used.reshape(1).astype(jnp.int32), xs_c, w_gate, w_up, w_down, FFN_TM)
    return _combine(pos_flat, h_c, route_out, ln2_g, ln2_b, y_c, DISPATCH_TM, n_first)


def kernel(x_prompt, x_sample, cache_kv, cache_win, state_hgrn, page_table, w_in, b_in, w_cmp1, w_cmp2, cmp_pe,
           hgrn_gamma, hgrn_norm, w_pa, w_pb, w_out, ln1_g, ln1_b, w_rg, b_rg, w_re, b_re, w_gate, w_up, w_down,
           ln2_g, ln2_b):
    n_p = x_prompt.shape[0] * x_prompt.shape[1]
    n_s = x_sample.shape[0] * x_sample.shape[1]
    x_all = jnp.concatenate([x_prompt.reshape(n_p, D_MODEL), x_sample.reshape(n_s, D_MODEL)], axis=0)
    x_bf = x_all.astype(jnp.bfloat16)
    bsz, seq = x_prompt.shape[:2]
    n_seq, t_new = x_sample.shape[:2]
    n_pages = page_table.shape[1]
    past_len = n_pages * PAGE_SIZE
    w = w_in[0]
    b = b_in[0]

    def seg(lo, hi, outs, tn, name, pad_to=None):
        ws, bs = w[:, lo:hi], b[lo:hi]
        if pad_to is not None:
            ws = jnp.pad(ws, ((0, 0), (0, pad_to - (hi - lo))))
            bs = jnp.pad(bs, (0, pad_to - (hi - lo)))
        return _proj(x_bf, ws.astype(jnp.bfloat16), bs, outs, PROJ_TM, tn, name)

    q_h, = seg(OFF_Q, OFF_KV, [(jnp.bfloat16, True)], 512, "proj_q")
    kv32, kv_h = seg(OFF_KV, OFF_NG, [(jnp.float32, False), (jnp.bfloat16, True)], 512, "proj_kv")
    ng, = seg(OFF_NG, OFF_H4, [(jnp.float32, False)], LANES, "proj_ng", pad_to=LANES)
    h4, = seg(OFF_H4, OFF_MG, [(jnp.float32, False)], 512, "proj_h4")
    mg, = seg(OFF_MG, PROJ_COLS, [(jnp.float32, False)], 512, "proj_mg")

    kv_p = kv32[:n_p].reshape(1, bsz, seq, 6, N_KV, HEAD_DIM)
    kv_s = kv32[n_p:].reshape(1, n_seq, t_new, 6, N_KV, HEAD_DIM)
    new_kv_prompt = kv_p[:, :, :, :KV_SLOTS]
    new_kv_sample = kv_s[:, :, :, :KV_SLOTS]
    new_win_prompt = kv_p[:, :, seq - min(WINDOW, seq):, 4:6]
    win_all = jnp.concatenate([cache_win, kv_s[:, :, :, 4:6].astype(cache_win.dtype)], axis=2)
    new_win_sample = win_all[:, :, win_all.shape[2] - min(WINDOW, win_all.shape[2]):]

    w1p = _cmp_w1_pairs(w_cmp1[0])
    n_cmp_p = (seq - CMP_LEN) // CMP_STRIDE + 1
    uv_p = _cmp_uv_rows(kv32, n_p, 2048, w1p)
    kcvc_p = _cmp_finish(uv_p, bsz, seq // CMP_STRIDE, w_cmp1[0], w_cmp2[0], cmp_pe[0], False, "cmp_finish_prompt")
    ng_t = ng[:n_p, :NG_COLS].reshape(bsz, seq, 3, N_KV, HPG).transpose(0, 3, 1, 2, 4).reshape(bsz, N_KV, seq, 3 * HPG)
    o_nsa_p = _nsa_prompt(q_h, kcvc_p, kv_h, ng_t, n_cmp_p)

    cache_t = cache_kv[0].transpose(0, 2, 3, 4, 1).reshape(cache_kv.shape[1], 2 * HALF_ROWS, PAGE_SIZE)
    win_t = cache_win[0].transpose(0, 2, 3, 4, 1).reshape(n_seq, HALF_ROWS, cache_win.shape[2])
    n_cmp_s = (past_len + t_new - CMP_LEN) // CMP_STRIDE + 1
    uv_s = _cmp_uv_sample(page_table, cache_t, w1p)
    kcvc_s = _cmp_finish(uv_s, n_seq, uv_s.shape[1], w_cmp1[0], w_cmp2[0], cmp_pe[0], True, "cmp_finish_sample")
    rows_s = N_HEADS * t_new
    q_s = q_h[:, n_p:].reshape(N_KV, HPG, n_seq, t_new, HEAD_DIM).transpose(2, 0, 1, 3, 4).reshape(
        n_seq, N_KV, HPG * t_new, HEAD_DIM)
    q_bd = jnp.einsum('sgrd,gk->sgrkd', q_s, jnp.eye(N_KV, dtype=q_s.dtype)).reshape(n_seq, rows_s, GD)
    new_kv = kv_h[:, n_p:].reshape(6, N_KV, n_seq, t_new, HEAD_DIM).transpose(2, 0, 3, 1, 4).reshape(
        n_seq, 6, t_new, GD)
    new_kv = jnp.pad(new_kv, ((0, 0), (0, 0), (0, 8 - t_new), (0, 0)))
    ng_r = ng[n_p:, :NG_COLS].reshape(n_seq, t_new, 3, N_KV, HPG).transpose(0, 3, 4, 1, 2).reshape(
        n_seq, rows_s, 3)
    o_nsa_s = _nsa_sample(page_table, cache_t, q_bd, kcvc_s, new_kv, win_t, ng_r, n_cmp_s, t_new)
    o_nsa_s = o_nsa_s.reshape(n_seq, N_KV, HPG, t_new, HEAD_DIM).transpose(0, 3, 1, 2, 4).reshape(n_s, NSA_WIDTH)

    lower = jnp.cumsum(jax.nn.softmax(hgrn_gamma.astype(jnp.float32), axis=0), axis=0)
    lb = lower[0].reshape(HG_HEADS, HG_DK)
    zero_state = jnp.zeros((bsz, HG_HEADS, HG_DV, HG_DK), jnp.float32)
    o_hg_p, st_p = _hgrn(h4, 0, bsz, seq, lb, hgrn_norm[0], zero_state, HG_CHUNK, HG_CHUNK, 512, 4, "hgrn_prompt")
    h4_s = jnp.pad(h4[n_p:].reshape(n_seq, t_new, -1), ((0, 0), (0, 8 - t_new), (0, 0))).reshape(n_seq * 8, -1)
    o_hg_s, st_s = _hgrn(h4_s, 0, n_seq, 8, lb, hgrn_norm[0], state_hgrn[0].transpose(0, 1, 3, 2),
                         8, t_new, 8, HG_HEADS, "hgrn_sample")
    o_hg_s = o_hg_s.reshape(n_seq, 8, HG_WIDTH)[:, :t_new].reshape(n_s, HG_WIDTH)
    new_state_prompt = st_p.transpose(0, 1, 3, 2)[None].astype(x_prompt.dtype)
    new_state_sample = st_s.transpose(0, 1, 3, 2)[None].astype(state_hgrn.dtype)

    o_nsa = jnp.concatenate([o_nsa_p, o_nsa_s], axis=0)
    o_hg = jnp.concatenate([o_hg_p, o_hg_s], axis=0)
    u = _mix(o_nsa, o_hg, mg, w_pa[0].astype(jnp.bfloat16), w_pb[0].astype(jnp.bfloat16), TAIL_TM)
    wr = jnp.zeros((D_MODEL, LANES), jnp.float32).at[:, :N_GROUPS].set(w_rg[0]).at[:, LG0:LG0 + N_EXPERTS].set(w_re[0])
    br = jnp.zeros((1, LANES), jnp.float32).at[0, :N_GROUPS].set(b_rg[0]).at[0, LG0:LG0 + N_EXPERTS].set(b_re[0])
    wr_hi, wr_lo = _split_bf16(wr)
    h, lg = _ln1(u, x_all, w_out[0].astype(jnp.bfloat16), ln1_g[0], ln1_b[0], wr_hi, wr_lo, br, TAIL_TM)
    out_p, out_s = _moe_and_norm(h, lg, w_gate[0], w_up[0], w_down[0], ln2_g[0], ln2_b[0], n_p)
    y_prompt = out_p.reshape(bsz, seq, D_MODEL)
    y_sample = out_s.reshape(n_seq, t_new, D_MODEL)
    return (y_prompt, y_sample, new_kv_prompt, new_kv_sample, new_win_prompt, new_win_sample,
            new_state_prompt, new_state_sample)
```

```python
import functools

import numpy as np
import jax
import jax.numpy as jnp
from jax import lax
from jax.experimental import pallas as pl
from jax.experimental.pallas import tpu as pltpu

D_MODEL = 2048
N_HEADS = 16
N_KV = 4
HPG = N_HEADS // N_KV
HEAD_DIM = 64
NSA_WIDTH = N_HEADS * HEAD_DIM
CMP_LEN = 32
CMP_STRIDE = 16
CMP_HID = 128
SEL_BLOCK = 64
N_SEL = 16
WINDOW = 512
QBLK = 128
KV_SLOTS = 4
PAGE_SIZE = 128
HG_HEADS = 8
HG_DK = 128
HG_DV = 128
HG_WIDTH = HG_HEADS * HG_DV
HG_CHUNK = 32
N_GROUPS = 4
EXP_PER_GROUP = 8
N_EXPERTS = N_GROUPS * EXP_PER_GROUP
D_EXPERT = 512
DEPTH = 1
DN_ALPHA = (2.0 * DEPTH) ** 0.25
LN_EPS = 1e-5
SCALE = HEAD_DIM ** -0.5
NEG = -1e30
BIG = 1e30

LANES = 128
KV_COLS = 6 * N_KV * HEAD_DIM
NG_COLS = 3 * N_HEADS
OFF_Q = 0
OFF_KV = NSA_WIDTH
OFF_NG = OFF_KV + KV_COLS
OFF_H4 = OFF_NG + NG_COLS
OFF_MG = OFF_H4 + 2 * HG_HEADS * HG_DK + 2 * HG_WIDTH
PROJ_COLS = OFF_MG + 2 * D_MODEL

VMEM_LIMIT = 56 * 1024 * 1024


def _sigmoid(x):
    return 1.0 / (1.0 + jnp.exp(-x))


def _proj_body(x_ref, w_ref, b_ref, *out_refs):
    acc = jnp.dot(x_ref[...], w_ref[...], preferred_element_type=jnp.float32) + b_ref[...]
    for o_ref in out_refs:
        if len(o_ref.shape) == 3:
            for k in range(o_ref.shape[0]):
                o_ref[k] = acc[:, k * HEAD_DIM:(k + 1) * HEAD_DIM].astype(o_ref.dtype)
        else:
            o_ref[...] = acc.astype(o_ref.dtype)


def _proj(x_bf, w_bf, b, col0, n, outs, tm, tn, name):
    m, k = x_bf.shape
    assert m % tm == 0 and n % tn == 0 and col0 % tn == 0
    cb0 = col0 // tn
    out_specs, out_shape = [], []
    for dt, per_head in outs:
        if per_head:
            out_specs.append(pl.BlockSpec((tn // HEAD_DIM, tm, HEAD_DIM), lambda i, j: (j, i, 0)))
            out_shape.append(jax.ShapeDtypeStruct((n // HEAD_DIM, m, HEAD_DIM), dt))
        else:
            out_specs.append(pl.BlockSpec((tm, tn), lambda i, j: (i, j)))
            out_shape.append(jax.ShapeDtypeStruct((m, n), dt))
    return pl.pallas_call(
        _proj_body,
        grid=(m // tm, n // tn),
        in_specs=[pl.BlockSpec((tm, k), lambda i, j: (i, 0)),
                  pl.BlockSpec((k, tn), lambda i, j: (0, cb0 + j)),
                  pl.BlockSpec((1, tn), lambda i, j: (0, cb0 + j))],
        out_specs=out_specs,
        out_shape=out_shape,
        compiler_params=pltpu.CompilerParams(dimension_semantics=("parallel", "parallel"),
                                             vmem_limit_bytes=VMEM_LIMIT),
        name=name,
    )(x_bf, w_bf, b.reshape(1, -1))


def _hgrn_body(hq_ref, hf_ref, hi_ref, hg_ref, lb_ref, nrm_ref, s0_ref, o_ref, sfin_ref, st_scr,
               *, chunk, n_valid, n_chunks):
    @pl.when(pl.program_id(2) == 0)
    def _():
        st_scr[...] = s0_ref[0]

    for h in range(st_scr.shape[0]):
        cols = slice(h * HG_DK, (h + 1) * HG_DK)
        o, st = _hgrn_head(hq_ref[:, cols], hf_ref[:, cols], hi_ref[:, cols], hg_ref[:, cols],
                           lb_ref[h], nrm_ref[h], st_scr[h], chunk, n_valid, n_chunks)
        st_scr[h] = st
        sfin_ref[0, h] = st
        o_ref[:, cols] = o.astype(o_ref.dtype)


def _hgrn_head(hq, hf, v, hg, lb, nrm, st, chunk, n_valid, n_chunks):
    rows = chunk * n_chunks
    q = hq * _sigmoid(hq)
    f = lb + (1.0 - lb) * _sigmoid(hf)
    k = 1.0 - f
    lc = jnp.log(f)
    row_in_chunk = lax.broadcasted_iota(jnp.int32, (rows, HG_DK), 0) % chunk
    if n_valid < chunk:
        live = row_in_chunk < n_valid
        q = jnp.where(live, q, 0.0)
        k = jnp.where(live, k, 0.0)
        v = jnp.where(live, v, 0.0)
        lc = jnp.where(live, lc, 0.0)
    bc = lc
    step = 1
    while step < chunk:
        bc = bc + jnp.where(row_in_chunk >= step, pltpu.roll(bc, step, axis=0), 0.0)
        step *= 2
    bc3 = bc.reshape(n_chunks, chunk, HG_DK)
    bl3 = bc3[:, chunk - 1:chunk, :]
    q3 = q.reshape(n_chunks, chunk, HG_DK)
    k3 = k.reshape(n_chunks, chunk, HG_DK)
    v3 = v.reshape(n_chunks, chunk, HG_DV).astype(jnp.bfloat16)
    qe3 = (q3 * jnp.exp(bc3)).astype(jnp.bfloat16)
    ke3 = (k3 * jnp.exp(-bc3)).astype(jnp.bfloat16)
    kd3 = (k3 * jnp.exp(bl3 - bc3)).astype(jnp.bfloat16)
    dec3 = jnp.exp(bl3)
    att = jnp.einsum('ctd,csd->cts', qe3, ke3, preferred_element_type=jnp.float32)
    tri = (lax.broadcasted_iota(jnp.int32, (chunk, chunk), 0)
           >= lax.broadcasted_iota(jnp.int32, (chunk, chunk), 1))
    att = jnp.where(tri[None], att, 0.0).astype(jnp.bfloat16)
    o_intra = jnp.einsum('cts,cse->cte', att, v3, preferred_element_type=jnp.float32)

    outs = []
    for c in range(n_chunks):
        o_c = lax.dot_general(qe3[c], st.astype(jnp.bfloat16), (((1,), (1,)), ((), ())),
                              preferred_element_type=jnp.float32)
        outs.append(o_c + o_intra[c])
        upd = lax.dot_general(v3[c], kd3[c], (((0,), (0,)), ((), ())),
                              preferred_element_type=jnp.float32)
        st = st * dec3[c] + upd
    o = jnp.concatenate(outs, axis=0) if n_chunks > 1 else outs[0]
    o = o * lax.rsqrt(jnp.mean(o * o, axis=-1, keepdims=True) + LN_EPS) * nrm
    return o * (hg * _sigmoid(hg)), st


def _hgrn(h4, row0, n_seq, t_seq, lb, nrm, s0_t, chunk, n_valid, block_rows, heads_per_step, name):
    assert t_seq % block_rows == 0 and block_rows % chunk == 0 and row0 % block_rows == 0
    nb = t_seq // block_rows
    rb0 = row0 // block_rows
    nh = heads_per_step
    hb = HG_HEADS // nh

    def col_spec(seg):
        return pl.BlockSpec((block_rows, nh * HG_DK),
                            lambda b, h, i, seg=seg: (rb0 + b * nb + i, seg * hb + h))

    body = functools.partial(_hgrn_body, chunk=chunk, n_valid=n_valid, n_chunks=block_rows // chunk)
    return pl.pallas_call(
        body,
        grid=(n_seq, hb, nb),
        in_specs=[col_spec(0), col_spec(1), col_spec(2), col_spec(3),
                  pl.BlockSpec((nh, 1, HG_DK), lambda b, h, i: (h, 0, 0)),
                  pl.BlockSpec((nh, 1, HG_DV), lambda b, h, i: (h, 0, 0)),
                  pl.BlockSpec((1, nh, HG_DV, HG_DK), lambda b, h, i: (b, h, 0, 0))],
        out_specs=[pl.BlockSpec((block_rows, nh * HG_DV), lambda b, h, i: (b * nb + i, h)),
                   pl.BlockSpec((1, nh, HG_DV, HG_DK), lambda b, h, i: (b, h, 0, 0))],
        out_shape=[jax.ShapeDtypeStruct((n_seq * t_seq, HG_WIDTH), jnp.bfloat16),
                   jax.ShapeDtypeStruct((n_seq, HG_HEADS, HG_DV, HG_DK), jnp.float32)],
        scratch_shapes=[pltpu.VMEM((nh, HG_DV, HG_DK), jnp.float32)],
        compiler_params=pltpu.CompilerParams(
            dimension_semantics=("parallel", "parallel", "arbitrary"), vmem_limit_bytes=VMEM_LIMIT),
        name=name,
    )(h4, h4, h4, h4, lb.reshape(HG_HEADS, 1, HG_DK), nrm.reshape(HG_HEADS, 1, HG_DV), s0_t)


UV_COLS = 2 * N_KV * 2 * CMP_HID


def _uv_taps(tap, w_ref, c, n):
    acc = jnp.zeros((n, 4 * CMP_HID), jnp.float32)
    for p in range(CMP_STRIDE // 2):
        x2 = jnp.concatenate([tap(2 * p), tap(2 * p + 1)], axis=-1)
        acc = acc + jnp.dot(x2, w_ref[c, p], preferred_element_type=jnp.float32)
    return acc


def _cmp_uv_body(x0_ref, x1_ref, x2_ref, x3_ref, w_ref, uv_ref):
    n = uv_ref.shape[0]
    for cgp, x_ref in enumerate((x0_ref, x1_ref, x2_ref, x3_ref)):
        tap = lambda j, x_ref=x_ref: x_ref[pl.ds(j, n, stride=CMP_STRIDE), :].astype(jnp.bfloat16)
        uv_ref[:, cgp * 512:(cgp + 1) * 512] = _uv_taps(tap, w_ref, cgp // 2, n)


def _cmp_w1_pairs(w_cmp1_l):
    w = jnp.concatenate([w_cmp1_l[:, :CMP_STRIDE], w_cmp1_l[:, CMP_STRIDE:]], axis=-1)
    z = jnp.zeros_like(w)
    top = jnp.concatenate([w, z], axis=-1)
    bot = jnp.concatenate([z, w], axis=-1)
    per_tap = jnp.concatenate([top, bot], axis=-2)
    return per_tap.reshape(2, CMP_STRIDE // 2, 2 * LANES, 4 * CMP_HID).astype(jnp.bfloat16)


def _cmp_uv_rows(kv, n_rows, rows_per_step, w1p):
    n = rows_per_step // CMP_STRIDE
    return pl.pallas_call(
        _cmp_uv_body,
        grid=(n_rows // rows_per_step,),
        in_specs=[pl.BlockSpec((rows_per_step, LANES), lambda i, cb=cb: (i, cb)) for cb in range(4)]
        + [pl.BlockSpec(w1p.shape, lambda i: (0, 0, 0, 0))],
        out_specs=pl.BlockSpec((n, UV_COLS), lambda i: (i, 0)),
        out_shape=jax.ShapeDtypeStruct((n_rows // CMP_STRIDE, UV_COLS), jnp.float32),
        compiler_params=pltpu.CompilerParams(dimension_semantics=("parallel",),
                                             vmem_limit_bytes=VMEM_LIMIT),
        name="cmp_uv_prompt",
    )(kv, kv, kv, kv, w1p)


def _gelu_tanh(x):
    return 0.5 * x * (1.0 + jnp.tanh(0.7978845608028654 * (x + 0.044715 * x * x * x)))


def _cmp_finish_body(uv_ref, pe_ref, w1_ref, w2_ref, out_ref):
    _cmp_finish_rows(lambda cols: uv_ref[0, :, cols], uv_ref.shape[1], pe_ref, w1_ref, w2_ref, out_ref)


def _cmp_finish_rows(uv_cols, n_ch, pe_ref, w1_ref, w2_ref, out_ref):
    for c in range(2):
        pe_term = jnp.dot(pe_ref[c], w1_ref[c], preferred_element_type=jnp.float32)[0:1, :]
        for g in range(N_KV):
            base = (c * N_KV + g) * 2 * CMP_HID
            u = uv_cols(slice(base, base + CMP_HID))
            v = uv_cols(slice(base + CMP_HID, base + 2 * CMP_HID))
            pre = u + pltpu.roll(v, n_ch - 1, axis=0) + pe_term
            hid = _gelu_tanh(pre).astype(jnp.bfloat16)
            res = jnp.dot(hid, w2_ref[c], preferred_element_type=jnp.float32).astype(out_ref.dtype)
            if out_ref.shape[1] == 2:
                out_ref[0, c, :, g * HEAD_DIM:(g + 1) * HEAD_DIM] = res
            else:
                out_ref[0, c * N_KV + g] = res


def _cmp_finish(uv, n_seq, n_ch, w_cmp1_l, w_cmp2_l, cmp_pe_l, name):
    pe = jnp.zeros((2, 8, CMP_LEN * HEAD_DIM), jnp.float32).at[:, 0].set(cmp_pe_l.reshape(2, -1))
    out_dims = (2 * N_KV, n_ch, HEAD_DIM)
    return pl.pallas_call(
        _cmp_finish_body,
        grid=(n_seq,),
        in_specs=[pl.BlockSpec((1, n_ch, UV_COLS), lambda b: (b, 0, 0)),
                  pl.BlockSpec((2, 8, CMP_LEN * HEAD_DIM), lambda b: (0, 0, 0)),
                  pl.BlockSpec((2, CMP_LEN * HEAD_DIM, CMP_HID), lambda b: (0, 0, 0)),
                  pl.BlockSpec((2, CMP_HID, HEAD_DIM), lambda b: (0, 0, 0))],
        out_specs=pl.BlockSpec((1,) + out_dims, lambda b: (b, 0, 0, 0)),
        out_shape=jax.ShapeDtypeStruct((n_seq,) + out_dims, jnp.bfloat16),
        compiler_params=pltpu.CompilerParams(dimension_semantics=("parallel",),
                                             vmem_limit_bytes=VMEM_LIMIT),
        name=name,
    )(uv.reshape(n_seq, n_ch, UV_COLS), pe.astype(jnp.bfloat16),
      w_cmp1_l.reshape(2, CMP_LEN * HEAD_DIM, CMP_HID).astype(jnp.bfloat16),
      w_cmp2_l.astype(jnp.bfloat16))


SLC_TK = 512
WIN_KEYS = WINDOW + QBLK


def _masked_softmax(s, valid):
    s = jnp.where(valid, s, NEG)
    m = jnp.max(s, axis=-1, keepdims=True)
    e = jnp.where(valid, jnp.exp(s - m), 0.0)
    return e / jnp.maximum(jnp.sum(e, axis=-1, keepdims=True), 1e-30)


def _split_bf16(x):
    hi = x.astype(jnp.bfloat16)
    return hi, (x - hi.astype(jnp.float32)).astype(jnp.bfloat16)


SUBLANES = 8


def _top_blocks_t(imp_t):
    nb, nq = imp_t.shape
    groups = [imp_t[SUBLANES * v:SUBLANES * (v + 1), :] for v in range(nb // SUBLANES)]
    sub = lax.broadcasted_iota(jnp.int32, (SUBLANES, nq), 0)
    beaten_by = [jnp.zeros((SUBLANES, nq), jnp.float32) for _ in groups]
    for i in range(nb):
        row = jnp.broadcast_to(imp_t[i:i + 1, :], (SUBLANES, nq))
        for v, gv in enumerate(groups):
            if v > i // SUBLANES:
                beats = row >= gv
            elif v < i // SUBLANES:
                beats = row > gv
            else:
                beats = (row > gv) | ((row == gv) & (sub > i % SUBLANES))
            beaten_by[v] = beaten_by[v] + jnp.where(beats, 1.0, 0.0)
    return jnp.concatenate([jnp.where(c < N_SEL, 1.0, 0.0) for c in beaten_by], axis=0)


def _split3_bf16(x):
    x1 = x.astype(jnp.bfloat16)
    r = x - x1.astype(jnp.float32)
    x2 = r.astype(jnp.bfloat16)
    return x1, x2, (r - x2.astype(jnp.float32)).astype(jnp.bfloat16)


XK = 2 * LANES


def _slc_key_columns(t):
    kpos = np.arange(t)
    cols = np.zeros((t, XK - HEAD_DIM), np.float32)
    cols[kpos, kpos // SEL_BLOCK] = 1.0
    cols[:, 64:67] = (kpos // 64)[:, None]
    cols[:, 67:70] = (kpos % 64)[:, None]
    cols[:, 70:73] = 1.0
    return cols


def _slc_query_columns():
    s1, s2, s3 = _split3_bf16(_alibi_slopes().reshape(N_KV, HPG, 1))
    cols = jnp.concatenate([64.0 * s1, 64.0 * s2, 64.0 * s3, s1, s2, s3], axis=-1)
    cols = jnp.pad(cols, ((0, 0), (0, 0), (0, LANES - 6)))
    return jnp.broadcast_to(cols[:, :, None, :], (N_KV, HPG, QBLK, LANES)).reshape(N_KV, HPG * QBLK, LANES)


def _nsa_prompt_body(q_ref, kc_ref, vc_ref, kx_ref, vs_ref, kw_ref, vw_ref, ng_ref, ov_ref, sx_ref, cb_ref, sl_ref,
                     o_ref, *, n_cmp):
    i = pl.program_id(2)
    nq = QBLK
    rows = HPG * nq
    q2 = (q_ref[...] * SCALE).reshape(rows, HEAD_DIM)
    slopes = sl_ref[0]
    qpos = i * nq + lax.broadcasted_iota(jnp.int32, (nq, 1), 0)
    nt = (((1,), (1,)), ((), ()))

    n_ch = kc_ref.shape[2]
    s = lax.dot_general(q2, kc_ref[0, 0], nt, preferred_element_type=jnp.float32).reshape(HPG, nq, n_ch)
    cidx = lax.broadcasted_iota(jnp.int32, (1, n_ch), 1)
    dist_c = qpos - (cidx * CMP_STRIDE + (CMP_LEN - 1))
    valid_c = ((dist_c >= 0) & (cidx < n_cmp))[None]
    p = _masked_softmax(s - slopes * dist_c.astype(jnp.float32)[None], valid_c)
    o_cmp = jnp.dot(p.reshape(rows, n_ch).astype(jnp.bfloat16), vc_ref[0, 0],
                    preferred_element_type=jnp.float32).reshape(HPG, nq, HEAD_DIM)

    p_hi, p_lo = _split_bf16(p[0] + p[1] + p[2] + p[3])
    imp_t = (lax.dot_general(ov_ref[...], p_hi, nt, preferred_element_type=jnp.float32)
             + lax.dot_general(ov_ref[...], p_lo, nt, preferred_element_type=jnp.float32))
    nsb = imp_t.shape[0]
    blk = lax.broadcasted_iota(jnp.int32, (nsb, 1), 0)
    cur = (i * nq + lax.broadcasted_iota(jnp.int32, (1, nq), 1)) // SEL_BLOCK
    imp_t = jnp.where(blk > cur, NEG, imp_t)
    imp_t = jnp.where((blk == cur) | (blk == 0), BIG, imp_t)
    sel_t = jnp.where(blk <= cur, _top_blocks_t(imp_t), 0.0)

    eye = (lax.broadcasted_iota(jnp.int32, (nsb, nsb), 0)
           == lax.broadcasted_iota(jnp.int32, (nsb, nsb), 1)).astype(jnp.bfloat16)
    drop = lax.dot_general(((sel_t - 1.0) * BIG).astype(jnp.bfloat16), eye, (((0,), (0,)), ((), ())),
                           preferred_element_type=jnp.float32)
    drop = jnp.concatenate([drop.astype(jnp.bfloat16)] * HPG, axis=0)
    slope_row = jnp.broadcast_to(slopes, (HPG, nq, 1)).reshape(rows, 1)
    qpos_row = jnp.concatenate([qpos] * HPG, axis=0).astype(jnp.float32)
    c1, c2, c3 = _split3_bf16(slope_row * qpos_row)
    lane = lax.broadcasted_iota(jnp.int32, (rows, LANES), 1)
    pos_cols = sx_ref[0].astype(jnp.float32)
    for k, ck in enumerate((c1, c2, c3)):
        pos_cols = jnp.where(lane == 6 + k, -ck.astype(jnp.float32), pos_cols)
    q_ext = jnp.concatenate([q2, drop, pos_cols.astype(jnp.bfloat16)], axis=1)

    def slc_tile(kt, carry, causal_bias):
        m, l, acc = carry
        k0 = pl.multiple_of(kt * SLC_TK, SLC_TK)
        s = lax.dot_general(q_ext, kx_ref[0, pl.ds(k0, SLC_TK), :], nt,
                            preferred_element_type=jnp.float32).reshape(HPG, nq, SLC_TK)
        if causal_bias is not None:
            s = s + causal_bias[None]
        m_new = jnp.maximum(m, jnp.max(s, axis=-1, keepdims=True))
        a = jnp.exp(m - m_new)
        e = jnp.exp(s - m_new)
        l = a * l + jnp.sum(e, axis=-1, keepdims=True)
        pv = jnp.dot(e.reshape(rows, SLC_TK).astype(jnp.bfloat16), vs_ref[0, pl.ds(k0, SLC_TK), :],
                     preferred_element_type=jnp.float32).reshape(HPG, nq, HEAD_DIM)
        return m_new, l, a * acc + pv

    init = (jnp.full((HPG, nq, 1), NEG, jnp.float32), jnp.zeros((HPG, nq, 1), jnp.float32),
            jnp.zeros((HPG, nq, HEAD_DIM), jnp.float32))
    last = (i * nq) // SLC_TK
    carry = lax.fori_loop(0, last // 2, lambda p, c: slc_tile(2 * p + 1, slc_tile(2 * p, c, None), None), init)
    carry = lax.fori_loop(2 * (last // 2), last, lambda kt, c: slc_tile(kt, c, None), carry)
    _, l, acc = slc_tile(last, carry, cb_ref[i % (SLC_TK // QBLK)])
    o_slc = acc / jnp.maximum(l, 1e-30)

    w0 = pl.multiple_of(jnp.maximum(i - WINDOW // QBLK, 0) * nq, nq)
    s = lax.dot_general(q2, kw_ref[0, pl.ds(w0, WIN_KEYS), :], nt,
                        preferred_element_type=jnp.float32).reshape(HPG, nq, WIN_KEYS)
    dist = qpos - (w0 + lax.broadcasted_iota(jnp.int32, (1, WIN_KEYS), 1))
    valid = ((dist >= 0) & (dist < WINDOW))[None]
    pw = _masked_softmax(s - slopes * dist.astype(jnp.float32)[None], valid)
    o_win = jnp.dot(pw.reshape(rows, WIN_KEYS).astype(jnp.bfloat16), vw_ref[0, pl.ds(w0, WIN_KEYS), :],
                    preferred_element_type=jnp.float32).reshape(HPG, nq, HEAD_DIM)

    gates = _sigmoid(ng_ref[0, 0])
    o_ref[...] = jnp.concatenate(
        [gates[:, h:h + 1] * o_cmp[h] + gates[:, HPG + h:HPG + h + 1] * o_slc[h]
         + gates[:, 2 * HPG + h:2 * HPG + h + 1] * o_win[h] for h in range(HPG)], axis=-1).astype(o_ref.dtype)


def _alibi_slopes():
    return jnp.asarray(2.0 ** (-8.0 * np.arange(1, N_HEADS + 1) / N_HEADS), jnp.float32).reshape(N_KV, HPG, 1, 1)


def _nsa_prompt(q_h, kcvc, kv_h, ng_t, n_cmp):
    bsz, _, t, _ = ng_t.shape
    nqb = t // QBLK
    n_ch = kcvc.shape[2]
    nsb = t // SEL_BLOCK
    st = np.arange(n_ch) * CMP_STRIDE
    bs = np.arange(nsb) * SEL_BLOCK
    overlap = ((st[:, None] <= bs[None, :] + SEL_BLOCK - 1) & (st[:, None] + CMP_LEN - 1 >= bs[None, :])
               & (np.arange(n_ch)[:, None] < n_cmp)).astype(np.float32)
    assert nsb == LANES - HEAD_DIM and t % SLC_TK == 0
    key_cols = jnp.asarray(np.tile(_slc_key_columns(t), (bsz, 1)), jnp.bfloat16)
    kx = jnp.concatenate([kv_h[2 * N_KV:3 * N_KV, :bsz * t],
                          jnp.broadcast_to(key_cols[None], (N_KV,) + key_cols.shape)], axis=-1)
    r = np.arange(SLC_TK // QBLK)[:, None, None] * QBLK + np.arange(QBLK)[None, :, None]
    causal = np.where(np.arange(SLC_TK)[None, None, :] <= r, 0.0, NEG).astype(np.float32)

    def kv_spec(slot):
        return pl.BlockSpec((1, t, HEAD_DIM), lambda b, g, i, slot=slot: (slot * N_KV + g, b, 0))

    return pl.pallas_call(
        functools.partial(_nsa_prompt_body, n_cmp=n_cmp),
        grid=(bsz, N_KV, nqb),
        in_specs=[pl.BlockSpec((HPG, QBLK, HEAD_DIM), lambda b, g, i: (g, b * nqb + i, 0)),
                  pl.BlockSpec((1, 1, n_ch, HEAD_DIM), lambda b, g, i: (b, g, 0, 0)),
                  pl.BlockSpec((1, 1, n_ch, HEAD_DIM), lambda b, g, i: (b, N_KV + g, 0, 0)),
                  pl.BlockSpec((1, t, XK), lambda b, g, i: (g, b, 0)),
                  kv_spec(3), kv_spec(4), kv_spec(5),
                  pl.BlockSpec((1, 1, QBLK, 3 * HPG), lambda b, g, i: (b, g, i, 0)),
                  pl.BlockSpec((nsb, n_ch), lambda b, g, i: (0, 0)),
                  pl.BlockSpec((1, HPG * QBLK, LANES), lambda b, g, i: (g, 0, 0)),
                  pl.BlockSpec(causal.shape, lambda b, g, i: (0, 0, 0)),
                  pl.BlockSpec((1, HPG, 1, 1), lambda b, g, i: (g, 0, 0, 0))],
        out_specs=pl.BlockSpec((QBLK, HPG * HEAD_DIM), lambda b, g, i: (b * nqb + i, g)),
        out_shape=jax.ShapeDtypeStruct((bsz * t, NSA_WIDTH), jnp.bfloat16),
        compiler_params=pltpu.CompilerParams(
            dimension_semantics=("parallel", "parallel", "arbitrary"), vmem_limit_bytes=VMEM_LIMIT),
        name="nsa_prompt",
    )(q_h, kcvc, kcvc, kx, kv_h, kv_h, kv_h, ng_t,
      jnp.asarray(overlap.T, jnp.bfloat16), _slc_query_columns(), jnp.asarray(causal), _alibi_slopes())


HALF_ROWS = 2 * N_KV * HEAD_DIM
GD = N_KV * HEAD_DIM


def _fetch_pages(pt_ref, cache_ref, sem, row0, per_step, dst_of):
    b = pl.program_id(0)
    slot = b % 2
    n_pages = pt_ref.shape[1]

    def page_copy(flat_page, sl, k):
        phys = pt_ref[flat_page // n_pages, flat_page % n_pages]
        return pltpu.make_async_copy(cache_ref.at[phys, pl.ds(row0, HALF_ROWS), :], dst_of(sl, k), sem.at[sl])

    def start_all(step, sl):
        def one(k, carry):
            page_copy(step * per_step + k, sl, k).start()
            return carry
        lax.fori_loop(0, per_step, one, 0)

    @pl.when(b == 0)
    def _():
        start_all(0, 0)

    @pl.when(b + 1 < pl.num_programs(0))
    def _():
        start_all(b + 1, 1 - slot)

    def wait_one(k, carry):
        page_copy(b * per_step + k, slot, k).wait()
        return carry
    lax.fori_loop(0, per_step, wait_one, 0)
    return slot


def _cmp_sample_body(pt_ref, cache_ref, w_ref, pe_ref, w1_ref, w2_ref, out_ref, buf, sem, xt, uv, *, steps):
    per_step = buf.shape[1]
    slot = _fetch_pages(pt_ref, cache_ref, sem, 0, per_step, lambda sl, k: buf.at[sl, k])
    n_cb = HALF_ROWS // LANES

    eye = (lax.broadcasted_iota(jnp.int32, (LANES, LANES), 0)
           == lax.broadcasted_iota(jnp.int32, (LANES, LANES), 1)).astype(jnp.bfloat16)

    def to_rows(k, carry):
        for cb in range(n_cb):
            blk = buf[slot, k, cb * LANES:(cb + 1) * LANES, :].astype(jnp.bfloat16)
            xt[k, cb] = lax.dot_general(eye, blk, (((1,), (1,)), ((), ())), preferred_element_type=jnp.float32)
        return carry
    lax.fori_loop(0, per_step, to_rows, 0, unroll=8)

    per_page = PAGE_SIZE // CMP_STRIDE
    n = per_step * per_page

    def tap(cgp, j):
        return xt[:, cgp, pl.ds(j, per_page, stride=CMP_STRIDE), :].reshape(n, LANES).astype(jnp.bfloat16)

    part = pl.program_id(0) % steps
    row0 = pl.multiple_of(part * n, n)
    for cgp in range(n_cb):
        uv[pl.ds(row0, n), cgp * 512:(cgp + 1) * 512] = _uv_taps(functools.partial(tap, cgp), w_ref, cgp // 2, n)

    @pl.when(part == steps - 1)
    def _():
        _cmp_finish_rows(lambda cols: uv[:, cols], uv.shape[0], pe_ref, w1_ref, w2_ref, out_ref)


UV_PAGES = 32


def _cmp_sample(page_table, cache_t, w1p, w_cmp1_l, w_cmp2_l, cmp_pe_l):
    n_seq, n_pages = page_table.shape
    assert n_pages % UV_PAGES == 0
    steps = n_pages // UV_PAGES
    n_ch = n_pages * (PAGE_SIZE // CMP_STRIDE)
    pe = jnp.zeros((2, 8, CMP_LEN * HEAD_DIM), jnp.float32).at[:, 0].set(cmp_pe_l.reshape(2, -1))
    return pl.pallas_call(
        functools.partial(_cmp_sample_body, steps=steps),
        grid_spec=pltpu.PrefetchScalarGridSpec(
            num_scalar_prefetch=1, grid=(n_seq * steps,),
            in_specs=[pl.BlockSpec(memory_space=pl.ANY),
                      pl.BlockSpec(w1p.shape, lambda b, pt: (0, 0, 0, 0)),
                      pl.BlockSpec((2, 8, CMP_LEN * HEAD_DIM), lambda b, pt: (0, 0, 0)),
                      pl.BlockSpec((2, CMP_LEN * HEAD_DIM, CMP_HID), lambda b, pt: (0, 0, 0)),
                      pl.BlockSpec((2, CMP_HID, HEAD_DIM), lambda b, pt: (0, 0, 0))],
            out_specs=pl.BlockSpec((1, 2, n_ch, GD), lambda b, pt: (b // steps, 0, 0, 0)),
            scratch_shapes=[pltpu.VMEM((2, UV_PAGES, HALF_ROWS, PAGE_SIZE), jnp.float32),
                            pltpu.SemaphoreType.DMA((2,)),
                            pltpu.VMEM((UV_PAGES, HALF_ROWS // LANES, PAGE_SIZE, LANES), jnp.float32),
                            pltpu.VMEM((n_ch, UV_COLS), jnp.float32)]),
        out_shape=jax.ShapeDtypeStruct((n_seq, 2, n_ch, GD), jnp.bfloat16),
        compiler_params=pltpu.CompilerParams(dimension_semantics=("arbitrary",),
                                             vmem_limit_bytes=VMEM_LIMIT),
        name="cmp_sample",
    )(page_table, cache_t, w1p, pe.astype(jnp.bfloat16),
      w_cmp1_l.reshape(2, CMP_LEN * HEAD_DIM, CMP_HID).astype(jnp.bfloat16), w_cmp2_l.astype(jnp.bfloat16))


def _group_diag(x, rows_per_group):
    grp = lax.broadcasted_iota(jnp.int32, (x.shape[0], 1), 0) // rows_per_group
    out = jnp.zeros((x.shape[0], HEAD_DIM), x.dtype)
    for g in range(N_KV):
        out = out + jnp.where(grp == g, x[:, g * HEAD_DIM:(g + 1) * HEAD_DIM], 0.0)
    return out


BIAS_POS_ROWS = 16


def _sample_bias_rows(n_lanes, past_len):
    kpos = np.arange(past_len)
    rows = np.zeros((n_lanes + BIAS_POS_ROWS, past_len), np.float32)
    rows[kpos // SEL_BLOCK, kpos] = 1.0
    rows[n_lanes:n_lanes + 3] = kpos // 64
    rows[n_lanes + 3:n_lanes + 6] = kpos % 64
    rows[n_lanes + 6:n_lanes + 9] = 1.0
    return rows


def _nsa_sample_body(pt_ref, cache_ref, q_ref, kcvc_ref, new_ref, win_ref, ng_ref, ov_ref, same_q_ref, sl_ref,
                     bq_ref, bk_ref, o_ref, buf, sem, *, n_cmp, t_new):
    past_len = buf.shape[2]
    slot = _fetch_pages(pt_ref, cache_ref, sem, HALF_ROWS, past_len // PAGE_SIZE,
                        lambda sl, k: buf.at[sl, :, pl.ds(pl.multiple_of(k * PAGE_SIZE, PAGE_SIZE), PAGE_SIZE)])
    rows = q_ref.shape[1]
    rpg = rows // N_KV
    nt = (((1,), (1,)), ((), ()))
    qbd = q_ref[0] * SCALE
    slope = sl_ref[...]
    row = lax.broadcasted_iota(jnp.int32, (rows, 1), 0)
    qtok = row % t_new
    qpos = past_len + qtok
    new_rows = new_ref.shape[2]
    new_idx = lax.broadcasted_iota(jnp.int32, (1, new_rows), 1)
    new_valid = (new_idx <= qtok) & (new_idx < t_new)
    new_bias = slope * (qtok - new_idx).astype(jnp.float32)

    n_ch = kcvc_ref.shape[2]
    s = lax.dot_general(qbd, kcvc_ref[0, 0], nt, preferred_element_type=jnp.float32)
    cidx = lax.broadcasted_iota(jnp.int32, (1, n_ch), 1)
    dist_c = qpos - (cidx * CMP_STRIDE + (CMP_LEN - 1))
    p = _masked_softmax(s - slope * dist_c.astype(jnp.float32), (dist_c >= 0) & (cidx < n_cmp))
    o_cmp = _group_diag(jnp.dot(p.astype(jnp.bfloat16), kcvc_ref[0, 1], preferred_element_type=jnp.float32), rpg)

    p_hi, p_lo = _split_bf16(p)
    psum = (jnp.dot(same_q_ref[...], p_hi, preferred_element_type=jnp.float32)
            + jnp.dot(same_q_ref[...], p_lo, preferred_element_type=jnp.float32))
    ps_hi, ps_lo = _split_bf16(psum)
    imp = (jnp.dot(ps_hi, ov_ref[...], preferred_element_type=jnp.float32)
           + jnp.dot(ps_lo, ov_ref[...], preferred_element_type=jnp.float32))
    n_lanes = imp.shape[-1]
    blk = lax.broadcasted_iota(jnp.int32, (1, n_lanes), 1)
    cur = qpos // SEL_BLOCK
    imp = jnp.where(blk > cur, -jnp.inf, imp)
    imp = jnp.where((blk == cur) | (blk == 0), BIG, imp)
    sel = jnp.zeros(imp.shape, jnp.float32)
    for _ in range(N_SEL):
        m = jnp.max(imp, axis=-1, keepdims=True)
        idx = jnp.min(jnp.where(imp == m, blk, n_lanes), axis=-1, keepdims=True)
        sel = jnp.where(blk == idx, 1.0, sel)
        imp = jnp.where(blk == idx, -jnp.inf, imp)

    c1, c2, c3 = _split3_bf16(slope * qpos.astype(jnp.float32))
    lane = lax.broadcasted_iota(jnp.int32, (rows, bq_ref.shape[1]), 1)
    pos_cols = bq_ref[...].astype(jnp.float32)
    for k, ck in enumerate((c1, c2, c3)):
        pos_cols = jnp.where(lane == 6 + k, -ck.astype(jnp.float32), pos_cols)
    bias_q = jnp.concatenate([((sel - 1.0) * BIG).astype(jnp.bfloat16), pos_cols.astype(jnp.bfloat16)], axis=1)
    s_all = (jnp.dot(qbd, buf[slot, 0:GD, :].astype(jnp.bfloat16), preferred_element_type=jnp.float32)
             + jnp.dot(bias_q, bk_ref[...], preferred_element_type=jnp.float32))
    s_new = lax.dot_general(qbd, new_ref[0, 2], nt, preferred_element_type=jnp.float32)
    s_new = jnp.where(new_valid, s_new - new_bias, NEG)
    m = jnp.maximum(jnp.max(s_all, axis=-1, keepdims=True), jnp.max(s_new, axis=-1, keepdims=True))
    e_all = jnp.exp(s_all - m)
    e_new = jnp.where(new_valid, jnp.exp(s_new - m), 0.0)
    l = jnp.sum(e_all, axis=-1, keepdims=True) + jnp.sum(e_new, axis=-1, keepdims=True)
    acc = (lax.dot_general(e_all.astype(jnp.bfloat16), buf[slot, GD:2 * GD, :].astype(jnp.bfloat16), nt,
                           preferred_element_type=jnp.float32)
           + jnp.dot(e_new.astype(jnp.bfloat16), new_ref[0, 3], preferred_element_type=jnp.float32))
    o_slc = _group_diag(acc, rpg) / jnp.maximum(l, 1e-30)

    w_buf = win_ref.shape[2]
    wdist = qpos - (past_len - w_buf + lax.broadcasted_iota(jnp.int32, (1, w_buf), 1))
    valid_w = (wdist >= 0) & (wdist < WINDOW)
    s_w = jnp.dot(qbd, win_ref[0, 0:GD, :].astype(jnp.bfloat16), preferred_element_type=jnp.float32)
    s_w = jnp.where(valid_w, s_w - slope * wdist.astype(jnp.float32), NEG)
    s_wn = lax.dot_general(qbd, new_ref[0, 4], nt, preferred_element_type=jnp.float32)
    s_wn = jnp.where(new_valid, s_wn - new_bias, NEG)
    m = jnp.maximum(jnp.max(s_w, axis=-1, keepdims=True), jnp.max(s_wn, axis=-1, keepdims=True))
    e_w = jnp.where(valid_w, jnp.exp(s_w - m), 0.0)
    e_wn = jnp.where(new_valid, jnp.exp(s_wn - m), 0.0)
    acc = (lax.dot_general(e_w.astype(jnp.bfloat16), win_ref[0, GD:2 * GD, :].astype(jnp.bfloat16), nt,
                           preferred_element_type=jnp.float32)
           + jnp.dot(e_wn.astype(jnp.bfloat16), new_ref[0, 5], preferred_element_type=jnp.float32))
    l = jnp.sum(e_w, axis=-1, keepdims=True) + jnp.sum(e_wn, axis=-1, keepdims=True)
    o_win = _group_diag(acc, rpg) / jnp.maximum(l, 1e-30)

    gates = _sigmoid(ng_ref[0])
    o_ref[0] = (gates[:, 0:1] * o_cmp + gates[:, 1:2] * o_slc + gates[:, 2:3] * o_win).astype(o_ref.dtype)


def _nsa_sample(page_table, cache_t, q_bd, kcvc, new_kv, win_t, ng_r, n_cmp, t_new):
    n_seq, n_pages = page_table.shape
    past_len = n_pages * PAGE_SIZE
    n_ch = kcvc.shape[2]
    rows = q_bd.shape[1]
    rpg = rows // N_KV
    nsb = -(-(past_len + t_new) // SEL_BLOCK)
    n_lanes = -(-nsb // LANES) * LANES
    st = np.arange(n_ch) * CMP_STRIDE
    bs = np.arange(n_lanes) * SEL_BLOCK
    overlap = ((st[:, None] <= bs[None, :] + SEL_BLOCK - 1) & (st[:, None] + CMP_LEN - 1 >= bs[None, :])
               & (np.arange(n_ch)[:, None] < n_cmp) & (np.arange(n_lanes)[None, :] < nsb)).astype(np.float32)
    r = np.arange(rows)
    same_q = ((r[:, None] // rpg == r[None, :] // rpg) & (r[:, None] % t_new == r[None, :] % t_new)
              ).astype(np.float32)
    slopes = jnp.repeat(_alibi_slopes().reshape(N_HEADS), t_new).reshape(rows, 1)
    s1, s2, s3 = _split3_bf16(slopes)
    bias_q = jnp.pad(jnp.concatenate([64.0 * s1, 64.0 * s2, 64.0 * s3, s1, s2, s3], axis=1),
                     ((0, 0), (0, BIAS_POS_ROWS - 6)))
    bias_k = jnp.asarray(_sample_bias_rows(n_lanes, past_len), jnp.bfloat16)
    w_buf = win_t.shape[2]
    return pl.pallas_call(
        functools.partial(_nsa_sample_body, n_cmp=n_cmp, t_new=t_new),
        grid_spec=pltpu.PrefetchScalarGridSpec(
            num_scalar_prefetch=1, grid=(n_seq,),
            in_specs=[pl.BlockSpec(memory_space=pl.ANY),
                      pl.BlockSpec((1, rows, GD), lambda b, pt: (b, 0, 0)),
                      pl.BlockSpec((1, 2, n_ch, GD), lambda b, pt: (b, 0, 0, 0)),
                      pl.BlockSpec((1, 6, new_kv.shape[2], GD), lambda b, pt: (b, 0, 0, 0)),
                      pl.BlockSpec((1, HALF_ROWS, w_buf), lambda b, pt: (b, 0, 0)),
                      pl.BlockSpec((1, rows, 3), lambda b, pt: (b, 0, 0)),
                      pl.BlockSpec((n_ch, n_lanes), lambda b, pt: (0, 0)),
                      pl.BlockSpec((rows, rows), lambda b, pt: (0, 0)),
                      pl.BlockSpec((rows, 1), lambda b, pt: (0, 0)),
                      pl.BlockSpec(bias_q.shape, lambda b, pt: (0, 0)),
                      pl.BlockSpec(bias_k.shape, lambda b, pt: (0, 0), pipeline_mode=pl.Buffered(1))],
            out_specs=pl.BlockSpec((1, rows, HEAD_DIM), lambda b, pt: (b, 0, 0)),
            scratch_shapes=[pltpu.VMEM((2, HALF_ROWS, past_len), jnp.float32),
                            pltpu.SemaphoreType.DMA((2,))]),
        out_shape=jax.ShapeDtypeStruct((n_seq, rows, HEAD_DIM), jnp.bfloat16),
        compiler_params=pltpu.CompilerParams(dimension_semantics=("arbitrary",),
                                             vmem_limit_bytes=VMEM_LIMIT),
        name="nsa_sample",
    )(page_table, cache_t, q_bd, kcvc, new_kv, win_t, ng_r,
      jnp.asarray(overlap, jnp.bfloat16), jnp.asarray(same_q, jnp.bfloat16), slopes, bias_q, bias_k)


def _mix_body(a_ref, b_ref, mga_ref, mgb_ref, wpa_ref, wpb_ref, u_ref):
    a = jnp.dot(a_ref[...], wpa_ref[...], preferred_element_type=jnp.float32)
    b = jnp.dot(b_ref[...], wpb_ref[...], preferred_element_type=jnp.float32)
    u_ref[...] = (_sigmoid(mga_ref[...]) * a + _sigmoid(mgb_ref[...]) * b).astype(u_ref.dtype)


def _mix(o_nsa, o_hg, mg, w_pa_bf, w_pb_bf, tm):
    n = o_nsa.shape[0]
    const = lambda i: (0, 0)
    return pl.pallas_call(
        _mix_body,
        grid=(n // tm,),
        in_specs=[pl.BlockSpec((tm, NSA_WIDTH), lambda i: (i, 0)),
                  pl.BlockSpec((tm, HG_WIDTH), lambda i: (i, 0)),
                  pl.BlockSpec((tm, D_MODEL), lambda i: (i, 0)),
                  pl.BlockSpec((tm, D_MODEL), lambda i: (i, 1)),
                  pl.BlockSpec((NSA_WIDTH, D_MODEL), const, pipeline_mode=pl.Buffered(1)),
                  pl.BlockSpec((HG_WIDTH, D_MODEL), const, pipeline_mode=pl.Buffered(1))],
        out_specs=pl.BlockSpec((tm, D_MODEL), lambda i: (i, 0)),
        out_shape=jax.ShapeDtypeStruct((n, D_MODEL), jnp.bfloat16),
        compiler_params=pltpu.CompilerParams(dimension_semantics=("parallel",),
                                             vmem_limit_bytes=VMEM_LIMIT),
        name="tail_mix",
    )(o_nsa, o_hg, mg, mg, w_pa_bf, w_pb_bf)


def _layer_norm(z, g, b):
    mu = jnp.mean(z, axis=-1, keepdims=True)
    zc = z - mu
    var = jnp.mean(zc * zc, axis=-1, keepdims=True)
    return zc * lax.rsqrt(var + LN_EPS) * g + b


CH = D_MODEL // LANES


def _store_chunked(ref, val):
    tm = val.shape[0]
    for k in range(CH):
        ref[pl.ds(k, tm, stride=CH), :] = val[:, k * LANES:(k + 1) * LANES]


def _load_chunked(ref, tm, lead=()):
    return jnp.concatenate([ref[lead + (pl.ds(k, tm, stride=CH), slice(None))] for k in range(CH)], axis=1)


def _ln1_body(u_ref, x_ref, wout_ref, g_ref, b_ref, wr_hi_ref, wr_lo_ref, br_ref, h_ref, lg_ref):
    y = jnp.dot(u_ref[...], wout_ref[...], preferred_element_type=jnp.float32)
    h = _layer_norm(DN_ALPHA * x_ref[...] + y, g_ref[...], b_ref[...])
    _store_chunked(h_ref, h)
    h_hi = h.astype(jnp.bfloat16)
    h_lo = (h - h_hi.astype(jnp.float32)).astype(jnp.bfloat16)
    lg = jnp.dot(h_hi, wr_hi_ref[...], preferred_element_type=jnp.float32)
    lg = lg + jnp.dot(h_lo, wr_hi_ref[...], preferred_element_type=jnp.float32)
    lg = lg + jnp.dot(h_hi, wr_lo_ref[...], preferred_element_type=jnp.float32)
    lg_ref[...] = lg + br_ref[...]


def _ln1(u, x_all, w_out_bf, g, b, wr_hi, wr_lo, br, tm):
    n = u.shape[0]
    const = lambda i: (0, 0)
    return pl.pallas_call(
        _ln1_body,
        grid=(n // tm,),
        in_specs=[pl.BlockSpec((tm, D_MODEL), lambda i: (i, 0)),
                  pl.BlockSpec((tm, D_MODEL), lambda i: (i, 0)),
                  pl.BlockSpec((D_MODEL, D_MODEL), const, pipeline_mode=pl.Buffered(1)),
                  pl.BlockSpec((1, D_MODEL), const),
                  pl.BlockSpec((1, D_MODEL), const),
                  pl.BlockSpec((D_MODEL, LANES), const),
                  pl.BlockSpec((D_MODEL, LANES), const),
                  pl.BlockSpec((1, LANES), const)],
        out_specs=[pl.BlockSpec((tm * CH, LANES), lambda i: (i, 0)),
                   pl.BlockSpec((tm, LANES), lambda i: (i, 0))],
        out_shape=[jax.ShapeDtypeStruct((n * CH, LANES), jnp.float32),
                   jax.ShapeDtypeStruct((n, LANES), jnp.float32)],
        compiler_params=pltpu.CompilerParams(dimension_semantics=("parallel",),
                                             vmem_limit_bytes=VMEM_LIMIT),
        name="tail_ln1",
    )(u, x_all, w_out_bf, g.reshape(1, -1), b.reshape(1, -1), wr_hi, wr_lo, br)


LG0 = N_GROUPS


def _route_body(lg_ref, tri_ref, out_ref, cnt_ref, carry_scr):
    @pl.when(pl.program_id(0) == 0)
    def _():
        carry_scr[...] = jnp.zeros_like(carry_scr)

    lg = lg_ref[...]
    tm = lg.shape[0]
    lane = lax.broadcasted_iota(jnp.int32, lg.shape, 1)
    is_g = lane < N_GROUPS
    gl = jnp.where(is_g, lg, NEG)
    gmax = jnp.max(gl, axis=-1, keepdims=True)
    grp = jnp.min(jnp.where(gl == gmax, lane, LANES), axis=-1, keepdims=True)
    g_w = 1.0 / jnp.sum(jnp.where(is_g, jnp.exp(lg - gmax), 0.0), axis=-1, keepdims=True)
    lo = LG0 + grp * EXP_PER_GROUP
    el = jnp.where((lane >= lo) & (lane < lo + EXP_PER_GROUP), lg, NEG)
    v1 = jnp.max(el, axis=-1, keepdims=True)
    i1 = jnp.min(jnp.where(el == v1, lane, LANES), axis=-1, keepdims=True)
    el2 = jnp.where(lane == i1, NEG, el)
    v2 = jnp.max(el2, axis=-1, keepdims=True)
    i2 = jnp.min(jnp.where(el2 == v2, lane, LANES), axis=-1, keepdims=True)
    e21 = jnp.exp(v2 - v1)
    w1 = g_w / (1.0 + e21)
    w2 = g_w * e21 / (1.0 + e21)
    hit1 = lane == i1
    hit2 = lane == i2
    onehot = jnp.where(hit1 | hit2, 1.0, 0.0)
    incl = jnp.dot(tri_ref[...], onehot.astype(jnp.bfloat16), preferred_element_type=jnp.float32)
    carry = carry_scr[...]
    before = incl - onehot + carry
    r1 = jnp.sum(jnp.where(hit1, before, 0.0), axis=-1, keepdims=True)
    r2 = jnp.sum(jnp.where(hit2, before, 0.0), axis=-1, keepdims=True)
    carry = carry + incl[tm - 1:tm, :]
    carry_scr[...] = carry
    cnt_ref[...] = carry
    out = jnp.where(lane == 0, (i1 - LG0).astype(jnp.float32), 0.0)
    out = jnp.where(lane == 1, (i2 - LG0).astype(jnp.float32), out)
    out = jnp.where(lane == 2, w1, out)
    out = jnp.where(lane == 3, w2, out)
    out = jnp.where(lane == 4, r1, out)
    out = jnp.where(lane == 5, r2, out)
    out_ref[...] = out


def _route(lg, tm):
    n = lg.shape[0]
    tri = (np.arange(tm)[:, None] >= np.arange(tm)[None, :]).astype(np.float32)
    return pl.pallas_call(
        _route_body,
        grid=(n // tm,),
        in_specs=[pl.BlockSpec((tm, LANES), lambda i: (i, 0)),
                  pl.BlockSpec((tm, tm), lambda i: (0, 0))],
        out_specs=[pl.BlockSpec((tm, LANES), lambda i: (i, 0)),
                   pl.BlockSpec((1, LANES), lambda i: (0, 0))],
        out_shape=[jax.ShapeDtypeStruct((n, LANES), jnp.float32),
                   jax.ShapeDtypeStruct((1, LANES), jnp.float32)],
        scratch_shapes=[pltpu.VMEM((1, LANES), jnp.float32)],
        compiler_params=pltpu.CompilerParams(dimension_semantics=("arbitrary",)),
        name="moe_route",
    )(lg, jnp.asarray(tri, jnp.bfloat16))


def _dispatch_body(pos_ref, h_ref, xs_in_ref, xs_ref, sem):
    del xs_in_ref
    tm = h_ref.shape[0] // CH
    base = pl.program_id(0) * tm

    def issue(t, carry):
        src = pl.multiple_of(t * CH, CH)
        for slot in range(2):
            dst = pl.multiple_of(pos_ref[2 * (base + t) + slot] * CH, CH)
            pltpu.make_async_copy(h_ref.at[pl.ds(src, CH)], xs_ref.at[pl.ds(dst, CH)], sem).start()
        return carry

    lax.fori_loop(0, tm, issue, 0)
    for _ in range(2):
        pltpu.make_async_copy(h_ref, xs_ref.at[pl.ds(0, tm * CH)], sem).wait()


DISPATCH_TM = 128


def _dispatch(pos_flat, h_c, n_slots):
    zeros = jnp.zeros((n_slots * CH, LANES), h_c.dtype)
    tm = DISPATCH_TM
    return pl.pallas_call(
        _dispatch_body,
        grid_spec=pltpu.PrefetchScalarGridSpec(
            num_scalar_prefetch=1, grid=(h_c.shape[0] // (tm * CH),),
            in_specs=[pl.BlockSpec((tm * CH, LANES), lambda i, pos: (i, 0)),
                      pl.BlockSpec(memory_space=pl.ANY)],
            out_specs=pl.BlockSpec(memory_space=pl.ANY),
            scratch_shapes=[pltpu.SemaphoreType.DMA(())]),
        out_shape=jax.ShapeDtypeStruct(zeros.shape, h_c.dtype),
        input_output_aliases={2: 0},
        compiler_params=pltpu.CompilerParams(dimension_semantics=("arbitrary",)),
        name="moe_dispatch",
    )(pos_flat, h_c, zeros)


def _ffn_body(te_ref, nu_ref, x_ref, wg_ref, wu_ref, wd_ref, y_ref, wg_bf, wu_bf, wd_bf):
    i = pl.program_id(0)
    prev = te_ref[jnp.maximum(i - 1, 0)]

    @pl.when((i == 0) | (te_ref[i] != prev))
    def _():
        wg_bf[...] = wg_ref[0].astype(jnp.bfloat16)
        wu_bf[...] = wu_ref[0].astype(jnp.bfloat16)
        wd_bf[...] = wd_ref[0].astype(jnp.bfloat16)

    @pl.when(i < nu_ref[0])
    def _():
        x = _load_chunked(x_ref, x_ref.shape[0] // CH).astype(jnp.bfloat16)
        g = jnp.dot(x, wg_bf[...], preferred_element_type=jnp.float32)
        u = jnp.dot(x, wu_bf[...], preferred_element_type=jnp.float32)
        hid = (g * _sigmoid(g) * u).astype(jnp.bfloat16)
        _store_chunked(y_ref, jnp.dot(hid, wd_bf[...], preferred_element_type=jnp.float32))

    @pl.when(i >= nu_ref[0])
    def _():
        y_ref[...] = jnp.zeros_like(y_ref)


def _ffn(tile_expert, n_used, xs_c, w_gate, w_up, w_down, tm):
    d = D_MODEL
    nt = xs_c.shape[0] // (tm * CH)
    return pl.pallas_call(
        _ffn_body,
        grid_spec=pltpu.PrefetchScalarGridSpec(
            num_scalar_prefetch=2, grid=(nt,),
            in_specs=[pl.BlockSpec((tm * CH, LANES), lambda i, te, nu: (i, 0)),
                      pl.BlockSpec((1, d, D_EXPERT), lambda i, te, nu: (te[i], 0, 0)),
                      pl.BlockSpec((1, d, D_EXPERT), lambda i, te, nu: (te[i], 0, 0)),
                      pl.BlockSpec((1, D_EXPERT, d), lambda i, te, nu: (te[i], 0, 0))],
            out_specs=pl.BlockSpec((tm * CH, LANES), lambda i, te, nu: (i, 0)),
            scratch_shapes=[pltpu.VMEM((d, D_EXPERT), jnp.bfloat16),
                            pltpu.VMEM((d, D_EXPERT), jnp.bfloat16),
                            pltpu.VMEM((D_EXPERT, d), jnp.bfloat16)]),
        out_shape=jax.ShapeDtypeStruct(xs_c.shape, jnp.float32),
        compiler_params=pltpu.CompilerParams(dimension_semantics=("arbitrary",),
                                             vmem_limit_bytes=VMEM_LIMIT),
        name="moe_ffn",
    )(tile_expert, n_used, xs_c, w_gate, w_up, w_down)


def _combine_body(pos_ref, h_ref, rw_ref, g_ref, b_ref, y_ref, out_a_ref, out_b_ref, buf, sem, *, tiles_a):
    tm = out_a_ref.shape[0]
    i = pl.program_id(0)
    cur = i % 2

    def gather(tile, parity):
        def one(t, carry):
            for slot in range(2):
                src = pl.multiple_of(pos_ref[2 * (tile * tm + t) + slot] * CH, CH)
                dst = pl.multiple_of(t * CH, CH)
                pltpu.make_async_copy(y_ref.at[pl.ds(src, CH)], buf.at[parity, slot, pl.ds(dst, CH)],
                                      sem.at[parity, slot]).start()
            return carry
        lax.fori_loop(0, tm, one, 0)

    @pl.when(i == 0)
    def _():
        gather(0, 0)

    @pl.when(i + 1 < pl.num_programs(0))
    def _():
        gather(i + 1, 1 - cur)

    for slot in range(2):
        pltpu.make_async_copy(y_ref.at[pl.ds(0, tm * CH)], buf.at[cur, slot], sem.at[cur, slot]).wait()
    rw = rw_ref[...]
    z = rw[:, 2:3] * _load_chunked(buf, tm, (cur, 0)) + rw[:, 3:4] * _load_chunked(buf, tm, (cur, 1))
    res = _layer_norm(DN_ALPHA * _load_chunked(h_ref, tm) + z, g_ref[...], b_ref[...])

    @pl.when(i < tiles_a)
    def _():
        out_a_ref[...] = res

    @pl.when(i >= tiles_a)
    def _():
        out_b_ref[...] = res


def _combine(pos_flat, h_c, route_out, g, b, y_c, tm, n_a):
    n, d = h_c.shape[0] // CH, D_MODEL
    assert n_a % tm == 0 and (n - n_a) % tm == 0 and 0 < n_a < n
    tiles_a = n_a // tm
    return pl.pallas_call(
        functools.partial(_combine_body, tiles_a=tiles_a),
        grid_spec=pltpu.PrefetchScalarGridSpec(
            num_scalar_prefetch=1, grid=(n // tm,),
            in_specs=[pl.BlockSpec((tm * CH, LANES), lambda i, pos: (i, 0)),
                      pl.BlockSpec((tm, LANES), lambda i, pos: (i, 0)),
                      pl.BlockSpec((1, d), lambda i, pos: (0, 0)),
                      pl.BlockSpec((1, d), lambda i, pos: (0, 0)),
                      pl.BlockSpec(memory_space=pl.ANY)],
            out_specs=[pl.BlockSpec((tm, d), lambda i, pos: (jnp.minimum(i, tiles_a - 1), 0)),
                       pl.BlockSpec((tm, d), lambda i, pos: (jnp.maximum(i - tiles_a, 0), 0))],
            scratch_shapes=[pltpu.VMEM((2, 2, tm * CH, LANES), jnp.float32),
                            pltpu.SemaphoreType.DMA((2, 2))]),
        out_shape=[jax.ShapeDtypeStruct((n_a, d), jnp.float32),
                   jax.ShapeDtypeStruct((n - n_a, d), jnp.float32)],
        compiler_params=pltpu.CompilerParams(dimension_semantics=("arbitrary",),
                                             vmem_limit_bytes=VMEM_LIMIT),
        name="moe_combine",
    )(pos_flat, h_c, route_out, g.reshape(1, -1), b.reshape(1, -1), y_c)


FFN_TM = 256
PROJ_TM = 1664
PROJ_TN = 512
TAIL_TM = 320


def _moe_and_norm(h_c, lg, w_gate, w_up, w_down, ln2_g, ln2_b, n_first):
    n = lg.shape[0]
    route_out, cnt = _route(lg, 640)
    eid = route_out[:, 0:2].astype(jnp.int32)
    rank = route_out[:, 4:6].astype(jnp.int32)
    counts = cnt[0, LG0:LG0 + N_EXPERTS].astype(jnp.int32)
    tiles_per = (counts + FFN_TM - 1) // FFN_TM
    tile_end = jnp.cumsum(tiles_per)
    row_start = (tile_end - tiles_per) * FFN_TM
    pos_flat = (row_start[eid] + rank).reshape(-1)
    nt = (2 * n) // FFN_TM + N_EXPERTS
    n_used = tile_end[-1]
    tile_ids = jnp.minimum(jnp.arange(nt, dtype=jnp.int32), n_used - 1)
    tile_expert = jnp.sum((tile_end[None, :] <= tile_ids[:, None]).astype(jnp.int32), axis=1)
    xs_c = _dispatch(pos_flat, h_c, nt * FFN_TM)
    y_c = _ffn(tile_expert, n_used.reshape(1).astype(jnp.int32), xs_c, w_gate, w_up, w_down, FFN_TM)
    return _combine(pos_flat, h_c, route_out, ln2_g, ln2_b, y_c, DISPATCH_TM, n_first)


def kernel(x_prompt, x_sample, cache_kv, cache_win, state_hgrn, page_table, w_in, b_in, w_cmp1, w_cmp2, cmp_pe,
           hgrn_gamma, hgrn_norm, w_pa, w_pb, w_out, ln1_g, ln1_b, w_rg, b_rg, w_re, b_re, w_gate, w_up, w_down,
           ln2_g, ln2_b):
    n_p = x_prompt.shape[0] * x_prompt.shape[1]
    n_s = x_sample.shape[0] * x_sample.shape[1]
    x_all = jnp.concatenate([x_prompt.reshape(n_p, D_MODEL), x_sample.reshape(n_s, D_MODEL)], axis=0)
    x_bf = x_all.astype(jnp.bfloat16)
    bsz, seq = x_prompt.shape[:2]
    n_seq, t_new = x_sample.shape[:2]
    n_pages = page_table.shape[1]
    past_len = n_pages * PAGE_SIZE
    w = w_in[0]
    b = b_in[0]

    gap = PROJ_TN - NG_COLS
    w_cat = jnp.concatenate([w[:, :OFF_H4], jnp.zeros((D_MODEL, gap), w.dtype), w[:, OFF_H4:]],
                            axis=1).astype(jnp.bfloat16)
    b_cat = jnp.concatenate([b[:OFF_H4], jnp.zeros((gap,), b.dtype), b[OFF_H4:]])

    def seg(col0, width, outs, tn, name):
        return _proj(x_bf, w_cat, b_cat, col0, width, outs, PROJ_TM, tn, name)

    q_h, = seg(OFF_Q, NSA_WIDTH, [(jnp.bfloat16, True)], PROJ_TN, "proj_q")
    kv32, kv_h = seg(OFF_KV, KV_COLS, [(jnp.float32, False), (jnp.bfloat16, True)], PROJ_TN, "proj_kv")
    ng, = seg(OFF_NG, LANES, [(jnp.float32, False)], LANES, "proj_ng")
    h4, = seg(OFF_H4 + gap, OFF_MG - OFF_H4, [(jnp.float32, False)], PROJ_TN, "proj_h4")
    mg, = seg(OFF_MG + gap, PROJ_COLS - OFF_MG, [(jnp.float32, False)], PROJ_TN, "proj_mg")

    kv_p = kv32[:n_p].reshape(1, bsz, seq, 6, N_KV, HEAD_DIM)
    kv_s = kv32[n_p:].reshape(1, n_seq, t_new, 6, N_KV, HEAD_DIM)
    new_kv_prompt = kv_p[:, :, :, :KV_SLOTS]
    new_kv_sample = kv_s[:, :, :, :KV_SLOTS]
    new_win_prompt = kv_p[:, :, seq - min(WINDOW, seq):, 4:6]
    win_all = jnp.concatenate([cache_win, kv_s[:, :, :, 4:6].astype(cache_win.dtype)], axis=2)
    new_win_sample = win_all[:, :, win_all.shape[2] - min(WINDOW, win_all.shape[2]):]

    w1p = _cmp_w1_pairs(w_cmp1[0])
    n_cmp_p = (seq - CMP_LEN) // CMP_STRIDE + 1
    uv_p = _cmp_uv_rows(kv32, n_p, 2048, w1p)
    kcvc_p = _cmp_finish(uv_p, bsz, seq // CMP_STRIDE, w_cmp1[0], w_cmp2[0], cmp_pe[0], "cmp_finish_prompt")
    ng_t = ng[:n_p, :NG_COLS].reshape(bsz, seq, 3, N_KV, HPG).transpose(0, 3, 1, 2, 4).reshape(bsz, N_KV, seq, 3 * HPG)
    o_nsa_p = _nsa_prompt(q_h, kcvc_p, kv_h, ng_t, n_cmp_p)

    cache_t = cache_kv[0].transpose(0, 2, 3, 4, 1).reshape(cache_kv.shape[1], 2 * HALF_ROWS, PAGE_SIZE)
    win_t = cache_win[0].transpose(0, 2, 3, 4, 1).reshape(n_seq, HALF_ROWS, cache_win.shape[2])
    n_cmp_s = (past_len + t_new - CMP_LEN) // CMP_STRIDE + 1
    kcvc_s = _cmp_sample(page_table, cache_t, w1p, w_cmp1[0], w_cmp2[0], cmp_pe[0])
    rows_s = N_HEADS * t_new
    q_s = q_h[:, n_p:].reshape(N_KV, HPG, n_seq, t_new, HEAD_DIM).transpose(2, 0, 1, 3, 4).reshape(
        n_seq, N_KV, HPG * t_new, HEAD_DIM)
    q_bd = jnp.einsum('sgrd,gk->sgrkd', q_s, jnp.eye(N_KV, dtype=q_s.dtype)).reshape(n_seq, rows_s, GD)
    new_kv = kv_h[:, n_p:].reshape(6, N_KV, n_seq, t_new, HEAD_DIM).transpose(2, 0, 3, 1, 4).reshape(
        n_seq, 6, t_new, GD)
    new_kv = jnp.pad(new_kv, ((0, 0), (0, 0), (0, 8 - t_new), (0, 0)))
    ng_r = ng[n_p:, :NG_COLS].reshape(n_seq, t_new, 3, N_KV, HPG).transpose(0, 3, 4, 1, 2).reshape(
        n_seq, rows_s, 3)
    o_nsa_s = _nsa_sample(page_table, cache_t, q_bd, kcvc_s, new_kv, win_t, ng_r, n_cmp_s, t_new)
    o_nsa_s = o_nsa_s.reshape(n_seq, N_KV, HPG, t_new, HEAD_DIM).transpose(0, 3, 1, 2, 4).reshape(n_s, NSA_WIDTH)

    lower = jnp.cumsum(jax.nn.softmax(hgrn_gamma.astype(jnp.float32), axis=0), axis=0)
    lb = lower[0].reshape(HG_HEADS, HG_DK)
    zero_state = jnp.zeros((bsz, HG_HEADS, HG_DV, HG_DK), jnp.float32)
    o_hg_p, st_p = _hgrn(h4, 0, bsz, seq, lb, hgrn_norm[0], zero_state, HG_CHUNK, HG_CHUNK, 512, 4, "hgrn_prompt")
    h4_s = jnp.pad(h4[n_p:].reshape(n_seq, t_new, -1), ((0, 0), (0, 8 - t_new), (0, 0))).reshape(n_seq * 8, -1)
    o_hg_s, st_s = _hgrn(h4_s, 0, n_seq, 8, lb, hgrn_norm[0], state_hgrn[0].transpose(0, 1, 3, 2),
                         8, t_new, 8, HG_HEADS, "hgrn_sample")
    o_hg_s = o_hg_s.reshape(n_seq, 8, HG_WIDTH)[:, :t_new].reshape(n_s, HG_WIDTH)
    new_state_prompt = st_p.transpose(0, 1, 3, 2)[None].astype(x_prompt.dtype)
    new_state_sample = st_s.transpose(0, 1, 3, 2)[None].astype(state_hgrn.dtype)

    o_nsa = jnp.concatenate([o_nsa_p, o_nsa_s], axis=0)
    o_hg = jnp.concatenate([o_hg_p, o_hg_s], axis=0)
    u = _mix(o_nsa, o_hg, mg, w_pa[0].astype(jnp.bfloat16), w_pb[0].astype(jnp.bfloat16), TAIL_TM)
    wr = jnp.zeros((D_MODEL, LANES), jnp.float32).at[:, :N_GROUPS].set(w_rg[0]).at[:, LG0:LG0 + N_EXPERTS].set(w_re[0])
    br = jnp.zeros((1, LANES), jnp.float32).at[0, :N_GROUPS].set(b_rg[0]).at[0, LG0:LG0 + N_EXPERTS].set(b_re[0])
    wr_hi, wr_lo = _split_bf16(wr)
    h, lg = _ln1(u, x_all, w_out[0].astype(jnp.bfloat16), ln1_g[0], ln1_b[0], wr_hi, wr_lo, br, TAIL_TM)
    out_p, out_s = _moe_and_norm(h, lg, w_gate[0], w_up[0], w_down[0], ln2_g[0], ln2_b[0], n_p)
    y_prompt = out_p.reshape(bsz, seq, D_MODEL)
    y_sample = out_s.reshape(n_seq, t_new, D_MODEL)
    return (y_prompt, y_sample, new_kv_prompt, new_kv_sample, new_win_prompt, new_win_sample,
            new_state_prompt, new_state_sample)
```

```python
import functools

import numpy as np
import jax
import jax.numpy as jnp
from jax import lax
from jax.experimental import pallas as pl
from jax.experimental.pallas import tpu as pltpu

D_MODEL = 2048
N_HEADS = 16
N_KV = 4
HPG = N_HEADS // N_KV
HEAD_DIM = 64
NSA_WIDTH = N_HEADS * HEAD_DIM
CMP_LEN = 32
CMP_STRIDE = 16
CMP_HID = 128
SEL_BLOCK = 64
N_SEL = 16
WINDOW = 512
QBLK = 128
KV_SLOTS = 4
PAGE_SIZE = 128
HG_HEADS = 8
HG_DK = 128
HG_DV = 128
HG_WIDTH = HG_HEADS * HG_DV
HG_CHUNK = 32
N_GROUPS = 4
EXP_PER_GROUP = 8
N_EXPERTS = N_GROUPS * EXP_PER_GROUP
D_EXPERT = 512
DEPTH = 1
DN_ALPHA = (2.0 * DEPTH) ** 0.25
LN_EPS = 1e-5
SCALE = HEAD_DIM ** -0.5
NEG = -1e30
BIG = 1e30

LANES = 128
KV_COLS = 6 * N_KV * HEAD_DIM
NG_COLS = 3 * N_HEADS
OFF_Q = 0
OFF_KV = NSA_WIDTH
OFF_NG = OFF_KV + KV_COLS
OFF_H4 = OFF_NG + NG_COLS
OFF_MG = OFF_H4 + 2 * HG_HEADS * HG_DK + 2 * HG_WIDTH
PROJ_COLS = OFF_MG + 2 * D_MODEL

VMEM_LIMIT = 56 * 1024 * 1024


def _sigmoid(x):
    return 1.0 / (1.0 + jnp.exp(-x))


def _proj_body(x_ref, w_ref, b_ref, *out_refs):
    acc = jnp.dot(x_ref[...], w_ref[...], preferred_element_type=jnp.float32) + b_ref[...]
    for o_ref in out_refs:
        if len(o_ref.shape) == 3:
            for k in range(o_ref.shape[0]):
                o_ref[k] = acc[:, k * HEAD_DIM:(k + 1) * HEAD_DIM].astype(o_ref.dtype)
        else:
            o_ref[...] = acc.astype(o_ref.dtype)


def _proj(x_bf, w_bf, b, outs, tm, tn, name):
    m, k = x_bf.shape
    n = w_bf.shape[1]
    assert m % tm == 0 and n % tn == 0
    out_specs, out_shape = [], []
    for dt, per_head in outs:
        if per_head:
            out_specs.append(pl.BlockSpec((tn // HEAD_DIM, tm, HEAD_DIM), lambda i, j: (j, i, 0)))
            out_shape.append(jax.ShapeDtypeStruct((n // HEAD_DIM, m, HEAD_DIM), dt))
        else:
            out_specs.append(pl.BlockSpec((tm, tn), lambda i, j: (i, j)))
            out_shape.append(jax.ShapeDtypeStruct((m, n), dt))
    return pl.pallas_call(
        _proj_body,
        grid=(m // tm, n // tn),
        in_specs=[pl.BlockSpec((tm, k), lambda i, j: (i, 0)),
                  pl.BlockSpec((k, tn), lambda i, j: (0, j)),
                  pl.BlockSpec((1, tn), lambda i, j: (0, j))],
        out_specs=out_specs,
        out_shape=out_shape,
        compiler_params=pltpu.CompilerParams(dimension_semantics=("parallel", "parallel"),
                                             vmem_limit_bytes=VMEM_LIMIT),
        name=name,
    )(x_bf, w_bf, b.reshape(1, -1))


def _hgrn_body(hq_ref, hf_ref, hi_ref, hg_ref, lb_ref, nrm_ref, s0_ref, o_ref, sfin_ref, st_scr,
               *, chunk, n_valid, n_chunks):
    @pl.when(pl.program_id(2) == 0)
    def _():
        st_scr[...] = s0_ref[0]

    for h in range(st_scr.shape[0]):
        cols = slice(h * HG_DK, (h + 1) * HG_DK)
        o, st = _hgrn_head(hq_ref[:, cols], hf_ref[:, cols], hi_ref[:, cols], hg_ref[:, cols],
                           lb_ref[h], nrm_ref[h], st_scr[h], chunk, n_valid, n_chunks)
        st_scr[h] = st
        sfin_ref[0, h] = st
        o_ref[:, cols] = o.astype(o_ref.dtype)


def _hgrn_head(hq, hf, v, hg, lb, nrm, st, chunk, n_valid, n_chunks):
    rows = chunk * n_chunks
    q = hq * _sigmoid(hq)
    f = lb + (1.0 - lb) * _sigmoid(hf)
    k = 1.0 - f
    lc = jnp.log(f)
    row_in_chunk = lax.broadcasted_iota(jnp.int32, (rows, HG_DK), 0) % chunk
    if n_valid < chunk:
        live = row_in_chunk < n_valid
        q = jnp.where(live, q, 0.0)
        k = jnp.where(live, k, 0.0)
        v = jnp.where(live, v, 0.0)
        lc = jnp.where(live, lc, 0.0)
    bc = lc
    step = 1
    while step < chunk:
        bc = bc + jnp.where(row_in_chunk >= step, pltpu.roll(bc, step, axis=0), 0.0)
        step *= 2
    bc3 = bc.reshape(n_chunks, chunk, HG_DK)
    bl3 = bc3[:, chunk - 1:chunk, :]
    q3 = q.reshape(n_chunks, chunk, HG_DK)
    k3 = k.reshape(n_chunks, chunk, HG_DK)
    v3 = v.reshape(n_chunks, chunk, HG_DV).astype(jnp.bfloat16)
    qe3 = (q3 * jnp.exp(bc3)).astype(jnp.bfloat16)
    ke3 = (k3 * jnp.exp(-bc3)).astype(jnp.bfloat16)
    kd3 = (k3 * jnp.exp(bl3 - bc3)).astype(jnp.bfloat16)
    dec3 = jnp.exp(bl3)
    att = jnp.einsum('ctd,csd->cts', qe3, ke3, preferred_element_type=jnp.float32)
    tri = (lax.broadcasted_iota(jnp.int32, (chunk, chunk), 0)
           >= lax.broadcasted_iota(jnp.int32, (chunk, chunk), 1))
    att = jnp.where(tri[None], att, 0.0).astype(jnp.bfloat16)
    o_intra = jnp.einsum('cts,cse->cte', att, v3, preferred_element_type=jnp.float32)

    outs = []
    for c in range(n_chunks):
        o_c = lax.dot_general(qe3[c], st.astype(jnp.bfloat16), (((1,), (1,)), ((), ())),
                              preferred_element_type=jnp.float32)
        outs.append(o_c + o_intra[c])
        upd = lax.dot_general(v3[c], kd3[c], (((0,), (0,)), ((), ())),
                              preferred_element_type=jnp.float32)
        st = st * dec3[c] + upd
    o = jnp.concatenate(outs, axis=0) if n_chunks > 1 else outs[0]
    o = o * lax.rsqrt(jnp.mean(o * o, axis=-1, keepdims=True) + LN_EPS) * nrm
    return o * (hg * _sigmoid(hg)), st


def _hgrn(h4, row0, n_seq, t_seq, lb, nrm, s0_t, chunk, n_valid, block_rows, heads_per_step, name):
    assert t_seq % block_rows == 0 and block_rows % chunk == 0 and row0 % block_rows == 0
    nb = t_seq // block_rows
    rb0 = row0 // block_rows
    nh = heads_per_step
    hb = HG_HEADS // nh

    def col_spec(seg):
        return pl.BlockSpec((block_rows, nh * HG_DK),
                            lambda b, h, i, seg=seg: (rb0 + b * nb + i, seg * hb + h))

    body = functools.partial(_hgrn_body, chunk=chunk, n_valid=n_valid, n_chunks=block_rows // chunk)
    return pl.pallas_call(
        body,
        grid=(n_seq, hb, nb),
        in_specs=[col_spec(0), col_spec(1), col_spec(2), col_spec(3),
                  pl.BlockSpec((nh, 1, HG_DK), lambda b, h, i: (h, 0, 0)),
                  pl.BlockSpec((nh, 1, HG_DV), lambda b, h, i: (h, 0, 0)),
                  pl.BlockSpec((1, nh, HG_DV, HG_DK), lambda b, h, i: (b, h, 0, 0))],
        out_specs=[pl.BlockSpec((block_rows, nh * HG_DV), lambda b, h, i: (b * nb + i, h)),
                   pl.BlockSpec((1, nh, HG_DV, HG_DK), lambda b, h, i: (b, h, 0, 0))],
        out_shape=[jax.ShapeDtypeStruct((n_seq * t_seq, HG_WIDTH), jnp.bfloat16),
                   jax.ShapeDtypeStruct((n_seq, HG_HEADS, HG_DV, HG_DK), jnp.float32)],
        scratch_shapes=[pltpu.VMEM((nh, HG_DV, HG_DK), jnp.float32)],
        compiler_params=pltpu.CompilerParams(
            dimension_semantics=("parallel", "parallel", "arbitrary"), vmem_limit_bytes=VMEM_LIMIT),
        name=name,
    )(h4, h4, h4, h4, lb.reshape(HG_HEADS, 1, HG_DK), nrm.reshape(HG_HEADS, 1, HG_DV), s0_t)


UV_COLS = 2 * N_KV * 2 * CMP_HID


def _uv_taps(tap, w_ref, c, n):
    acc = jnp.zeros((n, 4 * CMP_HID), jnp.float32)
    for p in range(CMP_STRIDE // 2):
        x2 = jnp.concatenate([tap(2 * p), tap(2 * p + 1)], axis=-1)
        acc = acc + jnp.dot(x2, w_ref[c, p], preferred_element_type=jnp.float32)
    return acc


def _cmp_uv_body(x0_ref, x1_ref, x2_ref, x3_ref, w_ref, uv_ref):
    n = uv_ref.shape[0]
    for cgp, x_ref in enumerate((x0_ref, x1_ref, x2_ref, x3_ref)):
        tap = lambda j, x_ref=x_ref: x_ref[pl.ds(j, n, stride=CMP_STRIDE), :].astype(jnp.bfloat16)
        uv_ref[:, cgp * 512:(cgp + 1) * 512] = _uv_taps(tap, w_ref, cgp // 2, n)


def _cmp_w1_pairs(w_cmp1_l):
    w = jnp.concatenate([w_cmp1_l[:, :CMP_STRIDE], w_cmp1_l[:, CMP_STRIDE:]], axis=-1)
    z = jnp.zeros_like(w)
    top = jnp.concatenate([w, z], axis=-1)
    bot = jnp.concatenate([z, w], axis=-1)
    per_tap = jnp.concatenate([top, bot], axis=-2)
    return per_tap.reshape(2, CMP_STRIDE // 2, 2 * LANES, 4 * CMP_HID).astype(jnp.bfloat16)


def _cmp_uv_rows(kv, n_rows, rows_per_step, w1p):
    n = rows_per_step // CMP_STRIDE
    return pl.pallas_call(
        _cmp_uv_body,
        grid=(n_rows // rows_per_step,),
        in_specs=[pl.BlockSpec((rows_per_step, LANES), lambda i, cb=cb: (i, cb)) for cb in range(4)]
        + [pl.BlockSpec(w1p.shape, lambda i: (0, 0, 0, 0))],
        out_specs=pl.BlockSpec((n, UV_COLS), lambda i: (i, 0)),
        out_shape=jax.ShapeDtypeStruct((n_rows // CMP_STRIDE, UV_COLS), jnp.float32),
        compiler_params=pltpu.CompilerParams(dimension_semantics=("parallel",),
                                             vmem_limit_bytes=VMEM_LIMIT),
        name="cmp_uv_prompt",
    )(kv, kv, kv, kv, w1p)


def _gelu_tanh(x):
    return 0.5 * x * (1.0 + jnp.tanh(0.7978845608028654 * (x + 0.044715 * x * x * x)))


def _cmp_finish_body(uv_ref, pe_ref, w1_ref, w2_ref, out_ref):
    _cmp_finish_rows(lambda cols: uv_ref[0, :, cols], uv_ref.shape[1], pe_ref, w1_ref, w2_ref, out_ref)


def _cmp_finish_rows(uv_cols, n_ch, pe_ref, w1_ref, w2_ref, out_ref):
    for c in range(2):
        pe_term = jnp.dot(pe_ref[c], w1_ref[c], preferred_element_type=jnp.float32)[0:1, :]
        for g in range(N_KV):
            base = (c * N_KV + g) * 2 * CMP_HID
            u = uv_cols(slice(base, base + CMP_HID))
            v = uv_cols(slice(base + CMP_HID, base + 2 * CMP_HID))
            pre = u + pltpu.roll(v, n_ch - 1, axis=0) + pe_term
            hid = _gelu_tanh(pre).astype(jnp.bfloat16)
            res = jnp.dot(hid, w2_ref[c], preferred_element_type=jnp.float32).astype(out_ref.dtype)
            if out_ref.shape[1] == 2:
                out_ref[0, c, :, g * HEAD_DIM:(g + 1) * HEAD_DIM] = res
            else:
                out_ref[0, c * N_KV + g] = res


def _cmp_finish(uv, n_seq, n_ch, w_cmp1_l, w_cmp2_l, cmp_pe_l, name):
    pe = jnp.zeros((2, 8, CMP_LEN * HEAD_DIM), jnp.float32).at[:, 0].set(cmp_pe_l.reshape(2, -1))
    out_dims = (2 * N_KV, n_ch, HEAD_DIM)
    return pl.pallas_call(
        _cmp_finish_body,
        grid=(n_seq,),
        in_specs=[pl.BlockSpec((1, n_ch, UV_COLS), lambda b: (b, 0, 0)),
                  pl.BlockSpec((2, 8, CMP_LEN * HEAD_DIM), lambda b: (0, 0, 0)),
                  pl.BlockSpec((2, CMP_LEN * HEAD_DIM, CMP_HID), lambda b: (0, 0, 0)),
                  pl.BlockSpec((2, CMP_HID, HEAD_DIM), lambda b: (0, 0, 0))],
        out_specs=pl.BlockSpec((1,) + out_dims, lambda b: (b, 0, 0, 0)),
        out_shape=jax.ShapeDtypeStruct((n_seq,) + out_dims, jnp.bfloat16),
        compiler_params=pltpu.CompilerParams(dimension_semantics=("parallel",),
                                             vmem_limit_bytes=VMEM_LIMIT),
        name=name,
    )(uv.reshape(n_seq, n_ch, UV_COLS), pe.astype(jnp.bfloat16),
      w_cmp1_l.reshape(2, CMP_LEN * HEAD_DIM, CMP_HID).astype(jnp.bfloat16),
      w_cmp2_l.astype(jnp.bfloat16))


SLC_TK = 512
WIN_KEYS = WINDOW + QBLK


def _masked_softmax(s, valid):
    s = jnp.where(valid, s, NEG)
    m = jnp.max(s, axis=-1, keepdims=True)
    e = jnp.where(valid, jnp.exp(s - m), 0.0)
    return e / jnp.maximum(jnp.sum(e, axis=-1, keepdims=True), 1e-30)


def _split_bf16(x):
    hi = x.astype(jnp.bfloat16)
    return hi, (x - hi.astype(jnp.float32)).astype(jnp.bfloat16)


SUBLANES = 8


def _top_blocks_t(imp_t):
    nb, nq = imp_t.shape
    groups = [imp_t[SUBLANES * v:SUBLANES * (v + 1), :] for v in range(nb // SUBLANES)]
    sub = lax.broadcasted_iota(jnp.int32, (SUBLANES, nq), 0)
    beaten_by = [jnp.zeros((SUBLANES, nq), jnp.float32) for _ in groups]
    for i in range(nb):
        row = jnp.broadcast_to(imp_t[i:i + 1, :], (SUBLANES, nq))
        for v, gv in enumerate(groups):
            if v > i // SUBLANES:
                beats = row >= gv
            elif v < i // SUBLANES:
                beats = row > gv
            else:
                beats = (row > gv) | ((row == gv) & (sub > i % SUBLANES))
            beaten_by[v] = beaten_by[v] + jnp.where(beats, 1.0, 0.0)
    return jnp.concatenate([jnp.where(c < N_SEL, 1.0, 0.0) for c in beaten_by], axis=0)


def _split3_bf16(x):
    x1 = x.astype(jnp.bfloat16)
    r = x - x1.astype(jnp.float32)
    x2 = r.astype(jnp.bfloat16)
    return x1, x2, (r - x2.astype(jnp.float32)).astype(jnp.bfloat16)


XK = 2 * LANES


def _slc_key_columns(t):
    kpos = np.arange(t)
    cols = np.zeros((t, XK - HEAD_DIM), np.float32)
    cols[kpos, kpos // SEL_BLOCK] = 1.0
    cols[:, 64:67] = (kpos // 64)[:, None]
    cols[:, 67:70] = (kpos % 64)[:, None]
    cols[:, 70:73] = 1.0
    return cols


def _slc_query_columns():
    s1, s2, s3 = _split3_bf16(_alibi_slopes().reshape(N_KV, HPG, 1))
    cols = jnp.concatenate([64.0 * s1, 64.0 * s2, 64.0 * s3, s1, s2, s3], axis=-1)
    cols = jnp.pad(cols, ((0, 0), (0, 0), (0, LANES - 6)))
    return jnp.broadcast_to(cols[:, :, None, :], (N_KV, HPG, QBLK, LANES)).reshape(N_KV, HPG * QBLK, LANES)


def _nsa_prompt_body(q_ref, kc_ref, vc_ref, kx_ref, vs_ref, kw_ref, vw_ref, ng_ref, ov_ref, sx_ref, cb_ref, sl_ref,
                     o_ref, *, n_cmp):
    i = pl.program_id(2)
    nq = QBLK
    rows = HPG * nq
    q2 = (q_ref[...] * SCALE).reshape(rows, HEAD_DIM)
    slopes = sl_ref[0]
    qpos = i * nq + lax.broadcasted_iota(jnp.int32, (nq, 1), 0)
    nt = (((1,), (1,)), ((), ()))

    n_ch = kc_ref.shape[2]
    s = lax.dot_general(q2, kc_ref[0, 0], nt, preferred_element_type=jnp.float32).reshape(HPG, nq, n_ch)
    cidx = lax.broadcasted_iota(jnp.int32, (1, n_ch), 1)
    dist_c = qpos - (cidx * CMP_STRIDE + (CMP_LEN - 1))
    valid_c = ((dist_c >= 0) & (cidx < n_cmp))[None]
    p = _masked_softmax(s - slopes * dist_c.astype(jnp.float32)[None], valid_c)
    o_cmp = jnp.dot(p.reshape(rows, n_ch).astype(jnp.bfloat16), vc_ref[0, 0],
                    preferred_element_type=jnp.float32).reshape(HPG, nq, HEAD_DIM)

    w0 = pl.multiple_of(jnp.maximum(i - WINDOW // QBLK, 0) * nq, nq)
    s = lax.dot_general(q2, kw_ref[0, pl.ds(w0, WIN_KEYS), :], nt,
                        preferred_element_type=jnp.float32).reshape(HPG, nq, WIN_KEYS)
    dist = qpos - (w0 + lax.broadcasted_iota(jnp.int32, (1, WIN_KEYS), 1))
    valid = ((dist >= 0) & (dist < WINDOW))[None]
    pw = _masked_softmax(s - slopes * dist.astype(jnp.float32)[None], valid)
    o_win = jnp.dot(pw.reshape(rows, WIN_KEYS).astype(jnp.bfloat16), vw_ref[0, pl.ds(w0, WIN_KEYS), :],
                    preferred_element_type=jnp.float32).reshape(HPG, nq, HEAD_DIM)

    p_hi, p_lo = _split_bf16(p[0] + p[1] + p[2] + p[3])
    imp_t = (lax.dot_general(ov_ref[...], p_hi, nt, preferred_element_type=jnp.float32)
             + lax.dot_general(ov_ref[...], p_lo, nt, preferred_element_type=jnp.float32))
    nsb = imp_t.shape[0]
    blk = lax.broadcasted_iota(jnp.int32, (nsb, 1), 0)
    cur = (i * nq + lax.broadcasted_iota(jnp.int32, (1, nq), 1)) // SEL_BLOCK
    imp_t = jnp.where(blk > cur, NEG, imp_t)
    imp_t = jnp.where((blk == cur) | (blk == 0), BIG, imp_t)
    sel_t = jnp.where(blk <= cur, _top_blocks_t(imp_t), 0.0)

    eye = (lax.broadcasted_iota(jnp.int32, (nsb, nsb), 0)
           == lax.broadcasted_iota(jnp.int32, (nsb, nsb), 1)).astype(jnp.bfloat16)
    drop = lax.dot_general(((sel_t - 1.0) * BIG).astype(jnp.bfloat16), eye, (((0,), (0,)), ((), ())),
                           preferred_element_type=jnp.float32)
    drop = jnp.concatenate([drop.astype(jnp.bfloat16)] * HPG, axis=0)
    slope_row = jnp.broadcast_to(slopes, (HPG, nq, 1)).reshape(rows, 1)
    qpos_row = jnp.concatenate([qpos] * HPG, axis=0).astype(jnp.float32)
    c1, c2, c3 = _split3_bf16(slope_row * qpos_row)
    lane = lax.broadcasted_iota(jnp.int32, (rows, LANES), 1)
    pos_cols = sx_ref[0].astype(jnp.float32)
    for k, ck in enumerate((c1, c2, c3)):
        pos_cols = jnp.where(lane == 6 + k, -ck.astype(jnp.float32), pos_cols)
    q_ext = jnp.concatenate([q2, drop, pos_cols.astype(jnp.bfloat16)], axis=1)

    def slc_tile(kt, carry, causal_bias):
        m, l, acc = carry
        k0 = pl.multiple_of(kt * SLC_TK, SLC_TK)
        s = lax.dot_general(q_ext, kx_ref[0, pl.ds(k0, SLC_TK), :], nt,
                            preferred_element_type=jnp.float32).reshape(HPG, nq, SLC_TK)
        if causal_bias is not None:
            s = s + causal_bias[None]
        m_new = jnp.maximum(m, jnp.max(s, axis=-1, keepdims=True))
        a = jnp.exp(m - m_new)
        e = jnp.exp(s - m_new)
        l = a * l + jnp.sum(e, axis=-1, keepdims=True)
        pv = jnp.dot(e.reshape(rows, SLC_TK).astype(jnp.bfloat16), vs_ref[0, pl.ds(k0, SLC_TK), :],
                     preferred_element_type=jnp.float32).reshape(HPG, nq, HEAD_DIM)
        return m_new, l, a * acc + pv

    init = (jnp.full((HPG, nq, 1), NEG, jnp.float32), jnp.zeros((HPG, nq, 1), jnp.float32),
            jnp.zeros((HPG, nq, HEAD_DIM), jnp.float32))
    last = (i * nq) // SLC_TK
    carry = lax.fori_loop(0, last // 2, lambda p, c: slc_tile(2 * p + 1, slc_tile(2 * p, c, None), None), init)
    carry = lax.fori_loop(2 * (last // 2), last, lambda kt, c: slc_tile(kt, c, None), carry)
    _, l, acc = slc_tile(last, carry, cb_ref[i % (SLC_TK // QBLK)])
    o_slc = acc / jnp.maximum(l, 1e-30)

    gates = _sigmoid(ng_ref[0, 0])
    o_ref[...] = jnp.concatenate(
        [gates[:, h:h + 1] * o_cmp[h] + gates[:, HPG + h:HPG + h + 1] * o_slc[h]
         + gates[:, 2 * HPG + h:2 * HPG + h + 1] * o_win[h] for h in range(HPG)], axis=-1).astype(o_ref.dtype)


def _alibi_slopes():
    return jnp.asarray(2.0 ** (-8.0 * np.arange(1, N_HEADS + 1) / N_HEADS), jnp.float32).reshape(N_KV, HPG, 1, 1)


def _nsa_prompt(q_h, kcvc, kv_h, ng_t, n_cmp):
    bsz, _, t, _ = ng_t.shape
    nqb = t // QBLK
    n_ch = kcvc.shape[2]
    nsb = t // SEL_BLOCK
    st = np.arange(n_ch) * CMP_STRIDE
    bs = np.arange(nsb) * SEL_BLOCK
    overlap = ((st[:, None] <= bs[None, :] + SEL_BLOCK - 1) & (st[:, None] + CMP_LEN - 1 >= bs[None, :])
               & (np.arange(n_ch)[:, None] < n_cmp)).astype(np.float32)
    assert nsb == LANES - HEAD_DIM and t % SLC_TK == 0
    key_cols = jnp.asarray(np.tile(_slc_key_columns(t), (bsz, 1)), jnp.bfloat16)
    kx = jnp.concatenate([kv_h[2 * N_KV:3 * N_KV, :bsz * t],
                          jnp.broadcast_to(key_cols[None], (N_KV,) + key_cols.shape)], axis=-1)
    r = np.arange(SLC_TK // QBLK)[:, None, None] * QBLK + np.arange(QBLK)[None, :, None]
    causal = np.where(np.arange(SLC_TK)[None, None, :] <= r, 0.0, NEG).astype(np.float32)

    def kv_spec(slot):
        return pl.BlockSpec((1, t, HEAD_DIM), lambda b, g, i, slot=slot: (slot * N_KV + g, b, 0))

    return pl.pallas_call(
        functools.partial(_nsa_prompt_body, n_cmp=n_cmp),
        grid=(bsz, N_KV, nqb),
        in_specs=[pl.BlockSpec((HPG, QBLK, HEAD_DIM), lambda b, g, i: (g, b * nqb + i, 0)),
                  pl.BlockSpec((1, 1, n_ch, HEAD_DIM), lambda b, g, i: (b, g, 0, 0)),
                  pl.BlockSpec((1, 1, n_ch, HEAD_DIM), lambda b, g, i: (b, N_KV + g, 0, 0)),
                  pl.BlockSpec((1, t, XK), lambda b, g, i: (g, b, 0)),
                  kv_spec(3), kv_spec(4), kv_spec(5),
                  pl.BlockSpec((1, 1, QBLK, 3 * HPG), lambda b, g, i: (b, g, i, 0)),
                  pl.BlockSpec((nsb, n_ch), lambda b, g, i: (0, 0)),
                  pl.BlockSpec((1, HPG * QBLK, LANES), lambda b, g, i: (g, 0, 0)),
                  pl.BlockSpec(causal.shape, lambda b, g, i: (0, 0, 0)),
                  pl.BlockSpec((1, HPG, 1, 1), lambda b, g, i: (g, 0, 0, 0))],
        out_specs=pl.BlockSpec((QBLK, HPG * HEAD_DIM), lambda b, g, i: (b * nqb + i, g)),
        out_shape=jax.ShapeDtypeStruct((bsz * t, NSA_WIDTH), jnp.bfloat16),
        compiler_params=pltpu.CompilerParams(
            dimension_semantics=("parallel", "parallel", "arbitrary"), vmem_limit_bytes=VMEM_LIMIT),
        name="nsa_prompt",
    )(q_h, kcvc, kcvc, kx, kv_h, kv_h, kv_h, ng_t,
      jnp.asarray(overlap.T, jnp.bfloat16), _slc_query_columns(), jnp.asarray(causal), _alibi_slopes())


HALF_ROWS = 2 * N_KV * HEAD_DIM
GD = N_KV * HEAD_DIM


def _fetch_pages(pt_ref, cache_ref, sem, row0, per_step, dst_of):
    b = pl.program_id(0)
    slot = b % 2
    n_pages = pt_ref.shape[1]

    def page_copy(flat_page, sl, k):
        phys = pt_ref[flat_page // n_pages, flat_page % n_pages]
        return pltpu.make_async_copy(cache_ref.at[phys, pl.ds(row0, HALF_ROWS), :], dst_of(sl, k), sem.at[sl])

    def start_all(step, sl):
        def one(k, carry):
            page_copy(step * per_step + k, sl, k).start()
            return carry
        lax.fori_loop(0, per_step, one, 0)

    @pl.when(b == 0)
    def _():
        start_all(0, 0)

    @pl.when(b + 1 < pl.num_programs(0))
    def _():
        start_all(b + 1, 1 - slot)

    def wait_one(k, carry):
        page_copy(b * per_step + k, slot, k).wait()
        return carry
    lax.fori_loop(0, per_step, wait_one, 0)
    return slot


def _cmp_sample_body(pt_ref, cache_ref, w_ref, pe_ref, w1_ref, w2_ref, out_ref, buf, sem, xt, uv, *, steps):
    per_step = buf.shape[1]
    slot = _fetch_pages(pt_ref, cache_ref, sem, 0, per_step, lambda sl, k: buf.at[sl, k])
    n_cb = HALF_ROWS // LANES

    eye = (lax.broadcasted_iota(jnp.int32, (LANES, LANES), 0)
           == lax.broadcasted_iota(jnp.int32, (LANES, LANES), 1)).astype(jnp.bfloat16)

    def to_rows(k, carry):
        for cb in range(n_cb):
            blk = buf[slot, k, cb * LANES:(cb + 1) * LANES, :].astype(jnp.bfloat16)
            xt[k, cb] = lax.dot_general(eye, blk, (((1,), (1,)), ((), ())), preferred_element_type=jnp.float32)
        return carry
    lax.fori_loop(0, per_step, to_rows, 0, unroll=8)

    per_page = PAGE_SIZE // CMP_STRIDE
    n = per_step * per_page

    def tap(cgp, j):
        return xt[:, cgp, pl.ds(j, per_page, stride=CMP_STRIDE), :].reshape(n, LANES).astype(jnp.bfloat16)

    part = pl.program_id(0) % steps
    row0 = pl.multiple_of(part * n, n)
    for cgp in range(n_cb):
        uv[pl.ds(row0, n), cgp * 512:(cgp + 1) * 512] = _uv_taps(functools.partial(tap, cgp), w_ref, cgp // 2, n)

    @pl.when(part == steps - 1)
    def _():
        _cmp_finish_rows(lambda cols: uv[:, cols], uv.shape[0], pe_ref, w1_ref, w2_ref, out_ref)


UV_PAGES = 32


def _cmp_sample(page_table, cache_t, w1p, w_cmp1_l, w_cmp2_l, cmp_pe_l):
    n_seq, n_pages = page_table.shape
    assert n_pages % UV_PAGES == 0
    steps = n_pages // UV_PAGES
    n_ch = n_pages * (PAGE_SIZE // CMP_STRIDE)
    pe = jnp.zeros((2, 8, CMP_LEN * HEAD_DIM), jnp.float32).at[:, 0].set(cmp_pe_l.reshape(2, -1))
    return pl.pallas_call(
        functools.partial(_cmp_sample_body, steps=steps),
        grid_spec=pltpu.PrefetchScalarGridSpec(
            num_scalar_prefetch=1, grid=(n_seq * steps,),
            in_specs=[pl.BlockSpec(memory_space=pl.ANY),
                      pl.BlockSpec(w1p.shape, lambda b, pt: (0, 0, 0, 0)),
                      pl.BlockSpec((2, 8, CMP_LEN * HEAD_DIM), lambda b, pt: (0, 0, 0)),
                      pl.BlockSpec((2, CMP_LEN * HEAD_DIM, CMP_HID), lambda b, pt: (0, 0, 0)),
                      pl.BlockSpec((2, CMP_HID, HEAD_DIM), lambda b, pt: (0, 0, 0))],
            out_specs=pl.BlockSpec((1, 2, n_ch, GD), lambda b, pt: (b // steps, 0, 0, 0)),
            scratch_shapes=[pltpu.VMEM((2, UV_PAGES, HALF_ROWS, PAGE_SIZE), jnp.float32),
                            pltpu.SemaphoreType.DMA((2,)),
                            pltpu.VMEM((UV_PAGES, HALF_ROWS // LANES, PAGE_SIZE, LANES), jnp.float32),
                            pltpu.VMEM((n_ch, UV_COLS), jnp.float32)]),
        out_shape=jax.ShapeDtypeStruct((n_seq, 2, n_ch, GD), jnp.bfloat16),
        compiler_params=pltpu.CompilerParams(dimension_semantics=("arbitrary",),
                                             vmem_limit_bytes=VMEM_LIMIT),
        name="cmp_sample",
    )(page_table, cache_t, w1p, pe.astype(jnp.bfloat16),
      w_cmp1_l.reshape(2, CMP_LEN * HEAD_DIM, CMP_HID).astype(jnp.bfloat16), w_cmp2_l.astype(jnp.bfloat16))


def _group_diag(x, rows_per_group):
    grp = lax.broadcasted_iota(jnp.int32, (x.shape[0], 1), 0) // rows_per_group
    out = jnp.zeros((x.shape[0], HEAD_DIM), x.dtype)
    for g in range(N_KV):
        out = out + jnp.where(grp == g, x[:, g * HEAD_DIM:(g + 1) * HEAD_DIM], 0.0)
    return out


BIAS_POS_ROWS = 16


def _sample_bias_rows(n_lanes, past_len):
    kpos = np.arange(past_len)
    rows = np.zeros((n_lanes + BIAS_POS_ROWS, past_len), np.float32)
    rows[kpos // SEL_BLOCK, kpos] = 1.0
    rows[n_lanes:n_lanes + 3] = kpos // 64
    rows[n_lanes + 3:n_lanes + 6] = kpos % 64
    rows[n_lanes + 6:n_lanes + 9] = 1.0
    return rows


def _nsa_sample_body(pt_ref, cache_ref, q_ref, kcvc_ref, new_ref, win_ref, ng_ref, ov_ref, same_q_ref, sl_ref,
                     bq_ref, bk_ref, o_ref, buf, sem, *, n_cmp, t_new):
    past_len = buf.shape[2]
    slot = _fetch_pages(pt_ref, cache_ref, sem, HALF_ROWS, past_len // PAGE_SIZE,
                        lambda sl, k: buf.at[sl, :, pl.ds(pl.multiple_of(k * PAGE_SIZE, PAGE_SIZE), PAGE_SIZE)])
    rows = q_ref.shape[1]
    rpg = rows // N_KV
    nt = (((1,), (1,)), ((), ()))
    qbd = q_ref[0] * SCALE
    slope = sl_ref[...]
    row = lax.broadcasted_iota(jnp.int32, (rows, 1), 0)
    qtok = row % t_new
    qpos = past_len + qtok
    new_rows = new_ref.shape[2]
    new_idx = lax.broadcasted_iota(jnp.int32, (1, new_rows), 1)
    new_valid = (new_idx <= qtok) & (new_idx < t_new)
    new_bias = slope * (qtok - new_idx).astype(jnp.float32)

    n_ch = kcvc_ref.shape[2]
    s = lax.dot_general(qbd, kcvc_ref[0, 0], nt, preferred_element_type=jnp.float32)
    cidx = lax.broadcasted_iota(jnp.int32, (1, n_ch), 1)
    dist_c = qpos - (cidx * CMP_STRIDE + (CMP_LEN - 1))
    p = _masked_softmax(s - slope * dist_c.astype(jnp.float32), (dist_c >= 0) & (cidx < n_cmp))
    o_cmp = _group_diag(jnp.dot(p.astype(jnp.bfloat16), kcvc_ref[0, 1], preferred_element_type=jnp.float32), rpg)

    p_hi, p_lo = _split_bf16(p)
    psum = (jnp.dot(same_q_ref[...], p_hi, preferred_element_type=jnp.float32)
            + jnp.dot(same_q_ref[...], p_lo, preferred_element_type=jnp.float32))
    ps_hi, ps_lo = _split_bf16(psum)
    imp = (jnp.dot(ps_hi, ov_ref[...], preferred_element_type=jnp.float32)
           + jnp.dot(ps_lo, ov_ref[...], preferred_element_type=jnp.float32))
    n_lanes = imp.shape[-1]
    blk = lax.broadcasted_iota(jnp.int32, (1, n_lanes), 1)
    cur = qpos // SEL_BLOCK
    imp = jnp.where(blk > cur, -jnp.inf, imp)
    imp = jnp.where((blk == cur) | (blk == 0), BIG, imp)
    sel = jnp.zeros(imp.shape, jnp.float32)
    for _ in range(N_SEL):
        m = jnp.max(imp, axis=-1, keepdims=True)
        idx = jnp.min(jnp.where(imp == m, blk, n_lanes), axis=-1, keepdims=True)
        sel = jnp.where(blk == idx, 1.0, sel)
        imp = jnp.where(blk == idx, -jnp.inf, imp)

    c1, c2, c3 = _split3_bf16(slope * qpos.astype(jnp.float32))
    lane = lax.broadcasted_iota(jnp.int32, (rows, bq_ref.shape[1]), 1)
    pos_cols = bq_ref[...].astype(jnp.float32)
    for k, ck in enumerate((c1, c2, c3)):
        pos_cols = jnp.where(lane == 6 + k, -ck.astype(jnp.float32), pos_cols)
    bias_q = jnp.concatenate([((sel - 1.0) * BIG).astype(jnp.bfloat16), pos_cols.astype(jnp.bfloat16)], axis=1)
    s_all = (jnp.dot(qbd, buf[slot, 0:GD, :].astype(jnp.bfloat16), preferred_element_type=jnp.float32)
             + jnp.dot(bias_q, bk_ref[...], preferred_element_type=jnp.float32))
    s_new = lax.dot_general(qbd, new_ref[0, 2], nt, preferred_element_type=jnp.float32)
    s_new = jnp.where(new_valid, s_new - new_bias, NEG)
    m = jnp.maximum(jnp.max(s_all, axis=-1, keepdims=True), jnp.max(s_new, axis=-1, keepdims=True))
    e_all = jnp.exp(s_all - m)
    e_new = jnp.where(new_valid, jnp.exp(s_new - m), 0.0)
    l = jnp.sum(e_all, axis=-1, keepdims=True) + jnp.sum(e_new, axis=-1, keepdims=True)
    acc = (lax.dot_general(e_all.astype(jnp.bfloat16), buf[slot, GD:2 * GD, :].astype(jnp.bfloat16), nt,
                           preferred_element_type=jnp.float32)
           + jnp.dot(e_new.astype(jnp.bfloat16), new_ref[0, 3], preferred_element_type=jnp.float32))
    o_slc = _group_diag(acc, rpg) / jnp.maximum(l, 1e-30)

    w_buf = win_ref.shape[2]
    wdist = qpos - (past_len - w_buf + lax.broadcasted_iota(jnp.int32, (1, w_buf), 1))
    valid_w = (wdist >= 0) & (wdist < WINDOW)
    s_w = jnp.dot(qbd, win_ref[0, 0:GD, :].astype(jnp.bfloat16), preferred_element_type=jnp.float32)
    s_w = jnp.where(valid_w, s_w - slope * wdist.astype(jnp.float32), NEG)
    s_wn = lax.dot_general(qbd, new_ref[0, 4], nt, preferred_element_type=jnp.float32)
    s_wn = jnp.where(new_valid, s_wn - new_bias, NEG)
    m = jnp.maximum(jnp.max(s_w, axis=-1, keepdims=True), jnp.max(s_wn, axis=-1, keepdims=True))
    e_w = jnp.where(valid_w, jnp.exp(s_w - m), 0.0)
    e_wn = jnp.where(new_valid, jnp.exp(s_wn - m), 0.0)
    acc = (lax.dot_general(e_w.astype(jnp.bfloat16), win_ref[0, GD:2 * GD, :].astype(jnp.bfloat16), nt,
                           preferred_element_type=jnp.float32)
           + jnp.dot(e_wn.astype(jnp.bfloat16), new_ref[0, 5], preferred_element_type=jnp.float32))
    l = jnp.sum(e_w, axis=-1, keepdims=True) + jnp.sum(e_wn, axis=-1, keepdims=True)
    o_win = _group_diag(acc, rpg) / jnp.maximum(l, 1e-30)

    gates = _sigmoid(ng_ref[0])
    o_ref[0] = (gates[:, 0:1] * o_cmp + gates[:, 1:2] * o_slc + gates[:, 2:3] * o_win).astype(o_ref.dtype)


def _nsa_sample(page_table, cache_t, q_bd, kcvc, new_kv, win_t, ng_r, n_cmp, t_new):
    n_seq, n_pages = page_table.shape
    past_len = n_pages * PAGE_SIZE
    n_ch = kcvc.shape[2]
    rows = q_bd.shape[1]
    rpg = rows // N_KV
    nsb = -(-(past_len + t_new) // SEL_BLOCK)
    n_lanes = -(-nsb // LANES) * LANES
    st = np.arange(n_ch) * CMP_STRIDE
    bs = np.arange(n_lanes) * SEL_BLOCK
    overlap = ((st[:, None] <= bs[None, :] + SEL_BLOCK - 1) & (st[:, None] + CMP_LEN - 1 >= bs[None, :])
               & (np.arange(n_ch)[:, None] < n_cmp) & (np.arange(n_lanes)[None, :] < nsb)).astype(np.float32)
    r = np.arange(rows)
    same_q = ((r[:, None] // rpg == r[None, :] // rpg) & (r[:, None] % t_new == r[None, :] % t_new)
              ).astype(np.float32)
    slopes = jnp.repeat(_alibi_slopes().reshape(N_HEADS), t_new).reshape(rows, 1)
    s1, s2, s3 = _split3_bf16(slopes)
    bias_q = jnp.pad(jnp.concatenate([64.0 * s1, 64.0 * s2, 64.0 * s3, s1, s2, s3], axis=1),
                     ((0, 0), (0, BIAS_POS_ROWS - 6)))
    bias_k = jnp.asarray(_sample_bias_rows(n_lanes, past_len), jnp.bfloat16)
    w_buf = win_t.shape[2]
    return pl.pallas_call(
        functools.partial(_nsa_sample_body, n_cmp=n_cmp, t_new=t_new),
        grid_spec=pltpu.PrefetchScalarGridSpec(
            num_scalar_prefetch=1, grid=(n_seq,),
            in_specs=[pl.BlockSpec(memory_space=pl.ANY),
                      pl.BlockSpec((1, rows, GD), lambda b, pt: (b, 0, 0)),
                      pl.BlockSpec((1, 2, n_ch, GD), lambda b, pt: (b, 0, 0, 0)),
                      pl.BlockSpec((1, 6, new_kv.shape[2], GD), lambda b, pt: (b, 0, 0, 0)),
                      pl.BlockSpec((1, HALF_ROWS, w_buf), lambda b, pt: (b, 0, 0)),
                      pl.BlockSpec((1, rows, 3), lambda b, pt: (b, 0, 0)),
                      pl.BlockSpec((n_ch, n_lanes), lambda b, pt: (0, 0)),
                      pl.BlockSpec((rows, rows), lambda b, pt: (0, 0)),
                      pl.BlockSpec((rows, 1), lambda b, pt: (0, 0)),
                      pl.BlockSpec(bias_q.shape, lambda b, pt: (0, 0)),
                      pl.BlockSpec(bias_k.shape, lambda b, pt: (0, 0), pipeline_mode=pl.Buffered(1))],
            out_specs=pl.BlockSpec((1, rows, HEAD_DIM), lambda b, pt: (b, 0, 0)),
            scratch_shapes=[pltpu.VMEM((2, HALF_ROWS, past_len), jnp.float32),
                            pltpu.SemaphoreType.DMA((2,))]),
        out_shape=jax.ShapeDtypeStruct((n_seq, rows, HEAD_DIM), jnp.bfloat16),
        compiler_params=pltpu.CompilerParams(dimension_semantics=("arbitrary",),
                                             vmem_limit_bytes=VMEM_LIMIT),
        name="nsa_sample",
    )(page_table, cache_t, q_bd, kcvc, new_kv, win_t, ng_r,
      jnp.asarray(overlap, jnp.bfloat16), jnp.asarray(same_q, jnp.bfloat16), slopes, bias_q, bias_k)


def _mix_body(a_ref, b_ref, mga_ref, mgb_ref, wpa_ref, wpb_ref, u_ref):
    a = jnp.dot(a_ref[...], wpa_ref[...], preferred_element_type=jnp.float32)
    b = jnp.dot(b_ref[...], wpb_ref[...], preferred_element_type=jnp.float32)
    u_ref[...] = (_sigmoid(mga_ref[...]) * a + _sigmoid(mgb_ref[...]) * b).astype(u_ref.dtype)


def _mix(o_nsa, o_hg, mg, w_pa_bf, w_pb_bf, tm):
    n = o_nsa.shape[0]
    const = lambda i: (0, 0)
    return pl.pallas_call(
        _mix_body,
        grid=(n // tm,),
        in_specs=[pl.BlockSpec((tm, NSA_WIDTH), lambda i: (i, 0)),
                  pl.BlockSpec((tm, HG_WIDTH), lambda i: (i, 0)),
                  pl.BlockSpec((tm, D_MODEL), lambda i: (i, 0)),
                  pl.BlockSpec((tm, D_MODEL), lambda i: (i, 1)),
                  pl.BlockSpec((NSA_WIDTH, D_MODEL), const, pipeline_mode=pl.Buffered(1)),
                  pl.BlockSpec((HG_WIDTH, D_MODEL), const, pipeline_mode=pl.Buffered(1))],
        out_specs=pl.BlockSpec((tm, D_MODEL), lambda i: (i, 0)),
        out_shape=jax.ShapeDtypeStruct((n, D_MODEL), jnp.bfloat16),
        compiler_params=pltpu.CompilerParams(dimension_semantics=("parallel",),
                                             vmem_limit_bytes=VMEM_LIMIT),
        name="tail_mix",
    )(o_nsa, o_hg, mg, mg, w_pa_bf, w_pb_bf)


def _layer_norm(z, g, b):
    mu = jnp.mean(z, axis=-1, keepdims=True)
    zc = z - mu
    var = jnp.mean(zc * zc, axis=-1, keepdims=True)
    return zc * lax.rsqrt(var + LN_EPS) * g + b


CH = D_MODEL // LANES


def _store_chunked(ref, val):
    tm = val.shape[0]
    for k in range(CH):
        ref[pl.ds(k, tm, stride=CH), :] = val[:, k * LANES:(k + 1) * LANES]


def _load_chunked(ref, tm, lead=()):
    return jnp.concatenate([ref[lead + (pl.ds(k, tm, stride=CH), slice(None))] for k in range(CH)], axis=1)


def _ln1_body(u_ref, x_ref, wout_ref, g_ref, b_ref, wr_hi_ref, wr_lo_ref, br_ref, h_ref, lg_ref):
    y = jnp.dot(u_ref[...], wout_ref[...], preferred_element_type=jnp.float32)
    h = _layer_norm(DN_ALPHA * x_ref[...] + y, g_ref[...], b_ref[...])
    _store_chunked(h_ref, h)
    h_hi = h.astype(jnp.bfloat16)
    h_lo = (h - h_hi.astype(jnp.float32)).astype(jnp.bfloat16)
    lg = jnp.dot(h_hi, wr_hi_ref[...], preferred_element_type=jnp.float32)
    lg = lg + jnp.dot(h_lo, wr_hi_ref[...], preferred_element_type=jnp.float32)
    lg = lg + jnp.dot(h_hi, wr_lo_ref[...], preferred_element_type=jnp.float32)
    lg_ref[...] = lg + br_ref[...]


def _ln1(u, x_all, w_out_bf, g, b, wr_hi, wr_lo, br, tm):
    n = u.shape[0]
    const = lambda i: (0, 0)
    return pl.pallas_call(
        _ln1_body,
        grid=(n // tm,),
        in_specs=[pl.BlockSpec((tm, D_MODEL), lambda i: (i, 0)),
                  pl.BlockSpec((tm, D_MODEL), lambda i: (i, 0)),
                  pl.BlockSpec((D_MODEL, D_MODEL), const, pipeline_mode=pl.Buffered(1)),
                  pl.BlockSpec((1, D_MODEL), const),
                  pl.BlockSpec((1, D_MODEL), const),
                  pl.BlockSpec((D_MODEL, LANES), const),
                  pl.BlockSpec((D_MODEL, LANES), const),
                  pl.BlockSpec((1, LANES), const)],
        out_specs=[pl.BlockSpec((tm * CH, LANES), lambda i: (i, 0)),
                   pl.BlockSpec((tm, LANES), lambda i: (i, 0))],
        out_shape=[jax.ShapeDtypeStruct((n * CH, LANES), jnp.float32),
                   jax.ShapeDtypeStruct((n, LANES), jnp.float32)],
        compiler_params=pltpu.CompilerParams(dimension_semantics=("parallel",),
                                             vmem_limit_bytes=VMEM_LIMIT),
        name="tail_ln1",
    )(u, x_all, w_out_bf, g.reshape(1, -1), b.reshape(1, -1), wr_hi, wr_lo, br)


LG0 = N_GROUPS


def _route_body(lg_ref, tri_ref, out_ref, cnt_ref, carry_scr):
    @pl.when(pl.program_id(0) == 0)
    def _():
        carry_scr[...] = jnp.zeros_like(carry_scr)

    lg = lg_ref[...]
    tm = lg.shape[0]
    lane = lax.broadcasted_iota(jnp.int32, lg.shape, 1)
    is_g = lane < N_GROUPS
    gl = jnp.where(is_g, lg, NEG)
    gmax = jnp.max(gl, axis=-1, keepdims=True)
    grp = jnp.min(jnp.where(gl == gmax, lane, LANES), axis=-1, keepdims=True)
    g_w = 1.0 / jnp.sum(jnp.where(is_g, jnp.exp(lg - gmax), 0.0), axis=-1, keepdims=True)
    lo = LG0 + grp * EXP_PER_GROUP
    el = jnp.where((lane >= lo) & (lane < lo + EXP_PER_GROUP), lg, NEG)
    v1 = jnp.max(el, axis=-1, keepdims=True)
    i1 = jnp.min(jnp.where(el == v1, lane, LANES), axis=-1, keepdims=True)
    el2 = jnp.where(lane == i1, NEG, el)
    v2 = jnp.max(el2, axis=-1, keepdims=True)
    i2 = jnp.min(jnp.where(el2 == v2, lane, LANES), axis=-1, keepdims=True)
    e21 = jnp.exp(v2 - v1)
    w1 = g_w / (1.0 + e21)
    w2 = g_w * e21 / (1.0 + e21)
    hit1 = lane == i1
    hit2 = lane == i2
    onehot = jnp.where(hit1 | hit2, 1.0, 0.0)
    incl = jnp.dot(tri_ref[...], onehot.astype(jnp.bfloat16), preferred_element_type=jnp.float32)
    carry = carry_scr[...]
    before = incl - onehot + carry
    r1 = jnp.sum(jnp.where(hit1, before, 0.0), axis=-1, keepdims=True)
    r2 = jnp.sum(jnp.where(hit2, before, 0.0), axis=-1, keepdims=True)
    carry = carry + incl[tm - 1:tm, :]
    carry_scr[...] = carry
    cnt_ref[...] = carry
    out = jnp.where(lane == 0, (i1 - LG0).astype(jnp.float32), 0.0)
    out = jnp.where(lane == 1, (i2 - LG0).astype(jnp.float32), out)
    out = jnp.where(lane == 2, w1, out)
    out = jnp.where(lane == 3, w2, out)
    out = jnp.where(lane == 4, r1, out)
    out = jnp.where(lane == 5, r2, out)
    out_ref[...] = out


def _route(lg, tm):
    n = lg.shape[0]
    tri = (np.arange(tm)[:, None] >= np.arange(tm)[None, :]).astype(np.float32)
    return pl.pallas_call(
        _route_body,
        grid=(n // tm,),
        in_specs=[pl.BlockSpec((tm, LANES), lambda i: (i, 0)),
                  pl.BlockSpec((tm, tm), lambda i: (0, 0))],
        out_specs=[pl.BlockSpec((tm, LANES), lambda i: (i, 0)),
                   pl.BlockSpec((1, LANES), lambda i: (0, 0))],
        out_shape=[jax.ShapeDtypeStruct((n, LANES), jnp.float32),
                   jax.ShapeDtypeStruct((1, LANES), jnp.float32)],
        scratch_shapes=[pltpu.VMEM((1, LANES), jnp.float32)],
        compiler_params=pltpu.CompilerParams(dimension_semantics=("arbitrary",)),
        name="moe_route",
    )(lg, jnp.asarray(tri, jnp.bfloat16))


def _dispatch_body(pos_ref, h_ref, xs_in_ref, xs_ref, sem):
    del xs_in_ref
    tm = h_ref.shape[0] // CH
    base = pl.program_id(0) * tm

    def issue(t, carry):
        src = pl.multiple_of(t * CH, CH)
        for slot in range(2):
            dst = pl.multiple_of(pos_ref[2 * (base + t) + slot] * CH, CH)
            pltpu.make_async_copy(h_ref.at[pl.ds(src, CH)], xs_ref.at[pl.ds(dst, CH)], sem).start()
        return carry

    lax.fori_loop(0, tm, issue, 0)
    for _ in range(2):
        pltpu.make_async_copy(h_ref, xs_ref.at[pl.ds(0, tm * CH)], sem).wait()


DISPATCH_TM = 128


def _dispatch(pos_flat, h_c, n_slots):
    zeros = jnp.zeros((n_slots * CH, LANES), h_c.dtype)
    tm = DISPATCH_TM
    return pl.pallas_call(
        _dispatch_body,
        grid_spec=pltpu.PrefetchScalarGridSpec(
            num_scalar_prefetch=1, grid=(h_c.shape[0] // (tm * CH),),
            in_specs=[pl.BlockSpec((tm * CH, LANES), lambda i, pos: (i, 0)),
                      pl.BlockSpec(memory_space=pl.ANY)],
            out_specs=pl.BlockSpec(memory_space=pl.ANY),
            scratch_shapes=[pltpu.SemaphoreType.DMA(())]),
        out_shape=jax.ShapeDtypeStruct(zeros.shape, h_c.dtype),
        input_output_aliases={2: 0},
        compiler_params=pltpu.CompilerParams(dimension_semantics=("arbitrary",)),
        name="moe_dispatch",
    )(pos_flat, h_c, zeros)


def _ffn_body(te_ref, nu_ref, x_ref, wg_ref, wu_ref, wd_ref, y_ref, wg_bf, wu_bf, wd_bf):
    i = pl.program_id(0)
    prev = te_ref[jnp.maximum(i - 1, 0)]

    @pl.when((i == 0) | (te_ref[i] != prev))
    def _():
        wg_bf[...] = wg_ref[0].astype(jnp.bfloat16)
        wu_bf[...] = wu_ref[0].astype(jnp.bfloat16)
        wd_bf[...] = wd_ref[0].astype(jnp.bfloat16)

    @pl.when(i < nu_ref[0])
    def _():
        x = _load_chunked(x_ref, x_ref.shape[0] // CH).astype(jnp.bfloat16)
        g = jnp.dot(x, wg_bf[...], preferred_element_type=jnp.float32)
        u = jnp.dot(x, wu_bf[...], preferred_element_type=jnp.float32)
        hid = (g * _sigmoid(g) * u).astype(jnp.bfloat16)
        _store_chunked(y_ref, jnp.dot(hid, wd_bf[...], preferred_element_type=jnp.float32))

    @pl.when(i >= nu_ref[0])
    def _():
        y_ref[...] = jnp.zeros_like(y_ref)


def _ffn(tile_expert, n_used, xs_c, w_gate, w_up, w_down, tm):
    d = D_MODEL
    nt = xs_c.shape[0] // (tm * CH)
    return pl.pallas_call(
        _ffn_body,
        grid_spec=pltpu.PrefetchScalarGridSpec(
            num_scalar_prefetch=2, grid=(nt,),
            in_specs=[pl.BlockSpec((tm * CH, LANES), lambda i, te, nu: (jnp.minimum(i, nu[0] - 1), 0)),
                      pl.BlockSpec((1, d, D_EXPERT), lambda i, te, nu: (te[i], 0, 0)),
                      pl.BlockSpec((1, d, D_EXPERT), lambda i, te, nu: (te[i], 0, 0)),
                      pl.BlockSpec((1, D_EXPERT, d), lambda i, te, nu: (te[i], 0, 0))],
            out_specs=pl.BlockSpec((tm * CH, LANES), lambda i, te, nu: (i, 0)),
            scratch_shapes=[pltpu.VMEM((d, D_EXPERT), jnp.bfloat16),
                            pltpu.VMEM((d, D_EXPERT), jnp.bfloat16),
                            pltpu.VMEM((D_EXPERT, d), jnp.bfloat16)]),
        out_shape=jax.ShapeDtypeStruct(xs_c.shape, jnp.float32),
        compiler_params=pltpu.CompilerParams(dimension_semantics=("arbitrary",),
                                             vmem_limit_bytes=VMEM_LIMIT),
        name="moe_ffn",
    )(tile_expert, n_used, xs_c, w_gate, w_up, w_down)


def _combine_body(pos_ref, h_ref, rw_ref, g_ref, b_ref, y_ref, out_a_ref, out_b_ref, buf, sem, *, tiles_a):
    tm = out_a_ref.shape[0]
    i = pl.program_id(0)
    cur = i % 2

    def gather(tile, parity):
        def one(t, carry):
            for slot in range(2):
                src = pl.multiple_of(pos_ref[2 * (tile * tm + t) + slot] * CH, CH)
                dst = pl.multiple_of(t * CH, CH)
                pltpu.make_async_copy(y_ref.at[pl.ds(src, CH)], buf.at[parity, slot, pl.ds(dst, CH)],
                                      sem.at[parity, slot]).start()
            return carry
        lax.fori_loop(0, tm, one, 0)

    @pl.when(i == 0)
    def _():
        gather(0, 0)

    @pl.when(i + 1 < pl.num_programs(0))
    def _():
        gather(i + 1, 1 - cur)

    for slot in range(2):
        pltpu.make_async_copy(y_ref.at[pl.ds(0, tm * CH)], buf.at[cur, slot], sem.at[cur, slot]).wait()
    rw = rw_ref[...]
    z = rw[:, 2:3] * _load_chunked(buf, tm, (cur, 0)) + rw[:, 3:4] * _load_chunked(buf, tm, (cur, 1))
    res = _layer_norm(DN_ALPHA * _load_chunked(h_ref, tm) + z, g_ref[...], b_ref[...])

    @pl.when(i < tiles_a)
    def _():
        out_a_ref[...] = res

    @pl.when(i >= tiles_a)
    def _():
        out_b_ref[...] = res


def _combine(pos_flat, h_c, route_out, g, b, y_c, tm, n_a):
    n, d = h_c.shape[0] // CH, D_MODEL
    assert n_a % tm == 0 and (n - n_a) % tm == 0 and 0 < n_a < n
    tiles_a = n_a // tm
    return pl.pallas_call(
        functools.partial(_combine_body, tiles_a=tiles_a),
        grid_spec=pltpu.PrefetchScalarGridSpec(
            num_scalar_prefetch=1, grid=(n // tm,),
            in_specs=[pl.BlockSpec((tm * CH, LANES), lambda i, pos: (i, 0)),
                      pl.BlockSpec((tm, LANES), lambda i, pos: (i, 0)),
                      pl.BlockSpec((1, d), lambda i, pos: (0, 0)),
                      pl.BlockSpec((1, d), lambda i, pos: (0, 0)),
                      pl.BlockSpec(memory_space=pl.ANY)],
            out_specs=[pl.BlockSpec((tm, d), lambda i, pos: (jnp.minimum(i, tiles_a - 1), 0)),
                       pl.BlockSpec((tm, d), lambda i, pos: (jnp.maximum(i - tiles_a, 0), 0))],
            scratch_shapes=[pltpu.VMEM((2, 2, tm * CH, LANES), jnp.float32),
                            pltpu.SemaphoreType.DMA((2, 2))]),
        out_shape=[jax.ShapeDtypeStruct((n_a, d), jnp.float32),
                   jax.ShapeDtypeStruct((n - n_a, d), jnp.float32)],
        compiler_params=pltpu.CompilerParams(dimension_semantics=("arbitrary",),
                                             vmem_limit_bytes=VMEM_LIMIT),
        name="moe_combine",
    )(pos_flat, h_c, route_out, g.reshape(1, -1), b.reshape(1, -1), y_c)


FFN_TM = 256
PROJ_TM = 1664
PROJ_TN = 512
TAIL_TM = 320


def _moe_and_norm(h_c, lg, w_gate, w_up, w_down, ln2_g, ln2_b, n_first):
    n = lg.shape[0]
    route_out, cnt = _route(lg, 640)
    eid = route_out[:, 0:2].astype(jnp.int32)
    rank = route_out[:, 4:6].astype(jnp.int32)
    counts = cnt[0, LG0:LG0 + N_EXPERTS].astype(jnp.int32)
    tiles_per = (counts + FFN_TM - 1) // FFN_TM
    tile_end = jnp.cumsum(tiles_per)
    row_start = (tile_end - tiles_per) * FFN_TM
    pos_flat = (row_start[eid] + rank).reshape(-1)
    nt = (2 * n) // FFN_TM + N_EXPERTS
    n_used = tile_end[-1]
    tile_ids = jnp.minimum(jnp.arange(nt, dtype=jnp.int32), n_used - 1)
    tile_expert = jnp.sum((tile_end[None, :] <= tile_ids[:, None]).astype(jnp.int32), axis=1)
    xs_c = _dispatch(pos_flat, h_c, nt * FFN_TM)
    y_c = _ffn(tile_expert, n_used.reshape(1).astype(jnp.int32), xs_c, w_gate, w_up, w_down, FFN_TM)
    return _combine(pos_flat, h_c, route_out, ln2_g, ln2_b, y_c, DISPATCH_TM, n_first)


def kernel(x_prompt, x_sample, cache_kv, cache_win, state_hgrn, page_table, w_in, b_in, w_cmp1, w_cmp2, cmp_pe,
           hgrn_gamma, hgrn_norm, w_pa, w_pb, w_out, ln1_g, ln1_b, w_rg, b_rg, w_re, b_re, w_gate, w_up, w_down,
           ln2_g, ln2_b):
    n_p = x_prompt.shape[0] * x_prompt.shape[1]
    n_s = x_sample.shape[0] * x_sample.shape[1]
    x_all = jnp.concatenate([x_prompt.reshape(n_p, D_MODEL), x_sample.reshape(n_s, D_MODEL)], axis=0)
    x_bf = x_all.astype(jnp.bfloat16)
    bsz, seq = x_prompt.shape[:2]
    n_seq, t_new = x_sample.shape[:2]
    n_pages = page_table.shape[1]
    past_len = n_pages * PAGE_SIZE
    w = w_in[0]
    b = b_in[0]

    def seg(lo, hi, outs, tn, name, pad_to=None):
        ws, bs = w[:, lo:hi], b[lo:hi]
        if pad_to is not None:
            ws = jnp.pad(ws, ((0, 0), (0, pad_to - (hi - lo))))
            bs = jnp.pad(bs, (0, pad_to - (hi - lo)))
        return _proj(x_bf, ws.astype(jnp.bfloat16), bs, outs, PROJ_TM, tn, name)

    q_h, = seg(OFF_Q, OFF_KV, [(jnp.bfloat16, True)], PROJ_TN, "proj_q")
    kv32, kv_h = seg(OFF_KV, OFF_NG, [(jnp.float32, False), (jnp.bfloat16, True)], PROJ_TN, "proj_kv")
    ng, = seg(OFF_NG, OFF_H4, [(jnp.float32, False)], LANES, "proj_ng", pad_to=LANES)
    h4, = seg(OFF_H4, OFF_MG, [(jnp.float32, False)], PROJ_TN, "proj_h4")
    mg, = seg(OFF_MG, PROJ_COLS, [(jnp.float32, False)], PROJ_TN, "proj_mg")

    kv_p = kv32[:n_p].reshape(1, bsz, seq, 6, N_KV, HEAD_DIM)
    kv_s = kv32[n_p:].reshape(1, n_seq, t_new, 6, N_KV, HEAD_DIM)
    new_kv_prompt = kv_p[:, :, :, :KV_SLOTS]
    new_kv_sample = kv_s[:, :, :, :KV_SLOTS]
    new_win_prompt = kv_p[:, :, seq - min(WINDOW, seq):, 4:6]
    win_all = jnp.concatenate([cache_win, kv_s[:, :, :, 4:6].astype(cache_win.dtype)], axis=2)
    new_win_sample = win_all[:, :, win_all.shape[2] - min(WINDOW, win_all.shape[2]):]

    w1p = _cmp_w1_pairs(w_cmp1[0])
    n_cmp_p = (seq - CMP_LEN) // CMP_STRIDE + 1
    uv_p = _cmp_uv_rows(kv32, n_p, 2048, w1p)
    kcvc_p = _cmp_finish(uv_p, bsz, seq // CMP_STRIDE, w_cmp1[0], w_cmp2[0], cmp_pe[0], "cmp_finish_prompt")
    ng_t = ng[:n_p, :NG_COLS].reshape(bsz, seq, 3, N_KV, HPG).transpose(0, 3, 1, 2, 4).reshape(bsz, N_KV, seq, 3 * HPG)
    o_nsa_p = _nsa_prompt(q_h, kcvc_p, kv_h, ng_t, n_cmp_p)

    cache_t = cache_kv[0].transpose(0, 2, 3, 4, 1).reshape(cache_kv.shape[1], 2 * HALF_ROWS, PAGE_SIZE)
    win_t = cache_win[0].transpose(0, 2, 3, 4, 1).reshape(n_seq, HALF_ROWS, cache_win.shape[2])
    n_cmp_s = (past_len + t_new - CMP_LEN) // CMP_STRIDE + 1
    kcvc_s = _cmp_sample(page_table, cache_t, w1p, w_cmp1[0], w_cmp2[0], cmp_pe[0])
    rows_s = N_HEADS * t_new
    q_s = q_h[:, n_p:].reshape(N_KV, HPG, n_seq, t_new, HEAD_DIM).transpose(2, 0, 1, 3, 4).reshape(
        n_seq, N_KV, HPG * t_new, HEAD_DIM)
    q_bd = jnp.einsum('sgrd,gk->sgrkd', q_s, jnp.eye(N_KV, dtype=q_s.dtype)).reshape(n_seq, rows_s, GD)
    new_kv = kv_h[:, n_p:].reshape(6, N_KV, n_seq, t_new, HEAD_DIM).transpose(2, 0, 3, 1, 4).reshape(
        n_seq, 6, t_new, GD)
    new_kv = jnp.pad(new_kv, ((0, 0), (0, 0), (0, 8 - t_new), (0, 0)))
    ng_r = ng[n_p:, :NG_COLS].reshape(n_seq, t_new, 3, N_KV, HPG).transpose(0, 3, 4, 1, 2).reshape(
        n_seq, rows_s, 3)
    o_nsa_s = _nsa_sample(page_table, cache_t, q_bd, kcvc_s, new_kv, win_t, ng_r, n_cmp_s, t_new)
    o_nsa_s = o_nsa_s.reshape(n_seq, N_KV, HPG, t_new, HEAD_DIM).transpose(0, 3, 1, 2, 4).reshape(n_s, NSA_WIDTH)

    lower = jnp.cumsum(jax.nn.softmax(hgrn_gamma.astype(jnp.float32), axis=0), axis=0)
    lb = lower[0].reshape(HG_HEADS, HG_DK)
    zero_state = jnp.zeros((bsz, HG_HEADS, HG_DV, HG_DK), jnp.float32)
    o_hg_p, st_p = _hgrn(h4, 0, bsz, seq, lb, hgrn_norm[0], zero_state, HG_CHUNK, HG_CHUNK, 512, 4, "hgrn_prompt")
    h4_s = jnp.pad(h4[n_p:].reshape(n_seq, t_new, -1), ((0, 0), (0, 8 - t_new), (0, 0))).reshape(n_seq * 8, -1)
    o_hg_s, st_s = _hgrn(h4_s, 0, n_seq, 8, lb, hgrn_norm[0], state_hgrn[0].transpose(0, 1, 3, 2),
                         8, t_new, 8, HG_HEADS, "hgrn_sample")
    o_hg_s = o_hg_s.reshape(n_seq, 8, HG_WIDTH)[:, :t_new].reshape(n_s, HG_WIDTH)
    new_state_prompt = st_p.transpose(0, 1, 3, 2)[None].astype(x_prompt.dtype)
    new_state_sample = st_s.transpose(0, 1, 3, 2)[None].astype(state_hgrn.dtype)

    o_nsa = jnp.concatenate([o_nsa_p, o_nsa_s], axis=0)
    o_hg = jnp.concatenate([o_hg_p, o_hg_s], axis=0)
    u = _mix(o_nsa, o_hg, mg, w_pa[0].astype(jnp.bfloat16), w_pb[0].astype(jnp.bfloat16), TAIL_TM)
    wr = jnp.zeros((D_MODEL, LANES), jnp.float32).at[:, :N_GROUPS].set(w_rg[0]).at[:, LG0:LG0 + N_EXPERTS].set(w_re[0])
    br = jnp.zeros((1, LANES), jnp.float32).at[0, :N_GROUPS].set(b_rg[0]).at[0, LG0:LG0 + N_EXPERTS].set(b_re[0])
    wr_hi, wr_lo = _split_bf16(wr)
    h, lg = _ln1(u, x_all, w_out[0].astype(jnp.bfloat16), ln1_g[0], ln1_b[0], wr_hi, wr_lo, br, TAIL_TM)
    out_p, out_s = _moe_and_norm(h, lg, w_gate[0], w_up[0], w_down[0], ln2_g[0], ln2_b[0], n_p)
    y_prompt = out_p.reshape(bsz, seq, D_MODEL)
    y_sample = out_s.reshape(n_seq, t_new, D_MODEL)
    return (y_prompt, y_sample, new_kv_prompt, new_kv_sample, new_win_prompt, new_win_sample,
            new_state_prompt, new_state_sample)
```

```python
import functools

import numpy as np
import jax
import jax.numpy as jnp
from jax import lax
from jax.experimental import pallas as pl
from jax.experimental.pallas import tpu as pltpu

D_MODEL = 2048
N_HEADS = 16
N_KV = 4
HPG = N_HEADS // N_KV
HEAD_DIM = 64
NSA_WIDTH = N_HEADS * HEAD_DIM
CMP_LEN = 32
CMP_STRIDE = 16
CMP_HID = 128
SEL_BLOCK = 64
N_SEL = 16
WINDOW = 512
QBLK = 128
KV_SLOTS = 4
PAGE_SIZE = 128
HG_HEADS = 8
HG_DK = 128
HG_DV = 128
HG_WIDTH = HG_HEADS * HG_DV
HG_CHUNK = 32
N_GROUPS = 4
EXP_PER_GROUP = 8
N_EXPERTS = N_GROUPS * EXP_PER_GROUP
D_EXPERT = 512
DEPTH = 1
DN_ALPHA = (2.0 * DEPTH) ** 0.25
LN_EPS = 1e-5
SCALE = HEAD_DIM ** -0.5
NEG = -1e30
BIG = 1e30

LANES = 128
KV_COLS = 6 * N_KV * HEAD_DIM
NG_COLS = 3 * N_HEADS
OFF_Q = 0
OFF_KV = NSA_WIDTH
OFF_NG = OFF_KV + KV_COLS
OFF_H4 = OFF_NG + NG_COLS
OFF_MG = OFF_H4 + 2 * HG_HEADS * HG_DK + 2 * HG_WIDTH
PROJ_COLS = OFF_MG + 2 * D_MODEL

VMEM_LIMIT = 56 * 1024 * 1024


def _sigmoid(x):
    return 1.0 / (1.0 + jnp.exp(-x))


def _proj_body(x_ref, w_ref, b_ref, *out_refs):
    acc = jnp.dot(x_ref[...], w_ref[...], preferred_element_type=jnp.float32) + b_ref[...]
    for o_ref in out_refs:
        if len(o_ref.shape) == 3:
            for k in range(o_ref.shape[0]):
                o_ref[k] = acc[:, k * HEAD_DIM:(k + 1) * HEAD_DIM].astype(o_ref.dtype)
        else:
            o_ref[...] = acc.astype(o_ref.dtype)


def _proj(x_bf, w_bf, b, outs, tm, tn, name):
    m, k = x_bf.shape
    n = w_bf.shape[1]
    assert m % tm == 0 and n % tn == 0
    out_specs, out_shape = [], []
    for dt, per_head in outs:
        if per_head:
            out_specs.append(pl.BlockSpec((tn // HEAD_DIM, tm, HEAD_DIM), lambda i, j: (j, i, 0)))
            out_shape.append(jax.ShapeDtypeStruct((n // HEAD_DIM, m, HEAD_DIM), dt))
        else:
            out_specs.append(pl.BlockSpec((tm, tn), lambda i, j: (i, j)))
            out_shape.append(jax.ShapeDtypeStruct((m, n), dt))
    return pl.pallas_call(
        _proj_body,
        grid=(m // tm, n // tn),
        in_specs=[pl.BlockSpec((tm, k), lambda i, j: (i, 0)),
                  pl.BlockSpec((k, tn), lambda i, j: (0, j)),
                  pl.BlockSpec((1, tn), lambda i, j: (0, j))],
        out_specs=out_specs,
        out_shape=out_shape,
        compiler_params=pltpu.CompilerParams(dimension_semantics=("parallel", "parallel"),
                                             vmem_limit_bytes=VMEM_LIMIT),
        name=name,
    )(x_bf, w_bf, b.reshape(1, -1))


def _hgrn_body(hq_ref, hf_ref, hi_ref, hg_ref, lb_ref, nrm_ref, s0_ref, o_ref, sfin_ref, st_scr,
               *, chunk, n_valid, n_chunks):
    @pl.when(pl.program_id(2) == 0)
    def _():
        st_scr[...] = s0_ref[0]

    for h in range(st_scr.shape[0]):
        cols = slice(h * HG_DK, (h + 1) * HG_DK)
        o, st = _hgrn_head(hq_ref[:, cols], hf_ref[:, cols], hi_ref[:, cols], hg_ref[:, cols],
                           lb_ref[h], nrm_ref[h], st_scr[h], chunk, n_valid, n_chunks)
        st_scr[h] = st
        sfin_ref[0, h] = st
        o_ref[:, cols] = o.astype(o_ref.dtype)


def _hgrn_head(hq, hf, v, hg, lb, nrm, st, chunk, n_valid, n_chunks):
    rows = chunk * n_chunks
    q = hq * _sigmoid(hq)
    f = lb + (1.0 - lb) * _sigmoid(hf)
    k = 1.0 - f
    lc = jnp.log(f)
    row_in_chunk = lax.broadcasted_iota(jnp.int32, (rows, HG_DK), 0) % chunk
    if n_valid < chunk:
        live = row_in_chunk < n_valid
        q = jnp.where(live, q, 0.0)
        k = jnp.where(live, k, 0.0)
        v = jnp.where(live, v, 0.0)
        lc = jnp.where(live, lc, 0.0)
    bc = lc
    step = 1
    while step < chunk:
        bc = bc + jnp.where(row_in_chunk >= step, pltpu.roll(bc, step, axis=0), 0.0)
        step *= 2
    bc3 = bc.reshape(n_chunks, chunk, HG_DK)
    bl3 = bc3[:, chunk - 1:chunk, :]
    q3 = q.reshape(n_chunks, chunk, HG_DK)
    k3 = k.reshape(n_chunks, chunk, HG_DK)
    v3 = v.reshape(n_chunks, chunk, HG_DV).astype(jnp.bfloat16)
    qe3 = (q3 * jnp.exp(bc3)).astype(jnp.bfloat16)
    ke3 = (k3 * jnp.exp(-bc3)).astype(jnp.bfloat16)
    kd3 = (k3 * jnp.exp(bl3 - bc3)).astype(jnp.bfloat16)
    dec3 = jnp.exp(bl3)
    att = jnp.einsum('ctd,csd->cts', qe3, ke3, preferred_element_type=jnp.float32)
    tri = (lax.broadcasted_iota(jnp.int32, (chunk, chunk), 0)
           >= lax.broadcasted_iota(jnp.int32, (chunk, chunk), 1))
    att = jnp.where(tri[None], att, 0.0).astype(jnp.bfloat16)
    o_intra = jnp.einsum('cts,cse->cte', att, v3, preferred_element_type=jnp.float32)

    outs = []
    for c in range(n_chunks):
        o_c = lax.dot_general(qe3[c], st.astype(jnp.bfloat16), (((1,), (1,)), ((), ())),
                              preferred_element_type=jnp.float32)
        outs.append(o_c + o_intra[c])
        upd = lax.dot_general(v3[c], kd3[c], (((0,), (0,)), ((), ())),
                              preferred_element_type=jnp.float32)
        st = st * dec3[c] + upd
    o = jnp.concatenate(outs, axis=0) if n_chunks > 1 else outs[0]
    o = o * lax.rsqrt(jnp.mean(o * o, axis=-1, keepdims=True) + LN_EPS) * nrm
    return o * (hg * _sigmoid(hg)), st


def _hgrn(h4, row0, n_seq, t_seq, lb, nrm, s0_t, chunk, n_valid, block_rows, heads_per_step, name):
    assert t_seq % block_rows == 0 and block_rows % chunk == 0 and row0 % block_rows == 0
    nb = t_seq // block_rows
    rb0 = row0 // block_rows
    nh = heads_per_step
    hb = HG_HEADS // nh

    def col_spec(seg):
        return pl.BlockSpec((block_rows, nh * HG_DK),
                            lambda b, h, i, seg=seg: (rb0 + b * nb + i, seg * hb + h))

    body = functools.partial(_hgrn_body, chunk=chunk, n_valid=n_valid, n_chunks=block_rows // chunk)
    return pl.pallas_call(
        body,
        grid=(n_seq, hb, nb),
        in_specs=[col_spec(0), col_spec(1), col_spec(2), col_spec(3),
                  pl.BlockSpec((nh, 1, HG_DK), lambda b, h, i: (h, 0, 0)),
                  pl.BlockSpec((nh, 1, HG_DV), lambda b, h, i: (h, 0, 0)),
                  pl.BlockSpec((1, nh, HG_DV, HG_DK), lambda b, h, i: (b, h, 0, 0))],
        out_specs=[pl.BlockSpec((block_rows, nh * HG_DV), lambda b, h, i: (b * nb + i, h)),
                   pl.BlockSpec((1, nh, HG_DV, HG_DK), lambda b, h, i: (b, h, 0, 0))],
        out_shape=[jax.ShapeDtypeStruct((n_seq * t_seq, HG_WIDTH), jnp.bfloat16),
                   jax.ShapeDtypeStruct((n_seq, HG_HEADS, HG_DV, HG_DK), jnp.float32)],
        scratch_shapes=[pltpu.VMEM((nh, HG_DV, HG_DK), jnp.float32)],
        compiler_params=pltpu.CompilerParams(
            dimension_semantics=("parallel", "parallel", "arbitrary"), vmem_limit_bytes=VMEM_LIMIT),
        name=name,
    )(h4, h4, h4, h4, lb.reshape(HG_HEADS, 1, HG_DK), nrm.reshape(HG_HEADS, 1, HG_DV), s0_t)


UV_COLS = 2 * N_KV * 2 * CMP_HID


def _uv_taps(tap, w_ref, c, n):
    acc = jnp.zeros((n, 4 * CMP_HID), jnp.float32)
    for p in range(CMP_STRIDE // 2):
        x2 = jnp.concatenate([tap(2 * p), tap(2 * p + 1)], axis=-1)
        acc = acc + jnp.dot(x2, w_ref[c, p], preferred_element_type=jnp.float32)
    return acc


def _cmp_uv_body(x0_ref, x1_ref, x2_ref, x3_ref, w_ref, uv_ref):
    n = uv_ref.shape[0]
    for cgp, x_ref in enumerate((x0_ref, x1_ref, x2_ref, x3_ref)):
        tap = lambda j, x_ref=x_ref: x_ref[pl.ds(j, n, stride=CMP_STRIDE), :].astype(jnp.bfloat16)
        uv_ref[:, cgp * 512:(cgp + 1) * 512] = _uv_taps(tap, w_ref, cgp // 2, n)


def _cmp_w1_pairs(w_cmp1_l):
    w = jnp.concatenate([w_cmp1_l[:, :CMP_STRIDE], w_cmp1_l[:, CMP_STRIDE:]], axis=-1)
    z = jnp.zeros_like(w)
    top = jnp.concatenate([w, z], axis=-1)
    bot = jnp.concatenate([z, w], axis=-1)
    per_tap = jnp.concatenate([top, bot], axis=-2)
    return per_tap.reshape(2, CMP_STRIDE // 2, 2 * LANES, 4 * CMP_HID).astype(jnp.bfloat16)


def _cmp_uv_rows(kv, n_rows, rows_per_step, w1p):
    n = rows_per_step // CMP_STRIDE
    return pl.pallas_call(
        _cmp_uv_body,
        grid=(n_rows // rows_per_step,),
        in_specs=[pl.BlockSpec((rows_per_step, LANES), lambda i, cb=cb: (i, cb)) for cb in range(4)]
        + [pl.BlockSpec(w1p.shape, lambda i: (0, 0, 0, 0))],
        out_specs=pl.BlockSpec((n, UV_COLS), lambda i: (i, 0)),
        out_shape=jax.ShapeDtypeStruct((n_rows // CMP_STRIDE, UV_COLS), jnp.float32),
        compiler_params=pltpu.CompilerParams(dimension_semantics=("parallel",),
                                             vmem_limit_bytes=VMEM_LIMIT),
        name="cmp_uv_prompt",
    )(kv, kv, kv, kv, w1p)


def _gelu_tanh(x):
    return 0.5 * x * (1.0 + jnp.tanh(0.7978845608028654 * (x + 0.044715 * x * x * x)))


def _cmp_finish_body(uv_ref, pe_ref, w1_ref, w2_ref, out_ref):
    _cmp_finish_rows(lambda cols: uv_ref[0, :, cols], uv_ref.shape[1], pe_ref, w1_ref, w2_ref, out_ref)


def _cmp_finish_rows(uv_cols, n_ch, pe_ref, w1_ref, w2_ref, out_ref):
    for c in range(2):
        pe_term = jnp.dot(pe_ref[c], w1_ref[c], preferred_element_type=jnp.float32)[0:1, :]
        for g in range(N_KV):
            base = (c * N_KV + g) * 2 * CMP_HID
            u = uv_cols(slice(base, base + CMP_HID))
            v = uv_cols(slice(base + CMP_HID, base + 2 * CMP_HID))
            pre = u + pltpu.roll(v, n_ch - 1, axis=0) + pe_term
            hid = _gelu_tanh(pre).astype(jnp.bfloat16)
            res = jnp.dot(hid, w2_ref[c], preferred_element_type=jnp.float32).astype(out_ref.dtype)
            if out_ref.shape[1] == 2:
                out_ref[0, c, :, g * HEAD_DIM:(g + 1) * HEAD_DIM] = res
            else:
                out_ref[0, c * N_KV + g] = res


def _cmp_finish(uv, n_seq, n_ch, w_cmp1_l, w_cmp2_l, cmp_pe_l, name):
    pe = jnp.zeros((2, 8, CMP_LEN * HEAD_DIM), jnp.float32).at[:, 0].set(cmp_pe_l.reshape(2, -1))
    out_dims = (2 * N_KV, n_ch, HEAD_DIM)
    return pl.pallas_call(
        _cmp_finish_body,
        grid=(n_seq,),
        in_specs=[pl.BlockSpec((1, n_ch, UV_COLS), lambda b: (b, 0, 0)),
                  pl.BlockSpec((2, 8, CMP_LEN * HEAD_DIM), lambda b: (0, 0, 0)),
                  pl.BlockSpec((2, CMP_LEN * HEAD_DIM, CMP_HID), lambda b: (0, 0, 0)),
                  pl.BlockSpec((2, CMP_HID, HEAD_DIM), lambda b: (0, 0, 0))],
        out_specs=pl.BlockSpec((1,) + out_dims, lambda b: (b, 0, 0, 0)),
        out_shape=jax.ShapeDtypeStruct((n_seq,) + out_dims, jnp.bfloat16),
        compiler_params=pltpu.CompilerParams(dimension_semantics=("parallel",),
                                             vmem_limit_bytes=VMEM_LIMIT),
        name=name,
    )(uv.reshape(n_seq, n_ch, UV_COLS), pe.astype(jnp.bfloat16),
      w_cmp1_l.reshape(2, CMP_LEN * HEAD_DIM, CMP_HID).astype(jnp.bfloat16),
      w_cmp2_l.astype(jnp.bfloat16))


SLC_TK = 512
WIN_KEYS = WINDOW + QBLK


def _masked_softmax(s, valid):
    s = jnp.where(valid, s, NEG)
    m = jnp.max(s, axis=-1, keepdims=True)
    e = jnp.where(valid, jnp.exp(s - m), 0.0)
    return e / jnp.maximum(jnp.sum(e, axis=-1, keepdims=True), 1e-30)


def _split_bf16(x):
    hi = x.astype(jnp.bfloat16)
    return hi, (x - hi.astype(jnp.float32)).astype(jnp.bfloat16)


SUBLANES = 8


def _top_blocks_t(imp_t):
    nb, nq = imp_t.shape
    groups = [imp_t[SUBLANES * v:SUBLANES * (v + 1), :] for v in range(nb // SUBLANES)]
    sub = lax.broadcasted_iota(jnp.int32, (SUBLANES, nq), 0)
    beaten_by = [jnp.zeros((SUBLANES, nq), jnp.float32) for _ in groups]
    for i in range(nb):
        row = jnp.broadcast_to(imp_t[i:i + 1, :], (SUBLANES, nq))
        for v, gv in enumerate(groups):
            if v > i // SUBLANES:
                beats = row >= gv
            elif v < i // SUBLANES:
                beats = row > gv
            else:
                beats = (row > gv) | ((row == gv) & (sub > i % SUBLANES))
            beaten_by[v] = beaten_by[v] + jnp.where(beats, 1.0, 0.0)
    return jnp.concatenate([jnp.where(c < N_SEL, 1.0, 0.0) for c in beaten_by], axis=0)


def _split3_bf16(x):
    x1 = x.astype(jnp.bfloat16)
    r = x - x1.astype(jnp.float32)
    x2 = r.astype(jnp.bfloat16)
    return x1, x2, (r - x2.astype(jnp.float32)).astype(jnp.bfloat16)


XK = 2 * LANES


def _slc_key_columns(t):
    kpos = np.arange(t)
    cols = np.zeros((t, XK - HEAD_DIM), np.float32)
    cols[kpos, kpos // SEL_BLOCK] = 1.0
    cols[:, 64:67] = (kpos // 64)[:, None]
    cols[:, 67:70] = (kpos % 64)[:, None]
    cols[:, 70:73] = 1.0
    return cols


def _slc_query_columns():
    s1, s2, s3 = _split3_bf16(_alibi_slopes().reshape(N_KV, HPG, 1))
    cols = jnp.concatenate([64.0 * s1, 64.0 * s2, 64.0 * s3, s1, s2, s3], axis=-1)
    cols = jnp.pad(cols, ((0, 0), (0, 0), (0, LANES - 6)))
    return jnp.broadcast_to(cols[:, :, None, :], (N_KV, HPG, QBLK, LANES)).reshape(N_KV, HPG * QBLK, LANES)


def _nsa_prompt_body(q_ref, kc_ref, vc_ref, kx_ref, vs_ref, kw_ref, vw_ref, ng_ref, ov_ref, sx_ref, cb_ref, sl_ref,
                     o_ref, *, n_cmp):
    i = pl.program_id(2)
    nq = QBLK
    rows = HPG * nq
    q2 = (q_ref[...] * SCALE).reshape(rows, HEAD_DIM)
    slopes = sl_ref[0]
    qpos = i * nq + lax.broadcasted_iota(jnp.int32, (nq, 1), 0)
    nt = (((1,), (1,)), ((), ()))

    n_ch = kc_ref.shape[2]
    s = lax.dot_general(q2, kc_ref[0, 0], nt, preferred_element_type=jnp.float32).reshape(HPG, nq, n_ch)
    cidx = lax.broadcasted_iota(jnp.int32, (1, n_ch), 1)
    dist_c = qpos - (cidx * CMP_STRIDE + (CMP_LEN - 1))
    valid_c = ((dist_c >= 0) & (cidx < n_cmp))[None]
    p = _masked_softmax(s - slopes * dist_c.astype(jnp.float32)[None], valid_c)
    o_cmp = jnp.dot(p.reshape(rows, n_ch).astype(jnp.bfloat16), vc_ref[0, 0],
                    preferred_element_type=jnp.float32).reshape(HPG, nq, HEAD_DIM)

    w0 = pl.multiple_of(jnp.maximum(i - WINDOW // QBLK, 0) * nq, nq)
    s = lax.dot_general(q2, kw_ref[0, pl.ds(w0, WIN_KEYS), :], nt,
                        preferred_element_type=jnp.float32).reshape(HPG, nq, WIN_KEYS)
    dist = qpos - (w0 + lax.broadcasted_iota(jnp.int32, (1, WIN_KEYS), 1))
    valid = ((dist >= 0) & (dist < WINDOW))[None]
    pw = _masked_softmax(s - slopes * dist.astype(jnp.float32)[None], valid)
    o_win = jnp.dot(pw.reshape(rows, WIN_KEYS).astype(jnp.bfloat16), vw_ref[0, pl.ds(w0, WIN_KEYS), :],
                    preferred_element_type=jnp.float32).reshape(HPG, nq, HEAD_DIM)

    p_hi, p_lo = _split_bf16(p[0] + p[1] + p[2] + p[3])
    imp_t = (lax.dot_general(ov_ref[...], p_hi, nt, preferred_element_type=jnp.float32)
             + lax.dot_general(ov_ref[...], p_lo, nt, preferred_element_type=jnp.float32))
    nsb = imp_t.shape[0]
    blk = lax.broadcasted_iota(jnp.int32, (nsb, 1), 0)
    cur = (i * nq + lax.broadcasted_iota(jnp.int32, (1, nq), 1)) // SEL_BLOCK
    imp_t = jnp.where(blk > cur, NEG, imp_t)
    imp_t = jnp.where((blk == cur) | (blk == 0), BIG, imp_t)
    sel_t = jnp.where(blk <= cur, _top_blocks_t(imp_t), 0.0)

    eye = (lax.broadcasted_iota(jnp.int32, (nsb, nsb), 0)
           == lax.broadcasted_iota(jnp.int32, (nsb, nsb), 1)).astype(jnp.bfloat16)
    drop = lax.dot_general(((sel_t - 1.0) * BIG).astype(jnp.bfloat16), eye, (((0,), (0,)), ((), ())),
                           preferred_element_type=jnp.float32)
    drop = jnp.concatenate([drop.astype(jnp.bfloat16)] * HPG, axis=0)
    slope_row = jnp.broadcast_to(slopes, (HPG, nq, 1)).reshape(rows, 1)
    qpos_row = jnp.concatenate([qpos] * HPG, axis=0).astype(jnp.float32)
    c1, c2, c3 = _split3_bf16(slope_row * qpos_row)
    lane = lax.broadcasted_iota(jnp.int32, (rows, LANES), 1)
    pos_cols = sx_ref[0].astype(jnp.float32)
    for k, ck in enumerate((c1, c2, c3)):
        pos_cols = jnp.where(lane == 6 + k, -ck.astype(jnp.float32), pos_cols)
    q_ext = jnp.concatenate([q2, drop, pos_cols.astype(jnp.bfloat16)], axis=1)

    def slc_tile(kt, carry, causal_bias):
        m, l, acc = carry
        k0 = pl.multiple_of(kt * SLC_TK, SLC_TK)
        s = lax.dot_general(q_ext, kx_ref[0, pl.ds(k0, SLC_TK), :], nt,
                            preferred_element_type=jnp.float32).reshape(HPG, nq, SLC_TK)
        if causal_bias is not None:
            s = s + causal_bias[None]
        m_new = jnp.maximum(m, jnp.max(s, axis=-1, keepdims=True))
        a = jnp.exp(m - m_new)
        e = jnp.exp(s - m_new)
        l = a * l + jnp.sum(e, axis=-1, keepdims=True)
        pv = jnp.dot(e.reshape(rows, SLC_TK).astype(jnp.bfloat16), vs_ref[0, pl.ds(k0, SLC_TK), :],
                     preferred_element_type=jnp.float32).reshape(HPG, nq, HEAD_DIM)
        return m_new, l, a * acc + pv

    init = (jnp.full((HPG, nq, 1), NEG, jnp.float32), jnp.zeros((HPG, nq, 1), jnp.float32),
            jnp.zeros((HPG, nq, HEAD_DIM), jnp.float32))
    last = (i * nq) // SLC_TK
    carry = lax.fori_loop(0, last // 2, lambda p, c: slc_tile(2 * p + 1, slc_tile(2 * p, c, None), None), init)
    carry = lax.fori_loop(2 * (last // 2), last, lambda kt, c: slc_tile(kt, c, None), carry)
    _, l, acc = slc_tile(last, carry, cb_ref[i % (SLC_TK // QBLK)])
    o_slc = acc / jnp.maximum(l, 1e-30)

    gates = _sigmoid(ng_ref[0, 0])
    o_ref[...] = jnp.concatenate(
        [gates[:, h:h + 1] * o_cmp[h] + gates[:, HPG + h:HPG + h + 1] * o_slc[h]
         + gates[:, 2 * HPG + h:2 * HPG + h + 1] * o_win[h] for h in range(HPG)], axis=-1).astype(o_ref.dtype)


def _alibi_slopes():
    return jnp.asarray(2.0 ** (-8.0 * np.arange(1, N_HEADS + 1) / N_HEADS), jnp.float32).reshape(N_KV, HPG, 1, 1)


def _nsa_prompt(q_h, kcvc, kv_h, ng_t, n_cmp):
    bsz, _, t, _ = ng_t.shape
    nqb = t // QBLK
    n_ch = kcvc.shape[2]
    nsb = t // SEL_BLOCK
    st = np.arange(n_ch) * CMP_STRIDE
    bs = np.arange(nsb) * SEL_BLOCK
    overlap = ((st[:, None] <= bs[None, :] + SEL_BLOCK - 1) & (st[:, None] + CMP_LEN - 1 >= bs[None, :])
               & (np.arange(n_ch)[:, None] < n_cmp)).astype(np.float32)
    assert nsb == LANES - HEAD_DIM and t % SLC_TK == 0
    key_cols = jnp.asarray(np.tile(_slc_key_columns(t), (bsz, 1)), jnp.bfloat16)
    kx = jnp.concatenate([kv_h[2 * N_KV:3 * N_KV, :bsz * t],
                          jnp.broadcast_to(key_cols[None], (N_KV,) + key_cols.shape)], axis=-1)
    r = np.arange(SLC_TK // QBLK)[:, None, None] * QBLK + np.arange(QBLK)[None, :, None]
    causal = np.where(np.arange(SLC_TK)[None, None, :] <= r, 0.0, NEG).astype(np.float32)

    def kv_spec(slot):
        return pl.BlockSpec((1, t, HEAD_DIM), lambda b, g, i, slot=slot: (slot * N_KV + g, b, 0))

    return pl.pallas_call(
        functools.partial(_nsa_prompt_body, n_cmp=n_cmp),
        grid=(bsz, N_KV, nqb),
        in_specs=[pl.BlockSpec((HPG, QBLK, HEAD_DIM), lambda b, g, i: (g, b * nqb + i, 0)),
                  pl.BlockSpec((1, 1, n_ch, HEAD_DIM), lambda b, g, i: (b, g, 0, 0)),
                  pl.BlockSpec((1, 1, n_ch, HEAD_DIM), lambda b, g, i: (b, N_KV + g, 0, 0)),
                  pl.BlockSpec((1, t, XK), lambda b, g, i: (g, b, 0)),
                  kv_spec(3), kv_spec(4), kv_spec(5),
                  pl.BlockSpec((1, 1, QBLK, 3 * HPG), lambda b, g, i: (b, g, i, 0)),
                  pl.BlockSpec((nsb, n_ch), lambda b, g, i: (0, 0)),
                  pl.BlockSpec((1, HPG * QBLK, LANES), lambda b, g, i: (g, 0, 0)),
                  pl.BlockSpec(causal.shape, lambda b, g, i: (0, 0, 0)),
                  pl.BlockSpec((1, HPG, 1, 1), lambda b, g, i: (g, 0, 0, 0))],
        out_specs=pl.BlockSpec((QBLK, HPG * HEAD_DIM), lambda b, g, i: (b * nqb + i, g)),
        out_shape=jax.ShapeDtypeStruct((bsz * t, NSA_WIDTH), jnp.bfloat16),
        compiler_params=pltpu.CompilerParams(
            dimension_semantics=("parallel", "parallel", "arbitrary"), vmem_limit_bytes=VMEM_LIMIT),
        name="nsa_prompt",
    )(q_h, kcvc, kcvc, kx, kv_h, kv_h, kv_h, ng_t,
      jnp.asarray(overlap.T, jnp.bfloat16), _slc_query_columns(), jnp.asarray(causal), _alibi_slopes())


HALF_ROWS = 2 * N_KV * HEAD_DIM
GD = N_KV * HEAD_DIM


def _fetch_pages(pt_ref, cache_ref, sem, row0, per_step, dst_of):
    b = pl.program_id(0)
    slot = b % 2
    n_pages = pt_ref.shape[1]

    def page_copy(flat_page, sl, k):
        phys = pt_ref[flat_page // n_pages, flat_page % n_pages]
        return pltpu.make_async_copy(cache_ref.at[phys, pl.ds(row0, HALF_ROWS), :], dst_of(sl, k), sem.at[sl])

    def start_all(step, sl):
        def one(k, carry):
            page_copy(step * per_step + k, sl, k).start()
            return carry
        lax.fori_loop(0, per_step, one, 0)

    @pl.when(b == 0)
    def _():
        start_all(0, 0)

    @pl.when(b + 1 < pl.num_programs(0))
    def _():
        start_all(b + 1, 1 - slot)

    def wait_one(k, carry):
        page_copy(b * per_step + k, slot, k).wait()
        return carry
    lax.fori_loop(0, per_step, wait_one, 0)
    return slot


def _cmp_sample_body(pt_ref, cache_ref, w_ref, pe_ref, w1_ref, w2_ref, out_ref, buf, sem, xt, uv, *, steps):
    per_step = buf.shape[1]
    slot = _fetch_pages(pt_ref, cache_ref, sem, 0, per_step, lambda sl, k: buf.at[sl, k])
    n_cb = HALF_ROWS // LANES

    eye = (lax.broadcasted_iota(jnp.int32, (LANES, LANES), 0)
           == lax.broadcasted_iota(jnp.int32, (LANES, LANES), 1)).astype(jnp.bfloat16)

    def to_rows(k, carry):
        for cb in range(n_cb):
            blk = buf[slot, k, cb * LANES:(cb + 1) * LANES, :].astype(jnp.bfloat16)
            xt[k, cb] = lax.dot_general(eye, blk, (((1,), (1,)), ((), ())), preferred_element_type=jnp.float32)
        return carry
    lax.fori_loop(0, per_step, to_rows, 0, unroll=8)

    per_page = PAGE_SIZE // CMP_STRIDE
    n = per_step * per_page

    def tap(cgp, j):
        return xt[:, cgp, pl.ds(j, per_page, stride=CMP_STRIDE), :].reshape(n, LANES).astype(jnp.bfloat16)

    part = pl.program_id(0) % steps
    row0 = pl.multiple_of(part * n, n)
    for cgp in range(n_cb):
        uv[pl.ds(row0, n), cgp * 512:(cgp + 1) * 512] = _uv_taps(functools.partial(tap, cgp), w_ref, cgp // 2, n)

    @pl.when(part == steps - 1)
    def _():
        _cmp_finish_rows(lambda cols: uv[:, cols], uv.shape[0], pe_ref, w1_ref, w2_ref, out_ref)


UV_PAGES = 32


def _cmp_sample(page_table, cache_t, w1p, w_cmp1_l, w_cmp2_l, cmp_pe_l):
    n_seq, n_pages = page_table.shape
    assert n_pages % UV_PAGES == 0
    steps = n_pages // UV_PAGES
    n_ch = n_pages * (PAGE_SIZE // CMP_STRIDE)
    pe = jnp.zeros((2, 8, CMP_LEN * HEAD_DIM), jnp.float32).at[:, 0].set(cmp_pe_l.reshape(2, -1))
    return pl.pallas_call(
        functools.partial(_cmp_sample_body, steps=steps),
        grid_spec=pltpu.PrefetchScalarGridSpec(
            num_scalar_prefetch=1, grid=(n_seq * steps,),
            in_specs=[pl.BlockSpec(memory_space=pl.ANY),
                      pl.BlockSpec(w1p.shape, lambda b, pt: (0, 0, 0, 0)),
                      pl.BlockSpec((2, 8, CMP_LEN * HEAD_DIM), lambda b, pt: (0, 0, 0)),
                      pl.BlockSpec((2, CMP_LEN * HEAD_DIM, CMP_HID), lambda b, pt: (0, 0, 0)),
                      pl.BlockSpec((2, CMP_HID, HEAD_DIM), lambda b, pt: (0, 0, 0))],
            out_specs=pl.BlockSpec((1, 2, n_ch, GD), lambda b, pt: (b // steps, 0, 0, 0)),
            scratch_shapes=[pltpu.VMEM((2, UV_PAGES, HALF_ROWS, PAGE_SIZE), jnp.float32),
                            pltpu.SemaphoreType.DMA((2,)),
                            pltpu.VMEM((UV_PAGES, HALF_ROWS // LANES, PAGE_SIZE, LANES), jnp.float32),
                            pltpu.VMEM((n_ch, UV_COLS), jnp.float32)]),
        out_shape=jax.ShapeDtypeStruct((n_seq, 2, n_ch, GD), jnp.bfloat16),
        compiler_params=pltpu.CompilerParams(dimension_semantics=("arbitrary",),
                                             vmem_limit_bytes=VMEM_LIMIT),
        name="cmp_sample",
    )(page_table, cache_t, w1p, pe.astype(jnp.bfloat16),
      w_cmp1_l.reshape(2, CMP_LEN * HEAD_DIM, CMP_HID).astype(jnp.bfloat16), w_cmp2_l.astype(jnp.bfloat16))


def _group_diag(x, rows_per_group):
    grp = lax.broadcasted_iota(jnp.int32, (x.shape[0], 1), 0) // rows_per_group
    out = jnp.zeros((x.shape[0], HEAD_DIM), x.dtype)
    for g in range(N_KV):
        out = out + jnp.where(grp == g, x[:, g * HEAD_DIM:(g + 1) * HEAD_DIM], 0.0)
    return out


BIAS_POS_ROWS = 16


def _sample_bias_rows(n_lanes, past_len):
    kpos = np.arange(past_len)
    rows = np.zeros((n_lanes + BIAS_POS_ROWS, past_len), np.float32)
    rows[kpos // SEL_BLOCK, kpos] = 1.0
    rows[n_lanes:n_lanes + 3] = kpos // 64
    rows[n_lanes + 3:n_lanes + 6] = kpos % 64
    rows[n_lanes + 6:n_lanes + 9] = 1.0
    return rows


def _nsa_sample_body(pt_ref, cache_ref, q_ref, kcvc_ref, new_ref, win_ref, ng_ref, ov_ref, same_q_ref, sl_ref,
                     bq_ref, bk_ref, o_ref, buf, sem, *, n_cmp, t_new):
    past_len = buf.shape[2]
    slot = _fetch_pages(pt_ref, cache_ref, sem, HALF_ROWS, past_len // PAGE_SIZE,
                        lambda sl, k: buf.at[sl, :, pl.ds(pl.multiple_of(k * PAGE_SIZE, PAGE_SIZE), PAGE_SIZE)])
    rows = q_ref.shape[1]
    rpg = rows // N_KV
    nt = (((1,), (1,)), ((), ()))
    qbd = q_ref[0] * SCALE
    slope = sl_ref[...]
    row = lax.broadcasted_iota(jnp.int32, (rows, 1), 0)
    qtok = row % t_new
    qpos = past_len + qtok
    new_rows = new_ref.shape[2]
    new_idx = lax.broadcasted_iota(jnp.int32, (1, new_rows), 1)
    new_valid = (new_idx <= qtok) & (new_idx < t_new)
    new_bias = slope * (qtok - new_idx).astype(jnp.float32)

    n_ch = kcvc_ref.shape[2]
    s = lax.dot_general(qbd, kcvc_ref[0, 0], nt, preferred_element_type=jnp.float32)
    cidx = lax.broadcasted_iota(jnp.int32, (1, n_ch), 1)
    dist_c = qpos - (cidx * CMP_STRIDE + (CMP_LEN - 1))
    p = _masked_softmax(s - slope * dist_c.astype(jnp.float32), (dist_c >= 0) & (cidx < n_cmp))
    o_cmp = _group_diag(jnp.dot(p.astype(jnp.bfloat16), kcvc_ref[0, 1], preferred_element_type=jnp.float32), rpg)

    p_hi, p_lo = _split_bf16(p)
    psum = (jnp.dot(same_q_ref[...], p_hi, preferred_element_type=jnp.float32)
            + jnp.dot(same_q_ref[...], p_lo, preferred_element_type=jnp.float32))
    ps_hi, ps_lo = _split_bf16(psum)
    imp_t = (lax.dot_general(ov_ref[...], ps_hi, nt, preferred_element_type=jnp.float32)
             + lax.dot_general(ov_ref[...], ps_lo, nt, preferred_element_type=jnp.float32))
    n_lanes = imp_t.shape[0]
    blk = lax.broadcasted_iota(jnp.int32, (n_lanes, 1), 0)
    cur = (past_len + lax.broadcasted_iota(jnp.int32, (1, rows), 1) % t_new) // SEL_BLOCK
    imp_t = jnp.where(blk > cur, -jnp.inf, imp_t)
    imp_t = jnp.where((blk == cur) | (blk == 0), BIG, imp_t)
    nsb_pad = -(-(-(-(past_len + t_new) // SEL_BLOCK)) // SUBLANES) * SUBLANES
    sel_t = jnp.concatenate([_top_blocks_t(imp_t[:nsb_pad]),
                             jnp.zeros((n_lanes - nsb_pad, rows), jnp.float32)], axis=0)
    eye = (lax.broadcasted_iota(jnp.int32, (n_lanes, n_lanes), 0)
           == lax.broadcasted_iota(jnp.int32, (n_lanes, n_lanes), 1)).astype(jnp.bfloat16)
    drop = lax.dot_general(((sel_t - 1.0) * BIG).astype(jnp.bfloat16), eye, (((0,), (0,)), ((), ())),
                           preferred_element_type=jnp.float32)

    c1, c2, c3 = _split3_bf16(slope * qpos.astype(jnp.float32))
    lane = lax.broadcasted_iota(jnp.int32, (rows, bq_ref.shape[1]), 1)
    pos_cols = bq_ref[...].astype(jnp.float32)
    for k, ck in enumerate((c1, c2, c3)):
        pos_cols = jnp.where(lane == 6 + k, -ck.astype(jnp.float32), pos_cols)
    bias_q = jnp.concatenate([drop.astype(jnp.bfloat16), pos_cols.astype(jnp.bfloat16)], axis=1)
    s_all = (jnp.dot(qbd, buf[slot, 0:GD, :].astype(jnp.bfloat16), preferred_element_type=jnp.float32)
             + jnp.dot(bias_q, bk_ref[...], preferred_element_type=jnp.float32))
    s_new = lax.dot_general(qbd, new_ref[0, 2], nt, preferred_element_type=jnp.float32)
    s_new = jnp.where(new_valid, s_new - new_bias, NEG)
    m = jnp.maximum(jnp.max(s_all, axis=-1, keepdims=True), jnp.max(s_new, axis=-1, keepdims=True))
    e_all = jnp.exp(s_all - m)
    e_new = jnp.where(new_valid, jnp.exp(s_new - m), 0.0)
    l = jnp.sum(e_all, axis=-1, keepdims=True) + jnp.sum(e_new, axis=-1, keepdims=True)
    acc = (lax.dot_general(e_all.astype(jnp.bfloat16), buf[slot, GD:2 * GD, :].astype(jnp.bfloat16), nt,
                           preferred_element_type=jnp.float32)
           + jnp.dot(e_new.astype(jnp.bfloat16), new_ref[0, 3], preferred_element_type=jnp.float32))
    o_slc = _group_diag(acc, rpg) / jnp.maximum(l, 1e-30)

    w_buf = win_ref.shape[2]
    wdist = qpos - (past_len - w_buf + lax.broadcasted_iota(jnp.int32, (1, w_buf), 1))
    valid_w = (wdist >= 0) & (wdist < WINDOW)
    s_w = jnp.dot(qbd, win_ref[0, 0:GD, :].astype(jnp.bfloat16), preferred_element_type=jnp.float32)
    s_w = jnp.where(valid_w, s_w - slope * wdist.astype(jnp.float32), NEG)
    s_wn = lax.dot_general(qbd, new_ref[0, 4], nt, preferred_element_type=jnp.float32)
    s_wn = jnp.where(new_valid, s_wn - new_bias, NEG)
    m = jnp.maximum(jnp.max(s_w, axis=-1, keepdims=True), jnp.max(s_wn, axis=-1, keepdims=True))
    e_w = jnp.where(valid_w, jnp.exp(s_w - m), 0.0)
    e_wn = jnp.where(new_valid, jnp.exp(s_wn - m), 0.0)
    acc = (lax.dot_general(e_w.astype(jnp.bfloat16), win_ref[0, GD:2 * GD, :].astype(jnp.bfloat16), nt,
                           preferred_element_type=jnp.float32)
           + jnp.dot(e_wn.astype(jnp.bfloat16), new_ref[0, 5], preferred_element_type=jnp.float32))
    l = jnp.sum(e_w, axis=-1, keepdims=True) + jnp.sum(e_wn, axis=-1, keepdims=True)
    o_win = _group_diag(acc, rpg) / jnp.maximum(l, 1e-30)

    gates = _sigmoid(ng_ref[0])
    o_ref[0] = (gates[:, 0:1] * o_cmp + gates[:, 1:2] * o_slc + gates[:, 2:3] * o_win).astype(o_ref.dtype)


def _nsa_sample(page_table, cache_t, q_bd, kcvc, new_kv, win_t, ng_r, n_cmp, t_new):
    n_seq, n_pages = page_table.shape
    past_len = n_pages * PAGE_SIZE
    n_ch = kcvc.shape[2]
    rows = q_bd.shape[1]
    rpg = rows // N_KV
    nsb = -(-(past_len + t_new) // SEL_BLOCK)
    n_lanes = -(-nsb // LANES) * LANES
    st = np.arange(n_ch) * CMP_STRIDE
    bs = np.arange(n_lanes) * SEL_BLOCK
    overlap = ((st[:, None] <= bs[None, :] + SEL_BLOCK - 1) & (st[:, None] + CMP_LEN - 1 >= bs[None, :])
               & (np.arange(n_ch)[:, None] < n_cmp) & (np.arange(n_lanes)[None, :] < nsb)).astype(np.float32)
    r = np.arange(rows)
    same_q = ((r[:, None] // rpg == r[None, :] // rpg) & (r[:, None] % t_new == r[None, :] % t_new)
              ).astype(np.float32)
    slopes = jnp.repeat(_alibi_slopes().reshape(N_HEADS), t_new).reshape(rows, 1)
    s1, s2, s3 = _split3_bf16(slopes)
    bias_q = jnp.pad(jnp.concatenate([64.0 * s1, 64.0 * s2, 64.0 * s3, s1, s2, s3], axis=1),
                     ((0, 0), (0, BIAS_POS_ROWS - 6)))
    bias_k = jnp.asarray(_sample_bias_rows(n_lanes, past_len), jnp.bfloat16)
    w_buf = win_t.shape[2]
    return pl.pallas_call(
        functools.partial(_nsa_sample_body, n_cmp=n_cmp, t_new=t_new),
        grid_spec=pltpu.PrefetchScalarGridSpec(
            num_scalar_prefetch=1, grid=(n_seq,),
            in_specs=[pl.BlockSpec(memory_space=pl.ANY),
                      pl.BlockSpec((1, rows, GD), lambda b, pt: (b, 0, 0)),
                      pl.BlockSpec((1, 2, n_ch, GD), lambda b, pt: (b, 0, 0, 0)),
                      pl.BlockSpec((1, 6, new_kv.shape[2], GD), lambda b, pt: (b, 0, 0, 0)),
                      pl.BlockSpec((1, HALF_ROWS, w_buf), lambda b, pt: (b, 0, 0)),
                      pl.BlockSpec((1, rows, 3), lambda b, pt: (b, 0, 0)),
                      pl.BlockSpec((n_lanes, n_ch), lambda b, pt: (0, 0)),
                      pl.BlockSpec((rows, rows), lambda b, pt: (0, 0)),
                      pl.BlockSpec((rows, 1), lambda b, pt: (0, 0)),
                      pl.BlockSpec(bias_q.shape, lambda b, pt: (0, 0)),
                      pl.BlockSpec(bias_k.shape, lambda b, pt: (0, 0), pipeline_mode=pl.Buffered(1))],
            out_specs=pl.BlockSpec((1, rows, HEAD_DIM), lambda b, pt: (b, 0, 0)),
            scratch_shapes=[pltpu.VMEM((2, HALF_ROWS, past_len), jnp.float32),
                            pltpu.SemaphoreType.DMA((2,))]),
        out_shape=jax.ShapeDtypeStruct((n_seq, rows, HEAD_DIM), jnp.bfloat16),
        compiler_params=pltpu.CompilerParams(dimension_semantics=("arbitrary",),
                                             vmem_limit_bytes=VMEM_LIMIT),
        name="nsa_sample",
    )(page_table, cache_t, q_bd, kcvc, new_kv, win_t, ng_r,
      jnp.asarray(overlap.T, jnp.bfloat16), jnp.asarray(same_q, jnp.bfloat16), slopes, bias_q, bias_k)


def _mix_body(a_ref, b_ref, mga_ref, mgb_ref, wpa_ref, wpb_ref, u_ref):
    a = jnp.dot(a_ref[...], wpa_ref[...], preferred_element_type=jnp.float32)
    b = jnp.dot(b_ref[...], wpb_ref[...], preferred_element_type=jnp.float32)
    u_ref[...] = (_sigmoid(mga_ref[...]) * a + _sigmoid(mgb_ref[...]) * b).astype(u_ref.dtype)


def _mix(o_nsa, o_hg, mg, w_pa_bf, w_pb_bf, tm):
    n = o_nsa.shape[0]
    const = lambda i: (0, 0)
    return pl.pallas_call(
        _mix_body,
        grid=(n // tm,),
        in_specs=[pl.BlockSpec((tm, NSA_WIDTH), lambda i: (i, 0)),
                  pl.BlockSpec((tm, HG_WIDTH), lambda i: (i, 0)),
                  pl.BlockSpec((tm, D_MODEL), lambda i: (i, 0)),
                  pl.BlockSpec((tm, D_MODEL), lambda i: (i, 1)),
                  pl.BlockSpec((NSA_WIDTH, D_MODEL), const, pipeline_mode=pl.Buffered(1)),
                  pl.BlockSpec((HG_WIDTH, D_MODEL), const, pipeline_mode=pl.Buffered(1))],
        out_specs=pl.BlockSpec((tm, D_MODEL), lambda i: (i, 0)),
        out_shape=jax.ShapeDtypeStruct((n, D_MODEL), jnp.bfloat16),
        compiler_params=pltpu.CompilerParams(dimension_semantics=("parallel",),
                                             vmem_limit_bytes=VMEM_LIMIT),
        name="tail_mix",
    )(o_nsa, o_hg, mg, mg, w_pa_bf, w_pb_bf)


def _layer_norm(z, g, b):
    mu = jnp.mean(z, axis=-1, keepdims=True)
    zc = z - mu
    var = jnp.mean(zc * zc, axis=-1, keepdims=True)
    return zc * lax.rsqrt(var + LN_EPS) * g + b


CH = D_MODEL // LANES


def _store_chunked(ref, val):
    tm = val.shape[0]
    for k in range(CH):
        ref[pl.ds(k, tm, stride=CH), :] = val[:, k * LANES:(k + 1) * LANES]


def _load_chunked(ref, tm, lead=()):
    return jnp.concatenate([ref[lead + (pl.ds(k, tm, stride=CH), slice(None))] for k in range(CH)], axis=1)


def _ln1_body(u_ref, x_ref, wout_ref, g_ref, b_ref, wr_hi_ref, wr_lo_ref, br_ref, h_ref, lg_ref):
    y = jnp.dot(u_ref[...], wout_ref[...], preferred_element_type=jnp.float32)
    h = _layer_norm(DN_ALPHA * x_ref[...] + y, g_ref[...], b_ref[...])
    _store_chunked(h_ref, h)
    h_hi = h.astype(jnp.bfloat16)
    h_lo = (h - h_hi.astype(jnp.float32)).astype(jnp.bfloat16)
    lg = jnp.dot(h_hi, wr_hi_ref[...], preferred_element_type=jnp.float32)
    lg = lg + jnp.dot(h_lo, wr_hi_ref[...], preferred_element_type=jnp.float32)
    lg = lg + jnp.dot(h_hi, wr_lo_ref[...], preferred_element_type=jnp.float32)
    lg_ref[...] = lg + br_ref[...]


def _ln1(u, x_all, w_out_bf, g, b, wr_hi, wr_lo, br, tm):
    n = u.shape[0]
    const = lambda i: (0, 0)
    return pl.pallas_call(
        _ln1_body,
        grid=(n // tm,),
        in_specs=[pl.BlockSpec((tm, D_MODEL), lambda i: (i, 0)),
                  pl.BlockSpec((tm, D_MODEL), lambda i: (i, 0)),
                  pl.BlockSpec((D_MODEL, D_MODEL), const, pipeline_mode=pl.Buffered(1)),
                  pl.BlockSpec((1, D_MODEL), const),
                  pl.BlockSpec((1, D_MODEL), const),
                  pl.BlockSpec((D_MODEL, LANES), const),
                  pl.BlockSpec((D_MODEL, LANES), const),
                  pl.BlockSpec((1, LANES), const)],
        out_specs=[pl.BlockSpec((tm * CH, LANES), lambda i: (i, 0)),
                   pl.BlockSpec((tm, LANES), lambda i: (i, 0))],
        out_shape=[jax.ShapeDtypeStruct((n * CH, LANES), jnp.float32),
                   jax.ShapeDtypeStruct((n, LANES), jnp.float32)],
        compiler_params=pltpu.CompilerParams(dimension_semantics=("parallel",),
                                             vmem_limit_bytes=VMEM_LIMIT),
        name="tail_ln1",
    )(u, x_all, w_out_bf, g.reshape(1, -1), b.reshape(1, -1), wr_hi, wr_lo, br)


LG0 = N_GROUPS


def _route_body(lg_ref, tri_ref, out_ref, cnt_ref, carry_scr):
    @pl.when(pl.program_id(0) == 0)
    def _():
        carry_scr[...] = jnp.zeros_like(carry_scr)

    lg = lg_ref[...]
    tm = lg.shape[0]
    lane = lax.broadcasted_iota(jnp.int32, lg.shape, 1)
    is_g = lane < N_GROUPS
    gl = jnp.where(is_g, lg, NEG)
    gmax = jnp.max(gl, axis=-1, keepdims=True)
    grp = jnp.min(jnp.where(gl == gmax, lane, LANES), axis=-1, keepdims=True)
    g_w = 1.0 / jnp.sum(jnp.where(is_g, jnp.exp(lg - gmax), 0.0), axis=-1, keepdims=True)
    lo = LG0 + grp * EXP_PER_GROUP
    el = jnp.where((lane >= lo) & (lane < lo + EXP_PER_GROUP), lg, NEG)
    v1 = jnp.max(el, axis=-1, keepdims=True)
    i1 = jnp.min(jnp.where(el == v1, lane, LANES), axis=-1, keepdims=True)
    el2 = jnp.where(lane == i1, NEG, el)
    v2 = jnp.max(el2, axis=-1, keepdims=True)
    i2 = jnp.min(jnp.where(el2 == v2, lane, LANES), axis=-1, keepdims=True)
    e21 = jnp.exp(v2 - v1)
    w1 = g_w / (1.0 + e21)
    w2 = g_w * e21 / (1.0 + e21)
    hit1 = lane == i1
    hit2 = lane == i2
    onehot = jnp.where(hit1 | hit2, 1.0, 0.0)
    incl = jnp.dot(tri_ref[...], onehot.astype(jnp.bfloat16), preferred_element_type=jnp.float32)
    carry = carry_scr[...]
    before = incl - onehot + carry
    r1 = jnp.sum(jnp.where(hit1, before, 0.0), axis=-1, keepdims=True)
    r2 = jnp.sum(jnp.where(hit2, before, 0.0), axis=-1, keepdims=True)
    carry = carry + incl[tm - 1:tm, :]
    carry_scr[...] = carry
    cnt_ref[...] = carry
    out = jnp.where(lane == 0, (i1 - LG0).astype(jnp.float32), 0.0)
    out = jnp.where(lane == 1, (i2 - LG0).astype(jnp.float32), out)
    out = jnp.where(lane == 2, w1, out)
    out = jnp.where(lane == 3, w2, out)
    out = jnp.where(lane == 4, r1, out)
    out = jnp.where(lane == 5, r2, out)
    out_ref[...] = out


def _route(lg, tm):
    n = lg.shape[0]
    tri = (np.arange(tm)[:, None] >= np.arange(tm)[None, :]).astype(np.float32)
    return pl.pallas_call(
        _route_body,
        grid=(n // tm,),
        in_specs=[pl.BlockSpec((tm, LANES), lambda i: (i, 0)),
                  pl.BlockSpec((tm, tm), lambda i: (0, 0))],
        out_specs=[pl.BlockSpec((tm, LANES), lambda i: (i, 0)),
                   pl.BlockSpec((1, LANES), lambda i: (0, 0))],
        out_shape=[jax.ShapeDtypeStruct((n, LANES), jnp.float32),
                   jax.ShapeDtypeStruct((1, LANES), jnp.float32)],
        scratch_shapes=[pltpu.VMEM((1, LANES), jnp.float32)],
        compiler_params=pltpu.CompilerParams(dimension_semantics=("arbitrary",)),
        name="moe_route",
    )(lg, jnp.asarray(tri, jnp.bfloat16))


def _dispatch_body(pos_ref, h_ref, xs_in_ref, xs_ref, sem):
    del xs_in_ref
    tm = h_ref.shape[0] // CH
    base = pl.program_id(0) * tm

    def issue(t, carry):
        src = pl.multiple_of(t * CH, CH)
        for slot in range(2):
            dst = pl.multiple_of(pos_ref[2 * (base + t) + slot] * CH, CH)
            pltpu.make_async_copy(h_ref.at[pl.ds(src, CH)], xs_ref.at[pl.ds(dst, CH)], sem).start()
        return carry

    lax.fori_loop(0, tm, issue, 0)
    for _ in range(2):
        pltpu.make_async_copy(h_ref, xs_ref.at[pl.ds(0, tm * CH)], sem).wait()


DISPATCH_TM = 128


def _dispatch(pos_flat, h_c, n_slots):
    zeros = jnp.zeros((n_slots * CH, LANES), h_c.dtype)
    tm = DISPATCH_TM
    return pl.pallas_call(
        _dispatch_body,
        grid_spec=pltpu.PrefetchScalarGridSpec(
            num_scalar_prefetch=1, grid=(h_c.shape[0] // (tm * CH),),
            in_specs=[pl.BlockSpec((tm * CH, LANES), lambda i, pos: (i, 0)),
                      pl.BlockSpec(memory_space=pl.ANY)],
            out_specs=pl.BlockSpec(memory_space=pl.ANY),
            scratch_shapes=[pltpu.SemaphoreType.DMA(())]),
        out_shape=jax.ShapeDtypeStruct(zeros.shape, h_c.dtype),
        input_output_aliases={2: 0},
        compiler_params=pltpu.CompilerParams(dimension_semantics=("arbitrary",)),
        name="moe_dispatch",
    )(pos_flat, h_c, zeros)


def _ffn_body(te_ref, nu_ref, x_ref, wg_ref, wu_ref, wd_ref, y_ref, wg_bf, wu_bf, wd_bf):
    i = pl.program_id(0)
    prev = te_ref[jnp.maximum(i - 1, 0)]

    @pl.when((i == 0) | (te_ref[i] != prev))
    def _():
        wg_bf[...] = wg_ref[0].astype(jnp.bfloat16)
        wu_bf[...] = wu_ref[0].astype(jnp.bfloat16)
        wd_bf[...] = wd_ref[0].astype(jnp.bfloat16)

    @pl.when(i < nu_ref[0])
    def _():
        x = _load_chunked(x_ref, x_ref.shape[0] // CH).astype(jnp.bfloat16)
        g = jnp.dot(x, wg_bf[...], preferred_element_type=jnp.float32)
        u = jnp.dot(x, wu_bf[...], preferred_element_type=jnp.float32)
        hid = (g * _sigmoid(g) * u).astype(jnp.bfloat16)
        _store_chunked(y_ref, jnp.dot(hid, wd_bf[...], preferred_element_type=jnp.float32))

    @pl.when(i >= nu_ref[0])
    def _():
        y_ref[...] = jnp.zeros_like(y_ref)


def _ffn(tile_expert, n_used, xs_c, w_gate, w_up, w_down, tm):
    d = D_MODEL
    nt = xs_c.shape[0] // (tm * CH)
    return pl.pallas_call(
        _ffn_body,
        grid_spec=pltpu.PrefetchScalarGridSpec(
            num_scalar_prefetch=2, grid=(nt,),
            in_specs=[pl.BlockSpec((tm * CH, LANES), lambda i, te, nu: (jnp.minimum(i, nu[0] - 1), 0)),
                      pl.BlockSpec((1, d, D_EXPERT), lambda i, te, nu: (te[i], 0, 0)),
                      pl.BlockSpec((1, d, D_EXPERT), lambda i, te, nu: (te[i], 0, 0)),
                      pl.BlockSpec((1, D_EXPERT, d), lambda i, te, nu: (te[i], 0, 0))],
            out_specs=pl.BlockSpec((tm * CH, LANES), lambda i, te, nu: (i, 0)),
            scratch_shapes=[pltpu.VMEM((d, D_EXPERT), jnp.bfloat16),
                            pltpu.VMEM((d, D_EXPERT), jnp.bfloat16),
                            pltpu.VMEM((D_EXPERT, d), jnp.bfloat16)]),
        out_shape=jax.ShapeDtypeStruct(xs_c.shape, jnp.float32),
        compiler_params=pltpu.CompilerParams(dimension_semantics=("arbitrary",),
                                             vmem_limit_bytes=VMEM_LIMIT),
        name="moe_ffn",
    )(tile_expert, n_used, xs_c, w_gate, w_up, w_down)


def _combine_body(pos_ref, h_ref, rw_ref, g_ref, b_ref, y_ref, out_a_ref, out_b_ref, buf, sem, *, tiles_a):
    tm = out_a_ref.shape[0]
    i = pl.program_id(0)
    cur = i % 2

    def gather(tile, parity):
        def one(t, carry):
            for slot in range(2):
                src = pl.multiple_of(pos_ref[2 * (tile * tm + t) + slot] * CH, CH)
                dst = pl.multiple_of(t * CH, CH)
                pltpu.make_async_copy(y_ref.at[pl.ds(src, CH)], buf.at[parity, slot, pl.ds(dst, CH)],
                                      sem.at[parity, slot]).start()
            return carry
        lax.fori_loop(0, tm, one, 0)

    @pl.when(i == 0)
    def _():
        gather(0, 0)

    @pl.when(i + 1 < pl.num_programs(0))
    def _():
        gather(i + 1, 1 - cur)

    for slot in range(2):
        pltpu.make_async_copy(y_ref.at[pl.ds(0, tm * CH)], buf.at[cur, slot], sem.at[cur, slot]).wait()
    rw = rw_ref[...]
    z = rw[:, 2:3] * _load_chunked(buf, tm, (cur, 0)) + rw[:, 3:4] * _load_chunked(buf, tm, (cur, 1))
    res = _layer_norm(DN_ALPHA * _load_chunked(h_ref, tm) + z, g_ref[...], b_ref[...])

    @pl.when(i < tiles_a)
    def _():
        out_a_ref[...] = res

    @pl.when(i >= tiles_a)
    def _():
        out_b_ref[...] = res


def _combine(pos_flat, h_c, route_out, g, b, y_c, tm, n_a):
    n, d = h_c.shape[0] // CH, D_MODEL
    assert n_a % tm == 0 and (n - n_a) % tm == 0 and 0 < n_a < n
    tiles_a = n_a // tm
    return pl.pallas_call(
        functools.partial(_combine_body, tiles_a=tiles_a),
        grid_spec=pltpu.PrefetchScalarGridSpec(
            num_scalar_prefetch=1, grid=(n // tm,),
            in_specs=[pl.BlockSpec((tm * CH, LANES), lambda i, pos: (i, 0)),
                      pl.BlockSpec((tm, LANES), lambda i, pos: (i, 0)),
                      pl.BlockSpec((1, d), lambda i, pos: (0, 0)),
                      pl.BlockSpec((1, d), lambda i, pos: (0, 0)),
                      pl.BlockSpec(memory_space=pl.ANY)],
            out_specs=[pl.BlockSpec((tm, d), lambda i, pos: (jnp.minimum(i, tiles_a - 1), 0)),
                       pl.BlockSpec((tm, d), lambda i, pos: (jnp.maximum(i - tiles_a, 0), 0))],
            scratch_shapes=[pltpu.VMEM((2, 2, tm * CH, LANES), jnp.float32),
                            pltpu.SemaphoreType.DMA((2, 2))]),
        out_shape=[jax.ShapeDtypeStruct((n_a, d), jnp.float32),
                   jax.ShapeDtypeStruct((n - n_a, d), jnp.float32)],
        compiler_params=pltpu.CompilerParams(dimension_semantics=("arbitrary",),
                                             vmem_limit_bytes=VMEM_LIMIT),
        name="moe_combine",
    )(pos_flat, h_c, route_out, g.reshape(1, -1), b.reshape(1, -1), y_c)


FFN_TM = 256
PROJ_TM = 1664
PROJ_TN = 512
TAIL_TM = 320


def _moe_and_norm(h_c, lg, w_gate, w_up, w_down, ln2_g, ln2_b, n_first):
    n = lg.shape[0]
    route_out, cnt = _route(lg, 640)
    eid = route_out[:, 0:2].astype(jnp.int32)
    rank = route_out[:, 4:6].astype(jnp.int32)
    counts = cnt[0, LG0:LG0 + N_EXPERTS].astype(jnp.int32)
    tiles_per = (counts + FFN_TM - 1) // FFN_TM
    tile_end = jnp.cumsum(tiles_per)
    row_start = (tile_end - tiles_per) * FFN_TM
    pos_flat = (row_start[eid] + rank).reshape(-1)
    nt = (2 * n) // FFN_TM + N_EXPERTS
    n_used = tile_end[-1]
    tile_ids = jnp.minimum(jnp.arange(nt, dtype=jnp.int32), n_used - 1)
    tile_expert = jnp.sum((tile_end[None, :] <= tile_ids[:, None]).astype(jnp.int32), axis=1)
    xs_c = _dispatch(pos_flat, h_c, nt * FFN_TM)
    y_c = _ffn(tile_expert, n_used.reshape(1).astype(jnp.int32), xs_c, w_gate, w_up, w_down, FFN_TM)
    return _combine(pos_flat, h_c, route_out, ln2_g, ln2_b, y_c, DISPATCH_TM, n_first)


def kernel(x_prompt, x_sample, cache_kv, cache_win, state_hgrn, page_table, w_in, b_in, w_cmp1, w_cmp2, cmp_pe,
           hgrn_gamma, hgrn_norm, w_pa, w_pb, w_out, ln1_g, ln1_b, w_rg, b_rg, w_re, b_re, w_gate, w_up, w_down,
           ln2_g, ln2_b):
    n_p = x_prompt.shape[0] * x_prompt.shape[1]
    n_s = x_sample.shape[0] * x_sample.shape[1]
    x_all = jnp.concatenate([x_prompt.reshape(n_p, D_MODEL), x_sample.reshape(n_s, D_MODEL)], axis=0)
    x_bf = x_all.astype(jnp.bfloat16)
    bsz, seq = x_prompt.shape[:2]
    n_seq, t_new = x_sample.shape[:2]
    n_pages = page_table.shape[1]
    past_len = n_pages * PAGE_SIZE
    w = w_in[0]
    b = b_in[0]

    def seg(lo, hi, outs, tn, name, pad_to=None):
        ws, bs = w[:, lo:hi], b[lo:hi]
        if pad_to is not None:
            ws = jnp.pad(ws, ((0, 0), (0, pad_to - (hi - lo))))
            bs = jnp.pad(bs, (0, pad_to - (hi - lo)))
        return _proj(x_bf, ws.astype(jnp.bfloat16), bs, outs, PROJ_TM, tn, name)

    q_h, = seg(OFF_Q, OFF_KV, [(jnp.bfloat16, True)], PROJ_TN, "proj_q")
    kv32, kv_h = seg(OFF_KV, OFF_NG, [(jnp.float32, False), (jnp.bfloat16, True)], PROJ_TN, "proj_kv")
    ng, = seg(OFF_NG, OFF_H4, [(jnp.float32, False)], LANES, "proj_ng", pad_to=LANES)
    h4, = seg(OFF_H4, OFF_MG, [(jnp.float32, False)], PROJ_TN, "proj_h4")
    mg, = seg(OFF_MG, PROJ_COLS, [(jnp.float32, False)], PROJ_TN, "proj_mg")

    kv_p = kv32[:n_p].reshape(1, bsz, seq, 6, N_KV, HEAD_DIM)
    kv_s = kv32[n_p:].reshape(1, n_seq, t_new, 6, N_KV, HEAD_DIM)
    new_kv_prompt = kv_p[:, :, :, :KV_SLOTS]
    new_kv_sample = kv_s[:, :, :, :KV_SLOTS]
    new_win_prompt = kv_p[:, :, seq - min(WINDOW, seq):, 4:6]
    win_all = jnp.concatenate([cache_win, kv_s[:, :, :, 4:6].astype(cache_win.dtype)], axis=2)
    new_win_sample = win_all[:, :, win_all.shape[2] - min(WINDOW, win_all.shape[2]):]

    w1p = _cmp_w1_pairs(w_cmp1[0])
    n_cmp_p = (seq - CMP_LEN) // CMP_STRIDE + 1
    uv_p = _cmp_uv_rows(kv32, n_p, 2048, w1p)
    kcvc_p = _cmp_finish(uv_p, bsz, seq // CMP_STRIDE, w_cmp1[0], w_cmp2[0], cmp_pe[0], "cmp_finish_prompt")
    ng_t = ng[:n_p, :NG_COLS].reshape(bsz, seq, 3, N_KV, HPG).transpose(0, 3, 1, 2, 4).reshape(bsz, N_KV, seq, 3 * HPG)
    o_nsa_p = _nsa_prompt(q_h, kcvc_p, kv_h, ng_t, n_cmp_p)

    cache_t = cache_kv[0].transpose(0, 2, 3, 4, 1).reshape(cache_kv.shape[1], 2 * HALF_ROWS, PAGE_SIZE)
    win_t = cache_win[0].transpose(0, 2, 3, 4, 1).reshape(n_seq, HALF_ROWS, cache_win.shape[2])
    n_cmp_s = (past_len + t_new - CMP_LEN) // CMP_STRIDE + 1
    kcvc_s = _cmp_sample(page_table, cache_t, w1p, w_cmp1[0], w_cmp2[0], cmp_pe[0])
    rows_s = N_HEADS * t_new
    q_s = q_h[:, n_p:].reshape(N_KV, HPG, n_seq, t_new, HEAD_DIM).transpose(2, 0, 1, 3, 4).reshape(
        n_seq, N_KV, HPG * t_new, HEAD_DIM)
    q_bd = jnp.einsum('sgrd,gk->sgrkd', q_s, jnp.eye(N_KV, dtype=q_s.dtype)).reshape(n_seq, rows_s, GD)
    new_kv = kv_h[:, n_p:].reshape(6, N_KV, n_seq, t_new, HEAD_DIM).transpose(2, 0, 3, 1, 4).reshape(
        n_seq, 6, t_new, GD)
    new_kv = jnp.pad(new_kv, ((0, 0), (0, 0), (0, 8 - t_new), (0, 0)))
    ng_r = ng[n_p:, :NG_COLS].reshape(n_seq, t_new, 3, N_KV, HPG).transpose(0, 3, 4, 1, 2).reshape(
        n_seq, rows_s, 3)
    o_nsa_s = _nsa_sample(page_table, cache_t, q_bd, kcvc_s, new_kv, win_t, ng_r, n_cmp_s, t_new)
    o_nsa_s = o_nsa_s.reshape(n_seq, N_KV, HPG, t_new, HEAD_DIM).transpose(0, 3, 1, 2, 4).reshape(n_s, NSA_WIDTH)

    lower = jnp.cumsum(jax.nn.softmax(hgrn_gamma.astype(jnp.float32), axis=0), axis=0)
    lb = lower[0].reshape(HG_HEADS, HG_DK)
    zero_state = jnp.zeros((bsz, HG_HEADS, HG_DV, HG_DK), jnp.float32)
    o_hg_p, st_p = _hgrn(h4, 0, bsz, seq, lb, hgrn_norm[0], zero_state, HG_CHUNK, HG_CHUNK, 512, HG_HEADS,
                         "hgrn_prompt")
    h4_s = jnp.pad(h4[n_p:].reshape(n_seq, t_new, -1), ((0, 0), (0, 8 - t_new), (0, 0))).reshape(n_seq * 8, -1)
    o_hg_s, st_s = _hgrn(h4_s, 0, n_seq, 8, lb, hgrn_norm[0], state_hgrn[0].transpose(0, 1, 3, 2),
                         8, t_new, 8, HG_HEADS, "hgrn_sample")
    o_hg_s = o_hg_s.reshape(n_seq, 8, HG_WIDTH)[:, :t_new].reshape(n_s, HG_WIDTH)
    new_state_prompt = st_p.transpose(0, 1, 3, 2)[None].astype(x_prompt.dtype)
    new_state_sample = st_s.transpose(0, 1, 3, 2)[None].astype(state_hgrn.dtype)

    o_nsa = jnp.concatenate([o_nsa_p, o_nsa_s], axis=0)
    o_hg = jnp.concatenate([o_hg_p, o_hg_s], axis=0)
    u = _mix(o_nsa, o_hg, mg, w_pa[0].astype(jnp.bfloat16), w_pb[0].astype(jnp.bfloat16), TAIL_TM)
    wr = jnp.zeros((D_MODEL, LANES), jnp.float32).at[:, :N_GROUPS].set(w_rg[0]).at[:, LG0:LG0 + N_EXPERTS].set(w_re[0])
    br = jnp.zeros((1, LANES), jnp.float32).at[0, :N_GROUPS].set(b_rg[0]).at[0, LG0:LG0 + N_EXPERTS].set(b_re[0])
    wr_hi, wr_lo = _split_bf16(wr)
    h, lg = _ln1(u, x_all, w_out[0].astype(jnp.bfloat16), ln1_g[0], ln1_b[0], wr_hi, wr_lo, br, TAIL_TM)
    out_p, out_s = _moe_and_norm(h, lg, w_gate[0], w_up[0], w_down[0], ln2_g[0], ln2_b[0], n_p)
    y_prompt = out_p.reshape(bsz, seq, D_MODEL)
    y_sample = out_s.reshape(n_seq, t_new, D_MODEL)
    return (y_prompt, y_sample, new_kv_prompt, new_kv_sample, new_win_prompt, new_win_sample,
            new_state_prompt, new_state_sample)
```

```python
import functools

import numpy as np
import jax
import jax.numpy as jnp
from jax import lax
from jax.experimental import pallas as pl
from jax.experimental.pallas import tpu as pltpu

D_MODEL = 2048
N_HEADS = 16
N_KV = 4
HPG = N_HEADS // N_KV
HEAD_DIM = 64
NSA_WIDTH = N_HEADS * HEAD_DIM
CMP_LEN = 32
CMP_STRIDE = 16
CMP_HID = 128
SEL_BLOCK = 64
N_SEL = 16
WINDOW = 512
QBLK = 128
KV_SLOTS = 4
PAGE_SIZE = 128
HG_HEADS = 8
HG_DK = 128
HG_DV = 128
HG_WIDTH = HG_HEADS * HG_DV
HG_CHUNK = 32
N_GROUPS = 4
EXP_PER_GROUP = 8
N_EXPERTS = N_GROUPS * EXP_PER_GROUP
D_EXPERT = 512
DEPTH = 1
DN_ALPHA = (2.0 * DEPTH) ** 0.25
LN_EPS = 1e-5
SCALE = HEAD_DIM ** -0.5
NEG = -1e30
BIG = 1e30

LANES = 128
KV_COLS = 6 * N_KV * HEAD_DIM
NG_COLS = 3 * N_HEADS
OFF_Q = 0
OFF_KV = NSA_WIDTH
OFF_NG = OFF_KV + KV_COLS
OFF_H4 = OFF_NG + NG_COLS
OFF_MG = OFF_H4 + 2 * HG_HEADS * HG_DK + 2 * HG_WIDTH
PROJ_COLS = OFF_MG + 2 * D_MODEL

VMEM_LIMIT = 56 * 1024 * 1024


def _sigmoid(x):
    return 1.0 / (1.0 + jnp.exp(-x))


def _proj_body(x_ref, w_ref, b_ref, *out_refs):
    acc = jnp.dot(x_ref[...], w_ref[...], preferred_element_type=jnp.float32) + b_ref[...]
    for o_ref in out_refs:
        if len(o_ref.shape) == 3:
            for k in range(o_ref.shape[0]):
                o_ref[k] = acc[:, k * HEAD_DIM:(k + 1) * HEAD_DIM].astype(o_ref.dtype)
        else:
            o_ref[...] = acc.astype(o_ref.dtype)


def _proj(x_bf, w_bf, b, outs, tm, tn, name):
    m, k = x_bf.shape
    n = w_bf.shape[1]
    assert m % tm == 0 and n % tn == 0
    out_specs, out_shape = [], []
    for dt, per_head in outs:
        if per_head:
            out_specs.append(pl.BlockSpec((tn // HEAD_DIM, tm, HEAD_DIM), lambda i, j: (j, i, 0)))
            out_shape.append(jax.ShapeDtypeStruct((n // HEAD_DIM, m, HEAD_DIM), dt))
        else:
            out_specs.append(pl.BlockSpec((tm, tn), lambda i, j: (i, j)))
            out_shape.append(jax.ShapeDtypeStruct((m, n), dt))
    return pl.pallas_call(
        _proj_body,
        grid=(m // tm, n // tn),
        in_specs=[pl.BlockSpec((tm, k), lambda i, j: (i, 0)),
                  pl.BlockSpec((k, tn), lambda i, j: (0, j)),
                  pl.BlockSpec((1, tn), lambda i, j: (0, j))],
        out_specs=out_specs,
        out_shape=out_shape,
        compiler_params=pltpu.CompilerParams(dimension_semantics=("parallel", "parallel"),
                                             vmem_limit_bytes=VMEM_LIMIT),
        name=name,
    )(x_bf, w_bf, b.reshape(1, -1))


def _hgrn_body(hq_ref, hf_ref, hi_ref, hg_ref, lb_ref, nrm_ref, s0_ref, o_ref, sfin_ref, st_scr,
               *, chunk, n_valid, n_chunks):
    @pl.when(pl.program_id(2) == 0)
    def _():
        st_scr[...] = s0_ref[0]

    for h in range(st_scr.shape[0]):
        cols = slice(h * HG_DK, (h + 1) * HG_DK)
        o, st = _hgrn_head(hq_ref[:, cols], hf_ref[:, cols], hi_ref[:, cols], hg_ref[:, cols],
                           lb_ref[h], nrm_ref[h], st_scr[h], chunk, n_valid, n_chunks)
        st_scr[h] = st
        sfin_ref[0, h] = st
        o_ref[:, cols] = o.astype(o_ref.dtype)


def _hgrn_head(hq, hf, v, hg, lb, nrm, st, chunk, n_valid, n_chunks):
    rows = chunk * n_chunks
    q = hq * _sigmoid(hq)
    f = lb + (1.0 - lb) * _sigmoid(hf)
    k = 1.0 - f
    lc = jnp.log(f)
    row_in_chunk = lax.broadcasted_iota(jnp.int32, (rows, HG_DK), 0) % chunk
    if n_valid < chunk:
        live = row_in_chunk < n_valid
        q = jnp.where(live, q, 0.0)
        k = jnp.where(live, k, 0.0)
        v = jnp.where(live, v, 0.0)
        lc = jnp.where(live, lc, 0.0)
    bc = lc
    step = 1
    while step < chunk:
        bc = bc + jnp.where(row_in_chunk >= step, pltpu.roll(bc, step, axis=0), 0.0)
        step *= 2
    bc3 = bc.reshape(n_chunks, chunk, HG_DK)
    bl3 = bc3[:, chunk - 1:chunk, :]
    q3 = q.reshape(n_chunks, chunk, HG_DK)
    k3 = k.reshape(n_chunks, chunk, HG_DK)
    v3 = v.reshape(n_chunks, chunk, HG_DV).astype(jnp.bfloat16)
    qe3 = (q3 * jnp.exp(bc3)).astype(jnp.bfloat16)
    ke3 = (k3 * jnp.exp(-bc3)).astype(jnp.bfloat16)
    kd3 = (k3 * jnp.exp(bl3 - bc3)).astype(jnp.bfloat16)
    dec3 = jnp.exp(bl3)
    att = jnp.einsum('ctd,csd->cts', qe3, ke3, preferred_element_type=jnp.float32)
    tri = (lax.broadcasted_iota(jnp.int32, (chunk, chunk), 0)
           >= lax.broadcasted_iota(jnp.int32, (chunk, chunk), 1))
    att = jnp.where(tri[None], att, 0.0).astype(jnp.bfloat16)
    o_intra = jnp.einsum('cts,cse->cte', att, v3, preferred_element_type=jnp.float32)

    outs = []
    for c in range(n_chunks):
        o_c = lax.dot_general(qe3[c], st.astype(jnp.bfloat16), (((1,), (1,)), ((), ())),
                              preferred_element_type=jnp.float32)
        outs.append(o_c + o_intra[c])
        upd = lax.dot_general(v3[c], kd3[c], (((0,), (0,)), ((), ())),
                              preferred_element_type=jnp.float32)
        st = st * dec3[c] + upd
    o = jnp.concatenate(outs, axis=0) if n_chunks > 1 else outs[0]
    o = o * lax.rsqrt(jnp.mean(o * o, axis=-1, keepdims=True) + LN_EPS) * nrm
    return o * (hg * _sigmoid(hg)), st


def _hgrn(h4, row0, n_seq, t_seq, lb, nrm, s0_t, chunk, n_valid, block_rows, heads_per_step, name):
    assert t_seq % block_rows == 0 and block_rows % chunk == 0 and row0 % block_rows == 0
    nb = t_seq // block_rows
    rb0 = row0 // block_rows
    nh = heads_per_step
    hb = HG_HEADS // nh

    def col_spec(seg):
        return pl.BlockSpec((block_rows, nh * HG_DK),
                            lambda b, h, i, seg=seg: (rb0 + b * nb + i, seg * hb + h))

    body = functools.partial(_hgrn_body, chunk=chunk, n_valid=n_valid, n_chunks=block_rows // chunk)
    return pl.pallas_call(
        body,
        grid=(n_seq, hb, nb),
        in_specs=[col_spec(0), col_spec(1), col_spec(2), col_spec(3),
                  pl.BlockSpec((nh, 1, HG_DK), lambda b, h, i: (h, 0, 0)),
                  pl.BlockSpec((nh, 1, HG_DV), lambda b, h, i: (h, 0, 0)),
                  pl.BlockSpec((1, nh, HG_DV, HG_DK), lambda b, h, i: (b, h, 0, 0))],
        out_specs=[pl.BlockSpec((block_rows, nh * HG_DV), lambda b, h, i: (b * nb + i, h)),
                   pl.BlockSpec((1, nh, HG_DV, HG_DK), lambda b, h, i: (b, h, 0, 0))],
        out_shape=[jax.ShapeDtypeStruct((n_seq * t_seq, HG_WIDTH), jnp.bfloat16),
                   jax.ShapeDtypeStruct((n_seq, HG_HEADS, HG_DV, HG_DK), jnp.float32)],
        scratch_shapes=[pltpu.VMEM((nh, HG_DV, HG_DK), jnp.float32)],
        compiler_params=pltpu.CompilerParams(
            dimension_semantics=("parallel", "parallel", "arbitrary"), vmem_limit_bytes=VMEM_LIMIT),
        name=name,
    )(h4, h4, h4, h4, lb.reshape(HG_HEADS, 1, HG_DK), nrm.reshape(HG_HEADS, 1, HG_DV), s0_t)


UV_COLS = 2 * N_KV * 2 * CMP_HID


def _uv_taps(tap, w_ref, c, n):
    acc = jnp.zeros((n, 4 * CMP_HID), jnp.float32)
    for p in range(CMP_STRIDE // 2):
        x2 = jnp.concatenate([tap(2 * p), tap(2 * p + 1)], axis=-1)
        acc = acc + jnp.dot(x2, w_ref[c, p], preferred_element_type=jnp.float32)
    return acc


def _cmp_uv_body(x0_ref, x1_ref, x2_ref, x3_ref, w_ref, uv_ref):
    n = uv_ref.shape[0]
    for cgp, x_ref in enumerate((x0_ref, x1_ref, x2_ref, x3_ref)):
        tap = lambda j, x_ref=x_ref: x_ref[pl.ds(j, n, stride=CMP_STRIDE), :].astype(jnp.bfloat16)
        uv_ref[:, cgp * 512:(cgp + 1) * 512] = _uv_taps(tap, w_ref, cgp // 2, n)


def _cmp_w1_pairs(w_cmp1_l):
    w = jnp.concatenate([w_cmp1_l[:, :CMP_STRIDE], w_cmp1_l[:, CMP_STRIDE:]], axis=-1)
    z = jnp.zeros_like(w)
    top = jnp.concatenate([w, z], axis=-1)
    bot = jnp.concatenate([z, w], axis=-1)
    per_tap = jnp.concatenate([top, bot], axis=-2)
    return per_tap.reshape(2, CMP_STRIDE // 2, 2 * LANES, 4 * CMP_HID).astype(jnp.bfloat16)


def _cmp_uv_rows(kv, n_rows, rows_per_step, w1p):
    n = rows_per_step // CMP_STRIDE
    return pl.pallas_call(
        _cmp_uv_body,
        grid=(n_rows // rows_per_step,),
        in_specs=[pl.BlockSpec((rows_per_step, LANES), lambda i, cb=cb: (i, cb)) for cb in range(4)]
        + [pl.BlockSpec(w1p.shape, lambda i: (0, 0, 0, 0))],
        out_specs=pl.BlockSpec((n, UV_COLS), lambda i: (i, 0)),
        out_shape=jax.ShapeDtypeStruct((n_rows // CMP_STRIDE, UV_COLS), jnp.float32),
        compiler_params=pltpu.CompilerParams(dimension_semantics=("parallel",),
                                             vmem_limit_bytes=VMEM_LIMIT),
        name="cmp_uv_prompt",
    )(kv, kv, kv, kv, w1p)


def _gelu_tanh(x):
    return 0.5 * x * (1.0 + jnp.tanh(0.7978845608028654 * (x + 0.044715 * x * x * x)))


def _cmp_finish_body(uv_ref, pe_ref, w1_ref, w2_ref, out_ref):
    _cmp_finish_rows(lambda cols: uv_ref[0, :, cols], uv_ref.shape[1], pe_ref, w1_ref, w2_ref, out_ref)


def _cmp_finish_rows(uv_cols, n_ch, pe_ref, w1_ref, w2_ref, out_ref):
    for c in range(2):
        pe_term = jnp.dot(pe_ref[c], w1_ref[c], preferred_element_type=jnp.float32)[0:1, :]
        for g in range(N_KV):
            base = (c * N_KV + g) * 2 * CMP_HID
            u = uv_cols(slice(base, base + CMP_HID))
            v = uv_cols(slice(base + CMP_HID, base + 2 * CMP_HID))
            pre = u + pltpu.roll(v, n_ch - 1, axis=0) + pe_term
            hid = _gelu_tanh(pre).astype(jnp.bfloat16)
            res = jnp.dot(hid, w2_ref[c], preferred_element_type=jnp.float32).astype(out_ref.dtype)
            if out_ref.shape[1] == 2:
                out_ref[0, c, :, g * HEAD_DIM:(g + 1) * HEAD_DIM] = res
            else:
                out_ref[0, c * N_KV + g] = res


def _cmp_finish(uv, n_seq, n_ch, w_cmp1_l, w_cmp2_l, cmp_pe_l, name):
    pe = jnp.zeros((2, 8, CMP_LEN * HEAD_DIM), jnp.float32).at[:, 0].set(cmp_pe_l.reshape(2, -1))
    out_dims = (2 * N_KV, n_ch, HEAD_DIM)
    return pl.pallas_call(
        _cmp_finish_body,
        grid=(n_seq,),
        in_specs=[pl.BlockSpec((1, n_ch, UV_COLS), lambda b: (b, 0, 0)),
                  pl.BlockSpec((2, 8, CMP_LEN * HEAD_DIM), lambda b: (0, 0, 0)),
                  pl.BlockSpec((2, CMP_LEN * HEAD_DIM, CMP_HID), lambda b: (0, 0, 0)),
                  pl.BlockSpec((2, CMP_HID, HEAD_DIM), lambda b: (0, 0, 0))],
        out_specs=pl.BlockSpec((1,) + out_dims, lambda b: (b, 0, 0, 0)),
        out_shape=jax.ShapeDtypeStruct((n_seq,) + out_dims, jnp.bfloat16),
        compiler_params=pltpu.CompilerParams(dimension_semantics=("parallel",),
                                             vmem_limit_bytes=VMEM_LIMIT),
        name=name,
    )(uv.reshape(n_seq, n_ch, UV_COLS), pe.astype(jnp.bfloat16),
      w_cmp1_l.reshape(2, CMP_LEN * HEAD_DIM, CMP_HID).astype(jnp.bfloat16),
      w_cmp2_l.astype(jnp.bfloat16))


SLC_TK = 512
WIN_KEYS = WINDOW + QBLK


def _masked_softmax(s, valid):
    s = jnp.where(valid, s, NEG)
    m = jnp.max(s, axis=-1, keepdims=True)
    e = jnp.where(valid, jnp.exp(s - m), 0.0)
    return e / jnp.maximum(jnp.sum(e, axis=-1, keepdims=True), 1e-30)


def _split_bf16(x):
    hi = x.astype(jnp.bfloat16)
    return hi, (x - hi.astype(jnp.float32)).astype(jnp.bfloat16)


SUBLANES = 8


def _top_blocks_t(imp_t):
    nb, nq = imp_t.shape
    groups = [imp_t[SUBLANES * v:SUBLANES * (v + 1), :] for v in range(nb // SUBLANES)]
    sub = lax.broadcasted_iota(jnp.int32, (SUBLANES, nq), 0)
    beaten_by = [jnp.zeros((SUBLANES, nq), jnp.float32) for _ in groups]
    for i in range(nb):
        row = jnp.broadcast_to(imp_t[i:i + 1, :], (SUBLANES, nq))
        for v, gv in enumerate(groups):
            if v > i // SUBLANES:
                beats = row >= gv
            elif v < i // SUBLANES:
                beats = row > gv
            else:
                beats = (row > gv) | ((row == gv) & (sub > i % SUBLANES))
            beaten_by[v] = beaten_by[v] + jnp.where(beats, 1.0, 0.0)
    return jnp.concatenate([jnp.where(c < N_SEL, 1.0, 0.0) for c in beaten_by], axis=0)


def _split3_bf16(x):
    x1 = x.astype(jnp.bfloat16)
    r = x - x1.astype(jnp.float32)
    x2 = r.astype(jnp.bfloat16)
    return x1, x2, (r - x2.astype(jnp.float32)).astype(jnp.bfloat16)


XK = 2 * LANES


def _slc_key_columns(t):
    kpos = np.arange(t)
    cols = np.zeros((t, XK - HEAD_DIM), np.float32)
    cols[kpos, kpos // SEL_BLOCK] = 1.0
    cols[:, 64:67] = (kpos // 64)[:, None]
    cols[:, 67:70] = (kpos % 64)[:, None]
    cols[:, 70:73] = 1.0
    return cols


def _slc_query_columns():
    s1, s2, s3 = _split3_bf16(_alibi_slopes().reshape(N_KV, HPG, 1))
    cols = jnp.concatenate([64.0 * s1, 64.0 * s2, 64.0 * s3, s1, s2, s3], axis=-1)
    cols = jnp.pad(cols, ((0, 0), (0, 0), (0, LANES - 6)))
    return jnp.broadcast_to(cols[:, :, None, :], (N_KV, HPG, QBLK, LANES)).reshape(N_KV, HPG * QBLK, LANES)


def _nsa_prompt_body(q_ref, kc_ref, vc_ref, kx_ref, vs_ref, kw_ref, vw_ref, ng_ref, ov_ref, sx_ref, cb_ref, sl_ref,
                     o_ref, *, n_cmp):
    i = pl.program_id(2)
    nq = QBLK
    rows = HPG * nq
    q2 = (q_ref[...] * SCALE).reshape(rows, HEAD_DIM)
    slopes = sl_ref[0]
    qpos = i * nq + lax.broadcasted_iota(jnp.int32, (nq, 1), 0)
    nt = (((1,), (1,)), ((), ()))

    n_ch = kc_ref.shape[2]
    s = lax.dot_general(q2, kc_ref[0, 0], nt, preferred_element_type=jnp.float32).reshape(HPG, nq, n_ch)
    cidx = lax.broadcasted_iota(jnp.int32, (1, n_ch), 1)
    dist_c = qpos - (cidx * CMP_STRIDE + (CMP_LEN - 1))
    valid_c = ((dist_c >= 0) & (cidx < n_cmp))[None]
    p = _masked_softmax(s - slopes * dist_c.astype(jnp.float32)[None], valid_c)
    o_cmp = jnp.dot(p.reshape(rows, n_ch).astype(jnp.bfloat16), vc_ref[0, 0],
                    preferred_element_type=jnp.float32).reshape(HPG, nq, HEAD_DIM)

    w0 = pl.multiple_of(jnp.maximum(i - WINDOW // QBLK, 0) * nq, nq)
    s = lax.dot_general(q2, kw_ref[0, pl.ds(w0, WIN_KEYS), :], nt,
                        preferred_element_type=jnp.float32).reshape(HPG, nq, WIN_KEYS)
    dist = qpos - (w0 + lax.broadcasted_iota(jnp.int32, (1, WIN_KEYS), 1))
    valid = ((dist >= 0) & (dist < WINDOW))[None]
    pw = _masked_softmax(s - slopes * dist.astype(jnp.float32)[None], valid)
    o_win = jnp.dot(pw.reshape(rows, WIN_KEYS).astype(jnp.bfloat16), vw_ref[0, pl.ds(w0, WIN_KEYS), :],
                    preferred_element_type=jnp.float32).reshape(HPG, nq, HEAD_DIM)

    p_hi, p_lo = _split_bf16(p[0] + p[1] + p[2] + p[3])
    imp_t = (lax.dot_general(ov_ref[...], p_hi, nt, preferred_element_type=jnp.float32)
             + lax.dot_general(ov_ref[...], p_lo, nt, preferred_element_type=jnp.float32))
    nsb = imp_t.shape[0]
    blk = lax.broadcasted_iota(jnp.int32, (nsb, 1), 0)
    cur = (i * nq + lax.broadcasted_iota(jnp.int32, (1, nq), 1)) // SEL_BLOCK
    imp_t = jnp.where(blk > cur, NEG, imp_t)
    imp_t = jnp.where((blk == cur) | (blk == 0), BIG, imp_t)
    sel_t = jnp.where(blk <= cur, _top_blocks_t(imp_t), 0.0)

    eye = (lax.broadcasted_iota(jnp.int32, (nsb, nsb), 0)
           == lax.broadcasted_iota(jnp.int32, (nsb, nsb), 1)).astype(jnp.bfloat16)
    drop = lax.dot_general(((sel_t - 1.0) * BIG).astype(jnp.bfloat16), eye, (((0,), (0,)), ((), ())),
                           preferred_element_type=jnp.float32)
    drop = jnp.concatenate([drop.astype(jnp.bfloat16)] * HPG, axis=0)
    slope_row = jnp.broadcast_to(slopes, (HPG, nq, 1)).reshape(rows, 1)
    qpos_row = jnp.concatenate([qpos] * HPG, axis=0).astype(jnp.float32)
    c1, c2, c3 = _split3_bf16(slope_row * qpos_row)
    lane = lax.broadcasted_iota(jnp.int32, (rows, LANES), 1)
    pos_cols = sx_ref[0].astype(jnp.float32)
    for k, ck in enumerate((c1, c2, c3)):
        pos_cols = jnp.where(lane == 6 + k, -ck.astype(jnp.float32), pos_cols)
    q_ext = jnp.concatenate([q2, drop, pos_cols.astype(jnp.bfloat16)], axis=1)

    def slc_tile(kt, carry, causal_bias):
        m, l, acc = carry
        k0 = pl.multiple_of(kt * SLC_TK, SLC_TK)
        s = lax.dot_general(q_ext, kx_ref[0, pl.ds(k0, SLC_TK), :], nt,
                            preferred_element_type=jnp.float32).reshape(HPG, nq, SLC_TK)
        if causal_bias is not None:
            s = s + causal_bias[None]
        m_new = jnp.maximum(m, jnp.max(s, axis=-1, keepdims=True))
        a = jnp.exp(m - m_new)
        e = jnp.exp(s - m_new)
        l = a * l + jnp.sum(e, axis=-1, keepdims=True)
        pv = jnp.dot(e.reshape(rows, SLC_TK).astype(jnp.bfloat16), vs_ref[0, pl.ds(k0, SLC_TK), :],
                     preferred_element_type=jnp.float32).reshape(HPG, nq, HEAD_DIM)
        return m_new, l, a * acc + pv

    init = (jnp.full((HPG, nq, 1), NEG, jnp.float32), jnp.zeros((HPG, nq, 1), jnp.float32),
            jnp.zeros((HPG, nq, HEAD_DIM), jnp.float32))
    last = (i * nq) // SLC_TK
    carry = lax.fori_loop(0, last // 2, lambda p, c: slc_tile(2 * p + 1, slc_tile(2 * p, c, None), None), init)
    carry = lax.fori_loop(2 * (last // 2), last, lambda kt, c: slc_tile(kt, c, None), carry)
    _, l, acc = slc_tile(last, carry, cb_ref[i % (SLC_TK // QBLK)])
    o_slc = acc / jnp.maximum(l, 1e-30)

    gates = _sigmoid(ng_ref[0, 0])
    o_ref[...] = jnp.concatenate(
        [gates[:, h:h + 1] * o_cmp[h] + gates[:, HPG + h:HPG + h + 1] * o_slc[h]
         + gates[:, 2 * HPG + h:2 * HPG + h + 1] * o_win[h] for h in range(HPG)], axis=-1).astype(o_ref.dtype)


def _alibi_slopes():
    return jnp.asarray(2.0 ** (-8.0 * np.arange(1, N_HEADS + 1) / N_HEADS), jnp.float32).reshape(N_KV, HPG, 1, 1)


def _nsa_prompt(q_h, kcvc, kv_h, ng_t, n_cmp):
    bsz, _, t, _ = ng_t.shape
    nqb = t // QBLK
    n_ch = kcvc.shape[2]
    nsb = t // SEL_BLOCK
    st = np.arange(n_ch) * CMP_STRIDE
    bs = np.arange(nsb) * SEL_BLOCK
    overlap = ((st[:, None] <= bs[None, :] + SEL_BLOCK - 1) & (st[:, None] + CMP_LEN - 1 >= bs[None, :])
               & (np.arange(n_ch)[:, None] < n_cmp)).astype(np.float32)
    assert nsb == LANES - HEAD_DIM and t % SLC_TK == 0
    key_cols = jnp.asarray(np.tile(_slc_key_columns(t), (bsz, 1)), jnp.bfloat16)
    kx = jnp.concatenate([kv_h[2 * N_KV:3 * N_KV, :bsz * t],
                          jnp.broadcast_to(key_cols[None], (N_KV,) + key_cols.shape)], axis=-1)
    r = np.arange(SLC_TK // QBLK)[:, None, None] * QBLK + np.arange(QBLK)[None, :, None]
    causal = np.where(np.arange(SLC_TK)[None, None, :] <= r, 0.0, NEG).astype(np.float32)

    def kv_spec(slot):
        return pl.BlockSpec((1, t, HEAD_DIM), lambda b, g, i, slot=slot: (slot * N_KV + g, b, 0))

    return pl.pallas_call(
        functools.partial(_nsa_prompt_body, n_cmp=n_cmp),
        grid=(bsz, N_KV, nqb),
        in_specs=[pl.BlockSpec((HPG, QBLK, HEAD_DIM), lambda b, g, i: (g, b * nqb + i, 0)),
                  pl.BlockSpec((1, 1, n_ch, HEAD_DIM), lambda b, g, i: (b, g, 0, 0)),
                  pl.BlockSpec((1, 1, n_ch, HEAD_DIM), lambda b, g, i: (b, N_KV + g, 0, 0)),
                  pl.BlockSpec((1, t, XK), lambda b, g, i: (g, b, 0)),
                  kv_spec(3), kv_spec(4), kv_spec(5),
                  pl.BlockSpec((1, 1, QBLK, 3 * HPG), lambda b, g, i: (b, g, i, 0)),
                  pl.BlockSpec((nsb, n_ch), lambda b, g, i: (0, 0)),
                  pl.BlockSpec((1, HPG * QBLK, LANES), lambda b, g, i: (g, 0, 0)),
                  pl.BlockSpec(causal.shape, lambda b, g, i: (0, 0, 0)),
                  pl.BlockSpec((1, HPG, 1, 1), lambda b, g, i: (g, 0, 0, 0))],
        out_specs=pl.BlockSpec((QBLK, HPG * HEAD_DIM), lambda b, g, i: (b * nqb + i, g)),
        out_shape=jax.ShapeDtypeStruct((bsz * t, NSA_WIDTH), jnp.bfloat16),
        compiler_params=pltpu.CompilerParams(
            dimension_semantics=("parallel", "parallel", "arbitrary"), vmem_limit_bytes=VMEM_LIMIT),
        name="nsa_prompt",
    )(q_h, kcvc, kcvc, kx, kv_h, kv_h, kv_h, ng_t,
      jnp.asarray(overlap.T, jnp.bfloat16), _slc_query_columns(), jnp.asarray(causal), _alibi_slopes())


HALF_ROWS = 2 * N_KV * HEAD_DIM
GD = N_KV * HEAD_DIM


def _fetch_pages(pt_ref, cache_ref, sem, row0, per_step, dst_of):
    b = pl.program_id(0)
    slot = b % 2
    n_pages = pt_ref.shape[1]

    def page_copy(flat_page, sl, k):
        phys = pt_ref[flat_page // n_pages, flat_page % n_pages]
        return pltpu.make_async_copy(cache_ref.at[phys, pl.ds(row0, HALF_ROWS), :], dst_of(sl, k), sem.at[sl])

    def start_all(step, sl):
        def one(k, carry):
            page_copy(step * per_step + k, sl, k).start()
            return carry
        lax.fori_loop(0, per_step, one, 0)

    @pl.when(b == 0)
    def _():
        start_all(0, 0)

    @pl.when(b + 1 < pl.num_programs(0))
    def _():
        start_all(b + 1, 1 - slot)

    def wait_one(k, carry):
        page_copy(b * per_step + k, slot, k).wait()
        return carry
    lax.fori_loop(0, per_step, wait_one, 0)
    return slot


def _cmp_sample_body(pt_ref, cache_ref, w_ref, pe_ref, w1_ref, w2_ref, out_ref, buf, sem, xt, uv, *, steps):
    per_step = buf.shape[1]
    slot = _fetch_pages(pt_ref, cache_ref, sem, 0, per_step, lambda sl, k: buf.at[sl, k])
    n_cb = HALF_ROWS // LANES

    eye = (lax.broadcasted_iota(jnp.int32, (LANES, LANES), 0)
           == lax.broadcasted_iota(jnp.int32, (LANES, LANES), 1)).astype(jnp.bfloat16)

    def to_rows(k, carry):
        for cb in range(n_cb):
            blk = buf[slot, k, cb * LANES:(cb + 1) * LANES, :].astype(jnp.bfloat16)
            xt[k, cb] = lax.dot_general(eye, blk, (((1,), (1,)), ((), ())), preferred_element_type=jnp.float32)
        return carry
    lax.fori_loop(0, per_step, to_rows, 0, unroll=8)

    per_page = PAGE_SIZE // CMP_STRIDE
    n = per_step * per_page

    def tap(cgp, j):
        return xt[:, cgp, pl.ds(j, per_page, stride=CMP_STRIDE), :].reshape(n, LANES).astype(jnp.bfloat16)

    part = pl.program_id(0) % steps
    row0 = pl.multiple_of(part * n, n)
    for cgp in range(n_cb):
        uv[pl.ds(row0, n), cgp * 512:(cgp + 1) * 512] = _uv_taps(functools.partial(tap, cgp), w_ref, cgp // 2, n)

    @pl.when(part == steps - 1)
    def _():
        _cmp_finish_rows(lambda cols: uv[:, cols], uv.shape[0], pe_ref, w1_ref, w2_ref, out_ref)


UV_PAGES = 32


def _cmp_sample(page_table, cache_t, w1p, w_cmp1_l, w_cmp2_l, cmp_pe_l):
    n_seq, n_pages = page_table.shape
    assert n_pages % UV_PAGES == 0
    steps = n_pages // UV_PAGES
    n_ch = n_pages * (PAGE_SIZE // CMP_STRIDE)
    pe = jnp.zeros((2, 8, CMP_LEN * HEAD_DIM), jnp.float32).at[:, 0].set(cmp_pe_l.reshape(2, -1))
    return pl.pallas_call(
        functools.partial(_cmp_sample_body, steps=steps),
        grid_spec=pltpu.PrefetchScalarGridSpec(
            num_scalar_prefetch=1, grid=(n_seq * steps,),
            in_specs=[pl.BlockSpec(memory_space=pl.ANY),
                      pl.BlockSpec(w1p.shape, lambda b, pt: (0, 0, 0, 0)),
                      pl.BlockSpec((2, 8, CMP_LEN * HEAD_DIM), lambda b, pt: (0, 0, 0)),
                      pl.BlockSpec((2, CMP_LEN * HEAD_DIM, CMP_HID), lambda b, pt: (0, 0, 0)),
                      pl.BlockSpec((2, CMP_HID, HEAD_DIM), lambda b, pt: (0, 0, 0))],
            out_specs=pl.BlockSpec((1, 2, n_ch, GD), lambda b, pt: (b // steps, 0, 0, 0)),
            scratch_shapes=[pltpu.VMEM((2, UV_PAGES, HALF_ROWS, PAGE_SIZE), jnp.float32),
                            pltpu.SemaphoreType.DMA((2,)),
                            pltpu.VMEM((UV_PAGES, HALF_ROWS // LANES, PAGE_SIZE, LANES), jnp.float32),
                            pltpu.VMEM((n_ch, UV_COLS), jnp.float32)]),
        out_shape=jax.ShapeDtypeStruct((n_seq, 2, n_ch, GD), jnp.bfloat16),
        compiler_params=pltpu.CompilerParams(dimension_semantics=("arbitrary",),
                                             vmem_limit_bytes=VMEM_LIMIT),
        name="cmp_sample",
    )(page_table, cache_t, w1p, pe.astype(jnp.bfloat16),
      w_cmp1_l.reshape(2, CMP_LEN * HEAD_DIM, CMP_HID).astype(jnp.bfloat16), w_cmp2_l.astype(jnp.bfloat16))


def _group_diag(x, rows_per_group):
    grp = lax.broadcasted_iota(jnp.int32, (x.shape[0], 1), 0) // rows_per_group
    out = jnp.zeros((x.shape[0], HEAD_DIM), x.dtype)
    for g in range(N_KV):
        out = out + jnp.where(grp == g, x[:, g * HEAD_DIM:(g + 1) * HEAD_DIM], 0.0)
    return out


BIAS_POS_ROWS = 16


def _sample_bias_rows(n_lanes, past_len):
    kpos = np.arange(past_len)
    rows = np.zeros((n_lanes + BIAS_POS_ROWS, past_len), np.float32)
    rows[kpos // SEL_BLOCK, kpos] = 1.0
    rows[n_lanes:n_lanes + 3] = kpos // 64
    rows[n_lanes + 3:n_lanes + 6] = kpos % 64
    rows[n_lanes + 6:n_lanes + 9] = 1.0
    return rows


def _nsa_sample_body(pt_ref, cache_ref, q_ref, kcvc_ref, new_ref, win_ref, ng_ref, ov_ref, same_q_ref, sl_ref,
                     bq_ref, bk_ref, o_ref, buf, sem, *, n_cmp, t_new):
    past_len = buf.shape[2]
    slot = _fetch_pages(pt_ref, cache_ref, sem, HALF_ROWS, past_len // PAGE_SIZE,
                        lambda sl, k: buf.at[sl, :, pl.ds(pl.multiple_of(k * PAGE_SIZE, PAGE_SIZE), PAGE_SIZE)])
    rows = q_ref.shape[1]
    rpg = rows // N_KV
    nt = (((1,), (1,)), ((), ()))
    qbd = q_ref[0] * SCALE
    slope = sl_ref[...]
    row = lax.broadcasted_iota(jnp.int32, (rows, 1), 0)
    qtok = row % t_new
    qpos = past_len + qtok
    new_rows = new_ref.shape[2]
    new_idx = lax.broadcasted_iota(jnp.int32, (1, new_rows), 1)
    new_valid = (new_idx <= qtok) & (new_idx < t_new)
    new_bias = slope * (qtok - new_idx).astype(jnp.float32)

    n_ch = kcvc_ref.shape[2]
    s = lax.dot_general(qbd, kcvc_ref[0, 0], nt, preferred_element_type=jnp.float32)
    cidx = lax.broadcasted_iota(jnp.int32, (1, n_ch), 1)
    dist_c = qpos - (cidx * CMP_STRIDE + (CMP_LEN - 1))
    p = _masked_softmax(s - slope * dist_c.astype(jnp.float32), (dist_c >= 0) & (cidx < n_cmp))
    o_cmp = _group_diag(jnp.dot(p.astype(jnp.bfloat16), kcvc_ref[0, 1], preferred_element_type=jnp.float32), rpg)

    p_hi, p_lo = _split_bf16(p)
    psum = (jnp.dot(same_q_ref[...], p_hi, preferred_element_type=jnp.float32)
            + jnp.dot(same_q_ref[...], p_lo, preferred_element_type=jnp.float32))
    ps_hi, ps_lo = _split_bf16(psum)
    imp_t = (lax.dot_general(ov_ref[...], ps_hi, nt, preferred_element_type=jnp.float32)
             + lax.dot_general(ov_ref[...], ps_lo, nt, preferred_element_type=jnp.float32))
    n_lanes = imp_t.shape[0]
    blk = lax.broadcasted_iota(jnp.int32, (n_lanes, 1), 0)
    cur = (past_len + lax.broadcasted_iota(jnp.int32, (1, rows), 1) % t_new) // SEL_BLOCK
    imp_t = jnp.where(blk > cur, -jnp.inf, imp_t)
    imp_t = jnp.where((blk == cur) | (blk == 0), BIG, imp_t)
    nsb_pad = -(-(-(-(past_len + t_new) // SEL_BLOCK)) // SUBLANES) * SUBLANES
    sel_t = jnp.concatenate([_top_blocks_t(imp_t[:nsb_pad]),
                             jnp.zeros((n_lanes - nsb_pad, rows), jnp.float32)], axis=0)
    eye = (lax.broadcasted_iota(jnp.int32, (n_lanes, n_lanes), 0)
           == lax.broadcasted_iota(jnp.int32, (n_lanes, n_lanes), 1)).astype(jnp.bfloat16)
    drop = lax.dot_general(((sel_t - 1.0) * BIG).astype(jnp.bfloat16), eye, (((0,), (0,)), ((), ())),
                           preferred_element_type=jnp.float32)

    c1, c2, c3 = _split3_bf16(slope * qpos.astype(jnp.float32))
    lane = lax.broadcasted_iota(jnp.int32, (rows, bq_ref.shape[1]), 1)
    pos_cols = bq_ref[...].astype(jnp.float32)
    for k, ck in enumerate((c1, c2, c3)):
        pos_cols = jnp.where(lane == 6 + k, -ck.astype(jnp.float32), pos_cols)
    bias_q = jnp.concatenate([drop.astype(jnp.bfloat16), pos_cols.astype(jnp.bfloat16)], axis=1)
    s_all = (jnp.dot(qbd, buf[slot, 0:GD, :].astype(jnp.bfloat16), preferred_element_type=jnp.float32)
             + jnp.dot(bias_q, bk_ref[...], preferred_element_type=jnp.float32))
    s_new = lax.dot_general(qbd, new_ref[0, 2], nt, preferred_element_type=jnp.float32)
    s_new = jnp.where(new_valid, s_new - new_bias, NEG)
    m = jnp.maximum(jnp.max(s_all, axis=-1, keepdims=True), jnp.max(s_new, axis=-1, keepdims=True))
    e_all = jnp.exp(s_all - m)
    e_new = jnp.where(new_valid, jnp.exp(s_new - m), 0.0)
    l = jnp.sum(e_all, axis=-1, keepdims=True) + jnp.sum(e_new, axis=-1, keepdims=True)
    acc = (lax.dot_general(e_all.astype(jnp.bfloat16), buf[slot, GD:2 * GD, :].astype(jnp.bfloat16), nt,
                           preferred_element_type=jnp.float32)
           + jnp.dot(e_new.astype(jnp.bfloat16), new_ref[0, 3], preferred_element_type=jnp.float32))
    o_slc = _group_diag(acc, rpg) / jnp.maximum(l, 1e-30)

    w_buf = win_ref.shape[2]
    wdist = qpos - (past_len - w_buf + lax.broadcasted_iota(jnp.int32, (1, w_buf), 1))
    valid_w = (wdist >= 0) & (wdist < WINDOW)
    s_w = jnp.dot(qbd, win_ref[0, 0:GD, :].astype(jnp.bfloat16), preferred_element_type=jnp.float32)
    s_w = jnp.where(valid_w, s_w - slope * wdist.astype(jnp.float32), NEG)
    s_wn = lax.dot_general(qbd, new_ref[0, 4], nt, preferred_element_type=jnp.float32)
    s_wn = jnp.where(new_valid, s_wn - new_bias, NEG)
    m = jnp.maximum(jnp.max(s_w, axis=-1, keepdims=True), jnp.max(s_wn, axis=-1, keepdims=True))
    e_w = jnp.where(valid_w, jnp.exp(s_w - m), 0.0)
    e_wn = jnp.where(new_valid, jnp.exp(s_wn - m), 0.0)
    acc = (lax.dot_general(e_w.astype(jnp.bfloat16), win_ref[0, GD:2 * GD, :].astype(jnp.bfloat16), nt,
                           preferred_element_type=jnp.float32)
           + jnp.dot(e_wn.astype(jnp.bfloat16), new_ref[0, 5], preferred_element_type=jnp.float32))
    l = jnp.sum(e_w, axis=-1, keepdims=True) + jnp.sum(e_wn, axis=-1, keepdims=True)
    o_win = _group_diag(acc, rpg) / jnp.maximum(l, 1e-30)

    gates = _sigmoid(ng_ref[0])
    o_ref[0] = (gates[:, 0:1] * o_cmp + gates[:, 1:2] * o_slc + gates[:, 2:3] * o_win).astype(o_ref.dtype)


def _nsa_sample(page_table, cache_t, q_bd, kcvc, new_kv, win_t, ng_r, n_cmp, t_new):
    n_seq, n_pages = page_table.shape
    past_len = n_pages * PAGE_SIZE
    n_ch = kcvc.shape[2]
    rows = q_bd.shape[1]
    rpg = rows // N_KV
    nsb = -(-(past_len + t_new) // SEL_BLOCK)
    n_lanes = -(-nsb // LANES) * LANES
    st = np.arange(n_ch) * CMP_STRIDE
    bs = np.arange(n_lanes) * SEL_BLOCK
    overlap = ((st[:, None] <= bs[None, :] + SEL_BLOCK - 1) & (st[:, None] + CMP_LEN - 1 >= bs[None, :])
               & (np.arange(n_ch)[:, None] < n_cmp) & (np.arange(n_lanes)[None, :] < nsb)).astype(np.float32)
    r = np.arange(rows)
    same_q = ((r[:, None] // rpg == r[None, :] // rpg) & (r[:, None] % t_new == r[None, :] % t_new)
              ).astype(np.float32)
    slopes = jnp.repeat(_alibi_slopes().reshape(N_HEADS), t_new).reshape(rows, 1)
    s1, s2, s3 = _split3_bf16(slopes)
    bias_q = jnp.pad(jnp.concatenate([64.0 * s1, 64.0 * s2, 64.0 * s3, s1, s2, s3], axis=1),
                     ((0, 0), (0, BIAS_POS_ROWS - 6)))
    bias_k = jnp.asarray(_sample_bias_rows(n_lanes, past_len), jnp.bfloat16)
    w_buf = win_t.shape[2]
    return pl.pallas_call(
        functools.partial(_nsa_sample_body, n_cmp=n_cmp, t_new=t_new),
        grid_spec=pltpu.PrefetchScalarGridSpec(
            num_scalar_prefetch=1, grid=(n_seq,),
            in_specs=[pl.BlockSpec(memory_space=pl.ANY),
                      pl.BlockSpec((1, rows, GD), lambda b, pt: (b, 0, 0)),
                      pl.BlockSpec((1, 2, n_ch, GD), lambda b, pt: (b, 0, 0, 0)),
                      pl.BlockSpec((1, 6, new_kv.shape[2], GD), lambda b, pt: (b, 0, 0, 0)),
                      pl.BlockSpec((1, HALF_ROWS, w_buf), lambda b, pt: (b, 0, 0)),
                      pl.BlockSpec((1, rows, 3), lambda b, pt: (b, 0, 0)),
                      pl.BlockSpec((n_lanes, n_ch), lambda b, pt: (0, 0)),
                      pl.BlockSpec((rows, rows), lambda b, pt: (0, 0)),
                      pl.BlockSpec((rows, 1), lambda b, pt: (0, 0)),
                      pl.BlockSpec(bias_q.shape, lambda b, pt: (0, 0)),
                      pl.BlockSpec(bias_k.shape, lambda b, pt: (0, 0), pipeline_mode=pl.Buffered(1))],
            out_specs=pl.BlockSpec((1, rows, HEAD_DIM), lambda b, pt: (b, 0, 0)),
            scratch_shapes=[pltpu.VMEM((2, HALF_ROWS, past_len), jnp.float32),
                            pltpu.SemaphoreType.DMA((2,))]),
        out_shape=jax.ShapeDtypeStruct((n_seq, rows, HEAD_DIM), jnp.bfloat16),
        compiler_params=pltpu.CompilerParams(dimension_semantics=("arbitrary",),
                                             vmem_limit_bytes=VMEM_LIMIT),
        name="nsa_sample",
    )(page_table, cache_t, q_bd, kcvc, new_kv, win_t, ng_r,
      jnp.asarray(overlap.T, jnp.bfloat16), jnp.asarray(same_q, jnp.bfloat16), slopes, bias_q, bias_k)


def _mix_body(a_ref, b_ref, mga_ref, mgb_ref, wpa_ref, wpb_ref, u_ref):
    a = jnp.dot(a_ref[...], wpa_ref[...], preferred_element_type=jnp.float32)
    b = jnp.dot(b_ref[...], wpb_ref[...], preferred_element_type=jnp.float32)
    u_ref[...] = (_sigmoid(mga_ref[...]) * a + _sigmoid(mgb_ref[...]) * b).astype(u_ref.dtype)


def _mix(o_nsa, o_hg, mg, w_pa_bf, w_pb_bf, tm):
    n = o_nsa.shape[0]
    const = lambda i: (0, 0)
    return pl.pallas_call(
        _mix_body,
        grid=(n // tm,),
        in_specs=[pl.BlockSpec((tm, NSA_WIDTH), lambda i: (i, 0)),
                  pl.BlockSpec((tm, HG_WIDTH), lambda i: (i, 0)),
                  pl.BlockSpec((tm, D_MODEL), lambda i: (i, 0)),
                  pl.BlockSpec((tm, D_MODEL), lambda i: (i, 1)),
                  pl.BlockSpec((NSA_WIDTH, D_MODEL), const, pipeline_mode=pl.Buffered(1)),
                  pl.BlockSpec((HG_WIDTH, D_MODEL), const, pipeline_mode=pl.Buffered(1))],
        out_specs=pl.BlockSpec((tm, D_MODEL), lambda i: (i, 0)),
        out_shape=jax.ShapeDtypeStruct((n, D_MODEL), jnp.bfloat16),
        compiler_params=pltpu.CompilerParams(dimension_semantics=("parallel",),
                                             vmem_limit_bytes=VMEM_LIMIT),
        name="tail_mix",
    )(o_nsa, o_hg, mg, mg, w_pa_bf, w_pb_bf)


def _layer_norm(z, g, b):
    mu = jnp.mean(z, axis=-1, keepdims=True)
    zc = z - mu
    var = jnp.mean(zc * zc, axis=-1, keepdims=True)
    return zc * lax.rsqrt(var + LN_EPS) * g + b


CH = D_MODEL // LANES


def _store_chunked(ref, val):
    tm = val.shape[0]
    for k in range(CH):
        ref[pl.ds(k, tm, stride=CH), :] = val[:, k * LANES:(k + 1) * LANES]


def _load_chunked(ref, tm, lead=()):
    return jnp.concatenate([ref[lead + (pl.ds(k, tm, stride=CH), slice(None))] for k in range(CH)], axis=1)


def _ln1_body(u_ref, x_ref, wout_ref, g_ref, b_ref, wr_hi_ref, wr_lo_ref, br_ref, h_ref, lg_ref):
    y = jnp.dot(u_ref[...], wout_ref[...], preferred_element_type=jnp.float32)
    h = _layer_norm(DN_ALPHA * x_ref[...] + y, g_ref[...], b_ref[...])
    _store_chunked(h_ref, h)
    h_hi = h.astype(jnp.bfloat16)
    h_lo = (h - h_hi.astype(jnp.float32)).astype(jnp.bfloat16)
    lg = jnp.dot(h_hi, wr_hi_ref[...], preferred_element_type=jnp.float32)
    lg = lg + jnp.dot(h_lo, wr_hi_ref[...], preferred_element_type=jnp.float32)
    lg = lg + jnp.dot(h_hi, wr_lo_ref[...], preferred_element_type=jnp.float32)
    lg_ref[...] = lg + br_ref[...]


def _ln1(u, x_all, w_out_bf, g, b, wr_hi, wr_lo, br, tm):
    n = u.shape[0]
    const = lambda i: (0, 0)
    return pl.pallas_call(
        _ln1_body,
        grid=(n // tm,),
        in_specs=[pl.BlockSpec((tm, D_MODEL), lambda i: (i, 0)),
                  pl.BlockSpec((tm, D_MODEL), lambda i: (i, 0)),
                  pl.BlockSpec((D_MODEL, D_MODEL), const, pipeline_mode=pl.Buffered(1)),
                  pl.BlockSpec((1, D_MODEL), const),
                  pl.BlockSpec((1, D_MODEL), const),
                  pl.BlockSpec((D_MODEL, LANES), const),
                  pl.BlockSpec((D_MODEL, LANES), const),
                  pl.BlockSpec((1, LANES), const)],
        out_specs=[pl.BlockSpec((tm * CH, LANES), lambda i: (i, 0)),
                   pl.BlockSpec((tm, LANES), lambda i: (i, 0))],
        out_shape=[jax.ShapeDtypeStruct((n * CH, LANES), jnp.float32),
                   jax.ShapeDtypeStruct((n, LANES), jnp.float32)],
        compiler_params=pltpu.CompilerParams(dimension_semantics=("parallel",),
                                             vmem_limit_bytes=VMEM_LIMIT),
        name="tail_ln1",
    )(u, x_all, w_out_bf, g.reshape(1, -1), b.reshape(1, -1), wr_hi, wr_lo, br)


LG0 = N_GROUPS


def _route_body(lg_ref, tri_ref, out_ref, cnt_ref, carry_scr):
    @pl.when(pl.program_id(0) == 0)
    def _():
        carry_scr[...] = jnp.zeros_like(carry_scr)

    lg = lg_ref[...]
    tm = lg.shape[0]
    lane = lax.broadcasted_iota(jnp.int32, lg.shape, 1)
    is_g = lane < N_GROUPS
    gl = jnp.where(is_g, lg, NEG)
    gmax = jnp.max(gl, axis=-1, keepdims=True)
    grp = jnp.min(jnp.where(gl == gmax, lane, LANES), axis=-1, keepdims=True)
    g_w = 1.0 / jnp.sum(jnp.where(is_g, jnp.exp(lg - gmax), 0.0), axis=-1, keepdims=True)
    lo = LG0 + grp * EXP_PER_GROUP
    el = jnp.where((lane >= lo) & (lane < lo + EXP_PER_GROUP), lg, NEG)
    v1 = jnp.max(el, axis=-1, keepdims=True)
    i1 = jnp.min(jnp.where(el == v1, lane, LANES), axis=-1, keepdims=True)
    el2 = jnp.where(lane == i1, NEG, el)
    v2 = jnp.max(el2, axis=-1, keepdims=True)
    i2 = jnp.min(jnp.where(el2 == v2, lane, LANES), axis=-1, keepdims=True)
    e21 = jnp.exp(v2 - v1)
    w1 = g_w / (1.0 + e21)
    w2 = g_w * e21 / (1.0 + e21)
    hit1 = lane == i1
    hit2 = lane == i2
    onehot = jnp.where(hit1 | hit2, 1.0, 0.0)
    incl = jnp.dot(tri_ref[...], onehot.astype(jnp.bfloat16), preferred_element_type=jnp.float32)
    carry = carry_scr[...]
    before = incl - onehot + carry
    r1 = jnp.sum(jnp.where(hit1, before, 0.0), axis=-1, keepdims=True)
    r2 = jnp.sum(jnp.where(hit2, before, 0.0), axis=-1, keepdims=True)
    carry = carry + incl[tm - 1:tm, :]
    carry_scr[...] = carry
    cnt_ref[...] = carry
    out = jnp.where(lane == 0, (i1 - LG0).astype(jnp.float32), 0.0)
    out = jnp.where(lane == 1, (i2 - LG0).astype(jnp.float32), out)
    out = jnp.where(lane == 2, w1, out)
    out = jnp.where(lane == 3, w2, out)
    out = jnp.where(lane == 4, r1, out)
    out = jnp.where(lane == 5, r2, out)
    out_ref[...] = out


def _route(lg, tm):
    n = lg.shape[0]
    tri = (np.arange(tm)[:, None] >= np.arange(tm)[None, :]).astype(np.float32)
    return pl.pallas_call(
        _route_body,
        grid=(n // tm,),
        in_specs=[pl.BlockSpec((tm, LANES), lambda i: (i, 0)),
                  pl.BlockSpec((tm, tm), lambda i: (0, 0))],
        out_specs=[pl.BlockSpec((tm, LANES), lambda i: (i, 0)),
                   pl.BlockSpec((1, LANES), lambda i: (0, 0))],
        out_shape=[jax.ShapeDtypeStruct((n, LANES), jnp.float32),
                   jax.ShapeDtypeStruct((1, LANES), jnp.float32)],
        scratch_shapes=[pltpu.VMEM((1, LANES), jnp.float32)],
        compiler_params=pltpu.CompilerParams(dimension_semantics=("arbitrary",)),
        name="moe_route",
    )(lg, jnp.asarray(tri, jnp.bfloat16))


def _dispatch_body(pos_ref, h_ref, xs_in_ref, xs_ref, sem):
    del xs_in_ref
    tm = h_ref.shape[0] // CH
    base = pl.program_id(0) * tm

    def issue(t, carry):
        src = pl.multiple_of(t * CH, CH)
        for slot in range(2):
            dst = pl.multiple_of(pos_ref[2 * (base + t) + slot] * CH, CH)
            pltpu.make_async_copy(h_ref.at[pl.ds(src, CH)], xs_ref.at[pl.ds(dst, CH)], sem).start()
        return carry

    lax.fori_loop(0, tm, issue, 0)
    for _ in range(2):
        pltpu.make_async_copy(h_ref, xs_ref.at[pl.ds(0, tm * CH)], sem).wait()


DISPATCH_TM = 128


def _dispatch(pos_flat, h_c, n_slots):
    zeros = jnp.zeros((n_slots * CH, LANES), h_c.dtype)
    tm = DISPATCH_TM
    return pl.pallas_call(
        _dispatch_body,
        grid_spec=pltpu.PrefetchScalarGridSpec(
            num_scalar_prefetch=1, grid=(h_c.shape[0] // (tm * CH),),
            in_specs=[pl.BlockSpec((tm * CH, LANES), lambda i, pos: (i, 0)),
                      pl.BlockSpec(memory_space=pl.ANY)],
            out_specs=pl.BlockSpec(memory_space=pl.ANY),
            scratch_shapes=[pltpu.SemaphoreType.DMA(())]),
        out_shape=jax.ShapeDtypeStruct(zeros.shape, h_c.dtype),
        input_output_aliases={2: 0},
        compiler_params=pltpu.CompilerParams(dimension_semantics=("arbitrary",)),
        name="moe_dispatch",
    )(pos_flat, h_c, zeros)


def _ffn_body(te_ref, nu_ref, x_ref, wg_ref, wu_ref, wd_ref, y_ref, wg_bf, wu_bf, wd_bf):
    i = pl.program_id(0)
    prev = te_ref[jnp.maximum(i - 1, 0)]

    @pl.when((i == 0) | (te_ref[i] != prev))
    def _():
        wg_bf[...] = wg_ref[0].astype(jnp.bfloat16)
        wu_bf[...] = wu_ref[0].astype(jnp.bfloat16)
        wd_bf[...] = wd_ref[0].astype(jnp.bfloat16)

    @pl.when(i < nu_ref[0])
    def _():
        x = _load_chunked(x_ref, x_ref.shape[0] // CH).astype(jnp.bfloat16)
        g = jnp.dot(x, wg_bf[...], preferred_element_type=jnp.float32)
        u = jnp.dot(x, wu_bf[...], preferred_element_type=jnp.float32)
        hid = (g * _sigmoid(g) * u).astype(jnp.bfloat16)
        _store_chunked(y_ref, jnp.dot(hid, wd_bf[...], preferred_element_type=jnp.float32))

    @pl.when(i >= nu_ref[0])
    def _():
        y_ref[...] = jnp.zeros_like(y_ref)


def _ffn(tile_expert, n_used, xs_c, w_gate, w_up, w_down, tm):
    d = D_MODEL
    nt = xs_c.shape[0] // (tm * CH)
    return pl.pallas_call(
        _ffn_body,
        grid_spec=pltpu.PrefetchScalarGridSpec(
            num_scalar_prefetch=2, grid=(nt,),
            in_specs=[pl.BlockSpec((tm * CH, LANES), lambda i, te, nu: (jnp.minimum(i, nu[0] - 1), 0)),
                      pl.BlockSpec((1, d, D_EXPERT), lambda i, te, nu: (te[i], 0, 0)),
                      pl.BlockSpec((1, d, D_EXPERT), lambda i, te, nu: (te[i], 0, 0)),
                      pl.BlockSpec((1, D_EXPERT, d), lambda i, te, nu: (te[i], 0, 0))],
            out_specs=pl.BlockSpec((tm * CH, LANES), lambda i, te, nu: (i, 0)),
            scratch_shapes=[pltpu.VMEM((d, D_EXPERT), jnp.bfloat16),
                            pltpu.VMEM((d, D_EXPERT), jnp.bfloat16),
                            pltpu.VMEM((D_EXPERT, d), jnp.bfloat16)]),
        out_shape=jax.ShapeDtypeStruct(xs_c.shape, jnp.float32),
        compiler_params=pltpu.CompilerParams(dimension_semantics=("arbitrary",),
                                             vmem_limit_bytes=VMEM_LIMIT),
        name="moe_ffn",
    )(tile_expert, n_used, xs_c, w_gate, w_up, w_down)


def _combine_body(pos_ref, h_ref, rw_ref, g_ref, b_ref, y_ref, out_a_ref, out_b_ref, buf, sem, *, tiles_a):
    tm = out_a_ref.shape[0]
    i = pl.program_id(0)
    cur = i % 2

    def gather(tile, parity):
        def one(t, carry):
            for slot in range(2):
                src = pl.multiple_of(pos_ref[2 * (tile * tm + t) + slot] * CH, CH)
                dst = pl.multiple_of(t * CH, CH)
                pltpu.make_async_copy(y_ref.at[pl.ds(src, CH)], buf.at[parity, slot, pl.ds(dst, CH)],
                                      sem.at[parity, slot]).start()
            return carry
        lax.fori_loop(0, tm, one, 0)

    @pl.when(i == 0)
    def _():
        gather(0, 0)

    @pl.when(i + 1 < pl.num_programs(0))
    def _():
        gather(i + 1, 1 - cur)

    for slot in range(2):
        pltpu.make_async_copy(y_ref.at[pl.ds(0, tm * CH)], buf.at[cur, slot], sem.at[cur, slot]).wait()
    rw = rw_ref[...]
    z = rw[:, 2:3] * _load_chunked(buf, tm, (cur, 0)) + rw[:, 3:4] * _load_chunked(buf, tm, (cur, 1))
    res = _layer_norm(DN_ALPHA * _load_chunked(h_ref, tm) + z, g_ref[...], b_ref[...])

    @pl.when(i < tiles_a)
    def _():
        out_a_ref[...] = res

    @pl.when(i >= tiles_a)
    def _():
        out_b_ref[...] = res


def _combine(pos_flat, h_c, route_out, g, b, y_c, tm, n_a):
    n, d = h_c.shape[0] // CH, D_MODEL
    assert n_a % tm == 0 and (n - n_a) % tm == 0 and 0 < n_a < n
    tiles_a = n_a // tm
    return pl.pallas_call(
        functools.partial(_combine_body, tiles_a=tiles_a),
        grid_spec=pltpu.PrefetchScalarGridSpec(
            num_scalar_prefetch=1, grid=(n // tm,),
            in_specs=[pl.BlockSpec((tm * CH, LANES), lambda i, pos: (i, 0)),
                      pl.BlockSpec((tm, LANES), lambda i, pos: (i, 0)),
                      pl.BlockSpec((1, d), lambda i, pos: (0, 0)),
                      pl.BlockSpec((1, d), lambda i, pos: (0, 0)),
                      pl.BlockSpec(memory_space=pl.ANY)],
            out_specs=[pl.BlockSpec((tm, d), lambda i, pos: (jnp.minimum(i, tiles_a - 1), 0)),
                       pl.BlockSpec((tm, d), lambda i, pos: (jnp.maximum(i - tiles_a, 0), 0))],
            scratch_shapes=[pltpu.VMEM((2, 2, tm * CH, LANES), jnp.float32),
                            pltpu.SemaphoreType.DMA((2, 2))]),
        out_shape=[jax.ShapeDtypeStruct((n_a, d), jnp.float32),
                   jax.ShapeDtypeStruct((n - n_a, d), jnp.float32)],
        compiler_params=pltpu.CompilerParams(dimension_semantics=("arbitrary",),
                                             vmem_limit_bytes=VMEM_LIMIT),
        name="moe_combine",
    )(pos_flat, h_c, route_out, g.reshape(1, -1), b.reshape(1, -1), y_c)


FFN_TM = 256
PROJ_TM = 1664
PROJ_TN = 512
TAIL_TM = 320


def _moe_and_norm(h_c, lg, w_gate, w_up, w_down, ln2_g, ln2_b, n_first):
    n = lg.shape[0]
    route_out, cnt = _route(lg, 640)
    eid = route_out[:, 0:2].astype(jnp.int32)
    rank = route_out[:, 4:6].astype(jnp.int32)
    counts = cnt[0, LG0:LG0 + N_EXPERTS].astype(jnp.int32)
    tiles_per = (counts + FFN_TM - 1) // FFN_TM
    tile_end = jnp.cumsum(tiles_per)
    row_start = (tile_end - tiles_per) * FFN_TM
    hit = eid[..., None] == jnp.arange(N_EXPERTS, dtype=jnp.int32)
    pos_flat = (jnp.sum(jnp.where(hit, row_start, 0), axis=-1) + rank).reshape(-1)
    nt = (2 * n) // FFN_TM + N_EXPERTS
    n_used = tile_end[-1]
    tile_ids = jnp.minimum(jnp.arange(nt, dtype=jnp.int32), n_used - 1)
    tile_expert = jnp.sum((tile_end[None, :] <= tile_ids[:, None]).astype(jnp.int32), axis=1)
    xs_c = _dispatch(pos_flat, h_c, nt * FFN_TM)
    y_c = _ffn(tile_expert, n_used.reshape(1).astype(jnp.int32), xs_c, w_gate, w_up, w_down, FFN_TM)
    return _combine(pos_flat, h_c, route_out, ln2_g, ln2_b, y_c, DISPATCH_TM, n_first)


def kernel(x_prompt, x_sample, cache_kv, cache_win, state_hgrn, page_table, w_in, b_in, w_cmp1, w_cmp2, cmp_pe,
           hgrn_gamma, hgrn_norm, w_pa, w_pb, w_out, ln1_g, ln1_b, w_rg, b_rg, w_re, b_re, w_gate, w_up, w_down,
           ln2_g, ln2_b):
    n_p = x_prompt.shape[0] * x_prompt.shape[1]
    n_s = x_sample.shape[0] * x_sample.shape[1]
    x_all = jnp.concatenate([x_prompt.reshape(n_p, D_MODEL), x_sample.reshape(n_s, D_MODEL)], axis=0)
    x_bf = x_all.astype(jnp.bfloat16)
    bsz, seq = x_prompt.shape[:2]
    n_seq, t_new = x_sample.shape[:2]
    n_pages = page_table.shape[1]
    past_len = n_pages * PAGE_SIZE
    w = w_in[0]
    b = b_in[0]

    def seg(lo, hi, outs, tn, name, pad_to=None):
        ws, bs = w[:, lo:hi], b[lo:hi]
        if pad_to is not None:
            ws = jnp.pad(ws, ((0, 0), (0, pad_to - (hi - lo))))
            bs = jnp.pad(bs, (0, pad_to - (hi - lo)))
        return _proj(x_bf, ws.astype(jnp.bfloat16), bs, outs, PROJ_TM, tn, name)

    q_h, = seg(OFF_Q, OFF_KV, [(jnp.bfloat16, True)], PROJ_TN, "proj_q")
    kv32, kv_h = seg(OFF_KV, OFF_NG, [(jnp.float32, False), (jnp.bfloat16, True)], PROJ_TN, "proj_kv")
    ng, = seg(OFF_NG, OFF_H4, [(jnp.float32, False)], LANES, "proj_ng", pad_to=LANES)
    h4, = seg(OFF_H4, OFF_MG, [(jnp.float32, False)], PROJ_TN, "proj_h4")
    mg, = seg(OFF_MG, PROJ_COLS, [(jnp.float32, False)], PROJ_TN, "proj_mg")

    kv_p = kv32[:n_p].reshape(1, bsz, seq, 6, N_KV, HEAD_DIM)
    kv_s = kv32[n_p:].reshape(1, n_seq, t_new, 6, N_KV, HEAD_DIM)
    new_kv_prompt = kv_p[:, :, :, :KV_SLOTS]
    new_kv_sample = kv_s[:, :, :, :KV_SLOTS]
    new_win_prompt = kv_p[:, :, seq - min(WINDOW, seq):, 4:6]
    win_all = jnp.concatenate([cache_win, kv_s[:, :, :, 4:6].astype(cache_win.dtype)], axis=2)
    new_win_sample = win_all[:, :, win_all.shape[2] - min(WINDOW, win_all.shape[2]):]

    w1p = _cmp_w1_pairs(w_cmp1[0])
    n_cmp_p = (seq - CMP_LEN) // CMP_STRIDE + 1
    uv_p = _cmp_uv_rows(kv32, n_p, 2048, w1p)
    kcvc_p = _cmp_finish(uv_p, bsz, seq // CMP_STRIDE, w_cmp1[0], w_cmp2[0], cmp_pe[0], "cmp_finish_prompt")
    ng_t = ng[:n_p, :NG_COLS].reshape(bsz, seq, 3, N_KV, HPG).transpose(0, 3, 1, 2, 4).reshape(bsz, N_KV, seq, 3 * HPG)
    o_nsa_p = _nsa_prompt(q_h, kcvc_p, kv_h, ng_t, n_cmp_p)

    cache_t = cache_kv[0].transpose(0, 2, 3, 4, 1).reshape(cache_kv.shape[1], 2 * HALF_ROWS, PAGE_SIZE)
    win_t = cache_win[0].transpose(0, 2, 3, 4, 1).reshape(n_seq, HALF_ROWS, cache_win.shape[2])
    n_cmp_s = (past_len + t_new - CMP_LEN) // CMP_STRIDE + 1
    kcvc_s = _cmp_sample(page_table, cache_t, w1p, w_cmp1[0], w_cmp2[0], cmp_pe[0])
    rows_s = N_HEADS * t_new
    q_s = q_h[:, n_p:].reshape(N_KV, HPG, n_seq, t_new, HEAD_DIM).transpose(2, 0, 1, 3, 4).reshape(
        n_seq, N_KV, HPG * t_new, HEAD_DIM)
    q_bd = jnp.einsum('sgrd,gk->sgrkd', q_s, jnp.eye(N_KV, dtype=q_s.dtype)).reshape(n_seq, rows_s, GD)
    new_kv = kv_h[:, n_p:].reshape(6, N_KV, n_seq, t_new, HEAD_DIM).transpose(2, 0, 3, 1, 4).reshape(
        n_seq, 6, t_new, GD)
    new_kv = jnp.pad(new_kv, ((0, 0), (0, 0), (0, 8 - t_new), (0, 0)))
    ng_r = ng[n_p:, :NG_COLS].reshape(n_seq, t_new, 3, N_KV, HPG).transpose(0, 3, 4, 1, 2).reshape(
        n_seq, rows_s, 3)
    o_nsa_s = _nsa_sample(page_table, cache_t, q_bd, kcvc_s, new_kv, win_t, ng_r, n_cmp_s, t_new)
    o_nsa_s = o_nsa_s.reshape(n_seq, N_KV, HPG, t_new, HEAD_DIM).transpose(0, 3, 1, 2, 4).reshape(n_s, NSA_WIDTH)

    lower = jnp.cumsum(jax.nn.softmax(hgrn_gamma.astype(jnp.float32), axis=0), axis=0)
    lb = lower[0].reshape(HG_HEADS, HG_DK)
    zero_state = jnp.zeros((bsz, HG_HEADS, HG_DV, HG_DK), jnp.float32)
    o_hg_p, st_p = _hgrn(h4, 0, bsz, seq, lb, hgrn_norm[0], zero_state, HG_CHUNK, HG_CHUNK, 512, HG_HEADS,
                         "hgrn_prompt")
    h4_s = jnp.pad(h4[n_p:].reshape(n_seq, t_new, -1), ((0, 0), (0, 8 - t_new), (0, 0))).reshape(n_seq * 8, -1)
    o_hg_s, st_s = _hgrn(h4_s, 0, n_seq, 8, lb, hgrn_norm[0], state_hgrn[0].transpose(0, 1, 3, 2),
                         8, t_new, 8, HG_HEADS, "hgrn_sample")
    o_hg_s = o_hg_s.reshape(n_seq, 8, HG_WIDTH)[:, :t_new].reshape(n_s, HG_WIDTH)
    new_state_prompt = st_p.transpose(0, 1, 3, 2)[None].astype(x_prompt.dtype)
    new_state_sample = st_s.transpose(0, 1, 3, 2)[None].astype(state_hgrn.dtype)

    o_nsa = jnp.concatenate([o_nsa_p, o_nsa_s], axis=0)
    o_hg = jnp.concatenate([o_hg_p, o_hg_s], axis=0)
    u = _mix(o_nsa, o_hg, mg, w_pa[0].astype(jnp.bfloat16), w_pb[0].astype(jnp.bfloat16), TAIL_TM)
    wr = jnp.zeros((D_MODEL, LANES), jnp.float32).at[:, :N_GROUPS].set(w_rg[0]).at[:, LG0:LG0 + N_EXPERTS].set(w_re[0])
    br = jnp.zeros((1, LANES), jnp.float32).at[0, :N_GROUPS].set(b_rg[0]).at[0, LG0:LG0 + N_EXPERTS].set(b_re[0])
    wr_hi, wr_lo = _split_bf16(wr)
    h, lg = _ln1(u, x_all, w_out[0].astype(jnp.bfloat16), ln1_g[0], ln1_b[0], wr_hi, wr_lo, br, TAIL_TM)
    out_p, out_s = _moe_and_norm(h, lg, w_gate[0], w_up[0], w_down[0], ln2_g[0], ln2_b[0], n_p)
    y_prompt = out_p.reshape(bsz, seq, D_MODEL)
    y_sample = out_s.reshape(n_seq, t_new, D_MODEL)
    return (y_prompt, y_sample, new_kv_prompt, new_kv_sample, new_win_prompt, new_win_sample,
            new_state_prompt, new_state_sample)
```

```python
import functools

import numpy as np
import jax
import jax.numpy as jnp
from jax import lax
from jax.experimental import pallas as pl
from jax.experimental.pallas import tpu as pltpu

D_MODEL = 2048
N_HEADS = 16
N_KV = 4
HPG = N_HEADS // N_KV
HEAD_DIM = 64
NSA_WIDTH = N_HEADS * HEAD_DIM
CMP_LEN = 32
CMP_STRIDE = 16
CMP_HID = 128
SEL_BLOCK = 64
N_SEL = 16
WINDOW = 512
QBLK = 128
KV_SLOTS = 4
PAGE_SIZE = 128
HG_HEADS = 8
HG_DK = 128
HG_DV = 128
HG_WIDTH = HG_HEADS * HG_DV
HG_CHUNK = 32
N_GROUPS = 4
EXP_PER_GROUP = 8
N_EXPERTS = N_GROUPS * EXP_PER_GROUP
D_EXPERT = 512
DEPTH = 1
DN_ALPHA = (2.0 * DEPTH) ** 0.25
LN_EPS = 1e-5
SCALE = HEAD_DIM ** -0.5
NEG = -1e30
BIG = 1e30

LANES = 128
KV_COLS = 6 * N_KV * HEAD_DIM
NG_COLS = 3 * N_HEADS
OFF_Q = 0
OFF_KV = NSA_WIDTH
OFF_NG = OFF_KV + KV_COLS
OFF_H4 = OFF_NG + NG_COLS
OFF_MG = OFF_H4 + 2 * HG_HEADS * HG_DK + 2 * HG_WIDTH
PROJ_COLS = OFF_MG + 2 * D_MODEL

VMEM_LIMIT = 56 * 1024 * 1024


def _sigmoid(x):
    return 1.0 / (1.0 + jnp.exp(-x))


def _proj_body(x_ref, w_ref, b_ref, *out_refs):
    acc = jnp.dot(x_ref[...], w_ref[...], preferred_element_type=jnp.float32) + b_ref[...]
    for o_ref in out_refs:
        if len(o_ref.shape) == 3:
            for k in range(o_ref.shape[0]):
                o_ref[k] = acc[:, k * HEAD_DIM:(k + 1) * HEAD_DIM].astype(o_ref.dtype)
        else:
            o_ref[...] = acc.astype(o_ref.dtype)


def _proj(x_bf, w_bf, b, outs, tm, tn, name):
    m, k = x_bf.shape
    n = w_bf.shape[1]
    assert m % tm == 0 and n % tn == 0
    out_specs, out_shape = [], []
    for dt, per_head in outs:
        if per_head:
            out_specs.append(pl.BlockSpec((tn // HEAD_DIM, tm, HEAD_DIM), lambda i, j: (j, i, 0)))
            out_shape.append(jax.ShapeDtypeStruct((n // HEAD_DIM, m, HEAD_DIM), dt))
        else:
            out_specs.append(pl.BlockSpec((tm, tn), lambda i, j: (i, j)))
            out_shape.append(jax.ShapeDtypeStruct((m, n), dt))
    return pl.pallas_call(
        _proj_body,
        grid=(m // tm, n // tn),
        in_specs=[pl.BlockSpec((tm, k), lambda i, j: (i, 0)),
                  pl.BlockSpec((k, tn), lambda i, j: (0, j)),
                  pl.BlockSpec((1, tn), lambda i, j: (0, j))],
        out_specs=out_specs,
        out_shape=out_shape,
        compiler_params=pltpu.CompilerParams(dimension_semantics=("parallel", "parallel"),
                                             vmem_limit_bytes=VMEM_LIMIT),
        name=name,
    )(x_bf, w_bf, b.reshape(1, -1))


def _hgrn_body(hq_ref, hf_ref, hi_ref, hg_ref, lb_ref, nrm_ref, s0_ref, o_ref, sfin_ref, st_scr,
               *, chunk, n_valid, n_chunks):
    @pl.when(pl.program_id(2) == 0)
    def _():
        st_scr[...] = s0_ref[0]

    for h in range(st_scr.shape[0]):
        cols = slice(h * HG_DK, (h + 1) * HG_DK)
        o, st = _hgrn_head(hq_ref[:, cols], hf_ref[:, cols], hi_ref[:, cols], hg_ref[:, cols],
                           lb_ref[h], nrm_ref[h], st_scr[h], chunk, n_valid, n_chunks)
        st_scr[h] = st
        sfin_ref[0, h] = st
        o_ref[:, cols] = o.astype(o_ref.dtype)


def _hgrn_head(hq, hf, v, hg, lb, nrm, st, chunk, n_valid, n_chunks):
    rows = chunk * n_chunks
    q = hq * _sigmoid(hq)
    f = lb + (1.0 - lb) * _sigmoid(hf)
    k = 1.0 - f
    lc = jnp.log(f)
    row_in_chunk = lax.broadcasted_iota(jnp.int32, (rows, HG_DK), 0) % chunk
    if n_valid < chunk:
        live = row_in_chunk < n_valid
        q = jnp.where(live, q, 0.0)
        k = jnp.where(live, k, 0.0)
        v = jnp.where(live, v, 0.0)
        lc = jnp.where(live, lc, 0.0)
    bc = lc
    step = 1
    while step < chunk:
        bc = bc + jnp.where(row_in_chunk >= step, pltpu.roll(bc, step, axis=0), 0.0)
        step *= 2
    bc3 = bc.reshape(n_chunks, chunk, HG_DK)
    bl3 = bc3[:, chunk - 1:chunk, :]
    q3 = q.reshape(n_chunks, chunk, HG_DK)
    k3 = k.reshape(n_chunks, chunk, HG_DK)
    v3 = v.reshape(n_chunks, chunk, HG_DV).astype(jnp.bfloat16)
    qe3 = (q3 * jnp.exp(bc3)).astype(jnp.bfloat16)
    ke3 = (k3 * jnp.exp(-bc3)).astype(jnp.bfloat16)
    kd3 = (k3 * jnp.exp(bl3 - bc3)).astype(jnp.bfloat16)
    dec3 = jnp.exp(bl3)
    att = jnp.einsum('ctd,csd->cts', qe3, ke3, preferred_element_type=jnp.float32)
    tri = (lax.broadcasted_iota(jnp.int32, (chunk, chunk), 0)
           >= lax.broadcasted_iota(jnp.int32, (chunk, chunk), 1))
    att = jnp.where(tri[None], att, 0.0).astype(jnp.bfloat16)
    o_intra = jnp.einsum('cts,cse->cte', att, v3, preferred_element_type=jnp.float32)

    outs = []
    for c in range(n_chunks):
        o_c = lax.dot_general(qe3[c], st.astype(jnp.bfloat16), (((1,), (1,)), ((), ())),
                              preferred_element_type=jnp.float32)
        outs.append(o_c + o_intra[c])
        upd = lax.dot_general(v3[c], kd3[c], (((0,), (0,)), ((), ())),
                              preferred_element_type=jnp.float32)
        st = st * dec3[c] + upd
    o = jnp.concatenate(outs, axis=0) if n_chunks > 1 else outs[0]
    o = o * lax.rsqrt(jnp.mean(o * o, axis=-1, keepdims=True) + LN_EPS) * nrm
    return o * (hg * _sigmoid(hg)), st


def _hgrn(h4, row0, n_seq, t_seq, lb, nrm, s0_t, chunk, n_valid, block_rows, heads_per_step, name):
    assert t_seq % block_rows == 0 and block_rows % chunk == 0 and row0 % block_rows == 0
    nb = t_seq // block_rows
    rb0 = row0 // block_rows
    nh = heads_per_step
    hb = HG_HEADS // nh

    def col_spec(seg):
        return pl.BlockSpec((block_rows, nh * HG_DK),
                            lambda b, h, i, seg=seg: (rb0 + b * nb + i, seg * hb + h))

    body = functools.partial(_hgrn_body, chunk=chunk, n_valid=n_valid, n_chunks=block_rows // chunk)
    return pl.pallas_call(
        body,
        grid=(n_seq, hb, nb),
        in_specs=[col_spec(0), col_spec(1), col_spec(2), col_spec(3),
                  pl.BlockSpec((nh, 1, HG_DK), lambda b, h, i: (h, 0, 0)),
                  pl.BlockSpec((nh, 1, HG_DV), lambda b, h, i: (h, 0, 0)),
                  pl.BlockSpec((1, nh, HG_DV, HG_DK), lambda b, h, i: (b, h, 0, 0))],
        out_specs=[pl.BlockSpec((block_rows, nh * HG_DV), lambda b, h, i: (b * nb + i, h)),
                   pl.BlockSpec((1, nh, HG_DV, HG_DK), lambda b, h, i: (b, h, 0, 0))],
        out_shape=[jax.ShapeDtypeStruct((n_seq * t_seq, HG_WIDTH), jnp.bfloat16),
                   jax.ShapeDtypeStruct((n_seq, HG_HEADS, HG_DV, HG_DK), jnp.float32)],
        scratch_shapes=[pltpu.VMEM((nh, HG_DV, HG_DK), jnp.float32)],
        compiler_params=pltpu.CompilerParams(
            dimension_semantics=("parallel", "parallel", "arbitrary"), vmem_limit_bytes=VMEM_LIMIT),
        name=name,
    )(h4, h4, h4, h4, lb.reshape(HG_HEADS, 1, HG_DK), nrm.reshape(HG_HEADS, 1, HG_DV), s0_t)


UV_COLS = 2 * N_KV * 2 * CMP_HID


def _uv_taps(tap, w_ref, c, n):
    acc = jnp.zeros((n, 4 * CMP_HID), jnp.float32)
    for p in range(CMP_STRIDE // 2):
        x2 = jnp.concatenate([tap(2 * p), tap(2 * p + 1)], axis=-1)
        acc = acc + jnp.dot(x2, w_ref[c, p], preferred_element_type=jnp.float32)
    return acc


def _cmp_uv_body(x0_ref, x1_ref, x2_ref, x3_ref, w_ref, uv_ref):
    n = uv_ref.shape[0]
    for cgp, x_ref in enumerate((x0_ref, x1_ref, x2_ref, x3_ref)):
        tap = lambda j, x_ref=x_ref: x_ref[pl.ds(j, n, stride=CMP_STRIDE), :].astype(jnp.bfloat16)
        uv_ref[:, cgp * 512:(cgp + 1) * 512] = _uv_taps(tap, w_ref, cgp // 2, n)


def _cmp_w1_pairs(w_cmp1_l):
    w = jnp.concatenate([w_cmp1_l[:, :CMP_STRIDE], w_cmp1_l[:, CMP_STRIDE:]], axis=-1)
    z = jnp.zeros_like(w)
    top = jnp.concatenate([w, z], axis=-1)
    bot = jnp.concatenate([z, w], axis=-1)
    per_tap = jnp.concatenate([top, bot], axis=-2)
    return per_tap.reshape(2, CMP_STRIDE // 2, 2 * LANES, 4 * CMP_HID).astype(jnp.bfloat16)


def _cmp_uv_rows(kv, n_rows, rows_per_step, w1p):
    n = rows_per_step // CMP_STRIDE
    return pl.pallas_call(
        _cmp_uv_body,
        grid=(n_rows // rows_per_step,),
        in_specs=[pl.BlockSpec((rows_per_step, LANES), lambda i, cb=cb: (i, cb)) for cb in range(4)]
        + [pl.BlockSpec(w1p.shape, lambda i: (0, 0, 0, 0))],
        out_specs=pl.BlockSpec((n, UV_COLS), lambda i: (i, 0)),
        out_shape=jax.ShapeDtypeStruct((n_rows // CMP_STRIDE, UV_COLS), jnp.float32),
        compiler_params=pltpu.CompilerParams(dimension_semantics=("parallel",),
                                             vmem_limit_bytes=VMEM_LIMIT),
        name="cmp_uv_prompt",
    )(kv, kv, kv, kv, w1p)


def _gelu_tanh(x):
    return 0.5 * x * (1.0 + jnp.tanh(0.7978845608028654 * (x + 0.044715 * x * x * x)))


def _cmp_finish_body(uv_ref, pe_ref, w1_ref, w2_ref, out_ref):
    _cmp_finish_rows(lambda cols: uv_ref[0, :, cols], uv_ref.shape[1], pe_ref, w1_ref, w2_ref, out_ref)


def _cmp_finish_rows(uv_cols, n_ch, pe_ref, w1_ref, w2_ref, out_ref):
    for c in range(2):
        pe_term = jnp.dot(pe_ref[c], w1_ref[c], preferred_element_type=jnp.float32)[0:1, :]
        for g in range(N_KV):
            base = (c * N_KV + g) * 2 * CMP_HID
            u = uv_cols(slice(base, base + CMP_HID))
            v = uv_cols(slice(base + CMP_HID, base + 2 * CMP_HID))
            pre = u + pltpu.roll(v, n_ch - 1, axis=0) + pe_term
            hid = _gelu_tanh(pre).astype(jnp.bfloat16)
            res = jnp.dot(hid, w2_ref[c], preferred_element_type=jnp.float32).astype(out_ref.dtype)
            if out_ref.shape[1] == 2:
                out_ref[0, c, :, g * HEAD_DIM:(g + 1) * HEAD_DIM] = res
            else:
                out_ref[0, c * N_KV + g] = res


def _cmp_finish(uv, n_seq, n_ch, w_cmp1_l, w_cmp2_l, cmp_pe_l, name):
    pe = jnp.zeros((2, 8, CMP_LEN * HEAD_DIM), jnp.float32).at[:, 0].set(cmp_pe_l.reshape(2, -1))
    out_dims = (2 * N_KV, n_ch, HEAD_DIM)
    return pl.pallas_call(
        _cmp_finish_body,
        grid=(n_seq,),
        in_specs=[pl.BlockSpec((1, n_ch, UV_COLS), lambda b: (b, 0, 0)),
                  pl.BlockSpec((2, 8, CMP_LEN * HEAD_DIM), lambda b: (0, 0, 0)),
                  pl.BlockSpec((2, CMP_LEN * HEAD_DIM, CMP_HID), lambda b: (0, 0, 0)),
                  pl.BlockSpec((2, CMP_HID, HEAD_DIM), lambda b: (0, 0, 0))],
        out_specs=pl.BlockSpec((1,) + out_dims, lambda b: (b, 0, 0, 0)),
        out_shape=jax.ShapeDtypeStruct((n_seq,) + out_dims, jnp.bfloat16),
        compiler_params=pltpu.CompilerParams(dimension_semantics=("parallel",),
                                             vmem_limit_bytes=VMEM_LIMIT),
        name=name,
    )(uv.reshape(n_seq, n_ch, UV_COLS), pe.astype(jnp.bfloat16),
      w_cmp1_l.reshape(2, CMP_LEN * HEAD_DIM, CMP_HID).astype(jnp.bfloat16),
      w_cmp2_l.astype(jnp.bfloat16))


SLC_TK = 512
WIN_KEYS = WINDOW + QBLK


def _masked_softmax(s, valid):
    s = jnp.where(valid, s, NEG)
    m = jnp.max(s, axis=-1, keepdims=True)
    e = jnp.where(valid, jnp.exp(s - m), 0.0)
    return e / jnp.maximum(jnp.sum(e, axis=-1, keepdims=True), 1e-30)


def _split_bf16(x):
    hi = x.astype(jnp.bfloat16)
    return hi, (x - hi.astype(jnp.float32)).astype(jnp.bfloat16)


SUBLANES = 8


def _top_blocks_t(imp_t):
    nb, nq = imp_t.shape
    groups = [imp_t[SUBLANES * v:SUBLANES * (v + 1), :] for v in range(nb // SUBLANES)]
    sub = lax.broadcasted_iota(jnp.int32, (SUBLANES, nq), 0)
    beaten_by = [jnp.zeros((SUBLANES, nq), jnp.float32) for _ in groups]
    for i in range(nb):
        row = jnp.broadcast_to(imp_t[i:i + 1, :], (SUBLANES, nq))
        for v, gv in enumerate(groups):
            if v > i // SUBLANES:
                beats = row >= gv
            elif v < i // SUBLANES:
                beats = row > gv
            else:
                beats = (row > gv) | ((row == gv) & (sub > i % SUBLANES))
            beaten_by[v] = beaten_by[v] + jnp.where(beats, 1.0, 0.0)
    return jnp.concatenate([jnp.where(c < N_SEL, 1.0, 0.0) for c in beaten_by], axis=0)


def _split3_bf16(x):
    x1 = x.astype(jnp.bfloat16)
    r = x - x1.astype(jnp.float32)
    x2 = r.astype(jnp.bfloat16)
    return x1, x2, (r - x2.astype(jnp.float32)).astype(jnp.bfloat16)


XK = 2 * LANES


def _slc_key_columns(t):
    kpos = np.arange(t)
    cols = np.zeros((t, XK - HEAD_DIM), np.float32)
    cols[kpos, kpos // SEL_BLOCK] = 1.0
    cols[:, 64:67] = (kpos // 64)[:, None]
    cols[:, 67:70] = (kpos % 64)[:, None]
    cols[:, 70:73] = 1.0
    return cols


def _slc_query_columns():
    s1, s2, s3 = _split3_bf16(_alibi_slopes().reshape(N_KV, HPG, 1))
    cols = jnp.concatenate([64.0 * s1, 64.0 * s2, 64.0 * s3, s1, s2, s3], axis=-1)
    cols = jnp.pad(cols, ((0, 0), (0, 0), (0, LANES - 6)))
    return jnp.broadcast_to(cols[:, :, None, :], (N_KV, HPG, QBLK, LANES)).reshape(N_KV, HPG * QBLK, LANES)


def _nsa_prompt_body(q_ref, kc_ref, vc_ref, kx_ref, vs_ref, kw_ref, vw_ref, ng_ref, ov_ref, sx_ref, cb_ref, sl_ref,
                     o_ref, *, n_cmp):
    i = pl.program_id(2)
    nq = QBLK
    rows = HPG * nq
    q2 = (q_ref[...] * SCALE).reshape(rows, HEAD_DIM)
    slopes = sl_ref[0]
    qpos = i * nq + lax.broadcasted_iota(jnp.int32, (nq, 1), 0)
    nt = (((1,), (1,)), ((), ()))

    n_ch = kc_ref.shape[2]
    s = lax.dot_general(q2, kc_ref[0, 0], nt, preferred_element_type=jnp.float32).reshape(HPG, nq, n_ch)
    cidx = lax.broadcasted_iota(jnp.int32, (1, n_ch), 1)
    dist_c = qpos - (cidx * CMP_STRIDE + (CMP_LEN - 1))
    valid_c = ((dist_c >= 0) & (cidx < n_cmp))[None]
    p = _masked_softmax(s - slopes * dist_c.astype(jnp.float32)[None], valid_c)
    o_cmp = jnp.dot(p.reshape(rows, n_ch).astype(jnp.bfloat16), vc_ref[0, 0],
                    preferred_element_type=jnp.float32).reshape(HPG, nq, HEAD_DIM)

    w0 = pl.multiple_of(jnp.maximum(i - WINDOW // QBLK, 0) * nq, nq)
    s = lax.dot_general(q2, kw_ref[0, pl.ds(w0, WIN_KEYS), :], nt,
                        preferred_element_type=jnp.float32).reshape(HPG, nq, WIN_KEYS)
    dist = qpos - (w0 + lax.broadcasted_iota(jnp.int32, (1, WIN_KEYS), 1))
    valid = ((dist >= 0) & (dist < WINDOW))[None]
    pw = _masked_softmax(s - slopes * dist.astype(jnp.float32)[None], valid)
    o_win = jnp.dot(pw.reshape(rows, WIN_KEYS).astype(jnp.bfloat16), vw_ref[0, pl.ds(w0, WIN_KEYS), :],
                    preferred_element_type=jnp.float32).reshape(HPG, nq, HEAD_DIM)

    p_hi, p_lo = _split_bf16(p[0] + p[1] + p[2] + p[3])
    imp_t = (lax.dot_general(ov_ref[...], p_hi, nt, preferred_element_type=jnp.float32)
             + lax.dot_general(ov_ref[...], p_lo, nt, preferred_element_type=jnp.float32))
    nsb = imp_t.shape[0]
    blk = lax.broadcasted_iota(jnp.int32, (nsb, 1), 0)
    cur = (i * nq + lax.broadcasted_iota(jnp.int32, (1, nq), 1)) // SEL_BLOCK
    imp_t = jnp.where(blk > cur, NEG, imp_t)
    imp_t = jnp.where((blk == cur) | (blk == 0), BIG, imp_t)
    sel_t = jnp.where(blk <= cur, _top_blocks_t(imp_t), 0.0)

    eye = (lax.broadcasted_iota(jnp.int32, (nsb, nsb), 0)
           == lax.broadcasted_iota(jnp.int32, (nsb, nsb), 1)).astype(jnp.bfloat16)
    drop = lax.dot_general(((sel_t - 1.0) * BIG).astype(jnp.bfloat16), eye, (((0,), (0,)), ((), ())),
                           preferred_element_type=jnp.float32)
    drop = jnp.concatenate([drop.astype(jnp.bfloat16)] * HPG, axis=0)
    slope_row = jnp.broadcast_to(slopes, (HPG, nq, 1)).reshape(rows, 1)
    qpos_row = jnp.concatenate([qpos] * HPG, axis=0).astype(jnp.float32)
    c1, c2, c3 = _split3_bf16(slope_row * qpos_row)
    lane = lax.broadcasted_iota(jnp.int32, (rows, LANES), 1)
    pos_cols = sx_ref[0].astype(jnp.float32)
    for k, ck in enumerate((c1, c2, c3)):
        pos_cols = jnp.where(lane == 6 + k, -ck.astype(jnp.float32), pos_cols)
    q_ext = jnp.concatenate([q2, drop, pos_cols.astype(jnp.bfloat16)], axis=1)

    def slc_tile(kt, carry, causal_bias):
        m, l, acc = carry
        k0 = pl.multiple_of(kt * SLC_TK, SLC_TK)
        s = lax.dot_general(q_ext, kx_ref[0, pl.ds(k0, SLC_TK), :], nt,
                            preferred_element_type=jnp.float32).reshape(HPG, nq, SLC_TK)
        if causal_bias is not None:
            s = s + causal_bias[None]
        m_new = jnp.maximum(m, jnp.max(s, axis=-1, keepdims=True))
        a = jnp.exp(m - m_new)
        e = jnp.exp(s - m_new)
        l = a * l + jnp.sum(e, axis=-1, keepdims=True)
        pv = jnp.dot(e.reshape(rows, SLC_TK).astype(jnp.bfloat16), vs_ref[0, pl.ds(k0, SLC_TK), :],
                     preferred_element_type=jnp.float32).reshape(HPG, nq, HEAD_DIM)
        return m_new, l, a * acc + pv

    init = (jnp.full((HPG, nq, 1), NEG, jnp.float32), jnp.zeros((HPG, nq, 1), jnp.float32),
            jnp.zeros((HPG, nq, HEAD_DIM), jnp.float32))
    last = (i * nq) // SLC_TK
    carry = lax.fori_loop(0, last // 2, lambda p, c: slc_tile(2 * p + 1, slc_tile(2 * p, c, None), None), init)
    carry = lax.fori_loop(2 * (last // 2), last, lambda kt, c: slc_tile(kt, c, None), carry)
    _, l, acc = slc_tile(last, carry, cb_ref[i % (SLC_TK // QBLK)])
    o_slc = acc / jnp.maximum(l, 1e-30)

    gates = _sigmoid(ng_ref[0, 0])
    o_ref[...] = jnp.concatenate(
        [gates[:, h:h + 1] * o_cmp[h] + gates[:, HPG + h:HPG + h + 1] * o_slc[h]
         + gates[:, 2 * HPG + h:2 * HPG + h + 1] * o_win[h] for h in range(HPG)], axis=-1).astype(o_ref.dtype)


def _alibi_slopes():
    return jnp.asarray(2.0 ** (-8.0 * np.arange(1, N_HEADS + 1) / N_HEADS), jnp.float32).reshape(N_KV, HPG, 1, 1)


def _nsa_prompt(q_h, kcvc, kv_h, ng_t, n_cmp):
    bsz, _, t, _ = ng_t.shape
    nqb = t // QBLK
    n_ch = kcvc.shape[2]
    nsb = t // SEL_BLOCK
    st = np.arange(n_ch) * CMP_STRIDE
    bs = np.arange(nsb) * SEL_BLOCK
    overlap = ((st[:, None] <= bs[None, :] + SEL_BLOCK - 1) & (st[:, None] + CMP_LEN - 1 >= bs[None, :])
               & (np.arange(n_ch)[:, None] < n_cmp)).astype(np.float32)
    assert nsb == LANES - HEAD_DIM and t % SLC_TK == 0
    key_cols = jnp.asarray(np.tile(_slc_key_columns(t), (bsz, 1)), jnp.bfloat16)
    kx = jnp.concatenate([kv_h[2 * N_KV:3 * N_KV, :bsz * t],
                          jnp.broadcast_to(key_cols[None], (N_KV,) + key_cols.shape)], axis=-1)
    r = np.arange(SLC_TK // QBLK)[:, None, None] * QBLK + np.arange(QBLK)[None, :, None]
    causal = np.where(np.arange(SLC_TK)[None, None, :] <= r, 0.0, NEG).astype(np.float32)

    def kv_spec(slot):
        return pl.BlockSpec((1, t, HEAD_DIM), lambda b, g, i, slot=slot: (slot * N_KV + g, b, 0))

    return pl.pallas_call(
        functools.partial(_nsa_prompt_body, n_cmp=n_cmp),
        grid=(bsz, N_KV, nqb),
        in_specs=[pl.BlockSpec((HPG, QBLK, HEAD_DIM), lambda b, g, i: (g, b * nqb + i, 0)),
                  pl.BlockSpec((1, 1, n_ch, HEAD_DIM), lambda b, g, i: (b, g, 0, 0)),
                  pl.BlockSpec((1, 1, n_ch, HEAD_DIM), lambda b, g, i: (b, N_KV + g, 0, 0)),
                  pl.BlockSpec((1, t, XK), lambda b, g, i: (g, b, 0)),
                  kv_spec(3), kv_spec(4), kv_spec(5),
                  pl.BlockSpec((1, 1, QBLK, 3 * HPG), lambda b, g, i: (b, g, i, 0)),
                  pl.BlockSpec((nsb, n_ch), lambda b, g, i: (0, 0)),
                  pl.BlockSpec((1, HPG * QBLK, LANES), lambda b, g, i: (g, 0, 0)),
                  pl.BlockSpec(causal.shape, lambda b, g, i: (0, 0, 0)),
                  pl.BlockSpec((1, HPG, 1, 1), lambda b, g, i: (g, 0, 0, 0))],
        out_specs=pl.BlockSpec((QBLK, HPG * HEAD_DIM), lambda b, g, i: (b * nqb + i, g)),
        out_shape=jax.ShapeDtypeStruct((bsz * t, NSA_WIDTH), jnp.bfloat16),
        compiler_params=pltpu.CompilerParams(
            dimension_semantics=("parallel", "parallel", "arbitrary"), vmem_limit_bytes=VMEM_LIMIT),
        name="nsa_prompt",
    )(q_h, kcvc, kcvc, kx, kv_h, kv_h, kv_h, ng_t,
      jnp.asarray(overlap.T, jnp.bfloat16), _slc_query_columns(), jnp.asarray(causal), _alibi_slopes())


HALF_ROWS = 2 * N_KV * HEAD_DIM
GD = N_KV * HEAD_DIM


def _fetch_pages(pt_ref, cache_ref, sem, row0, per_step, dst_of):
    b = pl.program_id(0)
    slot = b % 2
    n_pages = pt_ref.shape[1]

    def page_copy(flat_page, sl, k):
        phys = pt_ref[flat_page // n_pages, flat_page % n_pages]
        return pltpu.make_async_copy(cache_ref.at[phys, pl.ds(row0, HALF_ROWS), :], dst_of(sl, k), sem.at[sl])

    def start_all(step, sl):
        def one(k, carry):
            page_copy(step * per_step + k, sl, k).start()
            return carry
        lax.fori_loop(0, per_step, one, 0)

    @pl.when(b == 0)
    def _():
        start_all(0, 0)

    @pl.when(b + 1 < pl.num_programs(0))
    def _():
        start_all(b + 1, 1 - slot)

    def wait_one(k, carry):
        page_copy(b * per_step + k, slot, k).wait()
        return carry
    lax.fori_loop(0, per_step, wait_one, 0)
    return slot


def _cmp_sample_body(pt_ref, cache_ref, w_ref, pe_ref, w1_ref, w2_ref, out_ref, buf, sem, xt, uv, *, steps):
    per_step = buf.shape[1]
    slot = _fetch_pages(pt_ref, cache_ref, sem, 0, per_step, lambda sl, k: buf.at[sl, k])
    n_cb = HALF_ROWS // LANES

    eye = (lax.broadcasted_iota(jnp.int32, (LANES, LANES), 0)
           == lax.broadcasted_iota(jnp.int32, (LANES, LANES), 1)).astype(jnp.bfloat16)

    def to_rows(k, carry):
        for cb in range(n_cb):
            blk = buf[slot, k, cb * LANES:(cb + 1) * LANES, :].astype(jnp.bfloat16)
            xt[k, cb] = lax.dot_general(eye, blk, (((1,), (1,)), ((), ())), preferred_element_type=jnp.float32)
        return carry
    lax.fori_loop(0, per_step, to_rows, 0, unroll=8)

    per_page = PAGE_SIZE // CMP_STRIDE
    n = per_step * per_page

    def tap(cgp, j):
        return xt[:, cgp, pl.ds(j, per_page, stride=CMP_STRIDE), :].reshape(n, LANES).astype(jnp.bfloat16)

    part = pl.program_id(0) % steps
    row0 = pl.multiple_of(part * n, n)
    for cgp in range(n_cb):
        uv[pl.ds(row0, n), cgp * 512:(cgp + 1) * 512] = _uv_taps(functools.partial(tap, cgp), w_ref, cgp // 2, n)

    @pl.when(part == steps - 1)
    def _():
        _cmp_finish_rows(lambda cols: uv[:, cols], uv.shape[0], pe_ref, w1_ref, w2_ref, out_ref)


UV_PAGES = 32


def _cmp_sample(page_table, cache_t, w1p, w_cmp1_l, w_cmp2_l, cmp_pe_l):
    n_seq, n_pages = page_table.shape
    assert n_pages % UV_PAGES == 0
    steps = n_pages // UV_PAGES
    n_ch = n_pages * (PAGE_SIZE // CMP_STRIDE)
    pe = jnp.zeros((2, 8, CMP_LEN * HEAD_DIM), jnp.float32).at[:, 0].set(cmp_pe_l.reshape(2, -1))
    return pl.pallas_call(
        functools.partial(_cmp_sample_body, steps=steps),
        grid_spec=pltpu.PrefetchScalarGridSpec(
            num_scalar_prefetch=1, grid=(n_seq * steps,),
            in_specs=[pl.BlockSpec(memory_space=pl.ANY),
                      pl.BlockSpec(w1p.shape, lambda b, pt: (0, 0, 0, 0)),
                      pl.BlockSpec((2, 8, CMP_LEN * HEAD_DIM), lambda b, pt: (0, 0, 0)),
                      pl.BlockSpec((2, CMP_LEN * HEAD_DIM, CMP_HID), lambda b, pt: (0, 0, 0)),
                      pl.BlockSpec((2, CMP_HID, HEAD_DIM), lambda b, pt: (0, 0, 0))],
            out_specs=pl.BlockSpec((1, 2, n_ch, GD), lambda b, pt: (b // steps, 0, 0, 0)),
            scratch_shapes=[pltpu.VMEM((2, UV_PAGES, HALF_ROWS, PAGE_SIZE), jnp.float32),
                            pltpu.SemaphoreType.DMA((2,)),
                            pltpu.VMEM((UV_PAGES, HALF_ROWS // LANES, PAGE_SIZE, LANES), jnp.float32),
                            pltpu.VMEM((n_ch, UV_COLS), jnp.float32)]),
        out_shape=jax.ShapeDtypeStruct((n_seq, 2, n_ch, GD), jnp.bfloat16),
        compiler_params=pltpu.CompilerParams(dimension_semantics=("arbitrary",),
                                             vmem_limit_bytes=VMEM_LIMIT),
        name="cmp_sample",
    )(page_table, cache_t, w1p, pe.astype(jnp.bfloat16),
      w_cmp1_l.reshape(2, CMP_LEN * HEAD_DIM, CMP_HID).astype(jnp.bfloat16), w_cmp2_l.astype(jnp.bfloat16))


def _group_diag(x, rows_per_group):
    grp = lax.broadcasted_iota(jnp.int32, (x.shape[0], 1), 0) // rows_per_group
    out = jnp.zeros((x.shape[0], HEAD_DIM), x.dtype)
    for g in range(N_KV):
        out = out + jnp.where(grp == g, x[:, g * HEAD_DIM:(g + 1) * HEAD_DIM], 0.0)
    return out


BIAS_POS_ROWS = 16


def _sample_bias_rows(n_lanes, past_len):
    kpos = np.arange(past_len)
    rows = np.zeros((n_lanes + BIAS_POS_ROWS, past_len), np.float32)
    rows[kpos // SEL_BLOCK, kpos] = 1.0
    rows[n_lanes:n_lanes + 3] = kpos // 64
    rows[n_lanes + 3:n_lanes + 6] = kpos % 64
    rows[n_lanes + 6:n_lanes + 9] = 1.0
    return rows


def _nsa_sample_body(pt_ref, cache_ref, q_ref, kcvc_ref, new_ref, win_ref, ng_ref, ov_ref, same_q_ref, sl_ref,
                     bq_ref, bk_ref, o_ref, buf, sem, *, n_cmp, t_new):
    past_len = buf.shape[2]
    slot = _fetch_pages(pt_ref, cache_ref, sem, HALF_ROWS, past_len // PAGE_SIZE,
                        lambda sl, k: buf.at[sl, :, pl.ds(pl.multiple_of(k * PAGE_SIZE, PAGE_SIZE), PAGE_SIZE)])
    rows = q_ref.shape[1]
    rpg = rows // N_KV
    nt = (((1,), (1,)), ((), ()))
    qbd = q_ref[0] * SCALE
    slope = sl_ref[...]
    row = lax.broadcasted_iota(jnp.int32, (rows, 1), 0)
    qtok = row % t_new
    qpos = past_len + qtok
    new_rows = new_ref.shape[2]
    new_idx = lax.broadcasted_iota(jnp.int32, (1, new_rows), 1)
    new_valid = (new_idx <= qtok) & (new_idx < t_new)
    new_bias = slope * (qtok - new_idx).astype(jnp.float32)

    n_ch = kcvc_ref.shape[2]
    s = lax.dot_general(qbd, kcvc_ref[0, 0], nt, preferred_element_type=jnp.float32)
    cidx = lax.broadcasted_iota(jnp.int32, (1, n_ch), 1)
    dist_c = qpos - (cidx * CMP_STRIDE + (CMP_LEN - 1))
    p = _masked_softmax(s - slope * dist_c.astype(jnp.float32), (dist_c >= 0) & (cidx < n_cmp))
    o_cmp = _group_diag(jnp.dot(p.astype(jnp.bfloat16), kcvc_ref[0, 1], preferred_element_type=jnp.float32), rpg)

    p_hi, p_lo = _split_bf16(p)
    psum = (jnp.dot(same_q_ref[...], p_hi, preferred_element_type=jnp.float32)
            + jnp.dot(same_q_ref[...], p_lo, preferred_element_type=jnp.float32))
    ps_hi, ps_lo = _split_bf16(psum)
    imp_t = (lax.dot_general(ov_ref[...], ps_hi, nt, preferred_element_type=jnp.float32)
             + lax.dot_general(ov_ref[...], ps_lo, nt, preferred_element_type=jnp.float32))
    n_lanes = imp_t.shape[0]
    blk = lax.broadcasted_iota(jnp.int32, (n_lanes, 1), 0)
    cur = (past_len + lax.broadcasted_iota(jnp.int32, (1, rows), 1) % t_new) // SEL_BLOCK
    imp_t = jnp.where(blk > cur, -jnp.inf, imp_t)
    imp_t = jnp.where((blk == cur) | (blk == 0), BIG, imp_t)
    nsb_pad = -(-(-(-(past_len + t_new) // SEL_BLOCK)) // SUBLANES) * SUBLANES
    sel_t = jnp.concatenate([_top_blocks_t(imp_t[:nsb_pad]),
                             jnp.zeros((n_lanes - nsb_pad, rows), jnp.float32)], axis=0)
    eye = (lax.broadcasted_iota(jnp.int32, (n_lanes, n_lanes), 0)
           == lax.broadcasted_iota(jnp.int32, (n_lanes, n_lanes), 1)).astype(jnp.bfloat16)
    drop = lax.dot_general(((sel_t - 1.0) * BIG).astype(jnp.bfloat16), eye, (((0,), (0,)), ((), ())),
                           preferred_element_type=jnp.float32)

    c1, c2, c3 = _split3_bf16(slope * qpos.astype(jnp.float32))
    lane = lax.broadcasted_iota(jnp.int32, (rows, bq_ref.shape[1]), 1)
    pos_cols = bq_ref[...].astype(jnp.float32)
    for k, ck in enumerate((c1, c2, c3)):
        pos_cols = jnp.where(lane == 6 + k, -ck.astype(jnp.float32), pos_cols)
    bias_q = jnp.concatenate([drop.astype(jnp.bfloat16), pos_cols.astype(jnp.bfloat16)], axis=1)
    s_all = (jnp.dot(qbd, buf[slot, 0:GD, :].astype(jnp.bfloat16), preferred_element_type=jnp.float32)
             + jnp.dot(bias_q, bk_ref[...], preferred_element_type=jnp.float32))
    s_new = lax.dot_general(qbd, new_ref[0, 2], nt, preferred_element_type=jnp.float32)
    s_new = jnp.where(new_valid, s_new - new_bias, NEG)
    m = jnp.maximum(jnp.max(s_all, axis=-1, keepdims=True), jnp.max(s_new, axis=-1, keepdims=True))
    e_all = jnp.exp(s_all - m)
    e_new = jnp.where(new_valid, jnp.exp(s_new - m), 0.0)
    l = jnp.sum(e_all, axis=-1, keepdims=True) + jnp.sum(e_new, axis=-1, keepdims=True)
    acc = (lax.dot_general(e_all.astype(jnp.bfloat16), buf[slot, GD:2 * GD, :].astype(jnp.bfloat16), nt,
                           preferred_element_type=jnp.float32)
           + jnp.dot(e_new.astype(jnp.bfloat16), new_ref[0, 3], preferred_element_type=jnp.float32))
    o_slc = _group_diag(acc, rpg) / jnp.maximum(l, 1e-30)

    w_buf = win_ref.shape[2]
    wdist = qpos - (past_len - w_buf + lax.broadcasted_iota(jnp.int32, (1, w_buf), 1))
    valid_w = (wdist >= 0) & (wdist < WINDOW)
    s_w = jnp.dot(qbd, win_ref[0, 0:GD, :].astype(jnp.bfloat16), preferred_element_type=jnp.float32)
    s_w = jnp.where(valid_w, s_w - slope * wdist.astype(jnp.float32), NEG)
    s_wn = lax.dot_general(qbd, new_ref[0, 4], nt, preferred_element_type=jnp.float32)
    s_wn = jnp.where(new_valid, s_wn - new_bias, NEG)
    m = jnp.maximum(jnp.max(s_w, axis=-1, keepdims=True), jnp.max(s_wn, axis=-1, keepdims=True))
    e_w = jnp.where(valid_w, jnp.exp(s_w - m), 0.0)
    e_wn = jnp.where(new_valid, jnp.exp(s_wn - m), 0.0)
    acc = (lax.dot_general(e_w.astype(jnp.bfloat16), win_ref[0, GD:2 * GD, :].astype(jnp.bfloat16), nt,
                           preferred_element_type=jnp.float32)
           + jnp.dot(e_wn.astype(jnp.bfloat16), new_ref[0, 5], preferred_element_type=jnp.float32))
    l = jnp.sum(e_w, axis=-1, keepdims=True) + jnp.sum(e_wn, axis=-1, keepdims=True)
    o_win = _group_diag(acc, rpg) / jnp.maximum(l, 1e-30)

    gates = _sigmoid(ng_ref[0])
    o_ref[0] = (gates[:, 0:1] * o_cmp + gates[:, 1:2] * o_slc + gates[:, 2:3] * o_win).astype(o_ref.dtype)


def _nsa_sample(page_table, cache_t, q_bd, kcvc, new_kv, win_t, ng_r, n_cmp, t_new):
    n_seq, n_pages = page_table.shape
    past_len = n_pages * PAGE_SIZE
    n_ch = kcvc.shape[2]
    rows = q_bd.shape[1]
    rpg = rows // N_KV
    nsb = -(-(past_len + t_new) // SEL_BLOCK)
    n_lanes = -(-nsb // LANES) * LANES
    st = np.arange(n_ch) * CMP_STRIDE
    bs = np.arange(n_lanes) * SEL_BLOCK
    overlap = ((st[:, None] <= bs[None, :] + SEL_BLOCK - 1) & (st[:, None] + CMP_LEN - 1 >= bs[None, :])
               & (np.arange(n_ch)[:, None] < n_cmp) & (np.arange(n_lanes)[None, :] < nsb)).astype(np.float32)
    r = np.arange(rows)
    same_q = ((r[:, None] // rpg == r[None, :] // rpg) & (r[:, None] % t_new == r[None, :] % t_new)
              ).astype(np.float32)
    slopes = jnp.repeat(_alibi_slopes().reshape(N_HEADS), t_new).reshape(rows, 1)
    s1, s2, s3 = _split3_bf16(slopes)
    bias_q = jnp.pad(jnp.concatenate([64.0 * s1, 64.0 * s2, 64.0 * s3, s1, s2, s3], axis=1),
                     ((0, 0), (0, BIAS_POS_ROWS - 6)))
    bias_k = jnp.asarray(_sample_bias_rows(n_lanes, past_len), jnp.bfloat16)
    w_buf = win_t.shape[2]
    return pl.pallas_call(
        functools.partial(_nsa_sample_body, n_cmp=n_cmp, t_new=t_new),
        grid_spec=pltpu.PrefetchScalarGridSpec(
            num_scalar_prefetch=1, grid=(n_seq,),
            in_specs=[pl.BlockSpec(memory_space=pl.ANY),
                      pl.BlockSpec((1, rows, GD), lambda b, pt: (b, 0, 0)),
                      pl.BlockSpec((1, 2, n_ch, GD), lambda b, pt: (b, 0, 0, 0)),
                      pl.BlockSpec((1, 6, new_kv.shape[2], GD), lambda b, pt: (b, 0, 0, 0)),
                      pl.BlockSpec((1, HALF_ROWS, w_buf), lambda b, pt: (b, 0, 0)),
                      pl.BlockSpec((1, rows, 3), lambda b, pt: (b, 0, 0)),
                      pl.BlockSpec((n_lanes, n_ch), lambda b, pt: (0, 0)),
                      pl.BlockSpec((rows, rows), lambda b, pt: (0, 0)),
                      pl.BlockSpec((rows, 1), lambda b, pt: (0, 0)),
                      pl.BlockSpec(bias_q.shape, lambda b, pt: (0, 0)),
                      pl.BlockSpec(bias_k.shape, lambda b, pt: (0, 0), pipeline_mode=pl.Buffered(1))],
            out_specs=pl.BlockSpec((1, rows, HEAD_DIM), lambda b, pt: (b, 0, 0)),
            scratch_shapes=[pltpu.VMEM((2, HALF_ROWS, past_len), jnp.float32),
                            pltpu.SemaphoreType.DMA((2,))]),
        out_shape=jax.ShapeDtypeStruct((n_seq, rows, HEAD_DIM), jnp.bfloat16),
        compiler_params=pltpu.CompilerParams(dimension_semantics=("arbitrary",),
                                             vmem_limit_bytes=VMEM_LIMIT),
        name="nsa_sample",
    )(page_table, cache_t, q_bd, kcvc, new_kv, win_t, ng_r,
      jnp.asarray(overlap.T, jnp.bfloat16), jnp.asarray(same_q, jnp.bfloat16), slopes, bias_q, bias_k)


def _mix_body(a_ref, b_ref, mga_ref, mgb_ref, wpa_ref, wpb_ref, u_ref):
    a = jnp.dot(a_ref[...], wpa_ref[...], preferred_element_type=jnp.float32)
    b = jnp.dot(b_ref[...], wpb_ref[...], preferred_element_type=jnp.float32)
    u_ref[...] = (_sigmoid(mga_ref[...]) * a + _sigmoid(mgb_ref[...]) * b).astype(u_ref.dtype)


def _mix(o_nsa, o_hg, mg, w_pa_bf, w_pb_bf, tm):
    n = o_nsa.shape[0]
    const = lambda i: (0, 0)
    return pl.pallas_call(
        _mix_body,
        grid=(n // tm,),
        in_specs=[pl.BlockSpec((tm, NSA_WIDTH), lambda i: (i, 0)),
                  pl.BlockSpec((tm, HG_WIDTH), lambda i: (i, 0)),
                  pl.BlockSpec((tm, D_MODEL), lambda i: (i, 0)),
                  pl.BlockSpec((tm, D_MODEL), lambda i: (i, 1)),
                  pl.BlockSpec((NSA_WIDTH, D_MODEL), const, pipeline_mode=pl.Buffered(1)),
                  pl.BlockSpec((HG_WIDTH, D_MODEL), const, pipeline_mode=pl.Buffered(1))],
        out_specs=pl.BlockSpec((tm, D_MODEL), lambda i: (i, 0)),
        out_shape=jax.ShapeDtypeStruct((n, D_MODEL), jnp.bfloat16),
        compiler_params=pltpu.CompilerParams(dimension_semantics=("parallel",),
                                             vmem_limit_bytes=VMEM_LIMIT),
        name="tail_mix",
    )(o_nsa, o_hg, mg, mg, w_pa_bf, w_pb_bf)


def _layer_norm(z, g, b):
    mu = jnp.mean(z, axis=-1, keepdims=True)
    zc = z - mu
    var = jnp.mean(zc * zc, axis=-1, keepdims=True)
    return zc * lax.rsqrt(var + LN_EPS) * g + b


CH = D_MODEL // LANES


def _store_chunked(ref, val):
    tm = val.shape[0]
    for k in range(CH):
        ref[pl.ds(k, tm, stride=CH), :] = val[:, k * LANES:(k + 1) * LANES]


def _load_chunked(ref, tm, lead=()):
    return jnp.concatenate([ref[lead + (pl.ds(k, tm, stride=CH), slice(None))] for k in range(CH)], axis=1)


def _ln1_body(u_ref, x_ref, wout_ref, g_ref, b_ref, wr_hi_ref, wr_lo_ref, br_ref, h_ref, lg_ref):
    y = jnp.dot(u_ref[...], wout_ref[...], preferred_element_type=jnp.float32)
    h = _layer_norm(DN_ALPHA * x_ref[...] + y, g_ref[...], b_ref[...])
    _store_chunked(h_ref, h)
    h_hi = h.astype(jnp.bfloat16)
    h_lo = (h - h_hi.astype(jnp.float32)).astype(jnp.bfloat16)
    lg = jnp.dot(h_hi, wr_hi_ref[...], preferred_element_type=jnp.float32)
    lg = lg + jnp.dot(h_lo, wr_hi_ref[...], preferred_element_type=jnp.float32)
    lg = lg + jnp.dot(h_hi, wr_lo_ref[...], preferred_element_type=jnp.float32)
    lg_ref[...] = lg + br_ref[...]


def _ln1(u, x_all, w_out_bf, g, b, wr_hi, wr_lo, br, tm):
    n = u.shape[0]
    const = lambda i: (0, 0)
    return pl.pallas_call(
        _ln1_body,
        grid=(n // tm,),
        in_specs=[pl.BlockSpec((tm, D_MODEL), lambda i: (i, 0)),
                  pl.BlockSpec((tm, D_MODEL), lambda i: (i, 0)),
                  pl.BlockSpec((D_MODEL, D_MODEL), const, pipeline_mode=pl.Buffered(1)),
                  pl.BlockSpec((1, D_MODEL), const),
                  pl.BlockSpec((1, D_MODEL), const),
                  pl.BlockSpec((D_MODEL, LANES), const),
                  pl.BlockSpec((D_MODEL, LANES), const),
                  pl.BlockSpec((1, LANES), const)],
        out_specs=[pl.BlockSpec((tm * CH, LANES), lambda i: (i, 0)),
                   pl.BlockSpec((tm, LANES), lambda i: (i, 0))],
        out_shape=[jax.ShapeDtypeStruct((n * CH, LANES), jnp.float32),
                   jax.ShapeDtypeStruct((n, LANES), jnp.float32)],
        compiler_params=pltpu.CompilerParams(dimension_semantics=("parallel",),
                                             vmem_limit_bytes=VMEM_LIMIT),
        name="tail_ln1",
    )(u, x_all, w_out_bf, g.reshape(1, -1), b.reshape(1, -1), wr_hi, wr_lo, br)


LG0 = N_GROUPS


def _route_body(lg_ref, tri_ref, out_ref, cnt_ref, carry_scr):
    @pl.when(pl.program_id(0) == 0)
    def _():
        carry_scr[...] = jnp.zeros_like(carry_scr)

    lg = lg_ref[...]
    tm = lg.shape[0]
    lane = lax.broadcasted_iota(jnp.int32, lg.shape, 1)
    is_g = lane < N_GROUPS
    gl = jnp.where(is_g, lg, NEG)
    gmax = jnp.max(gl, axis=-1, keepdims=True)
    grp = jnp.min(jnp.where(gl == gmax, lane, LANES), axis=-1, keepdims=True)
    g_w = 1.0 / jnp.sum(jnp.where(is_g, jnp.exp(lg - gmax), 0.0), axis=-1, keepdims=True)
    lo = LG0 + grp * EXP_PER_GROUP
    el = jnp.where((lane >= lo) & (lane < lo + EXP_PER_GROUP), lg, NEG)
    v1 = jnp.max(el, axis=-1, keepdims=True)
    i1 = jnp.min(jnp.where(el == v1, lane, LANES), axis=-1, keepdims=True)
    el2 = jnp.where(lane == i1, NEG, el)
    v2 = jnp.max(el2, axis=-1, keepdims=True)
    i2 = jnp.min(jnp.where(el2 == v2, lane, LANES), axis=-1, keepdims=True)
    e21 = jnp.exp(v2 - v1)
    w1 = g_w / (1.0 + e21)
    w2 = g_w * e21 / (1.0 + e21)
    hit1 = lane == i1
    hit2 = lane == i2
    onehot = jnp.where(hit1 | hit2, 1.0, 0.0)
    incl = jnp.dot(tri_ref[...], onehot.astype(jnp.bfloat16), preferred_element_type=jnp.float32)
    carry = carry_scr[...]
    before = incl - onehot + carry
    r1 = jnp.sum(jnp.where(hit1, before, 0.0), axis=-1, keepdims=True)
    r2 = jnp.sum(jnp.where(hit2, before, 0.0), axis=-1, keepdims=True)
    carry = carry + incl[tm - 1:tm, :]
    carry_scr[...] = carry
    cnt_ref[...] = carry
    out = jnp.where(lane == 0, (i1 - LG0).astype(jnp.float32), 0.0)
    out = jnp.where(lane == 1, (i2 - LG0).astype(jnp.float32), out)
    out = jnp.where(lane == 2, w1, out)
    out = jnp.where(lane == 3, w2, out)
    out = jnp.where(lane == 4, r1, out)
    out = jnp.where(lane == 5, r2, out)
    out_ref[...] = out


def _route(lg, tm):
    n = lg.shape[0]
    tri = (np.arange(tm)[:, None] >= np.arange(tm)[None, :]).astype(np.float32)
    return pl.pallas_call(
        _route_body,
        grid=(n // tm,),
        in_specs=[pl.BlockSpec((tm, LANES), lambda i: (i, 0)),
                  pl.BlockSpec((tm, tm), lambda i: (0, 0))],
        out_specs=[pl.BlockSpec((tm, LANES), lambda i: (i, 0)),
                   pl.BlockSpec((1, LANES), lambda i: (0, 0))],
        out_shape=[jax.ShapeDtypeStruct((n, LANES), jnp.float32),
                   jax.ShapeDtypeStruct((1, LANES), jnp.float32)],
        scratch_shapes=[pltpu.VMEM((1, LANES), jnp.float32)],
        compiler_params=pltpu.CompilerParams(dimension_semantics=("arbitrary",)),
        name="moe_route",
    )(lg, jnp.asarray(tri, jnp.bfloat16))


def _dispatch_body(pos_ref, h_ref, xs_in_ref, xs_ref, sem):
    del xs_in_ref
    tm = h_ref.shape[0] // CH
    base = pl.program_id(0) * tm

    def issue(t, carry):
        src = pl.multiple_of(t * CH, CH)
        for slot in range(2):
            dst = pl.multiple_of(pos_ref[2 * (base + t) + slot] * CH, CH)
            pltpu.make_async_copy(h_ref.at[pl.ds(src, CH)], xs_ref.at[pl.ds(dst, CH)], sem).start(priority=slot)
        return carry

    lax.fori_loop(0, tm, issue, 0)
    for _ in range(2):
        pltpu.make_async_copy(h_ref, xs_ref.at[pl.ds(0, tm * CH)], sem).wait()


DISPATCH_TM = 128


def _dispatch(pos_flat, h_c, n_slots):
    zeros = jnp.zeros((n_slots * CH, LANES), h_c.dtype)
    tm = DISPATCH_TM
    return pl.pallas_call(
        _dispatch_body,
        grid_spec=pltpu.PrefetchScalarGridSpec(
            num_scalar_prefetch=1, grid=(h_c.shape[0] // (tm * CH),),
            in_specs=[pl.BlockSpec((tm * CH, LANES), lambda i, pos: (i, 0)),
                      pl.BlockSpec(memory_space=pl.ANY)],
            out_specs=pl.BlockSpec(memory_space=pl.ANY),
            scratch_shapes=[pltpu.SemaphoreType.DMA(())]),
        out_shape=jax.ShapeDtypeStruct(zeros.shape, h_c.dtype),
        input_output_aliases={2: 0},
        compiler_params=pltpu.CompilerParams(dimension_semantics=("arbitrary",)),
        name="moe_dispatch",
    )(pos_flat, h_c, zeros)


def _ffn_body(te_ref, nu_ref, x_ref, wg_ref, wu_ref, wd_ref, y_ref, wg_bf, wu_bf, wd_bf):
    i = pl.program_id(0)
    prev = te_ref[jnp.maximum(i - 1, 0)]

    @pl.when((i == 0) | (te_ref[i] != prev))
    def _():
        wg_bf[...] = wg_ref[0].astype(jnp.bfloat16)
        wu_bf[...] = wu_ref[0].astype(jnp.bfloat16)
        wd_bf[...] = wd_ref[0].astype(jnp.bfloat16)

    @pl.when(i < nu_ref[0])
    def _():
        x = _load_chunked(x_ref, x_ref.shape[0] // CH).astype(jnp.bfloat16)
        g = jnp.dot(x, wg_bf[...], preferred_element_type=jnp.float32)
        u = jnp.dot(x, wu_bf[...], preferred_element_type=jnp.float32)
        hid = (g * _sigmoid(g) * u).astype(jnp.bfloat16)
        _store_chunked(y_ref, jnp.dot(hid, wd_bf[...], preferred_element_type=jnp.float32))

    @pl.when(i >= nu_ref[0])
    def _():
        y_ref[...] = jnp.zeros_like(y_ref)


def _ffn(tile_expert, n_used, xs_c, w_gate, w_up, w_down, tm):
    d = D_MODEL
    nt = xs_c.shape[0] // (tm * CH)
    return pl.pallas_call(
        _ffn_body,
        grid_spec=pltpu.PrefetchScalarGridSpec(
            num_scalar_prefetch=2, grid=(nt,),
            in_specs=[pl.BlockSpec((tm * CH, LANES), lambda i, te, nu: (jnp.minimum(i, nu[0] - 1), 0)),
                      pl.BlockSpec((1, d, D_EXPERT), lambda i, te, nu: (te[i], 0, 0)),
                      pl.BlockSpec((1, d, D_EXPERT), lambda i, te, nu: (te[i], 0, 0)),
                      pl.BlockSpec((1, D_EXPERT, d), lambda i, te, nu: (te[i], 0, 0))],
            out_specs=pl.BlockSpec((tm * CH, LANES), lambda i, te, nu: (i, 0)),
            scratch_shapes=[pltpu.VMEM((d, D_EXPERT), jnp.bfloat16),
                            pltpu.VMEM((d, D_EXPERT), jnp.bfloat16),
                            pltpu.VMEM((D_EXPERT, d), jnp.bfloat16)]),
        out_shape=jax.ShapeDtypeStruct(xs_c.shape, jnp.float32),
        compiler_params=pltpu.CompilerParams(dimension_semantics=("arbitrary",),
                                             vmem_limit_bytes=VMEM_LIMIT),
        name="moe_ffn",
    )(tile_expert, n_used, xs_c, w_gate, w_up, w_down)


def _combine_body(pos_ref, h_ref, rw_ref, g_ref, b_ref, y_ref, out_a_ref, out_b_ref, buf, sem, *, tiles_a):
    tm = out_a_ref.shape[0]
    i = pl.program_id(0)
    cur = i % 2

    def gather(tile, parity):
        def one(t, carry):
            for slot in range(2):
                src = pl.multiple_of(pos_ref[2 * (tile * tm + t) + slot] * CH, CH)
                dst = pl.multiple_of(t * CH, CH)
                pltpu.make_async_copy(y_ref.at[pl.ds(src, CH)], buf.at[parity, slot, pl.ds(dst, CH)],
                                      sem.at[parity, slot]).start(priority=slot)
            return carry
        lax.fori_loop(0, tm, one, 0)

    @pl.when(i == 0)
    def _():
        gather(0, 0)

    @pl.when(i + 1 < pl.num_programs(0))
    def _():
        gather(i + 1, 1 - cur)

    for slot in range(2):
        pltpu.make_async_copy(y_ref.at[pl.ds(0, tm * CH)], buf.at[cur, slot], sem.at[cur, slot]).wait()
    rw = rw_ref[...]
    z = rw[:, 2:3] * _load_chunked(buf, tm, (cur, 0)) + rw[:, 3:4] * _load_chunked(buf, tm, (cur, 1))
    res = _layer_norm(DN_ALPHA * _load_chunked(h_ref, tm) + z, g_ref[...], b_ref[...])

    @pl.when(i < tiles_a)
    def _():
        out_a_ref[...] = res

    @pl.when(i >= tiles_a)
    def _():
        out_b_ref[...] = res


def _combine(pos_flat, h_c, route_out, g, b, y_c, tm, n_a):
    n, d = h_c.shape[0] // CH, D_MODEL
    assert n_a % tm == 0 and (n - n_a) % tm == 0 and 0 < n_a < n
    tiles_a = n_a // tm
    return pl.pallas_call(
        functools.partial(_combine_body, tiles_a=tiles_a),
        grid_spec=pltpu.PrefetchScalarGridSpec(
            num_scalar_prefetch=1, grid=(n // tm,),
            in_specs=[pl.BlockSpec((tm * CH, LANES), lambda i, pos: (i, 0)),
                      pl.BlockSpec((tm, LANES), lambda i, pos: (i, 0)),
                      pl.BlockSpec((1, d), lambda i, pos: (0, 0)),
                      pl.BlockSpec((1, d), lambda i, pos: (0, 0)),
                      pl.BlockSpec(memory_space=pl.ANY)],
            out_specs=[pl.BlockSpec((tm, d), lambda i, pos: (jnp.minimum(i, tiles_a - 1), 0)),
                       pl.BlockSpec((tm, d), lambda i, pos: (jnp.maximum(i - tiles_a, 0), 0))],
            scratch_shapes=[pltpu.VMEM((2, 2, tm * CH, LANES), jnp.float32),
                            pltpu.SemaphoreType.DMA((2, 2))]),
        out_shape=[jax.ShapeDtypeStruct((n_a, d), jnp.float32),
                   jax.ShapeDtypeStruct((n - n_a, d), jnp.float32)],
        compiler_params=pltpu.CompilerParams(dimension_semantics=("arbitrary",),
                                             vmem_limit_bytes=VMEM_LIMIT),
        name="moe_combine",
    )(pos_flat, h_c, route_out, g.reshape(1, -1), b.reshape(1, -1), y_c)


FFN_TM = 256
PROJ_TM = 1664
PROJ_TN = 512
TAIL_TM = 320


def _moe_and_norm(h_c, lg, w_gate, w_up, w_down, ln2_g, ln2_b, n_first):
    n = lg.shape[0]
    route_out, cnt = _route(lg, 640)
    eid = route_out[:, 0:2].astype(jnp.int32)
    rank = route_out[:, 4:6].astype(jnp.int32)
    counts = cnt[0, LG0:LG0 + N_EXPERTS].astype(jnp.int32)
    tiles_per = (counts + FFN_TM - 1) // FFN_TM
    tile_end = jnp.cumsum(tiles_per)
    row_start = (tile_end - tiles_per) * FFN_TM
    hit = eid[..., None] == jnp.arange(N_EXPERTS, dtype=jnp.int32)
    pos_flat = (jnp.sum(jnp.where(hit, row_start, 0), axis=-1) + rank).reshape(-1)
    nt = (2 * n) // FFN_TM + N_EXPERTS
    n_used = tile_end[-1]
    tile_ids = jnp.minimum(jnp.arange(nt, dtype=jnp.int32), n_used - 1)
    tile_expert = jnp.sum((tile_end[None, :] <= tile_ids[:, None]).astype(jnp.int32), axis=1)
    xs_c = _dispatch(pos_flat, h_c, nt * FFN_TM)
    y_c = _ffn(tile_expert, n_used.reshape(1).astype(jnp.int32), xs_c, w_gate, w_up, w_down, FFN_TM)
    return _combine(pos_flat, h_c, route_out, ln2_g, ln2_b, y_c, DISPATCH_TM, n_first)


def kernel(x_prompt, x_sample, cache_kv, cache_win, state_hgrn, page_table, w_in, b_in, w_cmp1, w_cmp2, cmp_pe,
           hgrn_gamma, hgrn_norm, w_pa, w_pb, w_out, ln1_g, ln1_b, w_rg, b_rg, w_re, b_re, w_gate, w_up, w_down,
           ln2_g, ln2_b):
    n_p = x_prompt.shape[0] * x_prompt.shape[1]
    n_s = x_sample.shape[0] * x_sample.shape[1]
    x_all = jnp.concatenate([x_prompt.reshape(n_p, D_MODEL), x_sample.reshape(n_s, D_MODEL)], axis=0)
    x_bf = x_all.astype(jnp.bfloat16)
    bsz, seq = x_prompt.shape[:2]
    n_seq, t_new = x_sample.shape[:2]
    n_pages = page_table.shape[1]
    past_len = n_pages * PAGE_SIZE
    w = w_in[0]
    b = b_in[0]

    def seg(lo, hi, outs, tn, name, pad_to=None):
        ws, bs = w[:, lo:hi], b[lo:hi]
        if pad_to is not None:
            ws = jnp.pad(ws, ((0, 0), (0, pad_to - (hi - lo))))
            bs = jnp.pad(bs, (0, pad_to - (hi - lo)))
        return _proj(x_bf, ws.astype(jnp.bfloat16), bs, outs, PROJ_TM, tn, name)

    q_h, = seg(OFF_Q, OFF_KV, [(jnp.bfloat16, True)], PROJ_TN, "proj_q")
    kv32, kv_h = seg(OFF_KV, OFF_NG, [(jnp.float32, False), (jnp.bfloat16, True)], PROJ_TN, "proj_kv")
    ng, = seg(OFF_NG, OFF_H4, [(jnp.float32, False)], LANES, "proj_ng", pad_to=LANES)
    h4, = seg(OFF_H4, OFF_MG, [(jnp.float32, False)], PROJ_TN, "proj_h4")
    mg, = seg(OFF_MG, PROJ_COLS, [(jnp.float32, False)], PROJ_TN, "proj_mg")

    kv_p = kv32[:n_p].reshape(1, bsz, seq, 6, N_KV, HEAD_DIM)
    kv_s = kv32[n_p:].reshape(1, n_seq, t_new, 6, N_KV, HEAD_DIM)
    new_kv_prompt = kv_p[:, :, :, :KV_SLOTS]
    new_kv_sample = kv_s[:, :, :, :KV_SLOTS]
    new_win_prompt = kv_p[:, :, seq - min(WINDOW, seq):, 4:6]
    win_all = jnp.concatenate([cache_win, kv_s[:, :, :, 4:6].astype(cache_win.dtype)], axis=2)
    new_win_sample = win_all[:, :, win_all.shape[2] - min(WINDOW, win_all.shape[2]):]

    w1p = _cmp_w1_pairs(w_cmp1[0])
    n_cmp_p = (seq - CMP_LEN) // CMP_STRIDE + 1
    uv_p = _cmp_uv_rows(kv32, n_p, 2048, w1p)
    kcvc_p = _cmp_finish(uv_p, bsz, seq // CMP_STRIDE, w_cmp1[0], w_cmp2[0], cmp_pe[0], "cmp_finish_prompt")
    ng_t = ng[:n_p, :NG_COLS].reshape(bsz, seq, 3, N_KV, HPG).transpose(0, 3, 1, 2, 4).reshape(bsz, N_KV, seq, 3 * HPG)
    o_nsa_p = _nsa_prompt(q_h, kcvc_p, kv_h, ng_t, n_cmp_p)

    cache_t = cache_kv[0].transpose(0, 2, 3, 4, 1).reshape(cache_kv.shape[1], 2 * HALF_ROWS, PAGE_SIZE)
    win_t = cache_win[0].transpose(0, 2, 3, 4, 1).reshape(n_seq, HALF_ROWS, cache_win.shape[2])
    n_cmp_s = (past_len + t_new - CMP_LEN) // CMP_STRIDE + 1
    kcvc_s = _cmp_sample(page_table, cache_t, w1p, w_cmp1[0], w_cmp2[0], cmp_pe[0])
    rows_s = N_HEADS * t_new
    q_s = q_h[:, n_p:].reshape(N_KV, HPG, n_seq, t_new, HEAD_DIM).transpose(2, 0, 1, 3, 4).reshape(
        n_seq, N_KV, HPG * t_new, HEAD_DIM)
    q_bd = jnp.einsum('sgrd,gk->sgrkd', q_s, jnp.eye(N_KV, dtype=q_s.dtype)).reshape(n_seq, rows_s, GD)
    new_kv = kv_h[:, n_p:].reshape(6, N_KV, n_seq, t_new, HEAD_DIM).transpose(2, 0, 3, 1, 4).reshape(
        n_seq, 6, t_new, GD)
    new_kv = jnp.pad(new_kv, ((0, 0), (0, 0), (0, 8 - t_new), (0, 0)))
    ng_r = ng[n_p:, :NG_COLS].reshape(n_seq, t_new, 3, N_KV, HPG).transpose(0, 3, 4, 1, 2).reshape(
        n_seq, rows_s, 3)
    o_nsa_s = _nsa_sample(page_table, cache_t, q_bd, kcvc_s, new_kv, win_t, ng_r, n_cmp_s, t_new)
    o_nsa_s = o_nsa_s.reshape(n_seq, N_KV, HPG, t_new, HEAD_DIM).transpose(0, 3, 1, 2, 4).reshape(n_s, NSA_WIDTH)

    lower = jnp.cumsum(jax.nn.softmax(hgrn_gamma.astype(jnp.float32), axis=0), axis=0)
    lb = lower[0].reshape(HG_HEADS, HG_DK)
    zero_state = jnp.zeros((bsz, HG_HEADS, HG_DV, HG_DK), jnp.float32)
    o_hg_p, st_p = _hgrn(h4, 0, bsz, seq, lb, hgrn_norm[0], zero_state, HG_CHUNK, HG_CHUNK, 512, HG_HEADS,
                         "hgrn_prompt")
    h4_s = jnp.pad(h4[n_p:].reshape(n_seq, t_new, -1), ((0, 0), (0, 8 - t_new), (0, 0))).reshape(n_seq * 8, -1)
    o_hg_s, st_s = _hgrn(h4_s, 0, n_seq, 8, lb, hgrn_norm[0], state_hgrn[0].transpose(0, 1, 3, 2),
                         8, t_new, 8, HG_HEADS, "hgrn_sample")
    o_hg_s = o_hg_s.reshape(n_seq, 8, HG_WIDTH)[:, :t_new].reshape(n_s, HG_WIDTH)
    new_state_prompt = st_p.transpose(0, 1, 3, 2)[None].astype(x_prompt.dtype)
    new_state_sample = st_s.transpose(0, 1, 3, 2)[None].astype(state_hgrn.dtype)

    o_nsa = jnp.concatenate([o_nsa_p, o_nsa_s], axis=0)
    o_hg = jnp.concatenate([o_hg_p, o_hg_s], axis=0)
    u = _mix(o_nsa, o_hg, mg, w_pa[0].astype(jnp.bfloat16), w_pb[0].astype(jnp.bfloat16), TAIL_TM)
    wr = jnp.zeros((D_MODEL, LANES), jnp.float32).at[:, :N_GROUPS].set(w_rg[0]).at[:, LG0:LG0 + N_EXPERTS].set(w_re[0])
    br = jnp.zeros((1, LANES), jnp.float32).at[0, :N_GROUPS].set(b_rg[0]).at[0, LG0:LG0 + N_EXPERTS].set(b_re[0])
    wr_hi, wr_lo = _split_bf16(wr)
    h, lg = _ln1(u, x_all, w_out[0].astype(jnp.bfloat16), ln1_g[0], ln1_b[0], wr_hi, wr_lo, br, TAIL_TM)
    out_p, out_s = _moe_and_norm(h, lg, w_gate[0], w_up[0], w_down[0], ln2_g[0], ln2_b[0], n_p)
    y_prompt = out_p.reshape(bsz, seq, D_MODEL)
    y_sample = out_s.reshape(n_seq, t_new, D_MODEL)
    return (y_prompt, y_sample, new_kv_prompt, new_kv_sample, new_win_prompt, new_win_sample,
            new_state_prompt, new_state_sample)
```
